```python
import jax
import jax.numpy as jnp
from jax import lax
import numpy as np

D_MODEL = 1024
BATCH = 32
SEQ = 256
DEPTH = 2
DEC_BATCH = 4
DEC_SEQ = 1024
PAST_LEN = 256

GRID_W = 64
N_BRANCH = 4
CONV_W = 512
CONV_K = 31
POOL_W = 512
POOL_WINDOWS = (2, 4, 8, 16)
POOL_GROUPS = 4
POOL_G = POOL_W // POOL_GROUPS
GQA_HEADS = 8
GQA_KV_HEADS = 2
GQA_GROUP = GQA_HEADS // GQA_KV_HEADS
GQA_HEAD_DIM = 64
MLA_HEADS = 8
MLA_Q_RANK = 384
MLA_KV_RANK = 256
MLA_NOPE = 64
MLA_ROPE = 32
MLA_V = 64
ROPE_BASE = 10000.0
N_EXPERTS = 32
TOP_K = 4
D_FF = 1024
SWIGLU_LIMIT = 7.0
SWIGLU_ALPHA = 1.702
MOE_BLOCK = 128
Q_BLOCK = 128
EPS = 1e-6
GQA_SCALE = GQA_HEAD_DIM ** -0.5
MLA_SCALE = (MLA_NOPE + MLA_ROPE) ** -0.5
IN_SIZES = (2 * CONV_W, POOL_W, GQA_HEADS * GQA_HEAD_DIM, GQA_KV_HEADS * GQA_HEAD_DIM,
            GQA_KV_HEADS * GQA_HEAD_DIM, MLA_Q_RANK, MLA_KV_RANK, MLA_ROPE, N_BRANCH * D_MODEL)
IN_TOTAL = sum(IN_SIZES)
IN_SPLIT = tuple(int(s) for s in np.cumsum(IN_SIZES)[:-1])

kernel_name = 'hybrid_diffusion_prefix_trunk_step'


def rms_norm(x, g):
    xf = x.astype(jnp.float32)
    y = xf * lax.rsqrt(jnp.mean(xf * xf, axis=-1, keepdims=True) + EPS)
    return y.astype(x.dtype) * g


def layer_norm(x, g, b):
    xf = x.astype(jnp.float32)
    mu = jnp.mean(xf, axis=-1, keepdims=True)
    xc = xf - mu
    y = xc * lax.rsqrt(jnp.mean(xc * xc, axis=-1, keepdims=True) + EPS)
    return y.astype(x.dtype) * g + b


def rope_1d(x, pos):
    half = x.shape[-1] // 2
    freqs = jnp.power(ROPE_BASE, -jnp.arange(half, dtype=jnp.float32) / half)
    ang = pos.astype(jnp.float32)[:, None] * freqs[None, :]
    cos = jnp.cos(ang)[None, :, None, :]
    sin = jnp.sin(ang)[None, :, None, :]
    x1 = x[..., :half].astype(jnp.float32)
    x2 = x[..., half:].astype(jnp.float32)
    return jnp.concatenate([x1 * cos - x2 * sin, x2 * cos + x1 * sin], axis=-1).astype(x.dtype)


def rope_2d(x, row, col):
    r = x.shape[-1] // 2
    return jnp.concatenate([rope_1d(x[..., :r], row), rope_1d(x[..., r:], col)], axis=-1)


def depthwise_conv(u, w, b):
    out = lax.conv_general_dilated(
        u, w[:, None, :], window_strides=(1,), padding=[(CONV_K // 2, CONV_K // 2)],
        dimension_numbers=('NWC', 'WIO', 'NWC'), feature_group_count=u.shape[-1])
    return out + b


def centred_mean(u, w):
    B, L, C = u.shape
    cs = jnp.concatenate([jnp.zeros((B, 1, C), jnp.float32),
                          jnp.cumsum(u.astype(jnp.float32), axis=1)], axis=1)
    t = jnp.arange(L)
    lo = jnp.clip(t - w // 2, 0, L)
    hi = jnp.clip(t - w // 2 + w, 0, L)
    cnt = (hi - lo).astype(jnp.float32)
    return (cs[:, hi] - cs[:, lo]) / cnt[None, :, None]


def block_attention(q, k, v, scale):
    B, Lq, Hkv, G, dk = q.shape
    nb = Lq // Q_BLOCK
    qb = q.reshape(B, nb, Q_BLOCK, Hkv, G, dk).transpose(1, 0, 2, 3, 4, 5)

    def one_block(qblk):
        s = jnp.einsum('bqhgd,bkhd->bhgqk', qblk, k, preferred_element_type=jnp.float32) * scale
        p = jax.nn.softmax(s, axis=-1)
        return jnp.einsum('bhgqk,bkhd->bqhgd', p.astype(v.dtype), v)

    ob = lax.map(one_block, qb)
    return ob.transpose(1, 0, 2, 3, 4, 5).reshape(B, Lq, Hkv * G, v.shape[-1])


def moe(x2, lp):
    T, D = x2.shape
    logits = (x2 @ lp['w_router'] + lp['b_router']).astype(jnp.float32)
    top_val, top_idx = lax.top_k(logits, TOP_K)
    gate_w = jax.nn.softmax(top_val, axis=-1)
    n_slot = T * TOP_K
    e_flat = top_idx.reshape(-1)
    tok_flat = jnp.arange(n_slot, dtype=jnp.int32) // TOP_K
    w_flat = gate_w.reshape(-1)
    order = jnp.argsort(e_flat)
    e_sorted = e_flat[order]
    counts = jnp.bincount(e_flat, length=N_EXPERTS)
    starts = jnp.cumsum(counts) - counts
    padded = (counts + MOE_BLOCK - 1) // MOE_BLOCK * MOE_BLOCK
    pad_ends = jnp.cumsum(padded)
    pad_starts = pad_ends - padded
    dest = pad_starts[e_sorted] + (jnp.arange(n_slot) - starts[e_sorted])
    n_blocks = -(-n_slot // MOE_BLOCK) + N_EXPERTS
    cap = n_blocks * MOE_BLOCK
    tok_buf = jnp.full((cap,), T, jnp.int32).at[dest].set(tok_flat[order])
    wt_buf = jnp.zeros((cap,), jnp.float32).at[dest].set(w_flat[order])
    blk_expert = jnp.minimum(
        jnp.searchsorted(pad_ends, jnp.arange(n_blocks) * MOE_BLOCK, side='right'), N_EXPERTS - 1)
    x_pad = jnp.concatenate([x2, jnp.zeros((1, D), x2.dtype)], axis=0)
    xb = x_pad[tok_buf].reshape(n_blocks, MOE_BLOCK, D)
    w_gu, b_gu, w_dn, b_dn = lp['w_gu'], lp['b_gu'], lp['w_dn'], lp['b_dn']

    def expert_block(args):
        xblk, e = args
        gu = xblk @ w_gu[e] + b_gu[e]
        gate = jnp.minimum(gu[:, 0::2], SWIGLU_LIMIT)
        up = jnp.clip(gu[:, 1::2], -SWIGLU_LIMIT, SWIGLU_LIMIT)
        glu = gate * jax.nn.sigmoid(gate * SWIGLU_ALPHA)
        return ((up + 1.0) * glu) @ w_dn[e] + b_dn[e]

    yb = lax.map(expert_block, (xb, blk_expert))
    y = jnp.zeros((T + 1, D), jnp.float32).at[tok_buf].add(
        yb.reshape(cap, D).astype(jnp.float32) * wt_buf[:, None])
    return y[:T].astype(x2.dtype)


def modulation(cond, w, b):
    return (jax.nn.silu(cond) @ w + b).reshape(cond.shape[0], 1, 6, D_MODEL)


def token_mixing(h, lp, pos, ctx):
    B, L, _ = h.shape
    (a_in, b_in, q_c, k_c, v_c, cq_d, ckv_d, kr_d, g_in) = jnp.split(h @ lp['w_in'], IN_SPLIT, axis=-1)

    a = a_in[..., :CONV_W] * jax.nn.sigmoid(a_in[..., CONV_W:])
    a = depthwise_conv(a, lp['conv_dw'], lp['conv_dw_b'])
    a = jax.nn.silu(layer_norm(a, lp['conv_ln_g'], lp['conv_ln_b']))
    br_a = a @ lp['w_conv_out']

    ub = b_in.reshape(B, L, POOL_GROUPS, POOL_G)
    pooled = jnp.stack([centred_mean(ub[:, :, i], w) for i, w in enumerate(POOL_WINDOWS)],
                       axis=2).astype(ub.dtype) - ub
    pb = jnp.einsum('blgc,gcd->blgd', pooled, lp['pool_w']).reshape(B, L, POOL_W) * lp['pool_scale']
    br_b = pb @ lp['w_pool_out']

    q = rms_norm(q_c.reshape(B, L, GQA_HEADS, GQA_HEAD_DIM), lp['gqa_qn_g'])
    k = rms_norm(k_c.reshape(B, L, GQA_KV_HEADS, GQA_HEAD_DIM), lp['gqa_kn_g'])
    v = v_c.reshape(B, L, GQA_KV_HEADS, GQA_HEAD_DIM)

    cq = rms_norm(cq_d, lp['mla_qn_g'])
    qm = (cq @ lp['w_mla_q_up']).reshape(B, L, MLA_HEADS, MLA_NOPE + MLA_ROPE)
    q_nope, q_rope = qm[..., :MLA_NOPE], qm[..., MLA_NOPE:]
    ckv = rms_norm(ckv_d, lp['mla_kvn_g'])
    kr = kr_d
    own = (k, v, ckv, kr)

    if pos is not None:
        row, col = pos
        q = rope_2d(q, row, col)
        k = rope_2d(k, row, col)
        q_rope = rope_2d(q_rope, row, col)
        kr = rope_2d(kr[:, :, None, :], row, col)[:, :, 0, :]
    if ctx is not None:
        k = jnp.concatenate([ctx[0], k], axis=1)
        v = jnp.concatenate([ctx[1], v], axis=1)
        ckv = jnp.concatenate([ctx[2], ckv], axis=1)
        kr = jnp.concatenate([ctx[3], kr], axis=1)

    o_c = block_attention(q.reshape(B, L, GQA_KV_HEADS, GQA_GROUP, GQA_HEAD_DIM), k, v, GQA_SCALE)
    br_c = o_c.reshape(B, L, GQA_HEADS * GQA_HEAD_DIM) @ lp['w_gqa_out']

    Lk = ckv.shape[1]
    kv = (ckv @ lp['w_mla_kv_up']).reshape(B, Lk, MLA_HEADS, MLA_NOPE + MLA_V)
    k_m = jnp.concatenate([kv[..., :MLA_NOPE],
                           jnp.broadcast_to(kr[:, :, None, :], (B, Lk, MLA_HEADS, MLA_ROPE))], axis=-1)
    q_m = jnp.concatenate([q_nope, q_rope], axis=-1)[:, :, :, None, :]
    o_d = block_attention(q_m, k_m, kv[..., MLA_NOPE:], MLA_SCALE)
    br_d = o_d.reshape(B, L, MLA_HEADS * MLA_V) @ lp['w_mla_out']

    g = jax.nn.sigmoid(g_in.reshape(B, L, N_BRANCH, D_MODEL))
    merged = g[:, :, 0] * br_a + g[:, :, 1] * br_b + g[:, :, 2] * br_c + g[:, :, 3] * br_d
    return merged @ lp['w_o'], own


def trunk_block(x, mod, lp, pos, ctx):
    h = rms_norm(x, lp['norm1_g']) * (1.0 + mod[:, :, 1]) + mod[:, :, 0]
    mix, own = token_mixing(h, lp, pos, ctx)
    x = x + mod[:, :, 2] * mix
    h = rms_norm(x, lp['norm2_g']) * (1.0 + mod[:, :, 4]) + mod[:, :, 3]
    B, L, D = h.shape
    ffn = moe(h.reshape(B * L, D), lp).reshape(B, L, D)
    return x + mod[:, :, 5] * ffn, own


def setup_inputs(seed: int = 0) -> dict:
    keys = jax.random.split(jax.random.key(seed), 40)
    counter = [0]

    def nrm(shape, scale=1.0):
        k = keys[counter[0]]
        counter[0] += 1
        return scale * jax.random.normal(k, shape, jnp.float32)

    def gain(shape):
        return 1.0 + 0.05 * nrm(shape)

    D, L = D_MODEL, DEPTH
    return {
        'x_prompt': nrm((BATCH, SEQ, D)),
        'x_sample': nrm((DEC_BATCH, DEC_SEQ, D)),
        'cache_gqa_k': nrm((DEC_BATCH, L, PAST_LEN, GQA_KV_HEADS, GQA_HEAD_DIM)),
        'cache_gqa_v': nrm((DEC_BATCH, L, PAST_LEN, GQA_KV_HEADS, GQA_HEAD_DIM)),
        'cache_mla_ckv': nrm((DEC_BATCH, L, PAST_LEN, MLA_KV_RANK)),
        'cache_mla_krope': nrm((DEC_BATCH, L, PAST_LEN, MLA_ROPE)),
        'c': nrm((DEC_BATCH, D)),
        'c_ctx': nrm((D,)),
        'norm1_g': gain((L, D)),
        'norm2_g': gain((L, D)),
        'w_mod': nrm((L, D, 6 * D), 0.5 * D ** -0.5),
        'b_mod': nrm((L, 6 * D), 0.02),
        'w_in': nrm((L, D, IN_TOTAL), D ** -0.5),
        'conv_dw': nrm((L, CONV_K, CONV_W), CONV_K ** -0.5),
        'conv_dw_b': nrm((L, CONV_W), 0.02),
        'conv_ln_g': gain((L, CONV_W)),
        'conv_ln_b': nrm((L, CONV_W), 0.02),
        'w_conv_out': nrm((L, CONV_W, D), CONV_W ** -0.5),
        'pool_w': nrm((L, POOL_GROUPS, POOL_G, POOL_G), POOL_G ** -0.5),
        'pool_scale': gain((L, POOL_W)),
        'w_pool_out': nrm((L, POOL_W, D), POOL_W ** -0.5),
        'gqa_qn_g': gain((L, GQA_HEAD_DIM)),
        'gqa_kn_g': gain((L, GQA_HEAD_DIM)),
        'w_gqa_out': nrm((L, GQA_HEADS * GQA_HEAD_DIM, D), (GQA_HEADS * GQA_HEAD_DIM) ** -0.5),
        'mla_qn_g': gain((L, MLA_Q_RANK)),
        'w_mla_q_up': nrm((L, MLA_Q_RANK, MLA_HEADS * (MLA_NOPE + MLA_ROPE)), MLA_Q_RANK ** -0.5),
        'mla_kvn_g': gain((L, MLA_KV_RANK)),
        'w_mla_kv_up': nrm((L, MLA_KV_RANK, MLA_HEADS * (MLA_NOPE + MLA_V)), MLA_KV_RANK ** -0.5),
        'w_mla_out': nrm((L, MLA_HEADS * MLA_V, D), (MLA_HEADS * MLA_V) ** -0.5),
        'w_o': nrm((L, D, D), D ** -0.5),
        'w_router': nrm((L, D, N_EXPERTS), D ** -0.5),
        'b_router': nrm((L, N_EXPERTS), 0.01),
        'w_gu': nrm((L, N_EXPERTS, D, 2 * D_FF), D ** -0.5),
        'b_gu': nrm((L, N_EXPERTS, 2 * D_FF), 0.01),
        'w_dn': nrm((L, N_EXPERTS, D_FF, D), D_FF ** -0.5),
        'b_dn': nrm((L, N_EXPERTS, D), 0.01),
        'final_g': gain((D,)),
    }


def reference(x_prompt, x_sample, cache_gqa_k, cache_gqa_v, cache_mla_ckv, cache_mla_krope, c, c_ctx,
              norm1_g, norm2_g, w_mod, b_mod, w_in, conv_dw, conv_dw_b, conv_ln_g, conv_ln_b, w_conv_out,
              pool_w, pool_scale, w_pool_out, gqa_qn_g, gqa_kn_g, w_gqa_out, mla_qn_g, w_mla_q_up,
              mla_kvn_g, w_mla_kv_up, w_mla_out, w_o, w_router, b_router, w_gu, b_gu, w_dn, b_dn, final_g):
    L_lat = x_sample.shape[1]
    rows = L_lat // GRID_W
    row = jnp.repeat(jnp.arange(rows, dtype=jnp.int32), GRID_W)
    col = jnp.arange(L_lat, dtype=jnp.int32) % GRID_W

    xp, xs = x_prompt, x_sample
    st_k, st_v, st_ckv, st_kr = [], [], [], []
    for l in range(DEPTH):
        lp = {
            'norm1_g': norm1_g[l], 'norm2_g': norm2_g[l], 'w_in': w_in[l],
            'conv_dw': conv_dw[l], 'conv_dw_b': conv_dw_b[l], 'conv_ln_g': conv_ln_g[l],
            'conv_ln_b': conv_ln_b[l], 'w_conv_out': w_conv_out[l],
            'pool_w': pool_w[l], 'pool_scale': pool_scale[l], 'w_pool_out': w_pool_out[l],
            'gqa_qn_g': gqa_qn_g[l], 'gqa_kn_g': gqa_kn_g[l], 'w_gqa_out': w_gqa_out[l],
            'mla_qn_g': mla_qn_g[l], 'w_mla_q_up': w_mla_q_up[l], 'mla_kvn_g': mla_kvn_g[l],
            'w_mla_kv_up': w_mla_kv_up[l], 'w_mla_out': w_mla_out[l], 'w_o': w_o[l],
            'w_router': w_router[l], 'b_router': b_router[l], 'w_gu': w_gu[l], 'b_gu': b_gu[l],
            'w_dn': w_dn[l], 'b_dn': b_dn[l],
        }
        mod_p = modulation(c_ctx[None, :], w_mod[l], b_mod[l])
        xp, (k_l, v_l, ckv_l, kr_l) = trunk_block(xp, mod_p, lp, None, None)
        st_k.append(k_l)
        st_v.append(v_l)
        st_ckv.append(ckv_l)
        st_kr.append(kr_l)
        mod_s = modulation(c, w_mod[l], b_mod[l])
        ctx_l = (cache_gqa_k[:, l], cache_gqa_v[:, l], cache_mla_ckv[:, l], cache_mla_krope[:, l])
        xs, _ = trunk_block(xs, mod_s, lp, (row, col), ctx_l)

    y_prompt = rms_norm(xp, final_g)
    y_sample = rms_norm(xs, final_g)
    state_gqa_k = jnp.stack(st_k, axis=1)
    state_gqa_v = jnp.stack(st_v, axis=1)
    state_mla_ckv = jnp.stack(st_ckv, axis=1)
    state_mla_krope = jnp.stack(st_kr, axis=1)
    return (y_prompt, y_sample, state_gqa_k, state_gqa_v, state_mla_ckv, state_mla_krope)
```

```python
import functools

import jax
import jax.numpy as jnp
import numpy as np
from jax import lax
from jax.experimental import pallas as pl
from jax.experimental.pallas import tpu as pltpu

D_MODEL = 1024
GRID_W = 64
CONV_W = 512
CONV_K = 31
POOL_W = 512
POOL_WINDOWS = (2, 4, 8, 16)
POOL_G = 128
GQA_HEADS = 8
GQA_KV_HEADS = 2
GQA_HEAD_DIM = 64
MLA_HEADS = 8
MLA_Q_RANK = 384
MLA_KV_RANK = 256
MLA_NOPE = 64
MLA_ROPE = 32
MLA_V = 64
ROPE_BASE = 10000.0
N_EXPERTS = 32
TOP_K = 4
D_FF = 1024
SWIGLU_LIMIT = 7.0
SWIGLU_ALPHA = 1.702
EPS = 1e-6
GQA_SCALE = GQA_HEAD_DIM ** -0.5
MLA_SCALE = (MLA_NOPE + MLA_ROPE) ** -0.5

LANES = 128
TM = 256
HALO = 16
MOE_BM = 256
W1_COLS = 3072
N_GATE = 4 * D_MODEL
VMEM_LIMIT = 56 * 1024 * 1024

BF = jnp.bfloat16
F32 = jnp.float32

_C_A, _C_B, _C_Q, _C_K, _C_V, _C_CQ, _C_CKV, _C_KR = 0, 1024, 1536, 2048, 2176, 2304, 2688, 2944
_SPLIT_GATE = 2976


def _dot(a, b):
    return jnp.dot(a, b, preferred_element_type=F32)


def _dot_nt(a, b):
    return lax.dot_general(a, b, (((1,), (1,)), ((), ())), preferred_element_type=F32)


def _rms(x):
    return x * lax.rsqrt(jnp.mean(x * x, axis=-1, keepdims=True) + EPS)


def _group_mean_sq(x, ones_bd, width):
    xx = x * x
    hi = xx.astype(BF)
    lo = (xx - hi.astype(F32)).astype(BF)
    return (_dot(hi, ones_bd) + _dot(lo, ones_bd)) * (1.0 / width)


def _tile_lanes(t, width):
    reps = width // LANES
    return t if reps == 1 else jnp.concatenate([t] * reps, axis=-1)


def _rope(x, cos, sin_a, sin_b, shift):
    w = x.shape[-1]
    return (x * _tile_lanes(cos, w) + pltpu.roll(x, w - shift, 1) * _tile_lanes(sin_a, w)
            + pltpu.roll(x, shift, 1) * _tile_lanes(sin_b, w))


def _modulated_norm(x, g, shift, scale):
    return _rms(x) * g * (1.0 + scale) + shift


def _mod_kernel(cond_ref, w_ref, b_ref, o_ref):
    c = cond_ref[...]
    s = (c * jax.nn.sigmoid(c)).astype(BF)
    o_ref[...] = _dot(s, w_ref[...].astype(BF)) + b_ref[...]


def _modulation(cond, w_mod, b_mod):
    depth, d, n = w_mod.shape
    rows = cond.shape[0]
    return pl.pallas_call(
        _mod_kernel,
        grid=(depth, n // D_MODEL),
        in_specs=[
            pl.BlockSpec((rows, d), lambda l, j: (0, 0)),
            pl.BlockSpec((None, d, D_MODEL), lambda l, j: (l, 0, j)),
            pl.BlockSpec((None, 1, D_MODEL), lambda l, j: (l, 0, j)),
        ],
        out_specs=pl.BlockSpec((None, rows, D_MODEL), lambda l, j: (l, 0, j)),
        out_shape=jax.ShapeDtypeStruct((depth, rows, n), F32),
        name="modulation",
    )(cond, w_mod, b_mod.reshape(depth, 1, n))


def _proj_kernel(x_ref, mod_ref, g1_ref, w1_ref, rope_ref, qn_ref, kn_ref, cqn_ref, kvn_ref,
                 wqup_ref, wkvup_ref, place_ref, ones_ref,
                 a_ref, b_ref, q_ref, k_ref, v_ref, qm_ref, km_ref, vm_ref,
                 ks_ref, vs_ref, ckvs_ref, krs_ref):
    x = x_ref[...]
    h = _modulated_norm(x, g1_ref[...], mod_ref[0:1, :], mod_ref[1:2, :])
    y = _dot(h.astype(BF), w1_ref[...])

    a_ref[...] = (y[:, _C_A:_C_A + CONV_W] * jax.nn.sigmoid(y[:, _C_A + CONV_W:_C_B])).astype(BF)
    b_ref[...] = y[:, _C_B:_C_Q].astype(BF)

    def tab(j):
        return rope_ref[:, j * LANES:(j + 1) * LANES]

    q = y[:, _C_Q:_C_K]
    q = q * lax.rsqrt(_group_mean_sq(q, ones_ref[...], GQA_HEAD_DIM) + EPS) * qn_ref[...]
    q = _rope(q, tab(0), tab(1), tab(2), GQA_HEAD_DIM // 4)
    q_ref[...] = (q * GQA_SCALE).astype(BF)

    k = y[:, _C_K:_C_V]
    k = k * lax.rsqrt(_group_mean_sq(k, ones_ref[0:LANES, 0:LANES], GQA_HEAD_DIM) + EPS) * kn_ref[...]
    ks_ref[...] = k
    k_ref[...] = _rope(k, tab(0), tab(1), tab(2), GQA_HEAD_DIM // 4).astype(BF)

    v = y[:, _C_V:_C_CQ]
    vs_ref[...] = v
    v_ref[...] = v.astype(BF)

    cq = _rms(y[:, _C_CQ:_C_CKV]) * cqn_ref[...]
    qm = _dot(cq.astype(BF), wqup_ref[...])
    qm = _rope(qm, tab(3), tab(4), tab(5), MLA_ROPE // 4)
    qm_ref[...] = (qm * MLA_SCALE).astype(BF)

    ckv = _rms(y[:, _C_CKV:_C_KR]) * kvn_ref[...]
    ckvs_ref[...] = ckv
    kv = _dot(ckv.astype(BF), wkvup_ref[...])
    kr = y[:, _C_KR:W1_COLS]
    krs_ref[...] = kr[:, 0:MLA_ROPE]
    kr_rot = _rope(kr, tab(6), tab(7), tab(8), MLA_ROPE // 4)
    km = kv[:, 0:MLA_HEADS * LANES] + _dot(kr_rot.astype(BF), place_ref[...])
    km_ref[...] = km.astype(BF)
    vm_ref[...] = kv[:, MLA_HEADS * LANES:].astype(BF)


def _const_spec(shape):
    nd = len(shape)
    return pl.BlockSpec(shape, lambda *_: (0,) * nd)


def _layer_spec(shape, l):
    nd = len(shape)
    return pl.BlockSpec((None,) + shape, lambda *_: (l,) + (0,) * nd)


def _proj(l, x, mod, geo, p):
    nt = geo["nt"]
    tok = nt * TM
    seq_row, rope_blk = geo["seq_row"], geo["rope_blk"]

    def tile(width, dtype):
        return pl.BlockSpec((TM, width), lambda i: (i, 0)), jax.ShapeDtypeStruct((tok, width), dtype)

    outs = [tile(CONV_W, BF), tile(POOL_W, BF), tile(512, BF), tile(LANES, BF), tile(LANES, BF),
            tile(MLA_HEADS * LANES, BF), tile(MLA_HEADS * LANES, BF), tile(MLA_HEADS * MLA_V, BF),
            tile(LANES, F32), tile(LANES, F32), tile(MLA_KV_RANK, F32), tile(MLA_ROPE, F32)]
    return pl.pallas_call(
        _proj_kernel,
        grid=(nt,),
        in_specs=[
            pl.BlockSpec((TM, D_MODEL), lambda i: (i, 0)),
            pl.BlockSpec((None, None, 6, D_MODEL), lambda i: (l, seq_row(i), 0, 0)),
            _layer_spec((1, D_MODEL), l),
            _layer_spec((D_MODEL, W1_COLS), l),
            pl.BlockSpec((TM, 9 * LANES), lambda i: (rope_blk(i), 0)),
            _layer_spec((1, 512), l),
            _layer_spec((1, LANES), l),
            _layer_spec((1, MLA_Q_RANK), l),
            _layer_spec((1, MLA_KV_RANK), l),
            _layer_spec((MLA_Q_RANK, MLA_HEADS * LANES), l),
            _layer_spec((MLA_KV_RANK, MLA_HEADS * (LANES + MLA_V)), l),
            _const_spec((LANES, MLA_HEADS * LANES)),
            _const_spec((512, 512)),
        ],
        out_specs=[o[0] for o in outs],
        out_shape=[o[1] for o in outs],
        compiler_params=pltpu.CompilerParams(dimension_semantics=("arbitrary",), vmem_limit_bytes=VMEM_LIMIT),
        name="proj",
    )(x, mod, p["norm1_g"], p["w1"], p["rope"], p["qn_g"], p["kn_g"], p["cqn_g"], p["kvn_g"],
      p["wqup"], p["wkvup"], p["place"], p["ones_bd"])


def _cache_kernel(ckv_ref, kr_ref, wkvup_ref, place_ref, km_ref, vm_ref):
    kv = _dot(ckv_ref[...].astype(BF), wkvup_ref[...])
    km = kv[:, 0:MLA_HEADS * LANES] + _dot(kr_ref[...].astype(BF), place_ref[...])
    km_ref[...] = km.astype(BF)
    vm_ref[...] = kv[:, MLA_HEADS * LANES:].astype(BF)


def _cache_prep(ckv, kr_pad, p):
    bs, depth, past, _ = ckv.shape
    return pl.pallas_call(
        _cache_kernel,
        grid=(bs, depth),
        in_specs=[
            pl.BlockSpec((None, None, past, MLA_KV_RANK), lambda b, l: (b, l, 0, 0)),
            pl.BlockSpec((None, None, past, LANES), lambda b, l: (b, l, 0, 0)),
            pl.BlockSpec((None, MLA_KV_RANK, MLA_HEADS * (LANES + MLA_V)), lambda b, l: (l, 0, 0)),
            pl.BlockSpec((LANES, MLA_HEADS * LANES), lambda b, l: (0, 0)),
        ],
        out_specs=[
            pl.BlockSpec((None, None, past, MLA_HEADS * LANES), lambda b, l: (b, l, 0, 0)),
            pl.BlockSpec((None, None, past, MLA_HEADS * MLA_V), lambda b, l: (b, l, 0, 0)),
        ],
        out_shape=[jax.ShapeDtypeStruct((bs, depth, past, MLA_HEADS * LANES), BF),
                   jax.ShapeDtypeStruct((bs, depth, past, MLA_HEADS * MLA_V), BF)],
        name="cache_prep",
    )(ckv, kr_pad, p["wkvup"], p["place"])


def _softmax_pv(s, v):
    m = jnp.max(s, axis=-1, keepdims=True)
    e = jnp.exp(s - m)
    l = jnp.sum(e, axis=-1, keepdims=True)
    return _dot(e.astype(BF), v) / l


def _attention_body(q, k, v, qm, km, vm, oc_ref, od_ref):
    lane_k = lax.broadcasted_iota(jnp.int32, k.shape, 1)
    lane_q = lax.broadcasted_iota(jnp.int32, (TM, LANES), 1)
    lo_k = lane_k < GQA_HEAD_DIM
    lo_q = lane_q < GQA_HEAD_DIM
    k32, v32 = k.astype(F32), v.astype(F32)
    k_sw = pltpu.roll(k32, GQA_HEAD_DIM, 1)
    v_sw = pltpu.roll(v32, GQA_HEAD_DIM, 1)
    k_dup = [jnp.where(lo_k, k32, k_sw).astype(BF), jnp.where(lo_k, k_sw, k32).astype(BF)]
    v_dup = [jnp.where(lo_k, v32, v_sw).astype(BF), jnp.where(lo_k, v_sw, v32).astype(BF)]
    zero = jnp.zeros((TM, LANES), BF)
    group = GQA_HEADS // GQA_KV_HEADS
    for j in range(GQA_HEADS // 2):
        qs = q[:, j * LANES:(j + 1) * LANES]
        g = (2 * j) // group
        o_lo = _softmax_pv(_dot_nt(jnp.where(lo_q, qs, zero), k_dup[g]), v_dup[g])
        o_hi = _softmax_pv(_dot_nt(jnp.where(lo_q, zero, qs), k_dup[g]), v_dup[g])
        oc_ref[:, j * LANES:(j + 1) * LANES] = jnp.where(lo_q, o_lo, o_hi).astype(BF)
    for j in range(MLA_HEADS // 2):
        vs = vm[:, j * LANES:(j + 1) * LANES]
        outs = []
        for h in (2 * j, 2 * j + 1):
            s = _dot_nt(qm[:, h * LANES:(h + 1) * LANES], km[:, h * LANES:(h + 1) * LANES])
            outs.append(_softmax_pv(s, vs))
        od_ref[:, j * LANES:(j + 1) * LANES] = jnp.where(lo_q, outs[0], outs[1]).astype(BF)


def _attn_prompt_kernel(q_ref, k_ref, v_ref, qm_ref, km_ref, vm_ref, oc_ref, od_ref):
    _attention_body(q_ref[...], k_ref[...], v_ref[...], qm_ref[...], km_ref[...], vm_ref[...], oc_ref, od_ref)


def _attn_sample_kernel(q_ref, k_ref, v_ref, qm_ref, km_ref, vm_ref, ck_ref, cv_ref, ckm_ref, cvm_ref,
                        oc_ref, od_ref):
    k = jnp.concatenate([ck_ref[...], k_ref[...]], axis=0)
    v = jnp.concatenate([cv_ref[...], v_ref[...]], axis=0)
    km = jnp.concatenate([ckm_ref[...], km_ref[...]], axis=0)
    vm = jnp.concatenate([cvm_ref[...], vm_ref[...]], axis=0)
    _attention_body(q_ref[...], k, v, qm_ref[...], km, vm, oc_ref, od_ref)


def _attention(l, geo, q, k, v, qm, km, vm, cache):
    n_p, bs, tps, nt = geo["np"], geo["bs"], geo["tps"], geo["nt"]
    tok = nt * TM
    ls = tps * TM
    off = (n_p * TM) // ls
    widths = (512, LANES, LANES, MLA_HEADS * LANES, MLA_HEADS * LANES, MLA_HEADS * MLA_V)
    params = pltpu.CompilerParams(dimension_semantics=("arbitrary",), vmem_limit_bytes=VMEM_LIMIT)
    oc_p, od_p = pl.pallas_call(
        _attn_prompt_kernel,
        grid=(n_p,),
        in_specs=[pl.BlockSpec((TM, w), lambda i: (i, 0)) for w in widths],
        out_specs=[pl.BlockSpec((TM, 512), lambda i: (i, 0))] * 2,
        out_shape=[jax.ShapeDtypeStruct((n_p * TM, 512), BF)] * 2,
        compiler_params=params,
        name="attn_prompt",
    )(q, k, v, qm, km, vm)
    ck, cv, ckm, cvm = cache
    past = ck.shape[2]
    q_spec = lambda w: pl.BlockSpec((TM, w), lambda b, j: (n_p + b * tps + j, 0))
    kv_spec = lambda w: pl.BlockSpec((ls, w), lambda b, j: (off + b, 0))
    c_spec = lambda w: pl.BlockSpec((None, None, past, w), lambda b, j: (b, l, 0, 0))
    params2 = pltpu.CompilerParams(dimension_semantics=("arbitrary", "arbitrary"), vmem_limit_bytes=VMEM_LIMIT)
    o_spec = pl.BlockSpec((TM, 512), lambda b, j: (b * tps + j, 0))
    oc_s, od_s = pl.pallas_call(
        _attn_sample_kernel,
        grid=(bs, tps),
        in_specs=[q_spec(512), kv_spec(LANES), kv_spec(LANES), q_spec(MLA_HEADS * LANES),
                  kv_spec(MLA_HEADS * LANES), kv_spec(MLA_HEADS * MLA_V),
                  c_spec(LANES), c_spec(LANES), c_spec(MLA_HEADS * LANES), c_spec(MLA_HEADS * MLA_V)],
        out_specs=[o_spec, o_spec],
        out_shape=[jax.ShapeDtypeStruct((bs * ls, 512), BF)] * 2,
        compiler_params=params2,
        name="attn_sample",
    )(q, k, v, qm, km, vm, ck, cv, ckm, cvm)
    return oc_p, od_p, oc_s, od_s


def _merge_kernel(geo, x_ref, mod_ref, a_ref, ap_ref, an_ref, b_ref, bp_ref, bn_ref,
                  ocp_ref, odp_ref, ocs_ref, ods_ref,
                  g1_ref, wg_ref, cw_ref, cb_ref, lng_ref, lnb_ref, wco_ref, pw_ref, ps_ref, wpo_ref,
                  wgo_ref, wmo_ref, wo_ref, g2_ref, wr_ref, br_ref,
                  xo_ref, h2_ref, ti_ref, tw_ref, abuf, bbuf):
    n_p, tps = geo["np"], geo["tps"]
    i = pl.program_id(0)
    j = jnp.where(i < n_p, 0, (i - n_p) % tps)
    n_seq_tiles = jnp.where(i < n_p, 1, tps)
    has_prev = j > 0
    has_next = j < n_seq_tiles - 1

    def fill(buf, cur, prev, nxt):
        buf[0:HALO, :] = jnp.where(has_prev, prev[...].astype(F32), 0.0)
        buf[HALO:HALO + TM, :] = cur[...].astype(F32)
        buf[HALO + TM:, :] = jnp.where(has_next, nxt[...].astype(F32), 0.0)

    fill(abuf, a_ref, ap_ref, an_ref)
    fill(bbuf, b_ref, bp_ref, bn_ref)

    rows = 32
    conv = []
    for r0 in range(0, TM, rows):
        acc = jnp.zeros((rows, CONV_W), F32)
        for t in range(CONV_K):
            s = r0 + t + HALO - CONV_K // 2
            acc = acc + abuf[s:s + rows, :] * cw_ref[t:t + 1, :]
        conv.append(acc)
    ca = jnp.concatenate(conv, axis=0) + cb_ref[...]
    mu = jnp.mean(ca, axis=-1, keepdims=True)
    xc = ca - mu
    ln = xc * lax.rsqrt(jnp.mean(xc * xc, axis=-1, keepdims=True) + EPS) * lng_ref[...] + lnb_ref[...]
    br_a = _dot((ln * jax.nn.sigmoid(ln)).astype(BF), wco_ref[...])

    pos = j * TM + lax.broadcasted_iota(jnp.int32, (TM, 1), 0)
    seq_len = n_seq_tiles * TM
    mixed = []
    for g, w in enumerate(POOL_WINDOWS):
        lo, hi = g * POOL_G, (g + 1) * POOL_G
        acc = jnp.zeros((TM, POOL_G), F32)
        for o in range(-(w // 2), w - w // 2):
            acc = acc + bbuf[HALO + o:HALO + o + TM, lo:hi]
        cnt = jnp.clip(pos - w // 2 + w, 0, seq_len) - jnp.clip(pos - w // 2, 0, seq_len)
        pooled = acc / cnt.astype(F32) - bbuf[HALO:HALO + TM, lo:hi]
        mixed.append(_dot(pooled.astype(BF), pw_ref[g]))
    pb = jnp.concatenate(mixed, axis=-1) * ps_ref[...]
    br_b = _dot(pb.astype(BF), wpo_ref[...])

    is_prompt = i < n_p
    br_c = _dot(jnp.where(is_prompt, ocp_ref[...], ocs_ref[...]), wgo_ref[...])
    br_d = _dot(jnp.where(is_prompt, odp_ref[...], ods_ref[...]), wmo_ref[...])

    x = x_ref[...]
    h = _modulated_norm(x, g1_ref[...], mod_ref[0:1, :], mod_ref[1:2, :])
    gates = jax.nn.sigmoid(_dot(h.astype(BF), wg_ref[...]))
    merged = (gates[:, 0:D_MODEL] * br_a + gates[:, D_MODEL:2 * D_MODEL] * br_b
              + gates[:, 2 * D_MODEL:3 * D_MODEL] * br_c + gates[:, 3 * D_MODEL:] * br_d)
    x = x + mod_ref[2:3, :] * _dot(merged.astype(BF), wo_ref[...])
    xo_ref[...] = x

    h2 = _modulated_norm(x, g2_ref[...], mod_ref[3:4, :], mod_ref[4:5, :])
    h2_ref[...] = h2

    logits = jnp.dot(h2, wr_ref[...], preferred_element_type=F32, precision=lax.Precision.HIGHEST) + br_ref[...]
    lane = lax.broadcasted_iota(jnp.int32, (TM, LANES), 1).astype(F32)
    neg = jnp.float32(-jnp.inf)
    lg = jnp.where(lane < N_EXPERTS, logits, neg)
    vals, idxs = [], []
    for _ in range(TOP_K):
        m = jnp.max(lg, axis=-1, keepdims=True)
        idx = jnp.min(jnp.where(lg == m, lane, float(LANES)), axis=-1, keepdims=True)
        vals.append(m)
        idxs.append(idx)
        lg = jnp.where(lane == idx, neg, lg)
    exps = [jnp.exp(v - vals[0]) for v in vals]
    denom = exps[0] + exps[1] + exps[2] + exps[3]
    ti = jnp.zeros((TM, LANES), F32)
    tw = jnp.zeros((TM, LANES), F32)
    for kk in range(TOP_K):
        ti = jnp.where(lane == kk, idxs[kk], ti)
        tw = jnp.where(lane == kk, exps[kk] / denom, tw)
    ti_ref[...] = ti.astype(jnp.int32)
    tw_ref[...] = tw


def _merge(l, x, mod, geo, a, b, attn, p):
    nt, n_p = geo["nt"], geo["np"]
    tok = nt * TM
    seq_row = geo["seq_row"]
    hb = TM // HALO
    last_hb = nt * hb - 1
    tile = lambda w: pl.BlockSpec((TM, w), lambda i: (i, 0))
    prev = lambda w: pl.BlockSpec((HALO, w), lambda i: (jnp.maximum(i * hb - 1, 0), 0))
    nxt = lambda w: pl.BlockSpec((HALO, w), lambda i: (jnp.minimum((i + 1) * hb, last_hb), 0))
    p_tile = pl.BlockSpec((TM, 512), lambda i: (jnp.minimum(i, n_p - 1), 0))
    s_tile = pl.BlockSpec((TM, 512), lambda i: (jnp.maximum(i - n_p, 0), 0))
    return pl.pallas_call(
        functools.partial(_merge_kernel, geo),
        grid=(nt,),
        in_specs=[
            tile(D_MODEL),
            pl.BlockSpec((None, None, 6, D_MODEL), lambda i: (l, seq_row(i), 0, 0)),
            tile(CONV_W), prev(CONV_W), nxt(CONV_W),
            tile(POOL_W), prev(POOL_W), nxt(POOL_W),
            p_tile, p_tile, s_tile, s_tile,
            _layer_spec((1, D_MODEL), l),
            _layer_spec((D_MODEL, N_GATE), l),
            _layer_spec((CONV_K, CONV_W), l),
            _layer_spec((1, CONV_W), l),
            _layer_spec((1, CONV_W), l),
            _layer_spec((1, CONV_W), l),
            _layer_spec((CONV_W, D_MODEL), l),
            _layer_spec((len(POOL_WINDOWS), POOL_G, POOL_G), l),
            _layer_spec((1, POOL_W), l),
            _layer_spec((POOL_W, D_MODEL), l),
            _layer_spec((512, D_MODEL), l),
            _layer_spec((512, D_MODEL), l),
            _layer_spec((D_MODEL, D_MODEL), l),
            _layer_spec((1, D_MODEL), l),
            _layer_spec((D_MODEL, LANES), l),
            _layer_spec((1, LANES), l),
        ],
        out_specs=[tile(D_MODEL), tile(D_MODEL), tile(LANES), tile(LANES)],
        out_shape=[jax.ShapeDtypeStruct((tok, D_MODEL), F32), jax.ShapeDtypeStruct((tok, D_MODEL), F32),
                   jax.ShapeDtypeStruct((tok, LANES), jnp.int32), jax.ShapeDtypeStruct((tok, LANES), F32)],
        scratch_shapes=[pltpu.VMEM((TM + 2 * HALO, CONV_W), F32), pltpu.VMEM((TM + 2 * HALO, POOL_W), F32)],
        compiler_params=pltpu.CompilerParams(dimension_semantics=("arbitrary",), vmem_limit_bytes=VMEM_LIMIT),
        name="merge",
    )(x, mod, a, a, a, b, b, b, *attn,
      p["norm1_g"], p["wgate"], p["conv_dw"], p["conv_dw_b"], p["conv_ln_g"], p["conv_ln_b"], p["w_conv_out"],
      p["pool_w"], p["pool_scale"], p["w_pool_out"], p["w_gqa_out"], p["w_mla_out"], p["w_o"],
      p["norm2_g"], p["w_router"], p["b_router"])


def _row_gather(src_hbm, idx_smem, slot, dst, sem, n_rows):
    unroll = 8

    def chunk(c, carry):
        for u in range(unroll):
            r = c * unroll + u
            pltpu.make_async_copy(src_hbm.at[pl.ds(idx_smem[slot, r], 1), :],
                                  dst.at[pl.ds(r, 1), :], sem).start()
        return carry

    lax.fori_loop(0, n_rows // unroll, chunk, 0)


def _expert_kernel(blk_e_ref, n_used_ref, rows_ref, x_hbm, wg_ref, wu_ref, bg_ref, bu_ref, wd_ref, bd_ref,
                   y_ref, xg, idx_smem, wd_bf, gsem, isem):
    i = pl.program_id(0)
    n_used = n_used_ref[0]
    slot = i % 2

    def start_gather(blk, s):
        cp = pltpu.make_async_copy(rows_ref.at[blk], idx_smem.at[s], isem)
        cp.start()
        cp.wait()
        _row_gather(x_hbm, idx_smem, s, xg.at[s], gsem.at[s], MOE_BM)

    @pl.when(i == 0)
    def _():
        start_gather(0, 0)

    @pl.when(i + 1 < n_used)
    def _():
        start_gather(i + 1, 1 - slot)

    @pl.when(i < n_used)
    def _():
        e_changed = jnp.logical_or(i == 0, blk_e_ref[i] != blk_e_ref[jnp.maximum(i - 1, 0)])

        @pl.when(e_changed)
        def _():
            wd_bf[...] = wd_ref[...].astype(BF)

        pltpu.make_async_copy(x_hbm.at[pl.ds(0, MOE_BM), :], xg.at[slot], gsem.at[slot]).wait()
        xb = xg[slot].astype(BF)
        gate = jnp.minimum(_dot(xb, wg_ref[...]) + bg_ref[...], SWIGLU_LIMIT)
        up = jnp.clip(_dot(xb, wu_ref[...]) + bu_ref[...], -SWIGLU_LIMIT, SWIGLU_LIMIT)
        glu = gate * jax.nn.sigmoid(gate * SWIGLU_ALPHA)
        y_ref[...] = _dot(((up + 1.0) * glu).astype(BF), wd_bf[...]) + bd_ref[...]

    @pl.when(i >= n_used)
    def _():
        y_ref[...] = jnp.zeros_like(y_ref)


def _experts(l, h2, blk_e, n_used, rows, p):
    n_blocks = rows.shape[0]
    w_spec = pl.BlockSpec((None, D_MODEL, D_FF), lambda i, be, nu: (be[i], 0, 0))
    b_spec = pl.BlockSpec((None, 1, D_FF), lambda i, be, nu: (be[i], 0, 0))
    grid_spec = pltpu.PrefetchScalarGridSpec(
        num_scalar_prefetch=2,
        grid=(n_blocks,),
        in_specs=[
            pl.BlockSpec((n_blocks, MOE_BM), lambda i, be, nu: (0, 0)),
            pl.BlockSpec(memory_space=pl.ANY),
            w_spec, w_spec, b_spec, b_spec,
            pl.BlockSpec((None, None, D_FF, D_MODEL), lambda i, be, nu: (l, be[i], 0, 0)),
            pl.BlockSpec((None, None, 1, D_MODEL), lambda i, be, nu: (l, be[i], 0, 0)),
        ],
        out_specs=pl.BlockSpec((MOE_BM, D_MODEL), lambda i, be, nu: (i, 0)),
        scratch_shapes=[
            pltpu.VMEM((2, MOE_BM, D_MODEL), F32),
            pltpu.SMEM((2, MOE_BM), jnp.int32),
            pltpu.VMEM((D_FF, D_MODEL), BF),
            pltpu.SemaphoreType.DMA((2,)),
            pltpu.SemaphoreType.DMA,
        ],
    )
    return pl.pallas_call(
        _expert_kernel,
        grid_spec=grid_spec,
        out_shape=jax.ShapeDtypeStruct((n_blocks * MOE_BM, D_MODEL), F32),
        compiler_params=pltpu.CompilerParams(dimension_semantics=("arbitrary",), vmem_limit_bytes=VMEM_LIMIT),
        name="experts",
    )(blk_e, n_used, rows, h2, p["w_gate"][l], p["w_up"][l], p["b_gate"][l], p["b_up"][l], p["w_dn"], p["b_dn"])


def _split_gate_up(w_gu_l, b_gu_l):
    gu = w_gu_l.reshape(N_EXPERTS, D_MODEL, D_FF, 2)
    bgu = b_gu_l.reshape(N_EXPERTS, 1, D_FF, 2)
    return gu[..., 0].astype(BF), gu[..., 1].astype(BF), bgu[..., 0], bgu[..., 1]


def _combine_kernel(final, nt, inv_ref, y_hbm, x_ref, mod_ref, tw_ref, fg_ref, o_ref, yg, idx_smem, gsem, isem):
    i = pl.program_id(0)
    slot = i % 2
    n_rows = TOP_K * TM

    def start_gather(t, s):
        cp = pltpu.make_async_copy(inv_ref.at[t], idx_smem.at[s], isem)
        cp.start()
        cp.wait()
        _row_gather(y_hbm, idx_smem, s, yg.at[s], gsem.at[s], n_rows)

    @pl.when(i == 0)
    def _():
        start_gather(0, 0)

    @pl.when(i + 1 < nt)
    def _():
        start_gather(i + 1, 1 - slot)

    pltpu.make_async_copy(y_hbm.at[pl.ds(0, n_rows), :], yg.at[slot], gsem.at[slot]).wait()
    tw = tw_ref[...]
    ffn = jnp.zeros((TM, D_MODEL), F32)
    for k in range(TOP_K):
        ffn = ffn + yg[slot, k * TM:(k + 1) * TM, :] * tw[:, k:k + 1]
    x = x_ref[...] + mod_ref[5:6, :] * ffn
    if final:
        x = _rms(x) * fg_ref[...]
    o_ref[...] = x


def _combine(l, final, y, inv, x, mod, tw, final_g, geo):
    nt = geo["nt"]
    seq_row = geo["seq_row"]
    tile = lambda w: pl.BlockSpec((TM, w), lambda i: (i, 0))
    return pl.pallas_call(
        functools.partial(_combine_kernel, final, nt),
        grid=(nt,),
        in_specs=[
            pl.BlockSpec((nt, TOP_K * TM), lambda i: (0, 0)),
            pl.BlockSpec(memory_space=pl.ANY),
            tile(D_MODEL),
            pl.BlockSpec((None, None, 6, D_MODEL), lambda i: (l, seq_row(i), 0, 0)),
            tile(LANES),
            _const_spec((1, D_MODEL)),
        ],
        out_specs=tile(D_MODEL),
        out_shape=jax.ShapeDtypeStruct((nt * TM, D_MODEL), F32),
        scratch_shapes=[
            pltpu.VMEM((2, TOP_K * TM, D_MODEL), F32),
            pltpu.SMEM((2, TOP_K * TM), jnp.int32),
            pltpu.SemaphoreType.DMA((2,)),
            pltpu.SemaphoreType.DMA,
        ],
        compiler_params=pltpu.CompilerParams(dimension_semantics=("arbitrary",), vmem_limit_bytes=VMEM_LIMIT),
        name="combine",
    )(inv, y, x, mod, tw, final_g)


def _dispatch_plan(top_idx, nt):
    tok = top_idx.shape[0]
    n_slot = tok * TOP_K
    e_flat = top_idx.reshape(-1)
    order = jnp.argsort(e_flat).astype(jnp.int32)
    e_sorted = e_flat[order]
    counts = jnp.bincount(e_flat, length=N_EXPERTS).astype(jnp.int32)
    starts = jnp.cumsum(counts) - counts
    padded = (counts + MOE_BM - 1) // MOE_BM * MOE_BM
    pad_ends = jnp.cumsum(padded)
    pad_starts = pad_ends - padded
    dest = pad_starts[e_sorted] + (jnp.arange(n_slot, dtype=jnp.int32) - starts[e_sorted])
    n_blocks = n_slot // MOE_BM + N_EXPERTS
    rows = jnp.zeros((n_blocks * MOE_BM,), jnp.int32).at[dest].set(order // TOP_K)
    inv = jnp.zeros((n_slot,), jnp.int32).at[order].set(dest)
    blk_e = jnp.minimum(jnp.searchsorted(pad_ends, jnp.arange(n_blocks, dtype=jnp.int32) * MOE_BM, side="right"),
                        N_EXPERTS - 1).astype(jnp.int32)
    n_used = (pad_ends[-1] // MOE_BM).astype(jnp.int32).reshape(1)
    inv = inv.reshape(nt, TM, TOP_K).transpose(0, 2, 1).reshape(nt, TOP_K * TM)
    return rows.reshape(n_blocks, MOE_BM), inv, blk_e, n_used


def _rope_tables(n_pos):
    pos = np.arange(n_pos)
    row, col = pos // GRID_W, pos % GRID_W
    lane = np.arange(LANES)

    def build(active, r, half):
        n_rot = 4 * half
        is_col = (r % n_rot) >= 2 * half
        rr = r % (2 * half)
        freq = np.power(ROPE_BASE, -(rr % half).astype(np.float64) / half)
        p = np.where(is_col[None, :], col[:, None], row[:, None]).astype(np.float64)
        ang = p * freq[None, :]
        first = rr < half
        cos = np.where(active[None, :], np.cos(ang), 1.0)
        sin_a = np.where((active & first)[None, :], -np.sin(ang), 0.0)
        sin_b = np.where((active & ~first)[None, :], np.sin(ang), 0.0)
        return [cos, sin_a, sin_b]

    tabs = build(np.ones(LANES, bool), lane % GQA_HEAD_DIM, GQA_HEAD_DIM // 4)
    in_rope = (lane >= MLA_NOPE) & (lane < MLA_NOPE + MLA_ROPE)
    tabs += build(in_rope, np.maximum(lane - MLA_NOPE, 0) % MLA_ROPE, MLA_ROPE // 4)
    tabs += build(lane < MLA_ROPE, lane % MLA_ROPE, MLA_ROPE // 4)
    table = np.concatenate(tabs, axis=1)
    ident = np.concatenate([np.ones((TM, LANES)), np.zeros((TM, LANES)), np.zeros((TM, LANES))] * 3, axis=1)
    return jnp.asarray(np.concatenate([ident, table], axis=0), F32)


def _placement():
    e = np.zeros((LANES, MLA_HEADS * LANES), np.float32)
    for h in range(MLA_HEADS):
        for r in range(MLA_ROPE):
            e[r, h * LANES + MLA_NOPE + r] = 1.0
    return jnp.asarray(e, BF)


def _block_diag_ones(n, g):
    idx = np.arange(n) // g
    return jnp.asarray((idx[:, None] == idx[None, :]).astype(np.float32), BF)


def kernel(x_prompt, x_sample, cache_gqa_k, cache_gqa_v, cache_mla_ckv, cache_mla_krope, c, c_ctx, norm1_g, norm2_g, w_mod, b_mod, w_in, conv_dw, conv_dw_b, conv_ln_g, conv_ln_b, w_conv_out, pool_w, pool_scale, w_pool_out, gqa_qn_g, gqa_kn_g, w_gqa_out, mla_qn_g, w_mla_q_up, mla_kvn_g, w_mla_kv_up, w_mla_out, w_o, w_router, b_router, w_gu, b_gu, w_dn, b_dn, final_g):
    bp, seq, d = x_prompt.shape
    bs, ls, _ = x_sample.shape
    depth = w_in.shape[0]
    past = cache_gqa_k.shape[2]
    assert seq == TM and d == D_MODEL and ls % TM == 0 and (bp * seq) % ls == 0
    tps = ls // TM
    n_p = bp
    nt = n_p + bs * tps
    geo = {
        "np": n_p, "bs": bs, "tps": tps, "nt": nt,
        "seq_row": lambda i: jnp.where(i < n_p, 0, 1 + (i - n_p) // tps),
        "rope_blk": lambda i: jnp.where(i < n_p, 0, 1 + (i - n_p) % tps),
    }

    n_cond = -(-(1 + bs) // 8) * 8
    cond = jnp.zeros((n_cond, d), F32).at[0].set(c_ctx).at[1:1 + bs].set(c)
    mod = _modulation(cond, w_mod, b_mod).reshape(depth, n_cond, 6, d)

    row = lambda v: v.reshape(depth, 1, -1)
    w1 = jnp.pad(w_in[:, :, :_SPLIT_GATE], ((0, 0), (0, 0), (0, W1_COLS - _SPLIT_GATE))).astype(BF)
    wqup = jnp.pad(w_mla_q_up.reshape(depth, MLA_Q_RANK, MLA_HEADS, MLA_NOPE + MLA_ROPE),
                   ((0, 0), (0, 0), (0, 0), (0, LANES - MLA_NOPE - MLA_ROPE)))
    wkv = w_mla_kv_up.reshape(depth, MLA_KV_RANK, MLA_HEADS, MLA_NOPE + MLA_V)
    wk_pad = jnp.pad(wkv[..., :MLA_NOPE], ((0, 0), (0, 0), (0, 0), (0, LANES - MLA_NOPE)))
    wkvup = jnp.concatenate([wk_pad.reshape(depth, MLA_KV_RANK, MLA_HEADS * LANES),
                             wkv[..., MLA_NOPE:].reshape(depth, MLA_KV_RANK, MLA_HEADS * MLA_V)], axis=-1)
    split = [_split_gate_up(w_gu[l], b_gu[l]) for l in range(depth)]
    p = {
        "norm1_g": row(norm1_g), "norm2_g": row(norm2_g),
        "w1": w1, "wgate": w_in[:, :, _SPLIT_GATE:].astype(BF),
        "rope": _rope_tables(ls), "place": _placement(), "ones_bd": _block_diag_ones(512, GQA_HEAD_DIM),
        "qn_g": row(jnp.tile(gqa_qn_g, (1, GQA_HEADS))), "kn_g": row(jnp.tile(gqa_kn_g, (1, GQA_KV_HEADS))),
        "cqn_g": row(mla_qn_g), "kvn_g": row(mla_kvn_g),
        "wqup": wqup.reshape(depth, MLA_Q_RANK, MLA_HEADS * LANES).astype(BF), "wkvup": wkvup.astype(BF),
        "conv_dw": conv_dw, "conv_dw_b": row(conv_dw_b), "conv_ln_g": row(conv_ln_g), "conv_ln_b": row(conv_ln_b),
        "w_conv_out": w_conv_out.astype(BF), "pool_w": pool_w.astype(BF), "pool_scale": row(pool_scale),
        "w_pool_out": w_pool_out.astype(BF), "w_gqa_out": w_gqa_out.astype(BF), "w_mla_out": w_mla_out.astype(BF),
        "w_o": w_o.astype(BF),
        "w_router": jnp.pad(w_router, ((0, 0), (0, 0), (0, LANES - N_EXPERTS))),
        "b_router": row(jnp.pad(b_router, ((0, 0), (0, LANES - N_EXPERTS)))),
        "w_gate": [t[0] for t in split], "w_up": [t[1] for t in split],
        "b_gate": [t[2] for t in split], "b_up": [t[3] for t in split],
        "w_dn": w_dn, "b_dn": b_dn.reshape(depth, N_EXPERTS, 1, d),
    }

    ckm, cvm = _cache_prep(cache_mla_ckv, jnp.pad(cache_mla_krope, ((0, 0), (0, 0), (0, 0), (0, LANES - MLA_ROPE))), p)
    cache = (cache_gqa_k.reshape(bs, depth, past, LANES).astype(BF),
             cache_gqa_v.reshape(bs, depth, past, LANES).astype(BF), ckm, cvm)

    x = jnp.concatenate([x_prompt.reshape(bp * seq, d), x_sample.reshape(bs * ls, d)], axis=0)
    n_ptok = bp * seq
    states = []
    for l in range(depth):
        a, b, q, k, v, qm, km, vm, ks, vs, ckvs, krs = _proj(l, x, mod, geo, p)
        states.append((ks[:n_ptok], vs[:n_ptok], ckvs[:n_ptok], krs[:n_ptok]))
        attn = _attention(l, geo, q, k, v, qm, km, vm, cache)
        x_mid, h2, top_i, top_w = _merge(l, x, mod, geo, a, b, attn, p)
        rows, inv, blk_e, n_used = _dispatch_plan(top_i[:, :TOP_K], nt)
        y = _experts(l, h2, blk_e, n_used, rows, p)
        x = _combine(l, l == depth - 1, y, inv, x_mid, mod, top_w, final_g.reshape(1, d), geo)

    y_prompt = x[:n_ptok].reshape(bp, seq, d)
    y_sample = x[n_ptok:].reshape(bs, ls, d)
    st = lambda j, shape: jnp.stack([s[j].reshape(shape) for s in states], axis=1)
    return (y_prompt, y_sample,
            st(0, (bp, seq, GQA_KV_HEADS, GQA_HEAD_DIM)), st(1, (bp, seq, GQA_KV_HEADS, GQA_HEAD_DIM)),
            st(2, (bp, seq, MLA_KV_RANK)), st(3, (bp, seq, MLA_ROPE)))
```

```python
import functools

import jax
import jax.numpy as jnp
import numpy as np
from jax import lax
from jax.experimental import pallas as pl
from jax.experimental.pallas import tpu as pltpu

D_MODEL = 1024
GRID_W = 64
CONV_W = 512
CONV_K = 31
POOL_W = 512
POOL_WINDOWS = (2, 4, 8, 16)
POOL_G = 128
GQA_HEADS = 8
GQA_KV_HEADS = 2
GQA_HEAD_DIM = 64
MLA_HEADS = 8
MLA_Q_RANK = 384
MLA_KV_RANK = 256
MLA_NOPE = 64
MLA_ROPE = 32
MLA_V = 64
ROPE_BASE = 10000.0
N_EXPERTS = 32
TOP_K = 4
D_FF = 1024
SWIGLU_LIMIT = 7.0
SWIGLU_ALPHA = 1.702
EPS = 1e-6
GQA_SCALE = GQA_HEAD_DIM ** -0.5
MLA_SCALE = (MLA_NOPE + MLA_ROPE) ** -0.5

LANES = 128
TM = 256
HALO = 16
MOE_BM = 256
R_TILE = 1280
W1_COLS = 3072
N_GATE = 4 * D_MODEL
VMEM_LIMIT = 56 * 1024 * 1024

BF = jnp.bfloat16
F32 = jnp.float32

_C_A, _C_B, _C_Q, _C_K, _C_V, _C_CQ, _C_CKV, _C_KR = 0, 1024, 1536, 2048, 2176, 2304, 2688, 2944
_SPLIT_GATE = 2976


def _dot(a, b):
    return jnp.dot(a, b, preferred_element_type=F32)


def _dot_nt(a, b):
    return lax.dot_general(a, b, (((1,), (1,)), ((), ())), preferred_element_type=F32)


def _rms(x):
    return x * lax.rsqrt(jnp.mean(x * x, axis=-1, keepdims=True) + EPS)


def _group_mean_sq(x, ones_bd, width):
    xx = x * x
    hi = xx.astype(BF)
    lo = (xx - hi.astype(F32)).astype(BF)
    return (_dot(hi, ones_bd) + _dot(lo, ones_bd)) * (1.0 / width)


def _tile_lanes(t, width):
    reps = width // LANES
    return t if reps == 1 else jnp.concatenate([t] * reps, axis=-1)


def _rope(x, cos, sin_a, sin_b, shift):
    w = x.shape[-1]
    return (x * _tile_lanes(cos, w) + pltpu.roll(x, w - shift, 1) * _tile_lanes(sin_a, w)
            + pltpu.roll(x, shift, 1) * _tile_lanes(sin_b, w))


def _modulated_norm(x, g, shift, scale):
    return _rms(x) * g * (1.0 + scale) + shift


def _mod_kernel(cond_ref, w_ref, b_ref, o_ref):
    c = cond_ref[...]
    s = (c * jax.nn.sigmoid(c)).astype(BF)
    o_ref[...] = _dot(s, w_ref[...].astype(BF)) + b_ref[...]


def _modulation(cond, w_mod, b_mod):
    depth, d, n = w_mod.shape
    rows = cond.shape[0]
    return pl.pallas_call(
        _mod_kernel,
        grid=(depth, n // D_MODEL),
        in_specs=[
            pl.BlockSpec((rows, d), lambda l, j: (0, 0)),
            pl.BlockSpec((None, d, D_MODEL), lambda l, j: (l, 0, j)),
            pl.BlockSpec((None, 1, D_MODEL), lambda l, j: (l, 0, j)),
        ],
        out_specs=pl.BlockSpec((None, rows, D_MODEL), lambda l, j: (l, 0, j)),
        out_shape=jax.ShapeDtypeStruct((depth, rows, n), F32),
        name="modulation",
    )(cond, w_mod, b_mod.reshape(depth, 1, n))


def _proj_kernel(x_ref, mod_ref, g1_ref, w1_ref, rope_ref, qn_ref, kn_ref, cqn_ref, kvn_ref,
                 wqup_ref, wkvup_ref, place_ref, ones_ref,
                 a_ref, b_ref, q_ref, k_ref, v_ref, qm_ref, km_ref, vm_ref,
                 ks_ref, vs_ref, ckvs_ref, krs_ref):
    x = x_ref[...]
    h = _modulated_norm(x, g1_ref[...], mod_ref[0:1, :], mod_ref[1:2, :])
    y = _dot(h.astype(BF), w1_ref[...])

    a_ref[...] = (y[:, _C_A:_C_A + CONV_W] * jax.nn.sigmoid(y[:, _C_A + CONV_W:_C_B])).astype(BF)
    b_ref[...] = y[:, _C_B:_C_Q].astype(BF)

    def tab(j):
        return rope_ref[:, j * LANES:(j + 1) * LANES]

    q = y[:, _C_Q:_C_K]
    q = q * lax.rsqrt(_group_mean_sq(q, ones_ref[...], GQA_HEAD_DIM) + EPS) * qn_ref[...]
    q = _rope(q, tab(0), tab(1), tab(2), GQA_HEAD_DIM // 4)
    q_ref[...] = (q * GQA_SCALE).astype(BF)

    k = y[:, _C_K:_C_V]
    k = k * lax.rsqrt(_group_mean_sq(k, ones_ref[0:LANES, 0:LANES], GQA_HEAD_DIM) + EPS) * kn_ref[...]
    ks_ref[...] = k
    k_ref[...] = _rope(k, tab(0), tab(1), tab(2), GQA_HEAD_DIM // 4).astype(BF)

    v = y[:, _C_V:_C_CQ]
    vs_ref[...] = v
    v_ref[...] = v.astype(BF)

    cq = _rms(y[:, _C_CQ:_C_CKV]) * cqn_ref[...]
    qm = _dot(cq.astype(BF), wqup_ref[...])
    qm = _rope(qm, tab(3), tab(4), tab(5), MLA_ROPE // 4)
    qm_ref[...] = (qm * MLA_SCALE).astype(BF)

    ckv = _rms(y[:, _C_CKV:_C_KR]) * kvn_ref[...]
    ckvs_ref[...] = ckv
    kv = _dot(ckv.astype(BF), wkvup_ref[...])
    kr = y[:, _C_KR:W1_COLS]
    krs_ref[...] = kr[:, 0:MLA_ROPE]
    kr_rot = _rope(kr, tab(6), tab(7), tab(8), MLA_ROPE // 4)
    km = kv[:, 0:MLA_HEADS * LANES] + _dot(kr_rot.astype(BF), place_ref[...])
    km_ref[...] = km.astype(BF)
    vm_ref[...] = kv[:, MLA_HEADS * LANES:].astype(BF)


def _const_spec(shape):
    nd = len(shape)
    return pl.BlockSpec(shape, lambda *_: (0,) * nd)


def _layer_spec(shape, l):
    nd = len(shape)
    return pl.BlockSpec((None,) + shape, lambda *_: (l,) + (0,) * nd)


def _proj(l, x, mod, geo, p):
    nt = geo["nt"]
    tok = nt * TM
    seq_row, rope_blk = geo["seq_row"], geo["rope_blk"]

    def tile(width, dtype):
        return pl.BlockSpec((TM, width), lambda i: (i, 0)), jax.ShapeDtypeStruct((tok, width), dtype)

    outs = [tile(CONV_W, BF), tile(POOL_W, BF), tile(512, BF), tile(LANES, BF), tile(LANES, BF),
            tile(MLA_HEADS * LANES, BF), tile(MLA_HEADS * LANES, BF), tile(MLA_HEADS * MLA_V, BF),
            tile(LANES, F32), tile(LANES, F32), tile(MLA_KV_RANK, F32), tile(MLA_ROPE, F32)]
    return pl.pallas_call(
        _proj_kernel,
        grid=(nt,),
        in_specs=[
            pl.BlockSpec((TM, D_MODEL), lambda i: (i, 0)),
            pl.BlockSpec((None, None, 6, D_MODEL), lambda i: (l, seq_row(i), 0, 0)),
            _layer_spec((1, D_MODEL), l),
            _layer_spec((D_MODEL, W1_COLS), l),
            pl.BlockSpec((TM, 9 * LANES), lambda i: (rope_blk(i), 0)),
            _layer_spec((1, 512), l),
            _layer_spec((1, LANES), l),
            _layer_spec((1, MLA_Q_RANK), l),
            _layer_spec((1, MLA_KV_RANK), l),
            _layer_spec((MLA_Q_RANK, MLA_HEADS * LANES), l),
            _layer_spec((MLA_KV_RANK, MLA_HEADS * (LANES + MLA_V)), l),
            _const_spec((LANES, MLA_HEADS * LANES)),
            _const_spec((512, 512)),
        ],
        out_specs=[o[0] for o in outs],
        out_shape=[o[1] for o in outs],
        compiler_params=pltpu.CompilerParams(dimension_semantics=("arbitrary",), vmem_limit_bytes=VMEM_LIMIT),
        name="proj",
    )(x, mod, p["norm1_g"], p["w1"], p["rope"], p["qn_g"], p["kn_g"], p["cqn_g"], p["kvn_g"],
      p["wqup"], p["wkvup"], p["place"], p["ones_bd"])


def _cache_kernel(ckv_ref, kr_ref, wkvup_ref, place_ref, km_ref, vm_ref):
    kv = _dot(ckv_ref[...].astype(BF), wkvup_ref[...])
    km = kv[:, 0:MLA_HEADS * LANES] + _dot(kr_ref[...].astype(BF), place_ref[...])
    km_ref[...] = km.astype(BF)
    vm_ref[...] = kv[:, MLA_HEADS * LANES:].astype(BF)


def _cache_prep(ckv, kr_pad, p):
    bs, depth, past, _ = ckv.shape
    return pl.pallas_call(
        _cache_kernel,
        grid=(bs, depth),
        in_specs=[
            pl.BlockSpec((None, None, past, MLA_KV_RANK), lambda b, l: (b, l, 0, 0)),
            pl.BlockSpec((None, None, past, LANES), lambda b, l: (b, l, 0, 0)),
            pl.BlockSpec((None, MLA_KV_RANK, MLA_HEADS * (LANES + MLA_V)), lambda b, l: (l, 0, 0)),
            pl.BlockSpec((LANES, MLA_HEADS * LANES), lambda b, l: (0, 0)),
        ],
        out_specs=[
            pl.BlockSpec((None, None, past, MLA_HEADS * LANES), lambda b, l: (b, l, 0, 0)),
            pl.BlockSpec((None, None, past, MLA_HEADS * MLA_V), lambda b, l: (b, l, 0, 0)),
        ],
        out_shape=[jax.ShapeDtypeStruct((bs, depth, past, MLA_HEADS * LANES), BF),
                   jax.ShapeDtypeStruct((bs, depth, past, MLA_HEADS * MLA_V), BF)],
        name="cache_prep",
    )(ckv, kr_pad, p["wkvup"], p["place"])


def _softmax_pv(s, v):
    m = jnp.max(s, axis=-1, keepdims=True)
    e = jnp.exp(s - m)
    l = jnp.sum(e, axis=-1, keepdims=True)
    return _dot(e.astype(BF), v) / l


def _attention_body(q, k, v, qm, km, vm, oc_ref, od_ref):
    lane_k = lax.broadcasted_iota(jnp.int32, k.shape, 1)
    lane_q = lax.broadcasted_iota(jnp.int32, (TM, LANES), 1)
    lo_k = lane_k < GQA_HEAD_DIM
    lo_q = lane_q < GQA_HEAD_DIM
    k32, v32 = k.astype(F32), v.astype(F32)
    k_sw = pltpu.roll(k32, GQA_HEAD_DIM, 1)
    v_sw = pltpu.roll(v32, GQA_HEAD_DIM, 1)
    k_dup = [jnp.where(lo_k, k32, k_sw).astype(BF), jnp.where(lo_k, k_sw, k32).astype(BF)]
    v_dup = [jnp.where(lo_k, v32, v_sw).astype(BF), jnp.where(lo_k, v_sw, v32).astype(BF)]
    zero = jnp.zeros((TM, LANES), BF)
    group = GQA_HEADS // GQA_KV_HEADS
    for j in range(GQA_HEADS // 2):
        qs = q[:, j * LANES:(j + 1) * LANES]
        g = (2 * j) // group
        o_lo = _softmax_pv(_dot_nt(jnp.where(lo_q, qs, zero), k_dup[g]), v_dup[g])
        o_hi = _softmax_pv(_dot_nt(jnp.where(lo_q, zero, qs), k_dup[g]), v_dup[g])
        oc_ref[:, j * LANES:(j + 1) * LANES] = jnp.where(lo_q, o_lo, o_hi).astype(BF)
    for j in range(MLA_HEADS // 2):
        vs = vm[:, j * LANES:(j + 1) * LANES]
        outs = []
        for h in (2 * j, 2 * j + 1):
            s = _dot_nt(qm[:, h * LANES:(h + 1) * LANES], km[:, h * LANES:(h + 1) * LANES])
            outs.append(_softmax_pv(s, vs))
        od_ref[:, j * LANES:(j + 1) * LANES] = jnp.where(lo_q, outs[0], outs[1]).astype(BF)


def _attn_prompt_kernel(q_ref, k_ref, v_ref, qm_ref, km_ref, vm_ref, oc_ref, od_ref):
    _attention_body(q_ref[...], k_ref[...], v_ref[...], qm_ref[...], km_ref[...], vm_ref[...], oc_ref, od_ref)


def _attn_sample_kernel(q_ref, k_ref, v_ref, qm_ref, km_ref, vm_ref, ck_ref, cv_ref, ckm_ref, cvm_ref,
                        oc_ref, od_ref):
    k = jnp.concatenate([ck_ref[...], k_ref[...]], axis=0)
    v = jnp.concatenate([cv_ref[...], v_ref[...]], axis=0)
    km = jnp.concatenate([ckm_ref[...], km_ref[...]], axis=0)
    vm = jnp.concatenate([cvm_ref[...], vm_ref[...]], axis=0)
    _attention_body(q_ref[...], k, v, qm_ref[...], km, vm, oc_ref, od_ref)


def _attention(l, geo, q, k, v, qm, km, vm, cache):
    n_p, bs, tps, nt = geo["np"], geo["bs"], geo["tps"], geo["nt"]
    tok = nt * TM
    ls = tps * TM
    off = (n_p * TM) // ls
    widths = (512, LANES, LANES, MLA_HEADS * LANES, MLA_HEADS * LANES, MLA_HEADS * MLA_V)
    params = pltpu.CompilerParams(dimension_semantics=("arbitrary",), vmem_limit_bytes=VMEM_LIMIT)
    oc_p, od_p = pl.pallas_call(
        _attn_prompt_kernel,
        grid=(n_p,),
        in_specs=[pl.BlockSpec((TM, w), lambda i: (i, 0)) for w in widths],
        out_specs=[pl.BlockSpec((TM, 512), lambda i: (i, 0))] * 2,
        out_shape=[jax.ShapeDtypeStruct((n_p * TM, 512), BF)] * 2,
        compiler_params=params,
        name="attn_prompt",
    )(q, k, v, qm, km, vm)
    ck, cv, ckm, cvm = cache
    past = ck.shape[2]
    q_spec = lambda w: pl.BlockSpec((TM, w), lambda b, j: (n_p + b * tps + j, 0))
    kv_spec = lambda w: pl.BlockSpec((ls, w), lambda b, j: (off + b, 0))
    c_spec = lambda w: pl.BlockSpec((None, None, past, w), lambda b, j: (b, l, 0, 0))
    params2 = pltpu.CompilerParams(dimension_semantics=("arbitrary", "arbitrary"), vmem_limit_bytes=VMEM_LIMIT)
    o_spec = pl.BlockSpec((TM, 512), lambda b, j: (b * tps + j, 0))
    oc_s, od_s = pl.pallas_call(
        _attn_sample_kernel,
        grid=(bs, tps),
        in_specs=[q_spec(512), kv_spec(LANES), kv_spec(LANES), q_spec(MLA_HEADS * LANES),
                  kv_spec(MLA_HEADS * LANES), kv_spec(MLA_HEADS * MLA_V),
                  c_spec(LANES), c_spec(LANES), c_spec(MLA_HEADS * LANES), c_spec(MLA_HEADS * MLA_V)],
        out_specs=[o_spec, o_spec],
        out_shape=[jax.ShapeDtypeStruct((bs * ls, 512), BF)] * 2,
        compiler_params=params2,
        name="attn_sample",
    )(q, k, v, qm, km, vm, ck, cv, ckm, cvm)
    return oc_p, od_p, oc_s, od_s


def _merge_kernel(geo, x_ref, mod_ref, a_ref, ap_ref, an_ref, b_ref, bp_ref, bn_ref,
                  ocp_ref, odp_ref, ocs_ref, ods_ref,
                  g1_ref, wg_ref, cw_ref, cb_ref, lng_ref, lnb_ref, wco_ref, pw_ref, ps_ref, wpo_ref,
                  wgo_ref, wmo_ref, wo_ref, g2_ref, wr_ref, br_ref,
                  ltri_ref, upper_ref,
                  xo_ref, xs_ref, rl_ref, tw_ref, meta_ref, abuf, bbuf):
    n_p, tps = geo["np"], geo["tps"]
    i = pl.program_id(0)
    j = jnp.where(i < n_p, 0, (i - n_p) % tps)
    n_seq_tiles = jnp.where(i < n_p, 1, tps)
    has_prev = j > 0
    has_next = j < n_seq_tiles - 1

    def fill(buf, cur, prev, nxt):
        buf[0:HALO, :] = jnp.where(has_prev, prev[...].astype(F32), 0.0)
        buf[HALO:HALO + TM, :] = cur[...].astype(F32)
        buf[HALO + TM:, :] = jnp.where(has_next, nxt[...].astype(F32), 0.0)

    fill(abuf, a_ref, ap_ref, an_ref)
    fill(bbuf, b_ref, bp_ref, bn_ref)

    rows = 32
    conv = []
    for r0 in range(0, TM, rows):
        acc = jnp.zeros((rows, CONV_W), F32)
        for t in range(CONV_K):
            s = r0 + t + HALO - CONV_K // 2
            acc = acc + abuf[s:s + rows, :] * cw_ref[t:t + 1, :]
        conv.append(acc)
    ca = jnp.concatenate(conv, axis=0) + cb_ref[...]
    mu = jnp.mean(ca, axis=-1, keepdims=True)
    xc = ca - mu
    ln = xc * lax.rsqrt(jnp.mean(xc * xc, axis=-1, keepdims=True) + EPS) * lng_ref[...] + lnb_ref[...]
    br_a = _dot((ln * jax.nn.sigmoid(ln)).astype(BF), wco_ref[...])

    pos = j * TM + lax.broadcasted_iota(jnp.int32, (TM, 1), 0)
    seq_len = n_seq_tiles * TM
    mixed = []
    for g, w in enumerate(POOL_WINDOWS):
        lo, hi = g * POOL_G, (g + 1) * POOL_G
        acc = jnp.zeros((TM, POOL_G), F32)
        for o in range(-(w // 2), w - w // 2):
            acc = acc + bbuf[HALO + o:HALO + o + TM, lo:hi]
        cnt = jnp.clip(pos - w // 2 + w, 0, seq_len) - jnp.clip(pos - w // 2, 0, seq_len)
        pooled = acc / cnt.astype(F32) - bbuf[HALO:HALO + TM, lo:hi]
        mixed.append(_dot(pooled.astype(BF), pw_ref[g]))
    pb = jnp.concatenate(mixed, axis=-1) * ps_ref[...]
    br_b = _dot(pb.astype(BF), wpo_ref[...])

    is_prompt = i < n_p
    br_c = _dot(jnp.where(is_prompt, ocp_ref[...], ocs_ref[...]), wgo_ref[...])
    br_d = _dot(jnp.where(is_prompt, odp_ref[...], ods_ref[...]), wmo_ref[...])

    x = x_ref[...]
    h = _modulated_norm(x, g1_ref[...], mod_ref[0:1, :], mod_ref[1:2, :])
    gates = jax.nn.sigmoid(_dot(h.astype(BF), wg_ref[...]))
    merged = (gates[:, 0:D_MODEL] * br_a + gates[:, D_MODEL:2 * D_MODEL] * br_b
              + gates[:, 2 * D_MODEL:3 * D_MODEL] * br_c + gates[:, 3 * D_MODEL:] * br_d)
    x = x + mod_ref[2:3, :] * _dot(merged.astype(BF), wo_ref[...])
    xo_ref[...] = x

    h2 = _modulated_norm(x, g2_ref[...], mod_ref[3:4, :], mod_ref[4:5, :])

    logits = jnp.dot(h2, wr_ref[...], preferred_element_type=F32, precision=lax.Precision.HIGHEST) + br_ref[...]
    lane = lax.broadcasted_iota(jnp.int32, (TM, LANES), 1).astype(F32)
    neg = jnp.float32(-jnp.inf)
    lg = jnp.where(lane < N_EXPERTS, logits, neg)
    vals, idxs = [], []
    for _ in range(TOP_K):
        m = jnp.max(lg, axis=-1, keepdims=True)
        idx = jnp.min(jnp.where(lg == m, lane, float(LANES)), axis=-1, keepdims=True)
        vals.append(m)
        idxs.append(idx)
        lg = jnp.where(lane == idx, neg, lg)
    exps = [jnp.exp(v - vals[0]) for v in vals]
    denom = exps[0] + exps[1] + exps[2] + exps[3]

    onehot = [(lane == idxs[kk]).astype(F32) for kk in range(TOP_K)]
    colsum = [jnp.sum(o, axis=0, keepdims=True) for o in onehot]
    cnt = colsum[0] + colsum[1] + colsum[2] + colsum[3]
    pad8 = jnp.floor((cnt + 7.0) * 0.125) * 8.0
    run_off = jnp.dot(jnp.broadcast_to(pad8, (8, LANES)), upper_ref[...], preferred_element_type=F32,
                      precision=lax.Precision.HIGHEST)[0:1, :]
    base = jnp.zeros((1, LANES), F32)
    rloc = []
    for kk in range(TOP_K):
        before = _dot(ltri_ref[...], onehot[kk].astype(BF))
        rloc.append(jnp.sum(onehot[kk] * (run_off + base + before), axis=-1, keepdims=True))
        base = base + colsum[kk]

    eye = (lax.broadcasted_iota(jnp.int32, (TM, TM), 0) == lax.broadcasted_iota(jnp.int32, (TM, TM), 1))
    r_sub = lax.broadcasted_iota(jnp.int32, (R_TILE, TM), 0).astype(F32)
    ones8 = jnp.ones((8, TM), F32)
    sel = jnp.zeros((R_TILE, TM), F32)
    for kk in range(TOP_K):
        row = jnp.dot(ones8, jnp.where(eye, rloc[kk], 0.0), preferred_element_type=F32,
                      precision=lax.Precision.HIGHEST)[0:1, :]
        sel = jnp.where(r_sub == row, 1.0, sel)
    xs_ref[...] = _dot(sel.astype(BF), h2.astype(BF))

    rl = jnp.zeros((TM, LANES), F32)
    tw = jnp.zeros((TM, LANES), F32)
    for kk in range(TOP_K):
        rl = jnp.where(lane == kk, rloc[kk], rl)
        tw = jnp.where(lane == kk, exps[kk] / denom, tw)
    rl_ref[...] = rl.astype(jnp.int32)
    tw_ref[...] = tw
    sub = lax.broadcasted_iota(jnp.int32, (8, LANES), 0)
    meta = jnp.where(sub == 0, jnp.broadcast_to(pad8, (8, LANES)),
                     jnp.where(sub == 1, jnp.broadcast_to(run_off, (8, LANES)), 0.0))
    meta_ref[...] = meta.astype(jnp.int32)


def _merge(l, x, mod, geo, a, b, attn, p):
    nt, n_p = geo["nt"], geo["np"]
    tok = nt * TM
    seq_row = geo["seq_row"]
    hb = TM // HALO
    last_hb = nt * hb - 1
    tile = lambda w: pl.BlockSpec((TM, w), lambda i: (i, 0))
    prev = lambda w: pl.BlockSpec((HALO, w), lambda i: (jnp.maximum(i * hb - 1, 0), 0))
    nxt = lambda w: pl.BlockSpec((HALO, w), lambda i: (jnp.minimum((i + 1) * hb, last_hb), 0))
    p_tile = pl.BlockSpec((TM, 512), lambda i: (jnp.minimum(i, n_p - 1), 0))
    s_tile = pl.BlockSpec((TM, 512), lambda i: (jnp.maximum(i - n_p, 0), 0))
    return pl.pallas_call(
        functools.partial(_merge_kernel, geo),
        grid=(nt,),
        in_specs=[
            tile(D_MODEL),
            pl.BlockSpec((None, None, 6, D_MODEL), lambda i: (l, seq_row(i), 0, 0)),
            tile(CONV_W), prev(CONV_W), nxt(CONV_W),
            tile(POOL_W), prev(POOL_W), nxt(POOL_W),
            p_tile, p_tile, s_tile, s_tile,
            _layer_spec((1, D_MODEL), l),
            _layer_spec((D_MODEL, N_GATE), l),
            _layer_spec((CONV_K, CONV_W), l),
            _layer_spec((1, CONV_W), l),
            _layer_spec((1, CONV_W), l),
            _layer_spec((1, CONV_W), l),
            _layer_spec((CONV_W, D_MODEL), l),
            _layer_spec((len(POOL_WINDOWS), POOL_G, POOL_G), l),
            _layer_spec((1, POOL_W), l),
            _layer_spec((POOL_W, D_MODEL), l),
            _layer_spec((512, D_MODEL), l),
            _layer_spec((512, D_MODEL), l),
            _layer_spec((D_MODEL, D_MODEL), l),
            _layer_spec((1, D_MODEL), l),
            _layer_spec((D_MODEL, LANES), l),
            _layer_spec((1, LANES), l),
            _const_spec((TM, TM)),
            _const_spec((LANES, LANES)),
        ],
        out_specs=[tile(D_MODEL), pl.BlockSpec((R_TILE, D_MODEL), lambda i: (i, 0)), tile(LANES), tile(LANES),
                   pl.BlockSpec((None, 8, LANES), lambda i: (i, 0, 0))],
        out_shape=[jax.ShapeDtypeStruct((tok, D_MODEL), F32), jax.ShapeDtypeStruct((nt * R_TILE, D_MODEL), F32),
                   jax.ShapeDtypeStruct((tok, LANES), jnp.int32), jax.ShapeDtypeStruct((tok, LANES), F32),
                   jax.ShapeDtypeStruct((nt, 8, LANES), jnp.int32)],
        scratch_shapes=[pltpu.VMEM((TM + 2 * HALO, CONV_W), F32), pltpu.VMEM((TM + 2 * HALO, POOL_W), F32)],
        compiler_params=pltpu.CompilerParams(dimension_semantics=("arbitrary",), vmem_limit_bytes=VMEM_LIMIT),
        name="merge",
    )(x, mod, a, a, a, b, b, b, *attn,
      p["norm1_g"], p["wgate"], p["conv_dw"], p["conv_dw_b"], p["conv_ln_g"], p["conv_ln_b"], p["w_conv_out"],
      p["pool_w"], p["pool_scale"], p["w_pool_out"], p["w_gqa_out"], p["w_mla_out"], p["w_o"],
      p["norm2_g"], p["w_router"], p["b_router"], p["ltri"], p["upper"])


def _expert_kernel(nt, blk_e, blk_row0, blk_t0, blk_t1, n_used_ref, pad8_ref, dst_ref, off_ref, tot8_ref,
                   xs_hbm, wg_ref, wu_ref, bg_ref, bu_ref, wd_ref, bd_ref,
                   y_hbm, xg, yb, zbuf, wd_bf, cnt_smem, gsem, osem, zsem):
    i = pl.program_id(0)
    n_used = n_used_ref[0]
    slot = i % 2
    last = pl.num_programs(0) - 1

    def for_pieces(b, fn):
        e = blk_e[b]
        b0 = blk_row0[b]

        def body(t, tot):
            j = t * N_EXPERTS + e
            run0 = dst_ref[j]
            lo = jnp.maximum(run0, b0)
            hi = jnp.minimum(run0 + pad8_ref[j], b0 + MOE_BM)
            n = pl.multiple_of(jnp.maximum(hi - lo, 0), 8)

            @pl.when(n > 0)
            def _():
                fn(pl.multiple_of(t * R_TILE + off_ref[j] + lo - run0, 8), pl.multiple_of(lo - b0, 8), n)

            return tot + n

        return lax.fori_loop(blk_t0[b], blk_t1[b], body, jnp.int32(0))

    def rows_copy(src, dst, sem, n):
        return pltpu.make_async_copy(src.at[pl.ds(0, n), :], dst.at[pl.ds(0, n), :], sem)

    def start_gather(b, s):
        def piece(row_t, row_b, n):
            pltpu.make_async_copy(xs_hbm.at[pl.ds(row_t, n), :], xg.at[s].at[pl.ds(row_b, n), :], gsem.at[s]).start()

        cnt_smem[s] = for_pieces(b, piece)

    def tail_copy(t):
        n = pl.multiple_of(R_TILE - tot8_ref[t], 8)
        return n, pltpu.make_async_copy(
            zbuf.at[pl.ds(0, n), :], y_hbm.at[pl.ds(pl.multiple_of(t * R_TILE + tot8_ref[t], 8), n), :], zsem)

    @pl.when(i == 0)
    def _():
        xg[...] = jnp.zeros_like(xg)
        zbuf[...] = jnp.zeros_like(zbuf)
        for s in range(4):
            cnt_smem[s] = 0

        def fill(t, c):
            n, cp = tail_copy(t)

            @pl.when(n > 0)
            def _():
                cp.start()

            return c

        lax.fori_loop(0, nt, fill, 0)

        @pl.when(n_used > 0)
        def _():
            start_gather(0, 0)

    @pl.when(i + 1 < n_used)
    def _():
        start_gather(i + 1, 1 - slot)

    @pl.when(i < n_used)
    def _():
        e_changed = jnp.logical_or(i == 0, blk_e[i] != blk_e[jnp.maximum(i - 1, 0)])

        @pl.when(e_changed)
        def _():
            wd_bf[...] = wd_ref[...].astype(BF)

        n_in = pl.multiple_of(cnt_smem[slot], 8)
        rows_copy(xs_hbm, xg.at[slot], gsem.at[slot], n_in).wait()
        xb = xg[slot].astype(BF)
        gate = jnp.minimum(_dot(xb, wg_ref[...]) + bg_ref[...], SWIGLU_LIMIT)
        up = jnp.clip(_dot(xb, wu_ref[...]) + bu_ref[...], -SWIGLU_LIMIT, SWIGLU_LIMIT)
        glu = gate * jax.nn.sigmoid(gate * SWIGLU_ALPHA)
        y = _dot(((up + 1.0) * glu).astype(BF), wd_bf[...]) + bd_ref[...]

        n_prev = pl.multiple_of(cnt_smem[2 + slot], 8)

        @pl.when(n_prev > 0)
        def _():
            rows_copy(yb.at[slot], y_hbm, osem.at[slot], n_prev).wait()

        yb[slot] = y.astype(BF).astype(F32)

        def piece(row_t, row_b, n):
            pltpu.make_async_copy(yb.at[slot].at[pl.ds(row_b, n), :], y_hbm.at[pl.ds(row_t, n), :],
                                  osem.at[slot]).start()

        cnt_smem[2 + slot] = for_pieces(i, piece)

    @pl.when(i == last)
    def _():
        for s in range(2):
            n_out = pl.multiple_of(cnt_smem[2 + s], 8)

            @pl.when(n_out > 0)
            def _():
                rows_copy(yb.at[s], y_hbm, osem.at[s], n_out).wait()

        def drain(t, c):
            n, cp = tail_copy(t)

            @pl.when(n > 0)
            def _():
                cp.wait()

            return c

        lax.fori_loop(0, nt, drain, 0)


def _experts(l, xs, plan, p, nt):
    n_blocks = plan["blk_e"].shape[0]
    n_pref = 9
    by_expert = lambda *lead: (lambda i, be, *_: lead + (be[i], 0, 0))
    w_spec = pl.BlockSpec((None, D_MODEL, D_FF), by_expert())
    b_spec = pl.BlockSpec((None, 1, D_FF), by_expert())
    grid_spec = pltpu.PrefetchScalarGridSpec(
        num_scalar_prefetch=n_pref,
        grid=(n_blocks,),
        in_specs=[
            pl.BlockSpec(memory_space=pl.ANY),
            w_spec, w_spec, b_spec, b_spec,
            pl.BlockSpec((None, None, D_FF, D_MODEL), by_expert(l)),
            pl.BlockSpec((None, None, 1, D_MODEL), by_expert(l)),
        ],
        out_specs=pl.BlockSpec(memory_space=pl.ANY),
        scratch_shapes=[
            pltpu.VMEM((2, MOE_BM, D_MODEL), F32),
            pltpu.VMEM((2, MOE_BM, D_MODEL), F32),
            pltpu.VMEM((R_TILE - TOP_K * TM, D_MODEL), F32),
            pltpu.VMEM((D_FF, D_MODEL), BF),
            pltpu.SMEM((4,), jnp.int32),
            pltpu.SemaphoreType.DMA((2,)),
            pltpu.SemaphoreType.DMA((2,)),
            pltpu.SemaphoreType.DMA,
        ],
    )
    return pl.pallas_call(
        functools.partial(_expert_kernel, nt),
        grid_spec=grid_spec,
        out_shape=jax.ShapeDtypeStruct((nt * R_TILE, D_MODEL), F32),
        compiler_params=pltpu.CompilerParams(dimension_semantics=("arbitrary",), vmem_limit_bytes=VMEM_LIMIT),
        name="experts",
    )(plan["blk_e"], plan["blk_row0"], plan["blk_t0"], plan["blk_t1"], plan["n_used"], plan["pad8"], plan["dst"],
      plan["off"], plan["tot8"],
      xs, p["w_gate"][l], p["w_up"][l], p["b_gate"][l], p["b_up"][l], p["w_dn"], p["b_dn"])


def _split_gate_up(w_gu_l, b_gu_l):
    gu = w_gu_l.reshape(N_EXPERTS, D_MODEL, D_FF, 2)
    bgu = b_gu_l.reshape(N_EXPERTS, 1, D_FF, 2)
    return gu[..., 0].astype(BF), gu[..., 1].astype(BF), bgu[..., 0], bgu[..., 1]


def _combine_kernel(final, y_ref, x_ref, mod_ref, rl_ref, tw_ref, fg_ref, o_ref):
    r_lane = lax.broadcasted_iota(jnp.int32, (TM, R_TILE), 1)
    rl = rl_ref[...]
    tw = tw_ref[...]
    sel = jnp.zeros((TM, R_TILE), F32)
    for k in range(TOP_K):
        sel = jnp.where(r_lane == rl[:, k:k + 1], tw[:, k:k + 1], sel)
    ffn = _dot(sel.astype(BF), y_ref[...].astype(BF))
    x = x_ref[...] + mod_ref[5:6, :] * ffn
    if final:
        x = _rms(x) * fg_ref[...]
    o_ref[...] = x


def _combine(l, final, y, x, mod, rl, tw, final_g, geo):
    nt = geo["nt"]
    seq_row = geo["seq_row"]
    tile = lambda w: pl.BlockSpec((TM, w), lambda i: (i, 0))
    return pl.pallas_call(
        functools.partial(_combine_kernel, final),
        grid=(nt,),
        in_specs=[
            pl.BlockSpec((R_TILE, D_MODEL), lambda i: (i, 0)),
            tile(D_MODEL),
            pl.BlockSpec((None, None, 6, D_MODEL), lambda i: (l, seq_row(i), 0, 0)),
            tile(LANES),
            tile(LANES),
            _const_spec((1, D_MODEL)),
        ],
        out_specs=tile(D_MODEL),
        out_shape=jax.ShapeDtypeStruct((nt * TM, D_MODEL), F32),
        compiler_params=pltpu.CompilerParams(dimension_semantics=("arbitrary",), vmem_limit_bytes=VMEM_LIMIT),
        name="combine",
    )(y, x, mod, rl, tw, final_g)


def _expert_plan(meta, nt):
    pad8 = meta[:, 0, :N_EXPERTS]
    off = meta[:, 1, :N_EXPERTS]
    ends = jnp.cumsum(pad8, axis=0)
    dst = ends - pad8
    tot = ends[-1]
    nb = (tot + MOE_BM - 1) // MOE_BM
    nb_end = jnp.cumsum(nb)
    n_blocks = (nt * TM * TOP_K + nt * N_EXPERTS * 7) // MOE_BM + N_EXPERTS
    b = jnp.arange(n_blocks, dtype=jnp.int32)
    blk_e = jnp.minimum(jnp.searchsorted(nb_end, b, side="right"), N_EXPERTS - 1).astype(jnp.int32)
    blk_row0 = (b - (nb_end - nb)[blk_e]) * MOE_BM
    ends_b = ends[:, blk_e]
    dst_b = dst[:, blk_e]
    blk_t0 = jnp.sum(ends_b <= blk_row0[None, :], axis=0)
    blk_t1 = jnp.sum(dst_b < blk_row0[None, :] + MOE_BM, axis=0)
    i32 = lambda v: v.astype(jnp.int32)
    return {"blk_e": blk_e, "blk_row0": i32(blk_row0), "blk_t0": i32(blk_t0), "blk_t1": i32(blk_t1),
            "n_used": i32(nb_end[-1]).reshape(1), "pad8": i32(pad8.reshape(-1)), "dst": i32(dst.reshape(-1)),
            "off": i32(off.reshape(-1)), "tot8": i32(jnp.sum(pad8, axis=1))}


def _rope_tables(n_pos):
    pos = np.arange(n_pos)
    row, col = pos // GRID_W, pos % GRID_W
    lane = np.arange(LANES)

    def build(active, r, half):
        n_rot = 4 * half
        is_col = (r % n_rot) >= 2 * half
        rr = r % (2 * half)
        freq = np.power(ROPE_BASE, -(rr % half).astype(np.float64) / half)
        p = np.where(is_col[None, :], col[:, None], row[:, None]).astype(np.float64)
        ang = p * freq[None, :]
        first = rr < half
        cos = np.where(active[None, :], np.cos(ang), 1.0)
        sin_a = np.where((active & first)[None, :], -np.sin(ang), 0.0)
        sin_b = np.where((active & ~first)[None, :], np.sin(ang), 0.0)
        return [cos, sin_a, sin_b]

    tabs = build(np.ones(LANES, bool), lane % GQA_HEAD_DIM, GQA_HEAD_DIM // 4)
    in_rope = (lane >= MLA_NOPE) & (lane < MLA_NOPE + MLA_ROPE)
    tabs += build(in_rope, np.maximum(lane - MLA_NOPE, 0) % MLA_ROPE, MLA_ROPE // 4)
    tabs += build(lane < MLA_ROPE, lane % MLA_ROPE, MLA_ROPE // 4)
    table = np.concatenate(tabs, axis=1)
    ident = np.concatenate([np.ones((TM, LANES)), np.zeros((TM, LANES)), np.zeros((TM, LANES))] * 3, axis=1)
    return jnp.asarray(np.concatenate([ident, table], axis=0), F32)


def _placement():
    e = np.zeros((LANES, MLA_HEADS * LANES), np.float32)
    for h in range(MLA_HEADS):
        for r in range(MLA_ROPE):
            e[r, h * LANES + MLA_NOPE + r] = 1.0
    return jnp.asarray(e, BF)


def _block_diag_ones(n, g):
    idx = np.arange(n) // g
    return jnp.asarray((idx[:, None] == idx[None, :]).astype(np.float32), BF)


def kernel(x_prompt, x_sample, cache_gqa_k, cache_gqa_v, cache_mla_ckv, cache_mla_krope, c, c_ctx, norm1_g, norm2_g, w_mod, b_mod, w_in, conv_dw, conv_dw_b, conv_ln_g, conv_ln_b, w_conv_out, pool_w, pool_scale, w_pool_out, gqa_qn_g, gqa_kn_g, w_gqa_out, mla_qn_g, w_mla_q_up, mla_kvn_g, w_mla_kv_up, w_mla_out, w_o, w_router, b_router, w_gu, b_gu, w_dn, b_dn, final_g):
    bp, seq, d = x_prompt.shape
    bs, ls, _ = x_sample.shape
    depth = w_in.shape[0]
    past = cache_gqa_k.shape[2]
    assert seq == TM and d == D_MODEL and ls % TM == 0 and (bp * seq) % ls == 0
    tps = ls // TM
    n_p = bp
    nt = n_p + bs * tps
    geo = {
        "np": n_p, "bs": bs, "tps": tps, "nt": nt,
        "seq_row": lambda i: jnp.where(i < n_p, 0, 1 + (i - n_p) // tps),
        "rope_blk": lambda i: jnp.where(i < n_p, 0, 1 + (i - n_p) % tps),
    }

    n_cond = -(-(1 + bs) // 8) * 8
    cond = jnp.zeros((n_cond, d), F32).at[0].set(c_ctx).at[1:1 + bs].set(c)
    mod = _modulation(cond, w_mod, b_mod).reshape(depth, n_cond, 6, d)

    row = lambda v: v.reshape(depth, 1, -1)
    w1 = jnp.pad(w_in[:, :, :_SPLIT_GATE], ((0, 0), (0, 0), (0, W1_COLS - _SPLIT_GATE))).astype(BF)
    wqup = jnp.pad(w_mla_q_up.reshape(depth, MLA_Q_RANK, MLA_HEADS, MLA_NOPE + MLA_ROPE),
                   ((0, 0), (0, 0), (0, 0), (0, LANES - MLA_NOPE - MLA_ROPE)))
    wkv = w_mla_kv_up.reshape(depth, MLA_KV_RANK, MLA_HEADS, MLA_NOPE + MLA_V)
    wk_pad = jnp.pad(wkv[..., :MLA_NOPE], ((0, 0), (0, 0), (0, 0), (0, LANES - MLA_NOPE)))
    wkvup = jnp.concatenate([wk_pad.reshape(depth, MLA_KV_RANK, MLA_HEADS * LANES),
                             wkv[..., MLA_NOPE:].reshape(depth, MLA_KV_RANK, MLA_HEADS * MLA_V)], axis=-1)
    split = [_split_gate_up(w_gu[l], b_gu[l]) for l in range(depth)]
    p = {
        "norm1_g": row(norm1_g), "norm2_g": row(norm2_g),
        "w1": w1, "wgate": w_in[:, :, _SPLIT_GATE:].astype(BF),
        "rope": _rope_tables(ls), "place": _placement(), "ones_bd": _block_diag_ones(512, GQA_HEAD_DIM),
        "ltri": jnp.asarray(np.tril(np.ones((TM, TM), np.float32), -1), BF),
        "upper": jnp.asarray(np.triu(np.ones((LANES, LANES), np.float32), 1), F32),
        "qn_g": row(jnp.tile(gqa_qn_g, (1, GQA_HEADS))), "kn_g": row(jnp.tile(gqa_kn_g, (1, GQA_KV_HEADS))),
        "cqn_g": row(mla_qn_g), "kvn_g": row(mla_kvn_g),
        "wqup": wqup.reshape(depth, MLA_Q_RANK, MLA_HEADS * LANES).astype(BF), "wkvup": wkvup.astype(BF),
        "conv_dw": conv_dw, "conv_dw_b": row(conv_dw_b), "conv_ln_g": row(conv_ln_g), "conv_ln_b": row(conv_ln_b),
        "w_conv_out": w_conv_out.astype(BF), "pool_w": pool_w.astype(BF), "pool_scale": row(pool_scale),
        "w_pool_out": w_pool_out.astype(BF), "w_gqa_out": w_gqa_out.astype(BF), "w_mla_out": w_mla_out.astype(BF),
        "w_o": w_o.astype(BF),
        "w_router": jnp.pad(w_router, ((0, 0), (0, 0), (0, LANES - N_EXPERTS))),
        "b_router": row(jnp.pad(b_router, ((0, 0), (0, LANES - N_EXPERTS)))),
        "w_gate": [t[0] for t in split], "w_up": [t[1] for t in split],
        "b_gate": [t[2] for t in split], "b_up": [t[3] for t in split],
        "w_dn": w_dn, "b_dn": b_dn.reshape(depth, N_EXPERTS, 1, d),
    }

    ckm, cvm = _cache_prep(cache_mla_ckv, jnp.pad(cache_mla_krope, ((0, 0), (0, 0), (0, 0), (0, LANES - MLA_ROPE))), p)
    cache = (cache_gqa_k.reshape(bs, depth, past, LANES).astype(BF),
             cache_gqa_v.reshape(bs, depth, past, LANES).astype(BF), ckm, cvm)

    x = jnp.concatenate([x_prompt.reshape(bp * seq, d), x_sample.reshape(bs * ls, d)], axis=0)
    n_ptok = bp * seq
    states = []
    for l in range(depth):
        a, b, q, k, v, qm, km, vm, ks, vs, ckvs, krs = _proj(l, x, mod, geo, p)
        states.append((ks[:n_ptok], vs[:n_ptok], ckvs[:n_ptok], krs[:n_ptok]))
        attn = _attention(l, geo, q, k, v, qm, km, vm, cache)
        x_mid, xs, slot_row, top_w, meta = _merge(l, x, mod, geo, a, b, attn, p)
        y = _experts(l, xs, _expert_plan(meta, nt), p, nt)
        x = _combine(l, l == depth - 1, y, x_mid, mod, slot_row, top_w, final_g.reshape(1, d), geo)

    y_prompt = x[:n_ptok].reshape(bp, seq, d)
    y_sample = x[n_ptok:].reshape(bs, ls, d)
    st = lambda j, shape: jnp.stack([s[j].reshape(shape) for s in states], axis=1)
    return (y_prompt, y_sample,
            st(0, (bp, seq, GQA_KV_HEADS, GQA_HEAD_DIM)), st(1, (bp, seq, GQA_KV_HEADS, GQA_HEAD_DIM)),
            st(2, (bp, seq, MLA_KV_RANK)), st(3, (bp, seq, MLA_ROPE)))
```

```python
import functools

import jax
import jax.numpy as jnp
import numpy as np
from jax import lax
from jax.experimental import pallas as pl
from jax.experimental.pallas import tpu as pltpu

D_MODEL = 1024
GRID_W = 64
CONV_W = 512
CONV_K = 31
POOL_W = 512
POOL_WINDOWS = (2, 4, 8, 16)
POOL_G = 128
GQA_HEADS = 8
GQA_KV_HEADS = 2
GQA_HEAD_DIM = 64
MLA_HEADS = 8
MLA_Q_RANK = 384
MLA_KV_RANK = 256
MLA_NOPE = 64
MLA_ROPE = 32
MLA_V = 64
ROPE_BASE = 10000.0
N_EXPERTS = 32
TOP_K = 4
D_FF = 1024
SWIGLU_LIMIT = 7.0
SWIGLU_ALPHA = 1.702
EPS = 1e-6
GQA_SCALE = GQA_HEAD_DIM ** -0.5
MLA_SCALE = (MLA_NOPE + MLA_ROPE) ** -0.5

LANES = 128
SUBLANES = 8
TM = 256
HALO = 16
MOE_BM = 256
R_TILE = 1280
W1_COLS = 3072
N_GATE = 4 * D_MODEL
VMEM_LIMIT = 56 * 1024 * 1024

BF = jnp.bfloat16
F32 = jnp.float32

_C_A, _C_B, _C_Q, _C_K, _C_V, _C_CQ, _C_CKV, _C_KR = 0, 1024, 1536, 2048, 2176, 2304, 2688, 2944
_SPLIT_GATE = 2976


def _dot(a, b):
    return jnp.dot(a, b, preferred_element_type=F32)


def _dot_nt(a, b):
    return lax.dot_general(a, b, (((1,), (1,)), ((), ())), preferred_element_type=F32)


def _rms(x):
    return x * lax.rsqrt(jnp.mean(x * x, axis=-1, keepdims=True) + EPS)


def _group_mean_sq(x, ones_bd, width):
    xx = x * x
    hi = xx.astype(BF)
    lo = (xx - hi.astype(F32)).astype(BF)
    return (_dot(hi, ones_bd) + _dot(lo, ones_bd)) * (1.0 / width)


def _tile_lanes(t, width):
    reps = width // LANES
    return t if reps == 1 else jnp.concatenate([t] * reps, axis=-1)


def _rope(x, cos, sin_a, sin_b, shift):
    w = x.shape[-1]
    return (x * _tile_lanes(cos, w) + pltpu.roll(x, w - shift, 1) * _tile_lanes(sin_a, w)
            + pltpu.roll(x, shift, 1) * _tile_lanes(sin_b, w))


def _modulated_norm(x, g, shift, scale):
    return _rms(x) * g * (1.0 + scale) + shift


def _mod_kernel(cond_ref, w_ref, b_ref, o_ref):
    c = cond_ref[...]
    s = (c * jax.nn.sigmoid(c)).astype(BF)
    o_ref[...] = _dot(s, w_ref[...].astype(BF)) + b_ref[...]


def _modulation(cond, w_mod, b_mod):
    depth, d, n = w_mod.shape
    rows = cond.shape[0]
    return pl.pallas_call(
        _mod_kernel,
        grid=(depth, n // D_MODEL),
        in_specs=[
            pl.BlockSpec((rows, d), lambda l, j: (0, 0)),
            pl.BlockSpec((None, d, D_MODEL), lambda l, j: (l, 0, j)),
            pl.BlockSpec((None, 1, D_MODEL), lambda l, j: (l, 0, j)),
        ],
        out_specs=pl.BlockSpec((None, rows, D_MODEL), lambda l, j: (l, 0, j)),
        out_shape=jax.ShapeDtypeStruct((depth, rows, n), F32),
        name="modulation",
    )(cond, w_mod, b_mod.reshape(depth, 1, n))


def _proj_kernel(x_ref, mod_ref, g1_ref, w1_ref, rope_ref, qn_ref, kn_ref, cqn_ref, kvn_ref,
                 wqup_ref, wkvup_ref, place_ref, ones_ref,
                 a_ref, b_ref, q_ref, k_ref, v_ref, qm_ref, km_ref, vm_ref,
                 ks_ref, vs_ref, ckvs_ref, krs_ref):
    x = x_ref[...]
    h = _modulated_norm(x, g1_ref[...], mod_ref[0:1, :], mod_ref[1:2, :])
    y = _dot(h.astype(BF), w1_ref[...])

    a_ref[...] = (y[:, _C_A:_C_A + CONV_W] * jax.nn.sigmoid(y[:, _C_A + CONV_W:_C_B])).astype(BF)
    b_ref[...] = y[:, _C_B:_C_Q].astype(BF)

    def tab(j):
        return rope_ref[:, j * LANES:(j + 1) * LANES]

    q = y[:, _C_Q:_C_K]
    q = q * lax.rsqrt(_group_mean_sq(q, ones_ref[...], GQA_HEAD_DIM) + EPS) * qn_ref[...]
    q = _rope(q, tab(0), tab(1), tab(2), GQA_HEAD_DIM // 4)
    q_ref[...] = (q * GQA_SCALE).astype(BF)

    k = y[:, _C_K:_C_V]
    k = k * lax.rsqrt(_group_mean_sq(k, ones_ref[0:LANES, 0:LANES], GQA_HEAD_DIM) + EPS) * kn_ref[...]
    ks_ref[...] = k
    k_ref[...] = _rope(k, tab(0), tab(1), tab(2), GQA_HEAD_DIM // 4).astype(BF)

    v = y[:, _C_V:_C_CQ]
    vs_ref[...] = v
    v_ref[...] = v.astype(BF)

    cq = _rms(y[:, _C_CQ:_C_CKV]) * cqn_ref[...]
    qm = _dot(cq.astype(BF), wqup_ref[...])
    qm = _rope(qm, tab(3), tab(4), tab(5), MLA_ROPE // 4)
    qm_ref[...] = (qm * MLA_SCALE).astype(BF)

    ckv = _rms(y[:, _C_CKV:_C_KR]) * kvn_ref[...]
    ckvs_ref[...] = ckv
    kv = _dot(ckv.astype(BF), wkvup_ref[...])
    kr = y[:, _C_KR:W1_COLS]
    krs_ref[...] = kr[:, 0:MLA_ROPE]
    kr_rot = _rope(kr, tab(6), tab(7), tab(8), MLA_ROPE // 4)
    km = kv[:, 0:MLA_HEADS * LANES] + _dot(kr_rot.astype(BF), place_ref[...])
    km_ref[...] = km.astype(BF)
    vm_ref[...] = kv[:, MLA_HEADS * LANES:].astype(BF)


def _const_spec(shape):
    nd = len(shape)
    return pl.BlockSpec(shape, lambda *_: (0,) * nd)


def _layer_spec(shape, l):
    nd = len(shape)
    return pl.BlockSpec((None,) + shape, lambda *_: (l,) + (0,) * nd)


def _proj(l, x, mod, geo, p):
    nt = geo["nt"]
    tok = nt * TM
    seq_row, rope_blk = geo["seq_row"], geo["rope_blk"]

    def tile(width, dtype):
        return pl.BlockSpec((TM, width), lambda i: (i, 0)), jax.ShapeDtypeStruct((tok, width), dtype)

    outs = [tile(CONV_W, BF), tile(POOL_W, BF), tile(512, BF), tile(LANES, BF), tile(LANES, BF),
            tile(MLA_HEADS * LANES, BF), tile(MLA_HEADS * LANES, BF), tile(MLA_HEADS * MLA_V, BF),
            tile(LANES, F32), tile(LANES, F32), tile(MLA_KV_RANK, F32), tile(MLA_ROPE, F32)]
    return pl.pallas_call(
        _proj_kernel,
        grid=(nt,),
        in_specs=[
            pl.BlockSpec((TM, D_MODEL), lambda i: (i, 0)),
            pl.BlockSpec((None, None, 6, D_MODEL), lambda i: (l, seq_row(i), 0, 0)),
            _layer_spec((1, D_MODEL), l),
            _layer_spec((D_MODEL, W1_COLS), l),
            pl.BlockSpec((TM, 9 * LANES), lambda i: (rope_blk(i), 0)),
            _layer_spec((1, 512), l),
            _layer_spec((1, LANES), l),
            _layer_spec((1, MLA_Q_RANK), l),
            _layer_spec((1, MLA_KV_RANK), l),
            _layer_spec((MLA_Q_RANK, MLA_HEADS * LANES), l),
            _layer_spec((MLA_KV_RANK, MLA_HEADS * (LANES + MLA_V)), l),
            _const_spec((LANES, MLA_HEADS * LANES)),
            _const_spec((512, 512)),
        ],
        out_specs=[o[0] for o in outs],
        out_shape=[o[1] for o in outs],
        compiler_params=pltpu.CompilerParams(dimension_semantics=("arbitrary",), vmem_limit_bytes=VMEM_LIMIT),
        name="proj",
    )(x, mod, p["norm1_g"], p["w1"], p["rope"], p["qn_g"], p["kn_g"], p["cqn_g"], p["kvn_g"],
      p["wqup"], p["wkvup"], p["place"], p["ones_bd"])


def _cache_kernel(ckv_ref, kr_ref, wkvup_ref, place_ref, km_ref, vm_ref):
    kv = _dot(ckv_ref[...].astype(BF), wkvup_ref[...])
    km = kv[:, 0:MLA_HEADS * LANES] + _dot(kr_ref[...].astype(BF), place_ref[...])
    km_ref[...] = km.astype(BF)
    vm_ref[...] = kv[:, MLA_HEADS * LANES:].astype(BF)


def _cache_prep(ckv, kr_pad, p):
    bs, depth, past, _ = ckv.shape
    return pl.pallas_call(
        _cache_kernel,
        grid=(bs, depth),
        in_specs=[
            pl.BlockSpec((None, None, past, MLA_KV_RANK), lambda b, l: (b, l, 0, 0)),
            pl.BlockSpec((None, None, past, LANES), lambda b, l: (b, l, 0, 0)),
            pl.BlockSpec((None, MLA_KV_RANK, MLA_HEADS * (LANES + MLA_V)), lambda b, l: (l, 0, 0)),
            pl.BlockSpec((LANES, MLA_HEADS * LANES), lambda b, l: (0, 0)),
        ],
        out_specs=[
            pl.BlockSpec((None, None, past, MLA_HEADS * LANES), lambda b, l: (b, l, 0, 0)),
            pl.BlockSpec((None, None, past, MLA_HEADS * MLA_V), lambda b, l: (b, l, 0, 0)),
        ],
        out_shape=[jax.ShapeDtypeStruct((bs, depth, past, MLA_HEADS * LANES), BF),
                   jax.ShapeDtypeStruct((bs, depth, past, MLA_HEADS * MLA_V), BF)],
        name="cache_prep",
    )(ckv, kr_pad, p["wkvup"], p["place"])


def _softmax_pv(s, v):
    m = jnp.max(s, axis=-1, keepdims=True)
    e = jnp.exp(s - m)
    l = jnp.sum(e, axis=-1, keepdims=True)
    return _dot(e.astype(BF), v) / l


def _attention_body(q, k, v, qm, km, vm, oc_ref, od_ref):
    lane_k = lax.broadcasted_iota(jnp.int32, k.shape, 1)
    lane_q = lax.broadcasted_iota(jnp.int32, (TM, LANES), 1)
    lo_k = lane_k < GQA_HEAD_DIM
    lo_q = lane_q < GQA_HEAD_DIM
    k32, v32 = k.astype(F32), v.astype(F32)
    k_sw = pltpu.roll(k32, GQA_HEAD_DIM, 1)
    v_sw = pltpu.roll(v32, GQA_HEAD_DIM, 1)
    k_dup = [jnp.where(lo_k, k32, k_sw).astype(BF), jnp.where(lo_k, k_sw, k32).astype(BF)]
    v_dup = [jnp.where(lo_k, v32, v_sw).astype(BF), jnp.where(lo_k, v_sw, v32).astype(BF)]
    zero = jnp.zeros((TM, LANES), BF)
    group = GQA_HEADS // GQA_KV_HEADS
    for j in range(GQA_HEADS // 2):
        qs = q[:, j * LANES:(j + 1) * LANES]
        g = (2 * j) // group
        o_lo = _softmax_pv(_dot_nt(jnp.where(lo_q, qs, zero), k_dup[g]), v_dup[g])
        o_hi = _softmax_pv(_dot_nt(jnp.where(lo_q, zero, qs), k_dup[g]), v_dup[g])
        oc_ref[:, j * LANES:(j + 1) * LANES] = jnp.where(lo_q, o_lo, o_hi).astype(BF)
    for j in range(MLA_HEADS // 2):
        vs = vm[:, j * LANES:(j + 1) * LANES]
        outs = []
        for h in (2 * j, 2 * j + 1):
            s = _dot_nt(qm[:, h * LANES:(h + 1) * LANES], km[:, h * LANES:(h + 1) * LANES])
            outs.append(_softmax_pv(s, vs))
        od_ref[:, j * LANES:(j + 1) * LANES] = jnp.where(lo_q, outs[0], outs[1]).astype(BF)


def _attn_prompt_kernel(q_ref, k_ref, v_ref, qm_ref, km_ref, vm_ref, oc_ref, od_ref):
    _attention_body(q_ref[...], k_ref[...], v_ref[...], qm_ref[...], km_ref[...], vm_ref[...], oc_ref, od_ref)


def _attn_sample_kernel(q_ref, k_ref, v_ref, qm_ref, km_ref, vm_ref, ck_ref, cv_ref, ckm_ref, cvm_ref,
                        oc_ref, od_ref):
    k = jnp.concatenate([ck_ref[...], k_ref[...]], axis=0)
    v = jnp.concatenate([cv_ref[...], v_ref[...]], axis=0)
    km = jnp.concatenate([ckm_ref[...], km_ref[...]], axis=0)
    vm = jnp.concatenate([cvm_ref[...], vm_ref[...]], axis=0)
    _attention_body(q_ref[...], k, v, qm_ref[...], km, vm, oc_ref, od_ref)


def _attention(l, geo, q, k, v, qm, km, vm, cache):
    n_p, bs, tps, nt = geo["np"], geo["bs"], geo["tps"], geo["nt"]
    tok = nt * TM
    ls = tps * TM
    off = (n_p * TM) // ls
    widths = (512, LANES, LANES, MLA_HEADS * LANES, MLA_HEADS * LANES, MLA_HEADS * MLA_V)
    params = pltpu.CompilerParams(dimension_semantics=("arbitrary",), vmem_limit_bytes=VMEM_LIMIT)
    oc_p, od_p = pl.pallas_call(
        _attn_prompt_kernel,
        grid=(n_p,),
        in_specs=[pl.BlockSpec((TM, w), lambda i: (i, 0)) for w in widths],
        out_specs=[pl.BlockSpec((TM, 512), lambda i: (i, 0))] * 2,
        out_shape=[jax.ShapeDtypeStruct((n_p * TM, 512), BF)] * 2,
        compiler_params=params,
        name="attn_prompt",
    )(q, k, v, qm, km, vm)
    ck, cv, ckm, cvm = cache
    past = ck.shape[2]
    q_spec = lambda w: pl.BlockSpec((TM, w), lambda b, j: (n_p + b * tps + j, 0))
    kv_spec = lambda w: pl.BlockSpec((ls, w), lambda b, j: (off + b, 0))
    c_spec = lambda w: pl.BlockSpec((None, None, past, w), lambda b, j: (b, l, 0, 0))
    params2 = pltpu.CompilerParams(dimension_semantics=("arbitrary", "arbitrary"), vmem_limit_bytes=VMEM_LIMIT)
    o_spec = pl.BlockSpec((TM, 512), lambda b, j: (b * tps + j, 0))
    oc_s, od_s = pl.pallas_call(
        _attn_sample_kernel,
        grid=(bs, tps),
        in_specs=[q_spec(512), kv_spec(LANES), kv_spec(LANES), q_spec(MLA_HEADS * LANES),
                  kv_spec(MLA_HEADS * LANES), kv_spec(MLA_HEADS * MLA_V),
                  c_spec(LANES), c_spec(LANES), c_spec(MLA_HEADS * LANES), c_spec(MLA_HEADS * MLA_V)],
        out_specs=[o_spec, o_spec],
        out_shape=[jax.ShapeDtypeStruct((bs * ls, 512), BF)] * 2,
        compiler_params=params2,
        name="attn_sample",
    )(q, k, v, qm, km, vm, ck, cv, ckm, cvm)
    return oc_p, od_p, oc_s, od_s


def _merge_kernel(geo, x_ref, mod_ref, a_ref, ap_ref, an_ref, b_ref, bp_ref, bn_ref,
                  ocp_ref, odp_ref, ocs_ref, ods_ref,
                  g1_ref, wg_ref, cw_ref, cb_ref, lng_ref, lnb_ref, wco_ref, pw_ref, ps_ref, wpo_ref,
                  wgo_ref, wmo_ref, wo_ref, g2_ref, wr_ref, br_ref,
                  ltri_ref, upper_ref,
                  xo_ref, xs_ref, rl_ref, tw_ref, meta_ref, abuf, bbuf, ashift):
    n_p, tps = geo["np"], geo["tps"]
    i = pl.program_id(0)
    j = jnp.where(i < n_p, 0, (i - n_p) % tps)
    n_seq_tiles = jnp.where(i < n_p, 1, tps)
    has_prev = j > 0
    has_next = j < n_seq_tiles - 1

    def fill(buf, cur, prev, nxt):
        buf[0:HALO, :] = jnp.where(has_prev, prev[...].astype(F32), 0.0)
        buf[HALO:HALO + TM, :] = cur[...].astype(F32)
        buf[HALO + TM:, :] = jnp.where(has_next, nxt[...].astype(F32), 0.0)

    fill(abuf, a_ref, ap_ref, an_ref)
    fill(bbuf, b_ref, bp_ref, bn_ref)

    sh_rows = TM + 2 * HALO - SUBLANES
    for sh in range(1, SUBLANES):
        ashift[sh - 1] = abuf[sh:sh + sh_rows, :]
    rows = 32
    conv = []
    for r0 in range(0, TM, rows):
        acc = jnp.zeros((rows, CONV_W), F32)
        for t in range(CONV_K):
            s = r0 + t + HALO - CONV_K // 2
            sh = s % SUBLANES
            tap = abuf[s:s + rows, :] if sh == 0 else ashift[sh - 1, s - sh:s - sh + rows, :]
            acc = acc + tap * cw_ref[t:t + 1, :]
        conv.append(acc)
    ca = jnp.concatenate(conv, axis=0) + cb_ref[...]
    mu = jnp.mean(ca, axis=-1, keepdims=True)
    xc = ca - mu
    ln = xc * lax.rsqrt(jnp.mean(xc * xc, axis=-1, keepdims=True) + EPS) * lng_ref[...] + lnb_ref[...]
    br_a = _dot((ln * jax.nn.sigmoid(ln)).astype(BF), wco_ref[...])

    pos = j * TM + lax.broadcasted_iota(jnp.int32, (TM, 1), 0)
    seq_len = n_seq_tiles * TM
    mixed = []
    for g, w in enumerate(POOL_WINDOWS):
        lo, hi = g * POOL_G, (g + 1) * POOL_G
        acc = jnp.zeros((TM, POOL_G), F32)
        for o in range(-(w // 2), w - w // 2):
            acc = acc + bbuf[HALO + o:HALO + o + TM, lo:hi]
        cnt = jnp.clip(pos - w // 2 + w, 0, seq_len) - jnp.clip(pos - w // 2, 0, seq_len)
        pooled = acc / cnt.astype(F32) - bbuf[HALO:HALO + TM, lo:hi]
        mixed.append(_dot(pooled.astype(BF), pw_ref[g]))
    pb = jnp.concatenate(mixed, axis=-1) * ps_ref[...]
    br_b = _dot(pb.astype(BF), wpo_ref[...])

    is_prompt = i < n_p
    br_c = _dot(jnp.where(is_prompt, ocp_ref[...], ocs_ref[...]), wgo_ref[...])
    br_d = _dot(jnp.where(is_prompt, odp_ref[...], ods_ref[...]), wmo_ref[...])

    x = x_ref[...]
    h = _modulated_norm(x, g1_ref[...], mod_ref[0:1, :], mod_ref[1:2, :])
    gates = jax.nn.sigmoid(_dot(h.astype(BF), wg_ref[...]))
    merged = (gates[:, 0:D_MODEL] * br_a + gates[:, D_MODEL:2 * D_MODEL] * br_b
              + gates[:, 2 * D_MODEL:3 * D_MODEL] * br_c + gates[:, 3 * D_MODEL:] * br_d)
    x = x + mod_ref[2:3, :] * _dot(merged.astype(BF), wo_ref[...])
    xo_ref[...] = x

    h2 = _modulated_norm(x, g2_ref[...], mod_ref[3:4, :], mod_ref[4:5, :])

    logits = jnp.dot(h2, wr_ref[...], preferred_element_type=F32, precision=lax.Precision.HIGHEST) + br_ref[...]
    lane = lax.broadcasted_iota(jnp.int32, (TM, LANES), 1).astype(F32)
    neg = jnp.float32(-jnp.inf)
    lg = jnp.where(lane < N_EXPERTS, logits, neg)
    vals, idxs = [], []
    for _ in range(TOP_K):
        m = jnp.max(lg, axis=-1, keepdims=True)
        idx = jnp.min(jnp.where(lg == m, lane, float(LANES)), axis=-1, keepdims=True)
        vals.append(m)
        idxs.append(idx)
        lg = jnp.where(lane == idx, neg, lg)
    exps = [jnp.exp(v - vals[0]) for v in vals]
    denom = exps[0] + exps[1] + exps[2] + exps[3]

    onehot = [(lane == idxs[kk]).astype(F32) for kk in range(TOP_K)]
    colsum = [jnp.sum(o, axis=0, keepdims=True) for o in onehot]
    cnt = colsum[0] + colsum[1] + colsum[2] + colsum[3]
    pad8 = jnp.floor((cnt + 7.0) * 0.125) * 8.0
    run_off = jnp.dot(jnp.broadcast_to(pad8, (8, LANES)), upper_ref[...], preferred_element_type=F32,
                      precision=lax.Precision.HIGHEST)[0:1, :]
    base = jnp.zeros((1, LANES), F32)
    rloc = []
    for kk in range(TOP_K):
        before = _dot(ltri_ref[...], onehot[kk].astype(BF))
        rloc.append(jnp.sum(onehot[kk] * (run_off + base + before), axis=-1, keepdims=True))
        base = base + colsum[kk]

    eye = (lax.broadcasted_iota(jnp.int32, (TM, TM), 0) == lax.broadcasted_iota(jnp.int32, (TM, TM), 1))
    r_sub = lax.broadcasted_iota(jnp.int32, (R_TILE, TM), 0).astype(F32)
    ones8 = jnp.ones((8, TM), F32)
    sel = jnp.zeros((R_TILE, TM), F32)
    for kk in range(TOP_K):
        row = jnp.dot(ones8, jnp.where(eye, rloc[kk], 0.0), preferred_element_type=F32,
                      precision=lax.Precision.HIGHEST)[0:1, :]
        sel = jnp.where(r_sub == row, 1.0, sel)
    xs_ref[...] = _dot(sel.astype(BF), h2.astype(BF))

    rl = jnp.zeros((TM, LANES), F32)
    tw = jnp.zeros((TM, LANES), F32)
    for kk in range(TOP_K):
        rl = jnp.where(lane == kk, rloc[kk], rl)
        tw = jnp.where(lane == kk, exps[kk] / denom, tw)
    rl_ref[...] = rl.astype(jnp.int32)
    tw_ref[...] = tw
    sub = lax.broadcasted_iota(jnp.int32, (8, LANES), 0)
    meta = jnp.where(sub == 0, jnp.broadcast_to(pad8, (8, LANES)),
                     jnp.where(sub == 1, jnp.broadcast_to(run_off, (8, LANES)), 0.0))
    meta_ref[...] = meta.astype(jnp.int32)


def _merge(l, x, mod, geo, a, b, attn, p):
    nt, n_p = geo["nt"], geo["np"]
    tok = nt * TM
    seq_row = geo["seq_row"]
    hb = TM // HALO
    last_hb = nt * hb - 1
    tile = lambda w: pl.BlockSpec((TM, w), lambda i: (i, 0))
    prev = lambda w: pl.BlockSpec((HALO, w), lambda i: (jnp.maximum(i * hb - 1, 0), 0))
    nxt = lambda w: pl.BlockSpec((HALO, w), lambda i: (jnp.minimum((i + 1) * hb, last_hb), 0))
    p_tile = pl.BlockSpec((TM, 512), lambda i: (jnp.minimum(i, n_p - 1), 0))
    s_tile = pl.BlockSpec((TM, 512), lambda i: (jnp.maximum(i - n_p, 0), 0))
    return pl.pallas_call(
        functools.partial(_merge_kernel, geo),
        grid=(nt,),
        in_specs=[
            tile(D_MODEL),
            pl.BlockSpec((None, None, 6, D_MODEL), lambda i: (l, seq_row(i), 0, 0)),
            tile(CONV_W), prev(CONV_W), nxt(CONV_W),
            tile(POOL_W), prev(POOL_W), nxt(POOL_W),
            p_tile, p_tile, s_tile, s_tile,
            _layer_spec((1, D_MODEL), l),
            _layer_spec((D_MODEL, N_GATE), l),
            _layer_spec((CONV_K, CONV_W), l),
            _layer_spec((1, CONV_W), l),
            _layer_spec((1, CONV_W), l),
            _layer_spec((1, CONV_W), l),
            _layer_spec((CONV_W, D_MODEL), l),
            _layer_spec((len(POOL_WINDOWS), POOL_G, POOL_G), l),
            _layer_spec((1, POOL_W), l),
            _layer_spec((POOL_W, D_MODEL), l),
            _layer_spec((512, D_MODEL), l),
            _layer_spec((512, D_MODEL), l),
            _layer_spec((D_MODEL, D_MODEL), l),
            _layer_spec((1, D_MODEL), l),
            _layer_spec((D_MODEL, LANES), l),
            _layer_spec((1, LANES), l),
            _const_spec((TM, TM)),
            _const_spec((LANES, LANES)),
        ],
        out_specs=[tile(D_MODEL), pl.BlockSpec((R_TILE, D_MODEL), lambda i: (i, 0)), tile(LANES), tile(LANES),
                   pl.BlockSpec((None, 8, LANES), lambda i: (i, 0, 0))],
        out_shape=[jax.ShapeDtypeStruct((tok, D_MODEL), F32), jax.ShapeDtypeStruct((nt * R_TILE, D_MODEL), F32),
                   jax.ShapeDtypeStruct((tok, LANES), jnp.int32), jax.ShapeDtypeStruct((tok, LANES), F32),
                   jax.ShapeDtypeStruct((nt, 8, LANES), jnp.int32)],
        scratch_shapes=[pltpu.VMEM((TM + 2 * HALO, CONV_W), F32), pltpu.VMEM((TM + 2 * HALO, POOL_W), F32),
                        pltpu.VMEM((SUBLANES - 1, TM + 2 * HALO - SUBLANES, CONV_W), F32)],
        compiler_params=pltpu.CompilerParams(dimension_semantics=("arbitrary",), vmem_limit_bytes=VMEM_LIMIT),
        name="merge",
    )(x, mod, a, a, a, b, b, b, *attn,
      p["norm1_g"], p["wgate"], p["conv_dw"], p["conv_dw_b"], p["conv_ln_g"], p["conv_ln_b"], p["w_conv_out"],
      p["pool_w"], p["pool_scale"], p["w_pool_out"], p["w_gqa_out"], p["w_mla_out"], p["w_o"],
      p["norm2_g"], p["w_router"], p["b_router"], p["ltri"], p["upper"])


def _expert_kernel(nt, blk_e, blk_row0, blk_t0, blk_t1, n_used_ref, pad8_ref, dst_ref, off_ref, tot8_ref,
                   xs_hbm, wg_ref, wu_ref, bg_ref, bu_ref, wd_ref, bd_ref,
                   y_hbm, xg, yb, zbuf, wd_bf, cnt_smem, gsem, osem, zsem):
    i = pl.program_id(0)
    n_used = n_used_ref[0]
    slot = i % 2
    last = pl.num_programs(0) - 1

    def for_pieces(b, fn):
        e = blk_e[b]
        b0 = blk_row0[b]

        def body(t, tot):
            j = t * N_EXPERTS + e
            run0 = dst_ref[j]
            lo = jnp.maximum(run0, b0)
            hi = jnp.minimum(run0 + pad8_ref[j], b0 + MOE_BM)
            n = pl.multiple_of(jnp.maximum(hi - lo, 0), 8)

            @pl.when(n > 0)
            def _():
                fn(pl.multiple_of(t * R_TILE + off_ref[j] + lo - run0, 8), pl.multiple_of(lo - b0, 8), n)

            return tot + n

        return lax.fori_loop(blk_t0[b], blk_t1[b], body, jnp.int32(0))

    def rows_copy(src, dst, sem, n):
        return pltpu.make_async_copy(src.at[pl.ds(0, n), :], dst.at[pl.ds(0, n), :], sem)

    def start_gather(b, s):
        def piece(row_t, row_b, n):
            pltpu.make_async_copy(xs_hbm.at[pl.ds(row_t, n), :], xg.at[s].at[pl.ds(row_b, n), :], gsem.at[s]).start()

        cnt_smem[s] = for_pieces(b, piece)

    def tail_copy(t):
        n = pl.multiple_of(R_TILE - tot8_ref[t], 8)
        return n, pltpu.make_async_copy(
            zbuf.at[pl.ds(0, n), :], y_hbm.at[pl.ds(pl.multiple_of(t * R_TILE + tot8_ref[t], 8), n), :], zsem)

    @pl.when(i == 0)
    def _():
        xg[...] = jnp.zeros_like(xg)
        zbuf[...] = jnp.zeros_like(zbuf)
        for s in range(4):
            cnt_smem[s] = 0

        def fill(t, c):
            n, cp = tail_copy(t)

            @pl.when(n > 0)
            def _():
                cp.start()

            return c

        lax.fori_loop(0, nt, fill, 0)

        @pl.when(n_used > 0)
        def _():
            start_gather(0, 0)

    @pl.when(i + 1 < n_used)
    def _():
        start_gather(i + 1, 1 - slot)

    @pl.when(i < n_used)
    def _():
        e_changed = jnp.logical_or(i == 0, blk_e[i] != blk_e[jnp.maximum(i - 1, 0)])

        @pl.when(e_changed)
        def _():
            wd_bf[...] = wd_ref[...].astype(BF)

        n_in = pl.multiple_of(cnt_smem[slot], 8)
        rows_copy(xs_hbm, xg.at[slot], gsem.at[slot], n_in).wait()
        xb = xg[slot].astype(BF)
        gate = jnp.minimum(_dot(xb, wg_ref[...]) + bg_ref[...], SWIGLU_LIMIT)
        up = jnp.clip(_dot(xb, wu_ref[...]) + bu_ref[...], -SWIGLU_LIMIT, SWIGLU_LIMIT)
        glu = gate * jax.nn.sigmoid(gate * SWIGLU_ALPHA)
        y = _dot(((up + 1.0) * glu).astype(BF), wd_bf[...]) + bd_ref[...]

        n_prev = pl.multiple_of(cnt_smem[2 + slot], 8)

        @pl.when(n_prev > 0)
        def _():
            rows_copy(yb.at[slot], y_hbm, osem.at[slot], n_prev).wait()

        yb[slot] = y.astype(BF).astype(F32)

        def piece(row_t, row_b, n):
            pltpu.make_async_copy(yb.at[slot].at[pl.ds(row_b, n), :], y_hbm.at[pl.ds(row_t, n), :],
                                  osem.at[slot]).start()

        cnt_smem[2 + slot] = for_pieces(i, piece)

    @pl.when(i == last)
    def _():
        for s in range(2):
            n_out = pl.multiple_of(cnt_smem[2 + s], 8)

            @pl.when(n_out > 0)
            def _():
                rows_copy(yb.at[s], y_hbm, osem.at[s], n_out).wait()

        def drain(t, c):
            n, cp = tail_copy(t)

            @pl.when(n > 0)
            def _():
                cp.wait()

            return c

        lax.fori_loop(0, nt, drain, 0)


def _experts(l, xs, plan, p, nt):
    n_blocks = plan["blk_e"].shape[0]
    n_pref = 9
    by_expert = lambda *lead: (lambda i, be, *_: lead + (be[i], 0, 0))
    w_spec = pl.BlockSpec((None, D_MODEL, D_FF), by_expert())
    b_spec = pl.BlockSpec((None, 1, D_FF), by_expert())
    grid_spec = pltpu.PrefetchScalarGridSpec(
        num_scalar_prefetch=n_pref,
        grid=(n_blocks,),
        in_specs=[
            pl.BlockSpec(memory_space=pl.ANY),
            w_spec, w_spec, b_spec, b_spec,
            pl.BlockSpec((None, None, D_FF, D_MODEL), by_expert(l)),
            pl.BlockSpec((None, None, 1, D_MODEL), by_expert(l)),
        ],
        out_specs=pl.BlockSpec(memory_space=pl.ANY),
        scratch_shapes=[
            pltpu.VMEM((2, MOE_BM, D_MODEL), F32),
            pltpu.VMEM((2, MOE_BM, D_MODEL), F32),
            pltpu.VMEM((R_TILE - TOP_K * TM, D_MODEL), F32),
            pltpu.VMEM((D_FF, D_MODEL), BF),
            pltpu.SMEM((4,), jnp.int32),
            pltpu.SemaphoreType.DMA((2,)),
            pltpu.SemaphoreType.DMA((2,)),
            pltpu.SemaphoreType.DMA,
        ],
    )
    return pl.pallas_call(
        functools.partial(_expert_kernel, nt),
        grid_spec=grid_spec,
        out_shape=jax.ShapeDtypeStruct((nt * R_TILE, D_MODEL), F32),
        compiler_params=pltpu.CompilerParams(dimension_semantics=("arbitrary",), vmem_limit_bytes=VMEM_LIMIT),
        name="experts",
    )(plan["blk_e"], plan["blk_row0"], plan["blk_t0"], plan["blk_t1"], plan["n_used"], plan["pad8"], plan["dst"],
      plan["off"], plan["tot8"],
      xs, p["w_gate"][l], p["w_up"][l], p["b_gate"][l], p["b_up"][l], p["w_dn"], p["b_dn"])


def _split_gate_up(w_gu_l, b_gu_l):
    gu = w_gu_l.reshape(N_EXPERTS, D_MODEL, D_FF, 2)
    bgu = b_gu_l.reshape(N_EXPERTS, 1, D_FF, 2)
    return gu[..., 0].astype(BF), gu[..., 1].astype(BF), bgu[..., 0], bgu[..., 1]


def _combine_kernel(final, y_ref, x_ref, mod_ref, rl_ref, tw_ref, fg_ref, o_ref):
    r_lane = lax.broadcasted_iota(jnp.int32, (TM, R_TILE), 1)
    rl = rl_ref[...]
    tw = tw_ref[...]
    sel = jnp.zeros((TM, R_TILE), F32)
    for k in range(TOP_K):
        sel = jnp.where(r_lane == rl[:, k:k + 1], tw[:, k:k + 1], sel)
    ffn = _dot(sel.astype(BF), y_ref[...].astype(BF))
    x = x_ref[...] + mod_ref[5:6, :] * ffn
    if final:
        x = _rms(x) * fg_ref[...]
    o_ref[...] = x


def _combine(l, final, y, x, mod, rl, tw, final_g, geo):
    nt = geo["nt"]
    seq_row = geo["seq_row"]
    tile = lambda w: pl.BlockSpec((TM, w), lambda i: (i, 0))
    return pl.pallas_call(
        functools.partial(_combine_kernel, final),
        grid=(nt,),
        in_specs=[
            pl.BlockSpec((R_TILE, D_MODEL), lambda i: (i, 0)),
            tile(D_MODEL),
            pl.BlockSpec((None, None, 6, D_MODEL), lambda i: (l, seq_row(i), 0, 0)),
            tile(LANES),
            tile(LANES),
            _const_spec((1, D_MODEL)),
        ],
        out_specs=tile(D_MODEL),
        out_shape=jax.ShapeDtypeStruct((nt * TM, D_MODEL), F32),
        compiler_params=pltpu.CompilerParams(dimension_semantics=("arbitrary",), vmem_limit_bytes=VMEM_LIMIT),
        name="combine",
    )(y, x, mod, rl, tw, final_g)


def _expert_plan(meta, nt):
    pad8 = meta[:, 0, :N_EXPERTS]
    off = meta[:, 1, :N_EXPERTS]
    ends = jnp.cumsum(pad8, axis=0)
    dst = ends - pad8
    tot = ends[-1]
    nb = (tot + MOE_BM - 1) // MOE_BM
    nb_end = jnp.cumsum(nb)
    n_blocks = (nt * TM * TOP_K + nt * N_EXPERTS * 7) // MOE_BM + N_EXPERTS
    b = jnp.arange(n_blocks, dtype=jnp.int32)
    blk_e = jnp.minimum(jnp.sum(nb_end[None, :] <= b[:, None], axis=1), N_EXPERTS - 1).astype(jnp.int32)
    blk_row0 = (b - (nb_end - nb)[blk_e]) * MOE_BM
    ends_b = ends[:, blk_e]
    dst_b = dst[:, blk_e]
    blk_t0 = jnp.sum(ends_b <= blk_row0[None, :], axis=0)
    blk_t1 = jnp.sum(dst_b < blk_row0[None, :] + MOE_BM, axis=0)
    i32 = lambda v: v.astype(jnp.int32)
    return {"blk_e": blk_e, "blk_row0": i32(blk_row0), "blk_t0": i32(blk_t0), "blk_t1": i32(blk_t1),
            "n_used": i32(nb_end[-1]).reshape(1), "pad8": i32(pad8.reshape(-1)), "dst": i32(dst.reshape(-1)),
            "off": i32(off.reshape(-1)), "tot8": i32(jnp.sum(pad8, axis=1))}


def _rope_tables(n_pos):
    pos = np.arange(n_pos)
    row, col = pos // GRID_W, pos % GRID_W
    lane = np.arange(LANES)

    def build(active, r, half):
        n_rot = 4 * half
        is_col = (r % n_rot) >= 2 * half
        rr = r % (2 * half)
        freq = np.power(ROPE_BASE, -(rr % half).astype(np.float64) / half)
        p = np.where(is_col[None, :], col[:, None], row[:, None]).astype(np.float64)
        ang = p * freq[None, :]
        first = rr < half
        cos = np.where(active[None, :], np.cos(ang), 1.0)
        sin_a = np.where((active & first)[None, :], -np.sin(ang), 0.0)
        sin_b = np.where((active & ~first)[None, :], np.sin(ang), 0.0)
        return [cos, sin_a, sin_b]

    tabs = build(np.ones(LANES, bool), lane % GQA_HEAD_DIM, GQA_HEAD_DIM // 4)
    in_rope = (lane >= MLA_NOPE) & (lane < MLA_NOPE + MLA_ROPE)
    tabs += build(in_rope, np.maximum(lane - MLA_NOPE, 0) % MLA_ROPE, MLA_ROPE // 4)
    tabs += build(lane < MLA_ROPE, lane % MLA_ROPE, MLA_ROPE // 4)
    table = np.concatenate(tabs, axis=1)
    ident = np.concatenate([np.ones((TM, LANES)), np.zeros((TM, LANES)), np.zeros((TM, LANES))] * 3, axis=1)
    return jnp.asarray(np.concatenate([ident, table], axis=0), F32)


def _placement():
    e = np.zeros((LANES, MLA_HEADS * LANES), np.float32)
    for h in range(MLA_HEADS):
        for r in range(MLA_ROPE):
            e[r, h * LANES + MLA_NOPE + r] = 1.0
    return jnp.asarray(e, BF)


def _block_diag_ones(n, g):
    idx = np.arange(n) // g
    return jnp.asarray((idx[:, None] == idx[None, :]).astype(np.float32), BF)


def kernel(x_prompt, x_sample, cache_gqa_k, cache_gqa_v, cache_mla_ckv, cache_mla_krope, c, c_ctx, norm1_g, norm2_g, w_mod, b_mod, w_in, conv_dw, conv_dw_b, conv_ln_g, conv_ln_b, w_conv_out, pool_w, pool_scale, w_pool_out, gqa_qn_g, gqa_kn_g, w_gqa_out, mla_qn_g, w_mla_q_up, mla_kvn_g, w_mla_kv_up, w_mla_out, w_o, w_router, b_router, w_gu, b_gu, w_dn, b_dn, final_g):
    bp, seq, d = x_prompt.shape
    bs, ls, _ = x_sample.shape
    depth = w_in.shape[0]
    past = cache_gqa_k.shape[2]
    assert seq == TM and d == D_MODEL and ls % TM == 0 and (bp * seq) % ls == 0
    tps = ls // TM
    n_p = bp
    nt = n_p + bs * tps
    geo = {
        "np": n_p, "bs": bs, "tps": tps, "nt": nt,
        "seq_row": lambda i: jnp.where(i < n_p, 0, 1 + (i - n_p) // tps),
        "rope_blk": lambda i: jnp.where(i < n_p, 0, 1 + (i - n_p) % tps),
    }

    n_cond = -(-(1 + bs) // 8) * 8
    cond = jnp.zeros((n_cond, d), F32).at[0].set(c_ctx).at[1:1 + bs].set(c)
    mod = _modulation(cond, w_mod, b_mod).reshape(depth, n_cond, 6, d)

    row = lambda v: v.reshape(depth, 1, -1)
    w1 = jnp.pad(w_in[:, :, :_SPLIT_GATE], ((0, 0), (0, 0), (0, W1_COLS - _SPLIT_GATE))).astype(BF)
    wqup = jnp.pad(w_mla_q_up.reshape(depth, MLA_Q_RANK, MLA_HEADS, MLA_NOPE + MLA_ROPE),
                   ((0, 0), (0, 0), (0, 0), (0, LANES - MLA_NOPE - MLA_ROPE)))
    wkv = w_mla_kv_up.reshape(depth, MLA_KV_RANK, MLA_HEADS, MLA_NOPE + MLA_V)
    wk_pad = jnp.pad(wkv[..., :MLA_NOPE], ((0, 0), (0, 0), (0, 0), (0, LANES - MLA_NOPE)))
    wkvup = jnp.concatenate([wk_pad.reshape(depth, MLA_KV_RANK, MLA_HEADS * LANES),
                             wkv[..., MLA_NOPE:].reshape(depth, MLA_KV_RANK, MLA_HEADS * MLA_V)], axis=-1)
    split = [_split_gate_up(w_gu[l], b_gu[l]) for l in range(depth)]
    p = {
        "norm1_g": row(norm1_g), "norm2_g": row(norm2_g),
        "w1": w1, "wgate": w_in[:, :, _SPLIT_GATE:].astype(BF),
        "rope": _rope_tables(ls), "place": _placement(), "ones_bd": _block_diag_ones(512, GQA_HEAD_DIM),
        "ltri": jnp.asarray(np.tril(np.ones((TM, TM), np.float32), -1), BF),
        "upper": jnp.asarray(np.triu(np.ones((LANES, LANES), np.float32), 1), F32),
        "qn_g": row(jnp.tile(gqa_qn_g, (1, GQA_HEADS))), "kn_g": row(jnp.tile(gqa_kn_g, (1, GQA_KV_HEADS))),
        "cqn_g": row(mla_qn_g), "kvn_g": row(mla_kvn_g),
        "wqup": wqup.reshape(depth, MLA_Q_RANK, MLA_HEADS * LANES).astype(BF), "wkvup": wkvup.astype(BF),
        "conv_dw": conv_dw, "conv_dw_b": row(conv_dw_b), "conv_ln_g": row(conv_ln_g), "conv_ln_b": row(conv_ln_b),
        "w_conv_out": w_conv_out.astype(BF), "pool_w": pool_w.astype(BF), "pool_scale": row(pool_scale),
        "w_pool_out": w_pool_out.astype(BF), "w_gqa_out": w_gqa_out.astype(BF), "w_mla_out": w_mla_out.astype(BF),
        "w_o": w_o.astype(BF),
        "w_router": jnp.pad(w_router, ((0, 0), (0, 0), (0, LANES - N_EXPERTS))),
        "b_router": row(jnp.pad(b_router, ((0, 0), (0, LANES - N_EXPERTS)))),
        "w_gate": [t[0] for t in split], "w_up": [t[1] for t in split],
        "b_gate": [t[2] for t in split], "b_up": [t[3] for t in split],
        "w_dn": w_dn, "b_dn": b_dn.reshape(depth, N_EXPERTS, 1, d),
    }

    ckm, cvm = _cache_prep(cache_mla_ckv, jnp.pad(cache_mla_krope, ((0, 0), (0, 0), (0, 0), (0, LANES - MLA_ROPE))), p)
    cache = (cache_gqa_k.reshape(bs, depth, past, LANES).astype(BF),
             cache_gqa_v.reshape(bs, depth, past, LANES).astype(BF), ckm, cvm)

    x = jnp.concatenate([x_prompt.reshape(bp * seq, d), x_sample.reshape(bs * ls, d)], axis=0)
    n_ptok = bp * seq
    states = []
    for l in range(depth):
        a, b, q, k, v, qm, km, vm, ks, vs, ckvs, krs = _proj(l, x, mod, geo, p)
        states.append((ks[:n_ptok], vs[:n_ptok], ckvs[:n_ptok], krs[:n_ptok]))
        attn = _attention(l, geo, q, k, v, qm, km, vm, cache)
        x_mid, xs, slot_row, top_w, meta = _merge(l, x, mod, geo, a, b, attn, p)
        y = _experts(l, xs, _expert_plan(meta, nt), p, nt)
        x = _combine(l, l == depth - 1, y, x_mid, mod, slot_row, top_w, final_g.reshape(1, d), geo)

    y_prompt = x[:n_ptok].reshape(bp, seq, d)
    y_sample = x[n_ptok:].reshape(bs, ls, d)
    st = lambda j, shape: jnp.stack([s[j].reshape(shape) for s in states], axis=1)
    return (y_prompt, y_sample,
            st(0, (bp, seq, GQA_KV_HEADS, GQA_HEAD_DIM)), st(1, (bp, seq, GQA_KV_HEADS, GQA_HEAD_DIM)),
            st(2, (bp, seq, MLA_KV_RANK)), st(3, (bp, seq, MLA_ROPE)))
```

```python
import functools

import jax
import jax.numpy as jnp
import numpy as np
from jax import lax
from jax.experimental import pallas as pl
from jax.experimental.pallas import tpu as pltpu

D_MODEL = 1024
GRID_W = 64
CONV_W = 512
CONV_K = 31
POOL_W = 512
POOL_WINDOWS = (2, 4, 8, 16)
POOL_G = 128
GQA_HEADS = 8
GQA_KV_HEADS = 2
GQA_HEAD_DIM = 64
MLA_HEADS = 8
MLA_Q_RANK = 384
MLA_KV_RANK = 256
MLA_NOPE = 64
MLA_ROPE = 32
MLA_V = 64
ROPE_BASE = 10000.0
N_EXPERTS = 32
TOP_K = 4
D_FF = 1024
SWIGLU_LIMIT = 7.0
SWIGLU_ALPHA = 1.702
EPS = 1e-6
GQA_SCALE = GQA_HEAD_DIM ** -0.5
MLA_SCALE = (MLA_NOPE + MLA_ROPE) ** -0.5

LANES = 128
SUBLANES = 8
TM = 256
HALO = 16
MOE_BM = 256
R_TILE = 1280
W1_COLS = 3072
N_GATE = 4 * D_MODEL
VMEM_LIMIT = 56 * 1024 * 1024

BF = jnp.bfloat16
F32 = jnp.float32

_C_A, _C_B, _C_Q, _C_K, _C_V, _C_CQ, _C_CKV, _C_KR = 0, 1024, 1536, 2048, 2176, 2304, 2688, 2944
_SPLIT_GATE = 2976


def _dot(a, b):
    return jnp.dot(a, b, preferred_element_type=F32)


def _dot_nt(a, b):
    return lax.dot_general(a, b, (((1,), (1,)), ((), ())), preferred_element_type=F32)


def _rms(x):
    return x * lax.rsqrt(jnp.mean(x * x, axis=-1, keepdims=True) + EPS)


def _group_mean_sq(x, ones_bd, width):
    xx = x * x
    hi = xx.astype(BF)
    lo = (xx - hi.astype(F32)).astype(BF)
    return (_dot(hi, ones_bd) + _dot(lo, ones_bd)) * (1.0 / width)


def _tile_lanes(t, width):
    reps = width // LANES
    return t if reps == 1 else jnp.concatenate([t] * reps, axis=-1)


def _rope(x, cos, sin_a, sin_b, shift):
    w = x.shape[-1]
    return (x * _tile_lanes(cos, w) + pltpu.roll(x, w - shift, 1) * _tile_lanes(sin_a, w)
            + pltpu.roll(x, shift, 1) * _tile_lanes(sin_b, w))


def _modulated_norm(x, g, shift, scale):
    return _rms(x) * g * (1.0 + scale) + shift


def _mod_kernel(cond_ref, w_ref, b_ref, o_ref):
    c = cond_ref[...]
    s = (c * jax.nn.sigmoid(c)).astype(BF)
    o_ref[...] = _dot(s, w_ref[...].astype(BF)) + b_ref[...]


def _modulation(cond, w_mod, b_mod):
    depth, d, n = w_mod.shape
    rows = cond.shape[0]
    return pl.pallas_call(
        _mod_kernel,
        grid=(depth, n // D_MODEL),
        in_specs=[
            pl.BlockSpec((rows, d), lambda l, j: (0, 0)),
            pl.BlockSpec((None, d, D_MODEL), lambda l, j: (l, 0, j)),
            pl.BlockSpec((None, 1, D_MODEL), lambda l, j: (l, 0, j)),
        ],
        out_specs=pl.BlockSpec((None, rows, D_MODEL), lambda l, j: (l, 0, j)),
        out_shape=jax.ShapeDtypeStruct((depth, rows, n), F32),
        name="modulation",
    )(cond, w_mod, b_mod.reshape(depth, 1, n))


def _proj_kernel(x_ref, mod_ref, g1_ref, w1_ref, rope_ref, qn_ref, kn_ref, cqn_ref, kvn_ref,
                 wqup_ref, wkvup_ref, place_ref, ones_ref,
                 a_ref, b_ref, q_ref, k_ref, v_ref, qm_ref, km_ref, vm_ref,
                 ks_ref, vs_ref, ckvs_ref, krs_ref):
    x = x_ref[...]
    h = _modulated_norm(x, g1_ref[...], mod_ref[0:1, :], mod_ref[1:2, :])
    y = _dot(h.astype(BF), w1_ref[...])

    a_ref[...] = (y[:, _C_A:_C_A + CONV_W] * jax.nn.sigmoid(y[:, _C_A + CONV_W:_C_B])).astype(BF)
    b_ref[...] = y[:, _C_B:_C_Q].astype(BF)

    def tab(j):
        return rope_ref[:, j * LANES:(j + 1) * LANES]

    q = y[:, _C_Q:_C_K]
    q = q * lax.rsqrt(_group_mean_sq(q, ones_ref[...], GQA_HEAD_DIM) + EPS) * qn_ref[...]
    q = _rope(q, tab(0), tab(1), tab(2), GQA_HEAD_DIM // 4)
    q_ref[...] = (q * GQA_SCALE).astype(BF)

    k = y[:, _C_K:_C_V]
    k = k * lax.rsqrt(_group_mean_sq(k, ones_ref[0:LANES, 0:LANES], GQA_HEAD_DIM) + EPS) * kn_ref[...]
    ks_ref[...] = k
    k_ref[...] = _rope(k, tab(0), tab(1), tab(2), GQA_HEAD_DIM // 4).astype(BF)

    v = y[:, _C_V:_C_CQ]
    vs_ref[...] = v
    v_ref[...] = v.astype(BF)

    cq = _rms(y[:, _C_CQ:_C_CKV]) * cqn_ref[...]
    qm = _dot(cq.astype(BF), wqup_ref[...])
    qm = _rope(qm, tab(3), tab(4), tab(5), MLA_ROPE // 4)
    qm_ref[...] = (qm * MLA_SCALE).astype(BF)

    ckv = _rms(y[:, _C_CKV:_C_KR]) * kvn_ref[...]
    ckvs_ref[...] = ckv
    kv = _dot(ckv.astype(BF), wkvup_ref[...])
    kr = y[:, _C_KR:W1_COLS]
    krs_ref[...] = kr[:, 0:MLA_ROPE]
    kr_rot = _rope(kr, tab(6), tab(7), tab(8), MLA_ROPE // 4)
    km = kv[:, 0:MLA_HEADS * LANES] + _dot(kr_rot.astype(BF), place_ref[...])
    km_ref[...] = km.astype(BF)
    vm_ref[...] = kv[:, MLA_HEADS * LANES:].astype(BF)


def _const_spec(shape):
    nd = len(shape)
    return pl.BlockSpec(shape, lambda *_: (0,) * nd)


def _layer_spec(shape, l):
    nd = len(shape)
    return pl.BlockSpec((None,) + shape, lambda *_: (l,) + (0,) * nd)


def _proj(l, x, mod, geo, p):
    nt = geo["nt"]
    tok = nt * TM
    seq_row, rope_blk = geo["seq_row"], geo["rope_blk"]

    def tile(width, dtype):
        return pl.BlockSpec((TM, width), lambda i: (i, 0)), jax.ShapeDtypeStruct((tok, width), dtype)

    outs = [tile(CONV_W, BF), tile(POOL_W, BF), tile(512, BF), tile(LANES, BF), tile(LANES, BF),
            tile(MLA_HEADS * LANES, BF), tile(MLA_HEADS * LANES, BF), tile(MLA_HEADS * MLA_V, BF),
            tile(LANES, F32), tile(LANES, F32), tile(MLA_KV_RANK, F32), tile(MLA_ROPE, F32)]
    return pl.pallas_call(
        _proj_kernel,
        grid=(nt,),
        in_specs=[
            pl.BlockSpec((TM, D_MODEL), lambda i: (i, 0)),
            pl.BlockSpec((None, None, 6, D_MODEL), lambda i: (l, seq_row(i), 0, 0)),
            _layer_spec((1, D_MODEL), l),
            _layer_spec((D_MODEL, W1_COLS), l),
            pl.BlockSpec((TM, 9 * LANES), lambda i: (rope_blk(i), 0)),
            _layer_spec((1, 512), l),
            _layer_spec((1, LANES), l),
            _layer_spec((1, MLA_Q_RANK), l),
            _layer_spec((1, MLA_KV_RANK), l),
            _layer_spec((MLA_Q_RANK, MLA_HEADS * LANES), l),
            _layer_spec((MLA_KV_RANK, MLA_HEADS * (LANES + MLA_V)), l),
            _const_spec((LANES, MLA_HEADS * LANES)),
            _const_spec((512, 512)),
        ],
        out_specs=[o[0] for o in outs],
        out_shape=[o[1] for o in outs],
        compiler_params=pltpu.CompilerParams(dimension_semantics=("arbitrary",), vmem_limit_bytes=VMEM_LIMIT),
        name="proj",
    )(x, mod, p["norm1_g"], p["w1"], p["rope"], p["qn_g"], p["kn_g"], p["cqn_g"], p["kvn_g"],
      p["wqup"], p["wkvup"], p["place"], p["ones_bd"])


def _cache_kernel(ckv_ref, kr_ref, wkvup_ref, place_ref, km_ref, vm_ref):
    kv = _dot(ckv_ref[...].astype(BF), wkvup_ref[...])
    km = kv[:, 0:MLA_HEADS * LANES] + _dot(kr_ref[...].astype(BF), place_ref[...])
    km_ref[...] = km.astype(BF)
    vm_ref[...] = kv[:, MLA_HEADS * LANES:].astype(BF)


def _cache_prep(ckv, kr_pad, p):
    bs, depth, past, _ = ckv.shape
    return pl.pallas_call(
        _cache_kernel,
        grid=(bs, depth),
        in_specs=[
            pl.BlockSpec((None, None, past, MLA_KV_RANK), lambda b, l: (b, l, 0, 0)),
            pl.BlockSpec((None, None, past, LANES), lambda b, l: (b, l, 0, 0)),
            pl.BlockSpec((None, MLA_KV_RANK, MLA_HEADS * (LANES + MLA_V)), lambda b, l: (l, 0, 0)),
            pl.BlockSpec((LANES, MLA_HEADS * LANES), lambda b, l: (0, 0)),
        ],
        out_specs=[
            pl.BlockSpec((None, None, past, MLA_HEADS * LANES), lambda b, l: (b, l, 0, 0)),
            pl.BlockSpec((None, None, past, MLA_HEADS * MLA_V), lambda b, l: (b, l, 0, 0)),
        ],
        out_shape=[jax.ShapeDtypeStruct((bs, depth, past, MLA_HEADS * LANES), BF),
                   jax.ShapeDtypeStruct((bs, depth, past, MLA_HEADS * MLA_V), BF)],
        name="cache_prep",
    )(ckv, kr_pad, p["wkvup"], p["place"])


def _softmax_pv(s, v):
    m = jnp.max(s, axis=-1, keepdims=True)
    e = jnp.exp(s - m)
    l = jnp.sum(e, axis=-1, keepdims=True)
    return _dot(e.astype(BF), v) / l


def _attention_body(q, k, v, qm, km, vm, oc_ref, od_ref):
    lane_k = lax.broadcasted_iota(jnp.int32, k.shape, 1)
    lane_q = lax.broadcasted_iota(jnp.int32, (TM, LANES), 1)
    lo_k = lane_k < GQA_HEAD_DIM
    lo_q = lane_q < GQA_HEAD_DIM
    k32, v32 = k.astype(F32), v.astype(F32)
    k_sw = pltpu.roll(k32, GQA_HEAD_DIM, 1)
    v_sw = pltpu.roll(v32, GQA_HEAD_DIM, 1)
    k_dup = [jnp.where(lo_k, k32, k_sw).astype(BF), jnp.where(lo_k, k_sw, k32).astype(BF)]
    v_dup = [jnp.where(lo_k, v32, v_sw).astype(BF), jnp.where(lo_k, v_sw, v32).astype(BF)]
    zero = jnp.zeros((TM, LANES), BF)
    group = GQA_HEADS // GQA_KV_HEADS
    for j in range(GQA_HEADS // 2):
        qs = q[:, j * LANES:(j + 1) * LANES]
        g = (2 * j) // group
        o_lo = _softmax_pv(_dot_nt(jnp.where(lo_q, qs, zero), k_dup[g]), v_dup[g])
        o_hi = _softmax_pv(_dot_nt(jnp.where(lo_q, zero, qs), k_dup[g]), v_dup[g])
        oc_ref[:, j * LANES:(j + 1) * LANES] = jnp.where(lo_q, o_lo, o_hi).astype(BF)
    for j in range(MLA_HEADS // 2):
        vs = vm[:, j * LANES:(j + 1) * LANES]
        outs = []
        for h in (2 * j, 2 * j + 1):
            s = _dot_nt(qm[:, h * LANES:(h + 1) * LANES], km[:, h * LANES:(h + 1) * LANES])
            outs.append(_softmax_pv(s, vs))
        od_ref[:, j * LANES:(j + 1) * LANES] = jnp.where(lo_q, outs[0], outs[1]).astype(BF)


def _attn_prompt_kernel(q_ref, k_ref, v_ref, qm_ref, km_ref, vm_ref, oc_ref, od_ref):
    _attention_body(q_ref[...], k_ref[...], v_ref[...], qm_ref[...], km_ref[...], vm_ref[...], oc_ref, od_ref)


def _attn_sample_kernel(q_ref, k_ref, v_ref, qm_ref, km_ref, vm_ref, ck_ref, cv_ref, ckm_ref, cvm_ref,
                        oc_ref, od_ref):
    k = jnp.concatenate([ck_ref[...], k_ref[...]], axis=0)
    v = jnp.concatenate([cv_ref[...], v_ref[...]], axis=0)
    km = jnp.concatenate([ckm_ref[...], km_ref[...]], axis=0)
    vm = jnp.concatenate([cvm_ref[...], vm_ref[...]], axis=0)
    _attention_body(q_ref[...], k, v, qm_ref[...], km, vm, oc_ref, od_ref)


def _attention(l, geo, q, k, v, qm, km, vm, cache):
    n_p, bs, tps, nt = geo["np"], geo["bs"], geo["tps"], geo["nt"]
    tok = nt * TM
    ls = tps * TM
    off = (n_p * TM) // ls
    widths = (512, LANES, LANES, MLA_HEADS * LANES, MLA_HEADS * LANES, MLA_HEADS * MLA_V)
    params = pltpu.CompilerParams(dimension_semantics=("arbitrary",), vmem_limit_bytes=VMEM_LIMIT)
    oc_p, od_p = pl.pallas_call(
        _attn_prompt_kernel,
        grid=(n_p,),
        in_specs=[pl.BlockSpec((TM, w), lambda i: (i, 0)) for w in widths],
        out_specs=[pl.BlockSpec((TM, 512), lambda i: (i, 0))] * 2,
        out_shape=[jax.ShapeDtypeStruct((n_p * TM, 512), BF)] * 2,
        compiler_params=params,
        name="attn_prompt",
    )(q, k, v, qm, km, vm)
    ck, cv, ckm, cvm = cache
    past = ck.shape[2]
    q_spec = lambda w: pl.BlockSpec((TM, w), lambda b, j: (n_p + b * tps + j, 0))
    kv_spec = lambda w: pl.BlockSpec((ls, w), lambda b, j: (off + b, 0))
    c_spec = lambda w: pl.BlockSpec((None, None, past, w), lambda b, j: (b, l, 0, 0))
    params2 = pltpu.CompilerParams(dimension_semantics=("arbitrary", "arbitrary"), vmem_limit_bytes=VMEM_LIMIT)
    o_spec = pl.BlockSpec((TM, 512), lambda b, j: (b * tps + j, 0))
    oc_s, od_s = pl.pallas_call(
        _attn_sample_kernel,
        grid=(bs, tps),
        in_specs=[q_spec(512), kv_spec(LANES), kv_spec(LANES), q_spec(MLA_HEADS * LANES),
                  kv_spec(MLA_HEADS * LANES), kv_spec(MLA_HEADS * MLA_V),
                  c_spec(LANES), c_spec(LANES), c_spec(MLA_HEADS * LANES), c_spec(MLA_HEADS * MLA_V)],
        out_specs=[o_spec, o_spec],
        out_shape=[jax.ShapeDtypeStruct((bs * ls, 512), BF)] * 2,
        compiler_params=params2,
        name="attn_sample",
    )(q, k, v, qm, km, vm, ck, cv, ckm, cvm)
    return oc_p, od_p, oc_s, od_s


def _merge_kernel(geo, x_ref, mod_ref, a_ref, ap_ref, an_ref, b_ref, bp_ref, bn_ref,
                  ocp_ref, odp_ref, ocs_ref, ods_ref,
                  g1_ref, wg_ref, cw_ref, cb_ref, lng_ref, lnb_ref, wco_ref, pw_ref, ps_ref, wpo_ref,
                  wgo_ref, wmo_ref, wo_ref, g2_ref, wr_ref, br_ref,
                  ltri_ref, upper_ref,
                  xo_ref, xs_ref, rl_ref, tw_ref, meta_ref, abuf, bbuf, ashift):
    n_p, tps = geo["np"], geo["tps"]
    i = pl.program_id(0)
    j = jnp.where(i < n_p, 0, (i - n_p) % tps)
    n_seq_tiles = jnp.where(i < n_p, 1, tps)
    has_prev = j > 0
    has_next = j < n_seq_tiles - 1

    def fill(buf, cur, prev, nxt):
        buf[0:HALO, :] = jnp.where(has_prev, prev[...].astype(F32), 0.0)
        buf[HALO:HALO + TM, :] = cur[...].astype(F32)
        buf[HALO + TM:, :] = jnp.where(has_next, nxt[...].astype(F32), 0.0)

    fill(abuf, a_ref, ap_ref, an_ref)
    fill(bbuf, b_ref, bp_ref, bn_ref)

    sh_rows = TM + 2 * HALO - SUBLANES
    for sh in range(1, SUBLANES):
        ashift[sh - 1] = abuf[sh:sh + sh_rows, :]
    rows = 32
    conv = []
    for r0 in range(0, TM, rows):
        acc = jnp.zeros((rows, CONV_W), F32)
        for t in range(CONV_K):
            s = r0 + t + HALO - CONV_K // 2
            sh = s % SUBLANES
            tap = abuf[s:s + rows, :] if sh == 0 else ashift[sh - 1, s - sh:s - sh + rows, :]
            acc = acc + tap * cw_ref[t:t + 1, :]
        conv.append(acc)
    ca = jnp.concatenate(conv, axis=0) + cb_ref[...]
    mu = jnp.mean(ca, axis=-1, keepdims=True)
    xc = ca - mu
    ln = xc * lax.rsqrt(jnp.mean(xc * xc, axis=-1, keepdims=True) + EPS) * lng_ref[...] + lnb_ref[...]
    br_a = _dot((ln * jax.nn.sigmoid(ln)).astype(BF), wco_ref[...])

    pos = j * TM + lax.broadcasted_iota(jnp.int32, (TM, 1), 0)
    seq_len = n_seq_tiles * TM
    mixed = []
    for g, w in enumerate(POOL_WINDOWS):
        lo, hi = g * POOL_G, (g + 1) * POOL_G
        acc = jnp.zeros((TM, POOL_G), F32)
        for o in range(-(w // 2), w - w // 2):
            acc = acc + bbuf[HALO + o:HALO + o + TM, lo:hi]
        cnt = jnp.clip(pos - w // 2 + w, 0, seq_len) - jnp.clip(pos - w // 2, 0, seq_len)
        pooled = acc / cnt.astype(F32) - bbuf[HALO:HALO + TM, lo:hi]
        mixed.append(_dot(pooled.astype(BF), pw_ref[g]))
    pb = jnp.concatenate(mixed, axis=-1) * ps_ref[...]
    br_b = _dot(pb.astype(BF), wpo_ref[...])

    is_prompt = i < n_p
    br_c = _dot(jnp.where(is_prompt, ocp_ref[...], ocs_ref[...]), wgo_ref[...])
    br_d = _dot(jnp.where(is_prompt, odp_ref[...], ods_ref[...]), wmo_ref[...])

    x = x_ref[...]
    h = _modulated_norm(x, g1_ref[...], mod_ref[0:1, :], mod_ref[1:2, :])
    gates = jax.nn.sigmoid(_dot(h.astype(BF), wg_ref[...]))
    merged = (gates[:, 0:D_MODEL] * br_a + gates[:, D_MODEL:2 * D_MODEL] * br_b
              + gates[:, 2 * D_MODEL:3 * D_MODEL] * br_c + gates[:, 3 * D_MODEL:] * br_d)
    x = x + mod_ref[2:3, :] * _dot(merged.astype(BF), wo_ref[...])
    xo_ref[...] = x

    h2 = _modulated_norm(x, g2_ref[...], mod_ref[3:4, :], mod_ref[4:5, :])

    logits = jnp.dot(h2, wr_ref[...], preferred_element_type=F32, precision=lax.Precision.HIGHEST) + br_ref[...]
    lane = lax.broadcasted_iota(jnp.int32, (TM, LANES), 1).astype(F32)
    neg = jnp.float32(-jnp.inf)
    lg = jnp.where(lane < N_EXPERTS, logits, neg)
    vals, idxs = [], []
    for _ in range(TOP_K):
        m = jnp.max(lg, axis=-1, keepdims=True)
        idx = jnp.min(jnp.where(lg == m, lane, float(LANES)), axis=-1, keepdims=True)
        vals.append(m)
        idxs.append(idx)
        lg = jnp.where(lane == idx, neg, lg)
    exps = [jnp.exp(v - vals[0]) for v in vals]
    denom = exps[0] + exps[1] + exps[2] + exps[3]

    onehot = [(lane == idxs[kk]).astype(F32) for kk in range(TOP_K)]
    colsum = [jnp.sum(o, axis=0, keepdims=True) for o in onehot]
    cnt = colsum[0] + colsum[1] + colsum[2] + colsum[3]
    pad8 = jnp.floor((cnt + 7.0) * 0.125) * 8.0
    run_off = jnp.dot(jnp.broadcast_to(pad8, (8, LANES)), upper_ref[...], preferred_element_type=F32,
                      precision=lax.Precision.HIGHEST)[0:1, :]
    base = jnp.zeros((1, LANES), F32)
    rloc = []
    for kk in range(TOP_K):
        before = _dot(ltri_ref[...], onehot[kk].astype(BF))
        rloc.append(jnp.sum(onehot[kk] * (run_off + base + before), axis=-1, keepdims=True))
        base = base + colsum[kk]

    eye = (lax.broadcasted_iota(jnp.int32, (TM, TM), 0) == lax.broadcasted_iota(jnp.int32, (TM, TM), 1))
    r_sub = lax.broadcasted_iota(jnp.int32, (R_TILE, TM), 0).astype(F32)
    ones8 = jnp.ones((8, TM), F32)
    sel = jnp.zeros((R_TILE, TM), F32)
    for kk in range(TOP_K):
        row = jnp.dot(ones8, jnp.where(eye, rloc[kk], 0.0), preferred_element_type=F32,
                      precision=lax.Precision.HIGHEST)[0:1, :]
        sel = jnp.where(r_sub == row, 1.0, sel)
    xs_ref[...] = _dot(sel.astype(BF), h2.astype(BF))

    rl = jnp.zeros((TM, LANES), F32)
    tw = jnp.zeros((TM, LANES), F32)
    for kk in range(TOP_K):
        rl = jnp.where(lane == kk, rloc[kk], rl)
        tw = jnp.where(lane == kk, exps[kk] / denom, tw)
    rl_ref[...] = rl.astype(jnp.int32)
    tw_ref[...] = tw
    sub = lax.broadcasted_iota(jnp.int32, (8, LANES), 0)
    meta = jnp.where(sub == 0, jnp.broadcast_to(pad8, (8, LANES)),
                     jnp.where(sub == 1, jnp.broadcast_to(run_off, (8, LANES)), 0.0))
    meta_ref[...] = meta.astype(jnp.int32)


def _merge(l, x, mod, geo, a, b, attn, p):
    nt, n_p = geo["nt"], geo["np"]
    tok = nt * TM
    seq_row = geo["seq_row"]
    hb = TM // HALO
    last_hb = nt * hb - 1
    tile = lambda w: pl.BlockSpec((TM, w), lambda i: (i, 0))
    prev = lambda w: pl.BlockSpec((HALO, w), lambda i: (jnp.maximum(i * hb - 1, 0), 0))
    nxt = lambda w: pl.BlockSpec((HALO, w), lambda i: (jnp.minimum((i + 1) * hb, last_hb), 0))
    p_tile = pl.BlockSpec((TM, 512), lambda i: (jnp.minimum(i, n_p - 1), 0))
    s_tile = pl.BlockSpec((TM, 512), lambda i: (jnp.maximum(i - n_p, 0), 0))
    return pl.pallas_call(
        functools.partial(_merge_kernel, geo),
        grid=(nt,),
        in_specs=[
            tile(D_MODEL),
            pl.BlockSpec((None, None, 6, D_MODEL), lambda i: (l, seq_row(i), 0, 0)),
            tile(CONV_W), prev(CONV_W), nxt(CONV_W),
            tile(POOL_W), prev(POOL_W), nxt(POOL_W),
            p_tile, p_tile, s_tile, s_tile,
            _layer_spec((1, D_MODEL), l),
            _layer_spec((D_MODEL, N_GATE), l),
            _layer_spec((CONV_K, CONV_W), l),
            _layer_spec((1, CONV_W), l),
            _layer_spec((1, CONV_W), l),
            _layer_spec((1, CONV_W), l),
            _layer_spec((CONV_W, D_MODEL), l),
            _layer_spec((len(POOL_WINDOWS), POOL_G, POOL_G), l),
            _layer_spec((1, POOL_W), l),
            _layer_spec((POOL_W, D_MODEL), l),
            _layer_spec((512, D_MODEL), l),
            _layer_spec((512, D_MODEL), l),
            _layer_spec((D_MODEL, D_MODEL), l),
            _layer_spec((1, D_MODEL), l),
            _layer_spec((D_MODEL, LANES), l),
            _layer_spec((1, LANES), l),
            _const_spec((TM, TM)),
            _const_spec((LANES, LANES)),
        ],
        out_specs=[tile(D_MODEL), pl.BlockSpec((R_TILE, D_MODEL), lambda i: (i, 0)), tile(LANES), tile(LANES),
                   pl.BlockSpec((None, 8, LANES), lambda i: (i, 0, 0))],
        out_shape=[jax.ShapeDtypeStruct((tok, D_MODEL), F32), jax.ShapeDtypeStruct((nt * R_TILE, D_MODEL), F32),
                   jax.ShapeDtypeStruct((tok, LANES), jnp.int32), jax.ShapeDtypeStruct((tok, LANES), F32),
                   jax.ShapeDtypeStruct((nt, 8, LANES), jnp.int32)],
        scratch_shapes=[pltpu.VMEM((TM + 2 * HALO, CONV_W), F32), pltpu.VMEM((TM + 2 * HALO, POOL_W), F32),
                        pltpu.VMEM((SUBLANES - 1, TM + 2 * HALO - SUBLANES, CONV_W), F32)],
        compiler_params=pltpu.CompilerParams(dimension_semantics=("arbitrary",), vmem_limit_bytes=VMEM_LIMIT),
        name="merge",
    )(x, mod, a, a, a, b, b, b, *attn,
      p["norm1_g"], p["wgate"], p["conv_dw"], p["conv_dw_b"], p["conv_ln_g"], p["conv_ln_b"], p["w_conv_out"],
      p["pool_w"], p["pool_scale"], p["w_pool_out"], p["w_gqa_out"], p["w_mla_out"], p["w_o"],
      p["norm2_g"], p["w_router"], p["b_router"], p["ltri"], p["upper"])


def _expert_kernel(nt, blk_e, blk_row0, blk_t0, blk_t1, n_used_ref, pad8_ref, dst_ref, off_ref, tot8_ref,
                   xs_hbm, wgu_ref, bg_ref, bu_ref, wd_ref, bd_ref, perm_ref,
                   y_hbm, xg, yb, zbuf, wg_bf, wu_bf, wd_bf, cnt_smem, gsem, osem, zsem):
    i = pl.program_id(0)
    n_used = n_used_ref[0]
    slot = i % 2
    last = pl.num_programs(0) - 1

    def for_pieces(b, fn):
        e = blk_e[b]
        b0 = blk_row0[b]

        def body(t, tot):
            j = t * N_EXPERTS + e
            run0 = dst_ref[j]
            lo = jnp.maximum(run0, b0)
            hi = jnp.minimum(run0 + pad8_ref[j], b0 + MOE_BM)
            n = pl.multiple_of(jnp.maximum(hi - lo, 0), 8)

            @pl.when(n > 0)
            def _():
                fn(pl.multiple_of(t * R_TILE + off_ref[j] + lo - run0, 8), pl.multiple_of(lo - b0, 8), n)

            return tot + n

        return lax.fori_loop(blk_t0[b], blk_t1[b], body, jnp.int32(0))

    def rows_copy(src, dst, sem, n):
        return pltpu.make_async_copy(src.at[pl.ds(0, n), :], dst.at[pl.ds(0, n), :], sem)

    def start_gather(b, s):
        def piece(row_t, row_b, n):
            pltpu.make_async_copy(xs_hbm.at[pl.ds(row_t, n), :], xg.at[s].at[pl.ds(row_b, n), :], gsem.at[s]).start()

        cnt_smem[s] = for_pieces(b, piece)

    def tail_copy(t):
        n = pl.multiple_of(R_TILE - tot8_ref[t], 8)
        return n, pltpu.make_async_copy(
            zbuf.at[pl.ds(0, n), :], y_hbm.at[pl.ds(pl.multiple_of(t * R_TILE + tot8_ref[t], 8), n), :], zsem)

    @pl.when(i == 0)
    def _():
        xg[...] = jnp.zeros_like(xg)
        zbuf[...] = jnp.zeros_like(zbuf)
        for s in range(4):
            cnt_smem[s] = 0

        def fill(t, c):
            n, cp = tail_copy(t)

            @pl.when(n > 0)
            def _():
                cp.start()

            return c

        lax.fori_loop(0, nt, fill, 0)

        @pl.when(n_used > 0)
        def _():
            start_gather(0, 0)

    @pl.when(i + 1 < n_used)
    def _():
        start_gather(i + 1, 1 - slot)

    @pl.when(i < n_used)
    def _():
        e_changed = jnp.logical_or(i == 0, blk_e[i] != blk_e[jnp.maximum(i - 1, 0)])

        @pl.when(e_changed)
        def _():
            wd_bf[...] = wd_ref[...].astype(BF)
            for c in range(D_FF // LANES):
                pair = _dot(wgu_ref[:, 2 * c * LANES:2 * (c + 1) * LANES].astype(BF), perm_ref[...])
                wg_bf[:, c * LANES:(c + 1) * LANES] = pair[:, 0:LANES].astype(BF)
                wu_bf[:, c * LANES:(c + 1) * LANES] = pair[:, LANES:].astype(BF)

        n_in = pl.multiple_of(cnt_smem[slot], 8)
        rows_copy(xs_hbm, xg.at[slot], gsem.at[slot], n_in).wait()
        xb = xg[slot].astype(BF)
        gate = jnp.minimum(_dot(xb, wg_bf[...]) + bg_ref[...], SWIGLU_LIMIT)
        up = jnp.clip(_dot(xb, wu_bf[...]) + bu_ref[...], -SWIGLU_LIMIT, SWIGLU_LIMIT)
        glu = gate * jax.nn.sigmoid(gate * SWIGLU_ALPHA)
        y = _dot(((up + 1.0) * glu).astype(BF), wd_bf[...]) + bd_ref[...]

        n_prev = pl.multiple_of(cnt_smem[2 + slot], 8)

        @pl.when(n_prev > 0)
        def _():
            rows_copy(yb.at[slot], y_hbm, osem.at[slot], n_prev).wait()

        yb[slot] = y.astype(BF).astype(F32)

        def piece(row_t, row_b, n):
            pltpu.make_async_copy(yb.at[slot].at[pl.ds(row_b, n), :], y_hbm.at[pl.ds(row_t, n), :],
                                  osem.at[slot]).start()

        cnt_smem[2 + slot] = for_pieces(i, piece)

    @pl.when(i == last)
    def _():
        for s in range(2):
            n_out = pl.multiple_of(cnt_smem[2 + s], 8)

            @pl.when(n_out > 0)
            def _():
                rows_copy(yb.at[s], y_hbm, osem.at[s], n_out).wait()

        def drain(t, c):
            n, cp = tail_copy(t)

            @pl.when(n > 0)
            def _():
                cp.wait()

            return c

        lax.fori_loop(0, nt, drain, 0)


def _experts(l, xs, plan, p, nt):
    n_blocks = plan["blk_e"].shape[0]
    n_pref = 9
    by_expert = lambda *lead: (lambda i, be, *_: lead + (be[i], 0, 0))
    b_spec = pl.BlockSpec((None, None, 1, D_FF), by_expert(l))
    grid_spec = pltpu.PrefetchScalarGridSpec(
        num_scalar_prefetch=n_pref,
        grid=(n_blocks,),
        in_specs=[
            pl.BlockSpec(memory_space=pl.ANY),
            pl.BlockSpec((None, None, D_MODEL, 2 * D_FF), by_expert(l)),
            b_spec, b_spec,
            pl.BlockSpec((None, None, D_FF, D_MODEL), by_expert(l)),
            pl.BlockSpec((None, None, 1, D_MODEL), by_expert(l)),
            pl.BlockSpec((2 * LANES, 2 * LANES), lambda i, *_: (0, 0)),
        ],
        out_specs=pl.BlockSpec(memory_space=pl.ANY),
        scratch_shapes=[
            pltpu.VMEM((2, MOE_BM, D_MODEL), F32),
            pltpu.VMEM((2, MOE_BM, D_MODEL), F32),
            pltpu.VMEM((R_TILE - TOP_K * TM, D_MODEL), F32),
            pltpu.VMEM((D_MODEL, D_FF), BF),
            pltpu.VMEM((D_MODEL, D_FF), BF),
            pltpu.VMEM((D_FF, D_MODEL), BF),
            pltpu.SMEM((4,), jnp.int32),
            pltpu.SemaphoreType.DMA((2,)),
            pltpu.SemaphoreType.DMA((2,)),
            pltpu.SemaphoreType.DMA,
        ],
    )
    return pl.pallas_call(
        functools.partial(_expert_kernel, nt),
        grid_spec=grid_spec,
        out_shape=jax.ShapeDtypeStruct((nt * R_TILE, D_MODEL), F32),
        compiler_params=pltpu.CompilerParams(dimension_semantics=("arbitrary",), vmem_limit_bytes=VMEM_LIMIT),
        name="experts",
    )(plan["blk_e"], plan["blk_row0"], plan["blk_t0"], plan["blk_t1"], plan["n_used"], plan["pad8"], plan["dst"],
      plan["off"], plan["tot8"],
      xs, p["w_gu"], p["b_gate"], p["b_up"], p["w_dn"], p["b_dn"], p["pair_perm"])


def _pair_perm():
    m = np.zeros((2 * LANES, 2 * LANES), np.float32)
    j = np.arange(LANES)
    m[2 * j, j] = 1.0
    m[2 * j + 1, LANES + j] = 1.0
    return jnp.asarray(m, BF)


def _combine_kernel(final, y_ref, x_ref, mod_ref, rl_ref, tw_ref, fg_ref, o_ref):
    r_lane = lax.broadcasted_iota(jnp.int32, (TM, R_TILE), 1)
    rl = rl_ref[...]
    tw = tw_ref[...]
    sel = jnp.zeros((TM, R_TILE), F32)
    for k in range(TOP_K):
        sel = jnp.where(r_lane == rl[:, k:k + 1], tw[:, k:k + 1], sel)
    ffn = _dot(sel.astype(BF), y_ref[...].astype(BF))
    x = x_ref[...] + mod_ref[5:6, :] * ffn
    if final:
        x = _rms(x) * fg_ref[...]
    o_ref[...] = x


def _combine(l, final, y, x, mod, rl, tw, final_g, geo, tile0, n_tiles):
    seq_row = geo["seq_row"]
    tile = lambda w: pl.BlockSpec((TM, w), lambda i: (i + tile0, 0))
    return pl.pallas_call(
        functools.partial(_combine_kernel, final),
        grid=(n_tiles,),
        in_specs=[
            pl.BlockSpec((R_TILE, D_MODEL), lambda i: (i + tile0, 0)),
            tile(D_MODEL),
            pl.BlockSpec((None, None, 6, D_MODEL), lambda i: (l, seq_row(i + tile0), 0, 0)),
            tile(LANES),
            tile(LANES),
            _const_spec((1, D_MODEL)),
        ],
        out_specs=pl.BlockSpec((TM, D_MODEL), lambda i: (i, 0)),
        out_shape=jax.ShapeDtypeStruct((n_tiles * TM, D_MODEL), F32),
        compiler_params=pltpu.CompilerParams(dimension_semantics=("arbitrary",), vmem_limit_bytes=VMEM_LIMIT),
        name="combine",
    )(y, x, mod, rl, tw, final_g)


def _expert_plan(meta, nt):
    pad8 = meta[:, 0, :N_EXPERTS]
    off = meta[:, 1, :N_EXPERTS]
    ends = jnp.cumsum(pad8, axis=0)
    dst = ends - pad8
    tot = ends[-1]
    nb = (tot + MOE_BM - 1) // MOE_BM
    nb_end = jnp.cumsum(nb)
    n_blocks = (nt * TM * TOP_K + nt * N_EXPERTS * 7) // MOE_BM + N_EXPERTS
    b = jnp.arange(n_blocks, dtype=jnp.int32)
    blk_e = jnp.minimum(jnp.sum(nb_end[None, :] <= b[:, None], axis=1), N_EXPERTS - 1).astype(jnp.int32)
    blk_row0 = (b - (nb_end - nb)[blk_e]) * MOE_BM
    ends_b = ends[:, blk_e]
    dst_b = dst[:, blk_e]
    blk_t0 = jnp.sum(ends_b <= blk_row0[None, :], axis=0)
    blk_t1 = jnp.sum(dst_b < blk_row0[None, :] + MOE_BM, axis=0)
    i32 = lambda v: v.astype(jnp.int32)
    return {"blk_e": blk_e, "blk_row0": i32(blk_row0), "blk_t0": i32(blk_t0), "blk_t1": i32(blk_t1),
            "n_used": i32(nb_end[-1]).reshape(1), "pad8": i32(pad8.reshape(-1)), "dst": i32(dst.reshape(-1)),
            "off": i32(off.reshape(-1)), "tot8": i32(jnp.sum(pad8, axis=1))}


def _rope_tables(n_pos):
    pos = np.arange(n_pos)
    row, col = pos // GRID_W, pos % GRID_W
    lane = np.arange(LANES)

    def build(active, r, half):
        n_rot = 4 * half
        is_col = (r % n_rot) >= 2 * half
        rr = r % (2 * half)
        freq = np.power(ROPE_BASE, -(rr % half).astype(np.float64) / half)
        p = np.where(is_col[None, :], col[:, None], row[:, None]).astype(np.float64)
        ang = p * freq[None, :]
        first = rr < half
        cos = np.where(active[None, :], np.cos(ang), 1.0)
        sin_a = np.where((active & first)[None, :], -np.sin(ang), 0.0)
        sin_b = np.where((active & ~first)[None, :], np.sin(ang), 0.0)
        return [cos, sin_a, sin_b]

    tabs = build(np.ones(LANES, bool), lane % GQA_HEAD_DIM, GQA_HEAD_DIM // 4)
    in_rope = (lane >= MLA_NOPE) & (lane < MLA_NOPE + MLA_ROPE)
    tabs += build(in_rope, np.maximum(lane - MLA_NOPE, 0) % MLA_ROPE, MLA_ROPE // 4)
    tabs += build(lane < MLA_ROPE, lane % MLA_ROPE, MLA_ROPE // 4)
    table = np.concatenate(tabs, axis=1)
    ident = np.concatenate([np.ones((TM, LANES)), np.zeros((TM, LANES)), np.zeros((TM, LANES))] * 3, axis=1)
    return jnp.asarray(np.concatenate([ident, table], axis=0), F32)


def _placement():
    e = np.zeros((LANES, MLA_HEADS * LANES), np.float32)
    for h in range(MLA_HEADS):
        for r in range(MLA_ROPE):
            e[r, h * LANES + MLA_NOPE + r] = 1.0
    return jnp.asarray(e, BF)


def _block_diag_ones(n, g):
    idx = np.arange(n) // g
    return jnp.asarray((idx[:, None] == idx[None, :]).astype(np.float32), BF)


def kernel(x_prompt, x_sample, cache_gqa_k, cache_gqa_v, cache_mla_ckv, cache_mla_krope, c, c_ctx, norm1_g, norm2_g, w_mod, b_mod, w_in, conv_dw, conv_dw_b, conv_ln_g, conv_ln_b, w_conv_out, pool_w, pool_scale, w_pool_out, gqa_qn_g, gqa_kn_g, w_gqa_out, mla_qn_g, w_mla_q_up, mla_kvn_g, w_mla_kv_up, w_mla_out, w_o, w_router, b_router, w_gu, b_gu, w_dn, b_dn, final_g):
    bp, seq, d = x_prompt.shape
    bs, ls, _ = x_sample.shape
    depth = w_in.shape[0]
    past = cache_gqa_k.shape[2]
    assert seq == TM and d == D_MODEL and ls % TM == 0 and (bp * seq) % ls == 0
    tps = ls // TM
    n_p = bp
    nt = n_p + bs * tps
    geo = {
        "np": n_p, "bs": bs, "tps": tps, "nt": nt,
        "seq_row": lambda i: jnp.where(i < n_p, 0, 1 + (i - n_p) // tps),
        "rope_blk": lambda i: jnp.where(i < n_p, 0, 1 + (i - n_p) % tps),
    }

    n_cond = -(-(1 + bs) // 8) * 8
    cond = jnp.zeros((n_cond, d), F32).at[0].set(c_ctx).at[1:1 + bs].set(c)
    mod = _modulation(cond, w_mod, b_mod).reshape(depth, n_cond, 6, d)

    row = lambda v: v.reshape(depth, 1, -1)
    w1 = jnp.pad(w_in[:, :, :_SPLIT_GATE], ((0, 0), (0, 0), (0, W1_COLS - _SPLIT_GATE))).astype(BF)
    wqup = jnp.pad(w_mla_q_up.reshape(depth, MLA_Q_RANK, MLA_HEADS, MLA_NOPE + MLA_ROPE),
                   ((0, 0), (0, 0), (0, 0), (0, LANES - MLA_NOPE - MLA_ROPE)))
    wkv = w_mla_kv_up.reshape(depth, MLA_KV_RANK, MLA_HEADS, MLA_NOPE + MLA_V)
    wk_pad = jnp.pad(wkv[..., :MLA_NOPE], ((0, 0), (0, 0), (0, 0), (0, LANES - MLA_NOPE)))
    wkvup = jnp.concatenate([wk_pad.reshape(depth, MLA_KV_RANK, MLA_HEADS * LANES),
                             wkv[..., MLA_NOPE:].reshape(depth, MLA_KV_RANK, MLA_HEADS * MLA_V)], axis=-1)
    bgu = b_gu.reshape(depth, N_EXPERTS, 1, D_FF, 2)
    p = {
        "norm1_g": row(norm1_g), "norm2_g": row(norm2_g),
        "w1": w1, "wgate": w_in[:, :, _SPLIT_GATE:].astype(BF),
        "rope": _rope_tables(ls), "place": _placement(), "ones_bd": _block_diag_ones(512, GQA_HEAD_DIM),
        "ltri": jnp.asarray(np.tril(np.ones((TM, TM), np.float32), -1), BF),
        "upper": jnp.asarray(np.triu(np.ones((LANES, LANES), np.float32), 1), F32),
        "qn_g": row(jnp.tile(gqa_qn_g, (1, GQA_HEADS))), "kn_g": row(jnp.tile(gqa_kn_g, (1, GQA_KV_HEADS))),
        "cqn_g": row(mla_qn_g), "kvn_g": row(mla_kvn_g),
        "wqup": wqup.reshape(depth, MLA_Q_RANK, MLA_HEADS * LANES).astype(BF), "wkvup": wkvup.astype(BF),
        "conv_dw": conv_dw, "conv_dw_b": row(conv_dw_b), "conv_ln_g": row(conv_ln_g), "conv_ln_b": row(conv_ln_b),
        "w_conv_out": w_conv_out.astype(BF), "pool_w": pool_w.astype(BF), "pool_scale": row(pool_scale),
        "w_pool_out": w_pool_out.astype(BF), "w_gqa_out": w_gqa_out.astype(BF), "w_mla_out": w_mla_out.astype(BF),
        "w_o": w_o.astype(BF),
        "w_router": jnp.pad(w_router, ((0, 0), (0, 0), (0, LANES - N_EXPERTS))),
        "b_router": row(jnp.pad(b_router, ((0, 0), (0, LANES - N_EXPERTS)))),
        "w_gu": w_gu, "b_gate": bgu[..., 0], "b_up": bgu[..., 1], "pair_perm": _pair_perm(),
        "w_dn": w_dn, "b_dn": b_dn.reshape(depth, N_EXPERTS, 1, d),
    }

    ckm, cvm = _cache_prep(cache_mla_ckv, jnp.pad(cache_mla_krope, ((0, 0), (0, 0), (0, 0), (0, LANES - MLA_ROPE))), p)
    cache = (cache_gqa_k.reshape(bs, depth, past, LANES).astype(BF),
             cache_gqa_v.reshape(bs, depth, past, LANES).astype(BF), ckm, cvm)

    x = jnp.concatenate([x_prompt.reshape(bp * seq, d), x_sample.reshape(bs * ls, d)], axis=0)
    n_ptok = bp * seq
    states = []
    for l in range(depth):
        a, b, q, k, v, qm, km, vm, ks, vs, ckvs, krs = _proj(l, x, mod, geo, p)
        states.append((ks[:n_ptok], vs[:n_ptok], ckvs[:n_ptok], krs[:n_ptok]))
        attn = _attention(l, geo, q, k, v, qm, km, vm, cache)
        x_mid, xs, slot_row, top_w, meta = _merge(l, x, mod, geo, a, b, attn, p)
        y = _experts(l, xs, _expert_plan(meta, nt), p, nt)
        combine = functools.partial(_combine, l, l == depth - 1, y, x_mid, mod, slot_row, top_w,
                                    final_g.reshape(1, d), geo)
        if l < depth - 1:
            x = combine(0, nt)

    y_prompt = combine(0, n_p).reshape(bp, seq, d)
    y_sample = combine(n_p, nt - n_p).reshape(bs, ls, d)
    st = lambda j, shape: jnp.stack([s[j].reshape(shape) for s in states], axis=1)
    return (y_prompt, y_sample,
            st(0, (bp, seq, GQA_KV_HEADS, GQA_HEAD_DIM)), st(1, (bp, seq, GQA_KV_HEADS, GQA_HEAD_DIM)),
            st(2, (bp, seq, MLA_KV_RANK)), st(3, (bp, seq, MLA_ROPE)))
```

```python
import functools

import jax
import jax.numpy as jnp
import numpy as np
from jax import lax
from jax.experimental import pallas as pl
from jax.experimental.pallas import tpu as pltpu

D_MODEL = 1024
GRID_W = 64
CONV_W = 512
CONV_K = 31
POOL_W = 512
POOL_WINDOWS = (2, 4, 8, 16)
POOL_G = 128
GQA_HEADS = 8
GQA_KV_HEADS = 2
GQA_HEAD_DIM = 64
MLA_HEADS = 8
MLA_Q_RANK = 384
MLA_KV_RANK = 256
MLA_NOPE = 64
MLA_ROPE = 32
MLA_V = 64
ROPE_BASE = 10000.0
N_EXPERTS = 32
TOP_K = 4
D_FF = 1024
SWIGLU_LIMIT = 7.0
SWIGLU_ALPHA = 1.702
EPS = 1e-6
GQA_SCALE = GQA_HEAD_DIM ** -0.5
MLA_SCALE = (MLA_NOPE + MLA_ROPE) ** -0.5

LANES = 128
SUBLANES = 8
TM = 256
HALO = 16
MOE_BM = 512
R_TILE = 1280
W1_COLS = 3072
N_GATE = 4 * D_MODEL
VMEM_LIMIT = 56 * 1024 * 1024

BF = jnp.bfloat16
F32 = jnp.float32

_C_A, _C_B, _C_Q, _C_K, _C_V, _C_CQ, _C_CKV, _C_KR = 0, 1024, 1536, 2048, 2176, 2304, 2688, 2944
_SPLIT_GATE = 2976


def _dot(a, b):
    return jnp.dot(a, b, preferred_element_type=F32)


def _dot_nt(a, b):
    return lax.dot_general(a, b, (((1,), (1,)), ((), ())), preferred_element_type=F32)


def _rms(x):
    return x * lax.rsqrt(jnp.mean(x * x, axis=-1, keepdims=True) + EPS)


def _group_mean_sq(x, ones_bd, width):
    xx = x * x
    hi = xx.astype(BF)
    lo = (xx - hi.astype(F32)).astype(BF)
    return (_dot(hi, ones_bd) + _dot(lo, ones_bd)) * (1.0 / width)


def _tile_lanes(t, width):
    reps = width // LANES
    return t if reps == 1 else jnp.concatenate([t] * reps, axis=-1)


def _rope(x, cos, sin_a, sin_b, shift):
    w = x.shape[-1]
    return (x * _tile_lanes(cos, w) + pltpu.roll(x, w - shift, 1) * _tile_lanes(sin_a, w)
            + pltpu.roll(x, shift, 1) * _tile_lanes(sin_b, w))


def _modulated_norm(x, g, shift, scale):
    return _rms(x) * g * (1.0 + scale) + shift


def _mod_kernel(cond_ref, w_ref, b_ref, o_ref):
    c = cond_ref[...]
    s = (c * jax.nn.sigmoid(c)).astype(BF)
    o_ref[...] = _dot(s, w_ref[...].astype(BF)) + b_ref[...]


def _modulation(cond, w_mod, b_mod):
    depth, d, n = w_mod.shape
    rows = cond.shape[0]
    return pl.pallas_call(
        _mod_kernel,
        grid=(depth, n // D_MODEL),
        in_specs=[
            pl.BlockSpec((rows, d), lambda l, j: (0, 0)),
            pl.BlockSpec((None, d, D_MODEL), lambda l, j: (l, 0, j)),
            pl.BlockSpec((None, 1, D_MODEL), lambda l, j: (l, 0, j)),
        ],
        out_specs=pl.BlockSpec((None, rows, D_MODEL), lambda l, j: (l, 0, j)),
        out_shape=jax.ShapeDtypeStruct((depth, rows, n), F32),
        name="modulation",
    )(cond, w_mod, b_mod.reshape(depth, 1, n))


def _proj_kernel(x_ref, mod_ref, g1_ref, w1_ref, rope_ref, qn_ref, kn_ref, cqn_ref, kvn_ref,
                 wqup_ref, wkvup_ref, place_ref, ones_ref,
                 a_ref, b_ref, q_ref, k_ref, v_ref, qm_ref, km_ref, vm_ref,
                 ks_ref, vs_ref, ckvs_ref, krs_ref):
    x = x_ref[...]
    h = _modulated_norm(x, g1_ref[...], mod_ref[0:1, :], mod_ref[1:2, :])
    y = _dot(h.astype(BF), w1_ref[...])

    a_ref[...] = (y[:, _C_A:_C_A + CONV_W] * jax.nn.sigmoid(y[:, _C_A + CONV_W:_C_B])).astype(BF)
    b_ref[...] = y[:, _C_B:_C_Q].astype(BF)

    def tab(j):
        return rope_ref[:, j * LANES:(j + 1) * LANES]

    q = y[:, _C_Q:_C_K]
    q = q * lax.rsqrt(_group_mean_sq(q, ones_ref[...], GQA_HEAD_DIM) + EPS) * qn_ref[...]
    q = _rope(q, tab(0), tab(1), tab(2), GQA_HEAD_DIM // 4)
    q_ref[...] = (q * GQA_SCALE).astype(BF)

    k = y[:, _C_K:_C_V]
    k = k * lax.rsqrt(_group_mean_sq(k, ones_ref[0:LANES, 0:LANES], GQA_HEAD_DIM) + EPS) * kn_ref[...]
    ks_ref[...] = k
    k_ref[...] = _rope(k, tab(0), tab(1), tab(2), GQA_HEAD_DIM // 4).astype(BF)

    v = y[:, _C_V:_C_CQ]
    vs_ref[...] = v
    v_ref[...] = v.astype(BF)

    cq = _rms(y[:, _C_CQ:_C_CKV]) * cqn_ref[...]
    qm = _dot(cq.astype(BF), wqup_ref[...])
    qm = _rope(qm, tab(3), tab(4), tab(5), MLA_ROPE // 4)
    qm_ref[...] = (qm * MLA_SCALE).astype(BF)

    ckv = _rms(y[:, _C_CKV:_C_KR]) * kvn_ref[...]
    ckvs_ref[...] = ckv
    kv = _dot(ckv.astype(BF), wkvup_ref[...])
    kr = y[:, _C_KR:W1_COLS]
    krs_ref[...] = kr[:, 0:MLA_ROPE]
    kr_rot = _rope(kr, tab(6), tab(7), tab(8), MLA_ROPE // 4)
    km = kv[:, 0:MLA_HEADS * LANES] + _dot(kr_rot.astype(BF), place_ref[...])
    km_ref[...] = km.astype(BF)
    vm_ref[...] = kv[:, MLA_HEADS * LANES:].astype(BF)


def _const_spec(shape):
    nd = len(shape)
    return pl.BlockSpec(shape, lambda *_: (0,) * nd)


def _layer_spec(shape, l):
    nd = len(shape)
    return pl.BlockSpec((None,) + shape, lambda *_: (l,) + (0,) * nd)


def _proj(l, x, mod, geo, p):
    nt = geo["nt"]
    tok = nt * TM
    seq_row, rope_blk = geo["seq_row"], geo["rope_blk"]

    def tile(width, dtype):
        return pl.BlockSpec((TM, width), lambda i: (i, 0)), jax.ShapeDtypeStruct((tok, width), dtype)

    outs = [tile(CONV_W, BF), tile(POOL_W, BF), tile(512, BF), tile(LANES, BF), tile(LANES, BF),
            tile(MLA_HEADS * LANES, BF), tile(MLA_HEADS * LANES, BF), tile(MLA_HEADS * MLA_V, BF),
            tile(LANES, F32), tile(LANES, F32), tile(MLA_KV_RANK, F32), tile(MLA_ROPE, F32)]
    return pl.pallas_call(
        _proj_kernel,
        grid=(nt,),
        in_specs=[
            pl.BlockSpec((TM, D_MODEL), lambda i: (i, 0)),
            pl.BlockSpec((None, None, 6, D_MODEL), lambda i: (l, seq_row(i), 0, 0)),
            _layer_spec((1, D_MODEL), l),
            _layer_spec((D_MODEL, W1_COLS), l),
            pl.BlockSpec((TM, 9 * LANES), lambda i: (rope_blk(i), 0)),
            _layer_spec((1, 512), l),
            _layer_spec((1, LANES), l),
            _layer_spec((1, MLA_Q_RANK), l),
            _layer_spec((1, MLA_KV_RANK), l),
            _layer_spec((MLA_Q_RANK, MLA_HEADS * LANES), l),
            _layer_spec((MLA_KV_RANK, MLA_HEADS * (LANES + MLA_V)), l),
            _const_spec((LANES, MLA_HEADS * LANES)),
            _const_spec((512, 512)),
        ],
        out_specs=[o[0] for o in outs],
        out_shape=[o[1] for o in outs],
        compiler_params=pltpu.CompilerParams(dimension_semantics=("arbitrary",), vmem_limit_bytes=VMEM_LIMIT),
        name="proj",
    )(x, mod, p["norm1_g"], p["w1"], p["rope"], p["qn_g"], p["kn_g"], p["cqn_g"], p["kvn_g"],
      p["wqup"], p["wkvup"], p["place"], p["ones_bd"])


def _cache_kernel(ckv_ref, kr_ref, wkvup_ref, place_ref, km_ref, vm_ref):
    kv = _dot(ckv_ref[...].astype(BF), wkvup_ref[...])
    km = kv[:, 0:MLA_HEADS * LANES] + _dot(kr_ref[...].astype(BF), place_ref[...])
    km_ref[...] = km.astype(BF)
    vm_ref[...] = kv[:, MLA_HEADS * LANES:].astype(BF)


def _cache_prep(ckv, kr_pad, p):
    bs, depth, past, _ = ckv.shape
    return pl.pallas_call(
        _cache_kernel,
        grid=(bs, depth),
        in_specs=[
            pl.BlockSpec((None, None, past, MLA_KV_RANK), lambda b, l: (b, l, 0, 0)),
            pl.BlockSpec((None, None, past, LANES), lambda b, l: (b, l, 0, 0)),
            pl.BlockSpec((None, MLA_KV_RANK, MLA_HEADS * (LANES + MLA_V)), lambda b, l: (l, 0, 0)),
            pl.BlockSpec((LANES, MLA_HEADS * LANES), lambda b, l: (0, 0)),
        ],
        out_specs=[
            pl.BlockSpec((None, None, past, MLA_HEADS * LANES), lambda b, l: (b, l, 0, 0)),
            pl.BlockSpec((None, None, past, MLA_HEADS * MLA_V), lambda b, l: (b, l, 0, 0)),
        ],
        out_shape=[jax.ShapeDtypeStruct((bs, depth, past, MLA_HEADS * LANES), BF),
                   jax.ShapeDtypeStruct((bs, depth, past, MLA_HEADS * MLA_V), BF)],
        name="cache_prep",
    )(ckv, kr_pad, p["wkvup"], p["place"])


def _softmax_pv(s, v):
    m = jnp.max(s, axis=-1, keepdims=True)
    e = jnp.exp(s - m)
    l = jnp.sum(e, axis=-1, keepdims=True)
    return _dot(e.astype(BF), v) / l


def _attention_body(q, k, v, qm, km, vm, oc_ref, od_ref):
    lane_k = lax.broadcasted_iota(jnp.int32, k.shape, 1)
    lane_q = lax.broadcasted_iota(jnp.int32, (TM, LANES), 1)
    lo_k = lane_k < GQA_HEAD_DIM
    lo_q = lane_q < GQA_HEAD_DIM
    k32, v32 = k.astype(F32), v.astype(F32)
    k_sw = pltpu.roll(k32, GQA_HEAD_DIM, 1)
    v_sw = pltpu.roll(v32, GQA_HEAD_DIM, 1)
    k_dup = [jnp.where(lo_k, k32, k_sw).astype(BF), jnp.where(lo_k, k_sw, k32).astype(BF)]
    v_dup = [jnp.where(lo_k, v32, v_sw).astype(BF), jnp.where(lo_k, v_sw, v32).astype(BF)]
    zero = jnp.zeros((TM, LANES), BF)
    group = GQA_HEADS // GQA_KV_HEADS
    for j in range(GQA_HEADS // 2):
        qs = q[:, j * LANES:(j + 1) * LANES]
        g = (2 * j) // group
        o_lo = _softmax_pv(_dot_nt(jnp.where(lo_q, qs, zero), k_dup[g]), v_dup[g])
        o_hi = _softmax_pv(_dot_nt(jnp.where(lo_q, zero, qs), k_dup[g]), v_dup[g])
        oc_ref[:, j * LANES:(j + 1) * LANES] = jnp.where(lo_q, o_lo, o_hi).astype(BF)
    for j in range(MLA_HEADS // 2):
        vs = vm[:, j * LANES:(j + 1) * LANES]
        outs = []
        for h in (2 * j, 2 * j + 1):
            s = _dot_nt(qm[:, h * LANES:(h + 1) * LANES], km[:, h * LANES:(h + 1) * LANES])
            outs.append(_softmax_pv(s, vs))
        od_ref[:, j * LANES:(j + 1) * LANES] = jnp.where(lo_q, outs[0], outs[1]).astype(BF)


def _attn_prompt_kernel(q_ref, k_ref, v_ref, qm_ref, km_ref, vm_ref, oc_ref, od_ref):
    _attention_body(q_ref[...], k_ref[...], v_ref[...], qm_ref[...], km_ref[...], vm_ref[...], oc_ref, od_ref)


def _attn_sample_kernel(q_ref, k_ref, v_ref, qm_ref, km_ref, vm_ref, ck_ref, cv_ref, ckm_ref, cvm_ref,
                        oc_ref, od_ref):
    k = jnp.concatenate([ck_ref[...], k_ref[...]], axis=0)
    v = jnp.concatenate([cv_ref[...], v_ref[...]], axis=0)
    km = jnp.concatenate([ckm_ref[...], km_ref[...]], axis=0)
    vm = jnp.concatenate([cvm_ref[...], vm_ref[...]], axis=0)
    _attention_body(q_ref[...], k, v, qm_ref[...], km, vm, oc_ref, od_ref)


def _attention(l, geo, q, k, v, qm, km, vm, cache):
    n_p, bs, tps, nt = geo["np"], geo["bs"], geo["tps"], geo["nt"]
    tok = nt * TM
    ls = tps * TM
    off = (n_p * TM) // ls
    widths = (512, LANES, LANES, MLA_HEADS * LANES, MLA_HEADS * LANES, MLA_HEADS * MLA_V)
    params = pltpu.CompilerParams(dimension_semantics=("arbitrary",), vmem_limit_bytes=VMEM_LIMIT)
    oc_p, od_p = pl.pallas_call(
        _attn_prompt_kernel,
        grid=(n_p,),
        in_specs=[pl.BlockSpec((TM, w), lambda i: (i, 0)) for w in widths],
        out_specs=[pl.BlockSpec((TM, 512), lambda i: (i, 0))] * 2,
        out_shape=[jax.ShapeDtypeStruct((n_p * TM, 512), BF)] * 2,
        compiler_params=params,
        name="attn_prompt",
    )(q, k, v, qm, km, vm)
    ck, cv, ckm, cvm = cache
    past = ck.shape[2]
    q_spec = lambda w: pl.BlockSpec((TM, w), lambda b, j: (n_p + b * tps + j, 0))
    kv_spec = lambda w: pl.BlockSpec((ls, w), lambda b, j: (off + b, 0))
    c_spec = lambda w: pl.BlockSpec((None, None, past, w), lambda b, j: (b, l, 0, 0))
    params2 = pltpu.CompilerParams(dimension_semantics=("arbitrary", "arbitrary"), vmem_limit_bytes=VMEM_LIMIT)
    o_spec = pl.BlockSpec((TM, 512), lambda b, j: (b * tps + j, 0))
    oc_s, od_s = pl.pallas_call(
        _attn_sample_kernel,
        grid=(bs, tps),
        in_specs=[q_spec(512), kv_spec(LANES), kv_spec(LANES), q_spec(MLA_HEADS * LANES),
                  kv_spec(MLA_HEADS * LANES), kv_spec(MLA_HEADS * MLA_V),
                  c_spec(LANES), c_spec(LANES), c_spec(MLA_HEADS * LANES), c_spec(MLA_HEADS * MLA_V)],
        out_specs=[o_spec, o_spec],
        out_shape=[jax.ShapeDtypeStruct((bs * ls, 512), BF)] * 2,
        compiler_params=params2,
        name="attn_sample",
    )(q, k, v, qm, km, vm, ck, cv, ckm, cvm)
    return oc_p, od_p, oc_s, od_s


def _merge_kernel(geo, x_ref, mod_ref, a_ref, ap_ref, an_ref, b_ref, bp_ref, bn_ref,
                  ocp_ref, odp_ref, ocs_ref, ods_ref,
                  g1_ref, wg_ref, cw_ref, cb_ref, lng_ref, lnb_ref, wco_ref, pw_ref, ps_ref, wpo_ref,
                  wgo_ref, wmo_ref, wo_ref, g2_ref, wr_ref, br_ref,
                  ltri_ref, upper_ref,
                  xo_ref, xs_ref, rl_ref, tw_ref, meta_ref, abuf, bbuf, ashift):
    n_p, tps = geo["np"], geo["tps"]
    i = pl.program_id(0)
    j = jnp.where(i < n_p, 0, (i - n_p) % tps)
    n_seq_tiles = jnp.where(i < n_p, 1, tps)
    has_prev = j > 0
    has_next = j < n_seq_tiles - 1

    def fill(buf, cur, prev, nxt):
        buf[0:HALO, :] = jnp.where(has_prev, prev[...].astype(F32), 0.0)
        buf[HALO:HALO + TM, :] = cur[...].astype(F32)
        buf[HALO + TM:, :] = jnp.where(has_next, nxt[...].astype(F32), 0.0)

    fill(abuf, a_ref, ap_ref, an_ref)
    fill(bbuf, b_ref, bp_ref, bn_ref)

    sh_rows = TM + 2 * HALO - SUBLANES
    for sh in range(1, SUBLANES):
        ashift[sh - 1] = abuf[sh:sh + sh_rows, :]
    rows = 32
    conv = []
    for r0 in range(0, TM, rows):
        acc = jnp.zeros((rows, CONV_W), F32)
        for t in range(CONV_K):
            s = r0 + t + HALO - CONV_K // 2
            sh = s % SUBLANES
            tap = abuf[s:s + rows, :] if sh == 0 else ashift[sh - 1, s - sh:s - sh + rows, :]
            acc = acc + tap * cw_ref[t:t + 1, :]
        conv.append(acc)
    ca = jnp.concatenate(conv, axis=0) + cb_ref[...]
    mu = jnp.mean(ca, axis=-1, keepdims=True)
    xc = ca - mu
    ln = xc * lax.rsqrt(jnp.mean(xc * xc, axis=-1, keepdims=True) + EPS) * lng_ref[...] + lnb_ref[...]
    br_a = _dot((ln * jax.nn.sigmoid(ln)).astype(BF), wco_ref[...])

    pos = j * TM + lax.broadcasted_iota(jnp.int32, (TM, 1), 0)
    seq_len = n_seq_tiles * TM
    mixed = []
    for g, w in enumerate(POOL_WINDOWS):
        lo, hi = g * POOL_G, (g + 1) * POOL_G
        acc = jnp.zeros((TM, POOL_G), F32)
        for o in range(-(w // 2), w - w // 2):
            acc = acc + bbuf[HALO + o:HALO + o + TM, lo:hi]
        cnt = jnp.clip(pos - w // 2 + w, 0, seq_len) - jnp.clip(pos - w // 2, 0, seq_len)
        pooled = acc / cnt.astype(F32) - bbuf[HALO:HALO + TM, lo:hi]
        mixed.append(_dot(pooled.astype(BF), pw_ref[g]))
    pb = jnp.concatenate(mixed, axis=-1) * ps_ref[...]
    br_b = _dot(pb.astype(BF), wpo_ref[...])

    is_prompt = i < n_p
    br_c = _dot(jnp.where(is_prompt, ocp_ref[...], ocs_ref[...]), wgo_ref[...])
    br_d = _dot(jnp.where(is_prompt, odp_ref[...], ods_ref[...]), wmo_ref[...])

    x = x_ref[...]
    h = _modulated_norm(x, g1_ref[...], mod_ref[0:1, :], mod_ref[1:2, :])
    gates = jax.nn.sigmoid(_dot(h.astype(BF), wg_ref[...]))
    merged = (gates[:, 0:D_MODEL] * br_a + gates[:, D_MODEL:2 * D_MODEL] * br_b
              + gates[:, 2 * D_MODEL:3 * D_MODEL] * br_c + gates[:, 3 * D_MODEL:] * br_d)
    x = x + mod_ref[2:3, :] * _dot(merged.astype(BF), wo_ref[...])
    xo_ref[...] = x

    h2 = _modulated_norm(x, g2_ref[...], mod_ref[3:4, :], mod_ref[4:5, :])

    logits = jnp.dot(h2, wr_ref[...], preferred_element_type=F32, precision=lax.Precision.HIGHEST) + br_ref[...]
    lane = lax.broadcasted_iota(jnp.int32, (TM, LANES), 1).astype(F32)
    neg = jnp.float32(-jnp.inf)
    lg = jnp.where(lane < N_EXPERTS, logits, neg)
    vals, idxs = [], []
    for _ in range(TOP_K):
        m = jnp.max(lg, axis=-1, keepdims=True)
        idx = jnp.min(jnp.where(lg == m, lane, float(LANES)), axis=-1, keepdims=True)
        vals.append(m)
        idxs.append(idx)
        lg = jnp.where(lane == idx, neg, lg)
    exps = [jnp.exp(v - vals[0]) for v in vals]
    denom = exps[0] + exps[1] + exps[2] + exps[3]

    onehot = [(lane == idxs[kk]).astype(F32) for kk in range(TOP_K)]
    colsum = [jnp.sum(o, axis=0, keepdims=True) for o in onehot]
    cnt = colsum[0] + colsum[1] + colsum[2] + colsum[3]
    pad8 = jnp.floor((cnt + 7.0) * 0.125) * 8.0
    run_off = jnp.dot(jnp.broadcast_to(pad8, (8, LANES)), upper_ref[...], preferred_element_type=F32,
                      precision=lax.Precision.HIGHEST)[0:1, :]
    base = jnp.zeros((1, LANES), F32)
    rloc = []
    for kk in range(TOP_K):
        before = _dot(ltri_ref[...], onehot[kk].astype(BF))
        rloc.append(jnp.sum(onehot[kk] * (run_off + base + before), axis=-1, keepdims=True))
        base = base + colsum[kk]

    eye = (lax.broadcasted_iota(jnp.int32, (TM, TM), 0) == lax.broadcasted_iota(jnp.int32, (TM, TM), 1))
    r_sub = lax.broadcasted_iota(jnp.int32, (R_TILE, TM), 0).astype(F32)
    ones8 = jnp.ones((8, TM), F32)
    sel = jnp.zeros((R_TILE, TM), F32)
    for kk in range(TOP_K):
        row = jnp.dot(ones8, jnp.where(eye, rloc[kk], 0.0), preferred_element_type=F32,
                      precision=lax.Precision.HIGHEST)[0:1, :]
        sel = jnp.where(r_sub == row, 1.0, sel)
    xs_ref[...] = _dot(sel.astype(BF), h2.astype(BF))

    rl = jnp.zeros((TM, LANES), F32)
    tw = jnp.zeros((TM, LANES), F32)
    for kk in range(TOP_K):
        rl = jnp.where(lane == kk, rloc[kk], rl)
        tw = jnp.where(lane == kk, exps[kk] / denom, tw)
    rl_ref[...] = rl.astype(jnp.int32)
    tw_ref[...] = tw
    sub = lax.broadcasted_iota(jnp.int32, (8, LANES), 0)
    meta = jnp.where(sub == 0, jnp.broadcast_to(pad8, (8, LANES)),
                     jnp.where(sub == 1, jnp.broadcast_to(run_off, (8, LANES)), 0.0))
    meta_ref[...] = meta.astype(jnp.int32)


def _merge(l, x, mod, geo, a, b, attn, p):
    nt, n_p = geo["nt"], geo["np"]
    tok = nt * TM
    seq_row = geo["seq_row"]
    hb = TM // HALO
    last_hb = nt * hb - 1
    tile = lambda w: pl.BlockSpec((TM, w), lambda i: (i, 0))
    prev = lambda w: pl.BlockSpec((HALO, w), lambda i: (jnp.maximum(i * hb - 1, 0), 0))
    nxt = lambda w: pl.BlockSpec((HALO, w), lambda i: (jnp.minimum((i + 1) * hb, last_hb), 0))
    p_tile = pl.BlockSpec((TM, 512), lambda i: (jnp.minimum(i, n_p - 1), 0))
    s_tile = pl.BlockSpec((TM, 512), lambda i: (jnp.maximum(i - n_p, 0), 0))
    return pl.pallas_call(
        functools.partial(_merge_kernel, geo),
        grid=(nt,),
        in_specs=[
            tile(D_MODEL),
            pl.BlockSpec((None, None, 6, D_MODEL), lambda i: (l, seq_row(i), 0, 0)),
            tile(CONV_W), prev(CONV_W), nxt(CONV_W),
            tile(POOL_W), prev(POOL_W), nxt(POOL_W),
            p_tile, p_tile, s_tile, s_tile,
            _layer_spec((1, D_MODEL), l),
            _layer_spec((D_MODEL, N_GATE), l),
            _layer_spec((CONV_K, CONV_W), l),
            _layer_spec((1, CONV_W), l),
            _layer_spec((1, CONV_W), l),
            _layer_spec((1, CONV_W), l),
            _layer_spec((CONV_W, D_MODEL), l),
            _layer_spec((len(POOL_WINDOWS), POOL_G, POOL_G), l),
            _layer_spec((1, POOL_W), l),
            _layer_spec((POOL_W, D_MODEL), l),
            _layer_spec((512, D_MODEL), l),
            _layer_spec((512, D_MODEL), l),
            _layer_spec((D_MODEL, D_MODEL), l),
            _layer_spec((1, D_MODEL), l),
            _layer_spec((D_MODEL, LANES), l),
            _layer_spec((1, LANES), l),
            _const_spec((TM, TM)),
            _const_spec((LANES, LANES)),
        ],
        out_specs=[tile(D_MODEL), pl.BlockSpec((R_TILE, D_MODEL), lambda i: (i, 0)), tile(LANES), tile(LANES),
                   pl.BlockSpec((None, 8, LANES), lambda i: (i, 0, 0))],
        out_shape=[jax.ShapeDtypeStruct((tok, D_MODEL), F32), jax.ShapeDtypeStruct((nt * R_TILE, D_MODEL), F32),
                   jax.ShapeDtypeStruct((tok, LANES), jnp.int32), jax.ShapeDtypeStruct((tok, LANES), F32),
                   jax.ShapeDtypeStruct((nt, 8, LANES), jnp.int32)],
        scratch_shapes=[pltpu.VMEM((TM + 2 * HALO, CONV_W), F32), pltpu.VMEM((TM + 2 * HALO, POOL_W), F32),
                        pltpu.VMEM((SUBLANES - 1, TM + 2 * HALO - SUBLANES, CONV_W), F32)],
        compiler_params=pltpu.CompilerParams(dimension_semantics=("arbitrary",), vmem_limit_bytes=VMEM_LIMIT),
        name="merge",
    )(x, mod, a, a, a, b, b, b, *attn,
      p["norm1_g"], p["wgate"], p["conv_dw"], p["conv_dw_b"], p["conv_ln_g"], p["conv_ln_b"], p["w_conv_out"],
      p["pool_w"], p["pool_scale"], p["w_pool_out"], p["w_gqa_out"], p["w_mla_out"], p["w_o"],
      p["norm2_g"], p["w_router"], p["b_router"], p["ltri"], p["upper"])


def _expert_kernel(nt, blk_e, blk_row0, blk_t0, blk_t1, n_used_ref, pad8_ref, dst_ref, off_ref, tot8_ref,
                   xs_hbm, wgu_ref, bg_ref, bu_ref, wd_ref, bd_ref, perm_ref,
                   y_hbm, xg, yb, zbuf, wg_bf, wu_bf, wd_bf, cnt_smem, gsem, osem, zsem):
    i = pl.program_id(0)
    n_used = n_used_ref[0]
    slot = i % 2
    last = pl.num_programs(0) - 1

    def for_pieces(b, fn):
        e = blk_e[b]
        b0 = blk_row0[b]

        def body(t, tot):
            j = t * N_EXPERTS + e
            run0 = dst_ref[j]
            lo = jnp.maximum(run0, b0)
            hi = jnp.minimum(run0 + pad8_ref[j], b0 + MOE_BM)
            n = pl.multiple_of(jnp.maximum(hi - lo, 0), 8)

            @pl.when(n > 0)
            def _():
                fn(pl.multiple_of(t * R_TILE + off_ref[j] + lo - run0, 8), pl.multiple_of(lo - b0, 8), n)

            return tot + n

        return lax.fori_loop(blk_t0[b], blk_t1[b], body, jnp.int32(0))

    def rows_copy(src, dst, sem, n):
        return pltpu.make_async_copy(src.at[pl.ds(0, n), :], dst.at[pl.ds(0, n), :], sem)

    def start_gather(b, s):
        def piece(row_t, row_b, n):
            pltpu.make_async_copy(xs_hbm.at[pl.ds(row_t, n), :], xg.at[s].at[pl.ds(row_b, n), :], gsem.at[s]).start()

        cnt_smem[s] = for_pieces(b, piece)

    def tail_copy(t):
        n = pl.multiple_of(R_TILE - tot8_ref[t], 8)
        return n, pltpu.make_async_copy(
            zbuf.at[pl.ds(0, n), :], y_hbm.at[pl.ds(pl.multiple_of(t * R_TILE + tot8_ref[t], 8), n), :], zsem)

    @pl.when(i == 0)
    def _():
        xg[...] = jnp.zeros_like(xg)
        zbuf[...] = jnp.zeros_like(zbuf)
        for s in range(4):
            cnt_smem[s] = 0

        def fill(t, c):
            n, cp = tail_copy(t)

            @pl.when(n > 0)
            def _():
                cp.start()

            return c

        lax.fori_loop(0, nt, fill, 0)

        @pl.when(n_used > 0)
        def _():
            start_gather(0, 0)

    @pl.when(i + 1 < n_used)
    def _():
        start_gather(i + 1, 1 - slot)

    @pl.when(i < n_used)
    def _():
        e_changed = jnp.logical_or(i == 0, blk_e[i] != blk_e[jnp.maximum(i - 1, 0)])

        @pl.when(e_changed)
        def _():
            wd_bf[...] = wd_ref[...].astype(BF)
            for c in range(D_FF // LANES):
                pair = _dot(wgu_ref[:, 2 * c * LANES:2 * (c + 1) * LANES].astype(BF), perm_ref[...])
                wg_bf[:, c * LANES:(c + 1) * LANES] = pair[:, 0:LANES].astype(BF)
                wu_bf[:, c * LANES:(c + 1) * LANES] = pair[:, LANES:].astype(BF)

        n_in = pl.multiple_of(cnt_smem[slot], 8)
        rows_copy(xs_hbm, xg.at[slot], gsem.at[slot], n_in).wait()
        xb = xg[slot].astype(BF)
        gate = jnp.minimum(_dot(xb, wg_bf[...]) + bg_ref[...], SWIGLU_LIMIT)
        up = jnp.clip(_dot(xb, wu_bf[...]) + bu_ref[...], -SWIGLU_LIMIT, SWIGLU_LIMIT)
        glu = gate * jax.nn.sigmoid(gate * SWIGLU_ALPHA)
        y = _dot(((up + 1.0) * glu).astype(BF), wd_bf[...]) + bd_ref[...]

        n_prev = pl.multiple_of(cnt_smem[2 + slot], 8)

        @pl.when(n_prev > 0)
        def _():
            rows_copy(yb.at[slot], y_hbm, osem.at[slot], n_prev).wait()

        yb[slot] = y.astype(BF).astype(F32)

        def piece(row_t, row_b, n):
            pltpu.make_async_copy(yb.at[slot].at[pl.ds(row_b, n), :], y_hbm.at[pl.ds(row_t, n), :],
                                  osem.at[slot]).start()

        cnt_smem[2 + slot] = for_pieces(i, piece)

    @pl.when(i == last)
    def _():
        for s in range(2):
            n_out = pl.multiple_of(cnt_smem[2 + s], 8)

            @pl.when(n_out > 0)
            def _():
                rows_copy(yb.at[s], y_hbm, osem.at[s], n_out).wait()

        def drain(t, c):
            n, cp = tail_copy(t)

            @pl.when(n > 0)
            def _():
                cp.wait()

            return c

        lax.fori_loop(0, nt, drain, 0)


def _experts(l, xs, plan, p, nt):
    n_blocks = plan["blk_e"].shape[0]
    n_pref = 9
    by_expert = lambda *lead: (lambda i, be, *_: lead + (be[i], 0, 0))
    b_spec = pl.BlockSpec((None, None, 1, D_FF), by_expert(l))
    grid_spec = pltpu.PrefetchScalarGridSpec(
        num_scalar_prefetch=n_pref,
        grid=(n_blocks,),
        in_specs=[
            pl.BlockSpec(memory_space=pl.ANY),
            pl.BlockSpec((None, None, D_MODEL, 2 * D_FF), by_expert(l)),
            b_spec, b_spec,
            pl.BlockSpec((None, None, D_FF, D_MODEL), by_expert(l)),
            pl.BlockSpec((None, None, 1, D_MODEL), by_expert(l)),
            pl.BlockSpec((2 * LANES, 2 * LANES), lambda i, *_: (0, 0)),
        ],
        out_specs=pl.BlockSpec(memory_space=pl.ANY),
        scratch_shapes=[
            pltpu.VMEM((2, MOE_BM, D_MODEL), F32),
            pltpu.VMEM((2, MOE_BM, D_MODEL), F32),
            pltpu.VMEM((R_TILE - TOP_K * TM, D_MODEL), F32),
            pltpu.VMEM((D_MODEL, D_FF), BF),
            pltpu.VMEM((D_MODEL, D_FF), BF),
            pltpu.VMEM((D_FF, D_MODEL), BF),
            pltpu.SMEM((4,), jnp.int32),
            pltpu.SemaphoreType.DMA((2,)),
            pltpu.SemaphoreType.DMA((2,)),
            pltpu.SemaphoreType.DMA,
        ],
    )
    return pl.pallas_call(
        functools.partial(_expert_kernel, nt),
        grid_spec=grid_spec,
        out_shape=jax.ShapeDtypeStruct((nt * R_TILE, D_MODEL), F32),
        compiler_params=pltpu.CompilerParams(dimension_semantics=("arbitrary",), vmem_limit_bytes=VMEM_LIMIT),
        name="experts",
    )(plan["blk_e"], plan["blk_row0"], plan["blk_t0"], plan["blk_t1"], plan["n_used"], plan["pad8"], plan["dst"],
      plan["off"], plan["tot8"],
      xs, p["w_gu"], p["b_gate"], p["b_up"], p["w_dn"], p["b_dn"], p["pair_perm"])


def _pair_perm():
    m = np.zeros((2 * LANES, 2 * LANES), np.float32)
    j = np.arange(LANES)
    m[2 * j, j] = 1.0
    m[2 * j + 1, LANES + j] = 1.0
    return jnp.asarray(m, BF)


def _combine_kernel(final, y_ref, x_ref, mod_ref, rl_ref, tw_ref, fg_ref, o_ref):
    r_lane = lax.broadcasted_iota(jnp.int32, (TM, R_TILE), 1)
    rl = rl_ref[...]
    tw = tw_ref[...]
    sel = jnp.zeros((TM, R_TILE), F32)
    for k in range(TOP_K):
        sel = jnp.where(r_lane == rl[:, k:k + 1], tw[:, k:k + 1], sel)
    ffn = _dot(sel.astype(BF), y_ref[...].astype(BF))
    x = x_ref[...] + mod_ref[5:6, :] * ffn
    if final:
        x = _rms(x) * fg_ref[...]
    o_ref[...] = x


def _combine(l, final, y, x, mod, rl, tw, final_g, geo, tile0, n_tiles):
    seq_row = geo["seq_row"]
    tile = lambda w: pl.BlockSpec((TM, w), lambda i: (i + tile0, 0))
    return pl.pallas_call(
        functools.partial(_combine_kernel, final),
        grid=(n_tiles,),
        in_specs=[
            pl.BlockSpec((R_TILE, D_MODEL), lambda i: (i + tile0, 0)),
            tile(D_MODEL),
            pl.BlockSpec((None, None, 6, D_MODEL), lambda i: (l, seq_row(i + tile0), 0, 0)),
            tile(LANES),
            tile(LANES),
            _const_spec((1, D_MODEL)),
        ],
        out_specs=pl.BlockSpec((TM, D_MODEL), lambda i: (i, 0)),
        out_shape=jax.ShapeDtypeStruct((n_tiles * TM, D_MODEL), F32),
        compiler_params=pltpu.CompilerParams(dimension_semantics=("arbitrary",), vmem_limit_bytes=VMEM_LIMIT),
        name="combine",
    )(y, x, mod, rl, tw, final_g)


def _expert_plan(meta, nt):
    pad8 = meta[:, 0, :N_EXPERTS]
    off = meta[:, 1, :N_EXPERTS]
    ends = jnp.cumsum(pad8, axis=0)
    dst = ends - pad8
    tot = ends[-1]
    nb = (tot + MOE_BM - 1) // MOE_BM
    nb_end = jnp.cumsum(nb)
    n_blocks = (nt * TM * TOP_K + nt * N_EXPERTS * 7) // MOE_BM + N_EXPERTS
    b = jnp.arange(n_blocks, dtype=jnp.int32)
    blk_e = jnp.minimum(jnp.sum(nb_end[None, :] <= b[:, None], axis=1), N_EXPERTS - 1).astype(jnp.int32)
    blk_row0 = (b - (nb_end - nb)[blk_e]) * MOE_BM
    ends_b = ends[:, blk_e]
    dst_b = dst[:, blk_e]
    blk_t0 = jnp.sum(ends_b <= blk_row0[None, :], axis=0)
    blk_t1 = jnp.sum(dst_b < blk_row0[None, :] + MOE_BM, axis=0)
    i32 = lambda v: v.astype(jnp.int32)
    return {"blk_e": blk_e, "blk_row0": i32(blk_row0), "blk_t0": i32(blk_t0), "blk_t1": i32(blk_t1),
            "n_used": i32(nb_end[-1]).reshape(1), "pad8": i32(pad8.reshape(-1)), "dst": i32(dst.reshape(-1)),
            "off": i32(off.reshape(-1)), "tot8": i32(jnp.sum(pad8, axis=1))}


def _rope_tables(n_pos):
    pos = np.arange(n_pos)
    row, col = pos // GRID_W, pos % GRID_W
    lane = np.arange(LANES)

    def build(active, r, half):
        n_rot = 4 * half
        is_col = (r % n_rot) >= 2 * half
        rr = r % (2 * half)
        freq = np.power(ROPE_BASE, -(rr % half).astype(np.float64) / half)
        p = np.where(is_col[None, :], col[:, None], row[:, None]).astype(np.float64)
        ang = p * freq[None, :]
        first = rr < half
        cos = np.where(active[None, :], np.cos(ang), 1.0)
        sin_a = np.where((active & first)[None, :], -np.sin(ang), 0.0)
        sin_b = np.where((active & ~first)[None, :], np.sin(ang), 0.0)
        return [cos, sin_a, sin_b]

    tabs = build(np.ones(LANES, bool), lane % GQA_HEAD_DIM, GQA_HEAD_DIM // 4)
    in_rope = (lane >= MLA_NOPE) & (lane < MLA_NOPE + MLA_ROPE)
    tabs += build(in_rope, np.maximum(lane - MLA_NOPE, 0) % MLA_ROPE, MLA_ROPE // 4)
    tabs += build(lane < MLA_ROPE, lane % MLA_ROPE, MLA_ROPE // 4)
    table = np.concatenate(tabs, axis=1)
    ident = np.concatenate([np.ones((TM, LANES)), np.zeros((TM, LANES)), np.zeros((TM, LANES))] * 3, axis=1)
    return jnp.asarray(np.concatenate([ident, table], axis=0), F32)


def _placement():
    e = np.zeros((LANES, MLA_HEADS * LANES), np.float32)
    for h in range(MLA_HEADS):
        for r in range(MLA_ROPE):
            e[r, h * LANES + MLA_NOPE + r] = 1.0
    return jnp.asarray(e, BF)


def _block_diag_ones(n, g):
    idx = np.arange(n) // g
    return jnp.asarray((idx[:, None] == idx[None, :]).astype(np.float32), BF)


def kernel(x_prompt, x_sample, cache_gqa_k, cache_gqa_v, cache_mla_ckv, cache_mla_krope, c, c_ctx, norm1_g, norm2_g, w_mod, b_mod, w_in, conv_dw, conv_dw_b, conv_ln_g, conv_ln_b, w_conv_out, pool_w, pool_scale, w_pool_out, gqa_qn_g, gqa_kn_g, w_gqa_out, mla_qn_g, w_mla_q_up, mla_kvn_g, w_mla_kv_up, w_mla_out, w_o, w_router, b_router, w_gu, b_gu, w_dn, b_dn, final_g):
    bp, seq, d = x_prompt.shape
    bs, ls, _ = x_sample.shape
    depth = w_in.shape[0]
    past = cache_gqa_k.shape[2]
    assert seq == TM and d == D_MODEL and ls % TM == 0 and (bp * seq) % ls == 0
    tps = ls // TM
    n_p = bp
    nt = n_p + bs * tps
    geo = {
        "np": n_p, "bs": bs, "tps": tps, "nt": nt,
        "seq_row": lambda i: jnp.where(i < n_p, 0, 1 + (i - n_p) // tps),
        "rope_blk": lambda i: jnp.where(i < n_p, 0, 1 + (i - n_p) % tps),
    }

    n_cond = -(-(1 + bs) // 8) * 8
    cond = jnp.zeros((n_cond, d), F32).at[0].set(c_ctx).at[1:1 + bs].set(c)
    mod = _modulation(cond, w_mod, b_mod).reshape(depth, n_cond, 6, d)

    row = lambda v: v.reshape(depth, 1, -1)
    w1 = jnp.pad(w_in[:, :, :_SPLIT_GATE], ((0, 0), (0, 0), (0, W1_COLS - _SPLIT_GATE))).astype(BF)
    wqup = jnp.pad(w_mla_q_up.reshape(depth, MLA_Q_RANK, MLA_HEADS, MLA_NOPE + MLA_ROPE),
                   ((0, 0), (0, 0), (0, 0), (0, LANES - MLA_NOPE - MLA_ROPE)))
    wkv = w_mla_kv_up.reshape(depth, MLA_KV_RANK, MLA_HEADS, MLA_NOPE + MLA_V)
    wk_pad = jnp.pad(wkv[..., :MLA_NOPE], ((0, 0), (0, 0), (0, 0), (0, LANES - MLA_NOPE)))
    wkvup = jnp.concatenate([wk_pad.reshape(depth, MLA_KV_RANK, MLA_HEADS * LANES),
                             wkv[..., MLA_NOPE:].reshape(depth, MLA_KV_RANK, MLA_HEADS * MLA_V)], axis=-1)
    bgu = b_gu.reshape(depth, N_EXPERTS, 1, D_FF, 2)
    p = {
        "norm1_g": row(norm1_g), "norm2_g": row(norm2_g),
        "w1": w1, "wgate": w_in[:, :, _SPLIT_GATE:].astype(BF),
        "rope": _rope_tables(ls), "place": _placement(), "ones_bd": _block_diag_ones(512, GQA_HEAD_DIM),
        "ltri": jnp.asarray(np.tril(np.ones((TM, TM), np.float32), -1), BF),
        "upper": jnp.asarray(np.triu(np.ones((LANES, LANES), np.float32), 1), F32),
        "qn_g": row(jnp.tile(gqa_qn_g, (1, GQA_HEADS))), "kn_g": row(jnp.tile(gqa_kn_g, (1, GQA_KV_HEADS))),
        "cqn_g": row(mla_qn_g), "kvn_g": row(mla_kvn_g),
        "wqup": wqup.reshape(depth, MLA_Q_RANK, MLA_HEADS * LANES).astype(BF), "wkvup": wkvup.astype(BF),
        "conv_dw": conv_dw, "conv_dw_b": row(conv_dw_b), "conv_ln_g": row(conv_ln_g), "conv_ln_b": row(conv_ln_b),
        "w_conv_out": w_conv_out.astype(BF), "pool_w": pool_w.astype(BF), "pool_scale": row(pool_scale),
        "w_pool_out": w_pool_out.astype(BF), "w_gqa_out": w_gqa_out.astype(BF), "w_mla_out": w_mla_out.astype(BF),
        "w_o": w_o.astype(BF),
        "w_router": jnp.pad(w_router, ((0, 0), (0, 0), (0, LANES - N_EXPERTS))),
        "b_router": row(jnp.pad(b_router, ((0, 0), (0, LANES - N_EXPERTS)))),
        "w_gu": w_gu, "b_gate": bgu[..., 0], "b_up": bgu[..., 1], "pair_perm": _pair_perm(),
        "w_dn": w_dn, "b_dn": b_dn.reshape(depth, N_EXPERTS, 1, d),
    }

    ckm, cvm = _cache_prep(cache_mla_ckv, jnp.pad(cache_mla_krope, ((0, 0), (0, 0), (0, 0), (0, LANES - MLA_ROPE))), p)
    cache = (cache_gqa_k.reshape(bs, depth, past, LANES).astype(BF),
             cache_gqa_v.reshape(bs, depth, past, LANES).astype(BF), ckm, cvm)

    x = jnp.concatenate([x_prompt.reshape(bp * seq, d), x_sample.reshape(bs * ls, d)], axis=0)
    n_ptok = bp * seq
    states = []
    for l in range(depth):
        a, b, q, k, v, qm, km, vm, ks, vs, ckvs, krs = _proj(l, x, mod, geo, p)
        states.append((ks[:n_ptok], vs[:n_ptok], ckvs[:n_ptok], krs[:n_ptok]))
        attn = _attention(l, geo, q, k, v, qm, km, vm, cache)
        x_mid, xs, slot_row, top_w, meta = _merge(l, x, mod, geo, a, b, attn, p)
        y = _experts(l, xs, _expert_plan(meta, nt), p, nt)
        combine = functools.partial(_combine, l, l == depth - 1, y, x_mid, mod, slot_row, top_w,
                                    final_g.reshape(1, d), geo)
        if l < depth - 1:
            x = combine(0, nt)

    y_prompt = combine(0, n_p).reshape(bp, seq, d)
    y_sample = combine(n_p, nt - n_p).reshape(bs, ls, d)
    st = lambda j, shape: jnp.stack([s[j].reshape(shape) for s in states], axis=1)
    return (y_prompt, y_sample,
            st(0, (bp, seq, GQA_KV_HEADS, GQA_HEAD_DIM)), st(1, (bp, seq, GQA_KV_HEADS, GQA_HEAD_DIM)),
            st(2, (bp, seq, MLA_KV_RANK)), st(3, (bp, seq, MLA_ROPE)))
```

```python
import functools

import jax
import jax.numpy as jnp
import numpy as np
from jax import lax
from jax.experimental import pallas as pl
from jax.experimental.pallas import tpu as pltpu

D_MODEL = 1024
GRID_W = 64
CONV_W = 512
CONV_K = 31
POOL_W = 512
POOL_WINDOWS = (2, 4, 8, 16)
POOL_G = 128
GQA_HEADS = 8
GQA_KV_HEADS = 2
GQA_HEAD_DIM = 64
MLA_HEADS = 8
MLA_Q_RANK = 384
MLA_KV_RANK = 256
MLA_NOPE = 64
MLA_ROPE = 32
MLA_V = 64
ROPE_BASE = 10000.0
N_EXPERTS = 32
TOP_K = 4
D_FF = 1024
SWIGLU_LIMIT = 7.0
SWIGLU_ALPHA = 1.702
EPS = 1e-6
GQA_SCALE = GQA_HEAD_DIM ** -0.5
MLA_SCALE = (MLA_NOPE + MLA_ROPE) ** -0.5

LANES = 128
SUBLANES = 8
TM = 256
HALO = 16
MOE_BM = 512
R_TILE = 1280
W1_COLS = 3072
N_GATE = 4 * D_MODEL
VMEM_LIMIT = 56 * 1024 * 1024

BF = jnp.bfloat16
F32 = jnp.float32

_C_A, _C_B, _C_Q, _C_K, _C_V, _C_CQ, _C_CKV, _C_KR = 0, 1024, 1536, 2048, 2176, 2304, 2688, 2944
_SPLIT_GATE = 2976


def _dot(a, b):
    return jnp.dot(a, b, preferred_element_type=F32)


def _dot_nt(a, b):
    return lax.dot_general(a, b, (((1,), (1,)), ((), ())), preferred_element_type=F32)


def _rms(x):
    return x * lax.rsqrt(jnp.mean(x * x, axis=-1, keepdims=True) + EPS)


def _group_mean_sq(x, ones_bd, width):
    xx = x * x
    hi = xx.astype(BF)
    lo = (xx - hi.astype(F32)).astype(BF)
    return (_dot(hi, ones_bd) + _dot(lo, ones_bd)) * (1.0 / width)


def _tile_lanes(t, width):
    reps = width // LANES
    return t if reps == 1 else jnp.concatenate([t] * reps, axis=-1)


def _rope(x, cos, sin_a, sin_b, shift):
    w = x.shape[-1]
    return (x * _tile_lanes(cos, w) + pltpu.roll(x, w - shift, 1) * _tile_lanes(sin_a, w)
            + pltpu.roll(x, shift, 1) * _tile_lanes(sin_b, w))


def _modulated_norm(x, g, shift, scale):
    return _rms(x) * g * (1.0 + scale) + shift


def _mod_kernel(cond_ref, w_ref, b_ref, o_ref):
    c = cond_ref[...]
    s = (c * jax.nn.sigmoid(c)).astype(BF)
    o_ref[...] = _dot(s, w_ref[...].astype(BF)) + b_ref[...]


def _modulation(cond, w_mod, b_mod):
    depth, d, n = w_mod.shape
    rows = cond.shape[0]
    return pl.pallas_call(
        _mod_kernel,
        grid=(depth, n // D_MODEL),
        in_specs=[
            pl.BlockSpec((rows, d), lambda l, j: (0, 0)),
            pl.BlockSpec((None, d, D_MODEL), lambda l, j: (l, 0, j)),
            pl.BlockSpec((None, 1, D_MODEL), lambda l, j: (l, 0, j)),
        ],
        out_specs=pl.BlockSpec((None, rows, D_MODEL), lambda l, j: (l, 0, j)),
        out_shape=jax.ShapeDtypeStruct((depth, rows, n), F32),
        name="modulation",
    )(cond, w_mod, b_mod.reshape(depth, 1, n))


def _proj_kernel(x_ref, mod_ref, g1_ref, w1_ref, rope_ref, qn_ref, kn_ref, cqn_ref, kvn_ref,
                 wqup_ref, wkvup_ref, place_ref, ones_ref,
                 a_ref, b_ref, q_ref, k_ref, v_ref, qm_ref, km_ref, vm_ref,
                 ks_ref, vs_ref, ckvs_ref, krs_ref):
    x = x_ref[...]
    h = _modulated_norm(x, g1_ref[...], mod_ref[0:1, :], mod_ref[1:2, :])
    y = _dot(h.astype(BF), w1_ref[...])

    a_ref[...] = (y[:, _C_A:_C_A + CONV_W] * jax.nn.sigmoid(y[:, _C_A + CONV_W:_C_B])).astype(BF)
    b_ref[...] = y[:, _C_B:_C_Q].astype(BF)

    def tab(j):
        return rope_ref[:, j * LANES:(j + 1) * LANES]

    q = y[:, _C_Q:_C_K]
    q = q * lax.rsqrt(_group_mean_sq(q, ones_ref[...], GQA_HEAD_DIM) + EPS) * qn_ref[...]
    q = _rope(q, tab(0), tab(1), tab(2), GQA_HEAD_DIM // 4)
    q_ref[...] = (q * GQA_SCALE).astype(BF)

    k = y[:, _C_K:_C_V]
    k = k * lax.rsqrt(_group_mean_sq(k, ones_ref[0:LANES, 0:LANES], GQA_HEAD_DIM) + EPS) * kn_ref[...]
    ks_ref[...] = k
    k_ref[...] = _rope(k, tab(0), tab(1), tab(2), GQA_HEAD_DIM // 4).astype(BF)

    v = y[:, _C_V:_C_CQ]
    vs_ref[...] = v
    v_ref[...] = v.astype(BF)

    cq = _rms(y[:, _C_CQ:_C_CKV]) * cqn_ref[...]
    qm = _dot(cq.astype(BF), wqup_ref[...])
    qm = _rope(qm, tab(3), tab(4), tab(5), MLA_ROPE // 4)
    qm_ref[...] = (qm * MLA_SCALE).astype(BF)

    ckv = _rms(y[:, _C_CKV:_C_KR]) * kvn_ref[...]
    ckvs_ref[...] = ckv
    kv = _dot(ckv.astype(BF), wkvup_ref[...])
    kr = y[:, _C_KR:W1_COLS]
    krs_ref[...] = kr[:, 0:MLA_ROPE]
    kr_rot = _rope(kr, tab(6), tab(7), tab(8), MLA_ROPE // 4)
    km = kv[:, 0:MLA_HEADS * LANES] + _dot(kr_rot.astype(BF), place_ref[...])
    km_ref[...] = km.astype(BF)
    vm_ref[...] = kv[:, MLA_HEADS * LANES:].astype(BF)


def _const_spec(shape):
    nd = len(shape)
    return pl.BlockSpec(shape, lambda *_: (0,) * nd)


def _layer_spec(shape, l):
    nd = len(shape)
    return pl.BlockSpec((None,) + shape, lambda *_: (l,) + (0,) * nd)


def _proj(l, x, mod, geo, p):
    nt = geo["nt"]
    tok = nt * TM
    seq_row, rope_blk = geo["seq_row"], geo["rope_blk"]

    def tile(width, dtype):
        return pl.BlockSpec((TM, width), lambda i: (i, 0)), jax.ShapeDtypeStruct((tok, width), dtype)

    outs = [tile(CONV_W, BF), tile(POOL_W, BF), tile(512, BF), tile(LANES, BF), tile(LANES, BF),
            tile(MLA_HEADS * LANES, BF), tile(MLA_HEADS * LANES, BF), tile(MLA_HEADS * MLA_V, BF),
            tile(LANES, F32), tile(LANES, F32), tile(MLA_KV_RANK, F32), tile(MLA_ROPE, F32)]
    return pl.pallas_call(
        _proj_kernel,
        grid=(nt,),
        in_specs=[
            pl.BlockSpec((TM, D_MODEL), lambda i: (i, 0)),
            pl.BlockSpec((None, None, 6, D_MODEL), lambda i: (l, seq_row(i), 0, 0)),
            _layer_spec((1, D_MODEL), l),
            _layer_spec((D_MODEL, W1_COLS), l),
            pl.BlockSpec((TM, 9 * LANES), lambda i: (rope_blk(i), 0)),
            _layer_spec((1, 512), l),
            _layer_spec((1, LANES), l),
            _layer_spec((1, MLA_Q_RANK), l),
            _layer_spec((1, MLA_KV_RANK), l),
            _layer_spec((MLA_Q_RANK, MLA_HEADS * LANES), l),
            _layer_spec((MLA_KV_RANK, MLA_HEADS * (LANES + MLA_V)), l),
            _const_spec((LANES, MLA_HEADS * LANES)),
            _const_spec((512, 512)),
        ],
        out_specs=[o[0] for o in outs],
        out_shape=[o[1] for o in outs],
        compiler_params=pltpu.CompilerParams(dimension_semantics=("arbitrary",), vmem_limit_bytes=VMEM_LIMIT),
        name="proj",
    )(x, mod, p["norm1_g"], p["w1"], p["rope"], p["qn_g"], p["kn_g"], p["cqn_g"], p["kvn_g"],
      p["wqup"], p["wkvup"], p["place"], p["ones_bd"])


def _cache_kernel(ckv_ref, kr_ref, wkvup_ref, place_ref, km_ref, vm_ref):
    kv = _dot(ckv_ref[...].astype(BF), wkvup_ref[...])
    km = kv[:, 0:MLA_HEADS * LANES] + _dot(kr_ref[...].astype(BF), place_ref[...])
    km_ref[...] = km.astype(BF)
    vm_ref[...] = kv[:, MLA_HEADS * LANES:].astype(BF)


def _cache_prep(ckv, kr_pad, p):
    bs, depth, past, _ = ckv.shape
    return pl.pallas_call(
        _cache_kernel,
        grid=(bs, depth),
        in_specs=[
            pl.BlockSpec((None, None, past, MLA_KV_RANK), lambda b, l: (b, l, 0, 0)),
            pl.BlockSpec((None, None, past, LANES), lambda b, l: (b, l, 0, 0)),
            pl.BlockSpec((None, MLA_KV_RANK, MLA_HEADS * (LANES + MLA_V)), lambda b, l: (l, 0, 0)),
            pl.BlockSpec((LANES, MLA_HEADS * LANES), lambda b, l: (0, 0)),
        ],
        out_specs=[
            pl.BlockSpec((None, None, past, MLA_HEADS * LANES), lambda b, l: (b, l, 0, 0)),
            pl.BlockSpec((None, None, past, MLA_HEADS * MLA_V), lambda b, l: (b, l, 0, 0)),
        ],
        out_shape=[jax.ShapeDtypeStruct((bs, depth, past, MLA_HEADS * LANES), BF),
                   jax.ShapeDtypeStruct((bs, depth, past, MLA_HEADS * MLA_V), BF)],
        name="cache_prep",
    )(ckv, kr_pad, p["wkvup"], p["place"])


def _softmax_pv(s, v):
    m = jnp.max(s, axis=-1, keepdims=True)
    e = jnp.exp(s - m)
    l = jnp.sum(e, axis=-1, keepdims=True)
    return _dot(e.astype(BF), v) / l


def _attention_body(q, k, v, qm, km, vm, oc_ref, od_ref):
    lane_k = lax.broadcasted_iota(jnp.int32, k.shape, 1)
    lane_q = lax.broadcasted_iota(jnp.int32, (TM, LANES), 1)
    lo_k = lane_k < GQA_HEAD_DIM
    lo_q = lane_q < GQA_HEAD_DIM
    k32, v32 = k.astype(F32), v.astype(F32)
    k_sw = pltpu.roll(k32, GQA_HEAD_DIM, 1)
    v_sw = pltpu.roll(v32, GQA_HEAD_DIM, 1)
    k_dup = [jnp.where(lo_k, k32, k_sw).astype(BF), jnp.where(lo_k, k_sw, k32).astype(BF)]
    v_dup = [jnp.where(lo_k, v32, v_sw).astype(BF), jnp.where(lo_k, v_sw, v32).astype(BF)]
    zero = jnp.zeros((TM, LANES), BF)
    group = GQA_HEADS // GQA_KV_HEADS
    for j in range(GQA_HEADS // 2):
        qs = q[:, j * LANES:(j + 1) * LANES]
        g = (2 * j) // group
        o_lo = _softmax_pv(_dot_nt(jnp.where(lo_q, qs, zero), k_dup[g]), v_dup[g])
        o_hi = _softmax_pv(_dot_nt(jnp.where(lo_q, zero, qs), k_dup[g]), v_dup[g])
        oc_ref[:, j * LANES:(j + 1) * LANES] = jnp.where(lo_q, o_lo, o_hi).astype(BF)
    for j in range(MLA_HEADS // 2):
        vs = vm[:, j * LANES:(j + 1) * LANES]
        outs = []
        for h in (2 * j, 2 * j + 1):
            s = _dot_nt(qm[:, h * LANES:(h + 1) * LANES], km[:, h * LANES:(h + 1) * LANES])
            outs.append(_softmax_pv(s, vs))
        od_ref[:, j * LANES:(j + 1) * LANES] = jnp.where(lo_q, outs[0], outs[1]).astype(BF)


def _attn_prompt_kernel(q_ref, k_ref, v_ref, qm_ref, km_ref, vm_ref, oc_ref, od_ref):
    _attention_body(q_ref[...], k_ref[...], v_ref[...], qm_ref[...], km_ref[...], vm_ref[...], oc_ref, od_ref)


def _attn_sample_kernel(q_ref, k_ref, v_ref, qm_ref, km_ref, vm_ref, ck_ref, cv_ref, ckm_ref, cvm_ref,
                        oc_ref, od_ref):
    k = jnp.concatenate([ck_ref[...], k_ref[...]], axis=0)
    v = jnp.concatenate([cv_ref[...], v_ref[...]], axis=0)
    km = jnp.concatenate([ckm_ref[...], km_ref[...]], axis=0)
    vm = jnp.concatenate([cvm_ref[...], vm_ref[...]], axis=0)
    _attention_body(q_ref[...], k, v, qm_ref[...], km, vm, oc_ref, od_ref)


def _attention(l, geo, q, k, v, qm, km, vm, cache):
    n_p, bs, tps, nt = geo["np"], geo["bs"], geo["tps"], geo["nt"]
    tok = nt * TM
    ls = tps * TM
    off = (n_p * TM) // ls
    widths = (512, LANES, LANES, MLA_HEADS * LANES, MLA_HEADS * LANES, MLA_HEADS * MLA_V)
    params = pltpu.CompilerParams(dimension_semantics=("arbitrary",), vmem_limit_bytes=VMEM_LIMIT)
    oc_p, od_p = pl.pallas_call(
        _attn_prompt_kernel,
        grid=(n_p,),
        in_specs=[pl.BlockSpec((TM, w), lambda i: (i, 0)) for w in widths],
        out_specs=[pl.BlockSpec((TM, 512), lambda i: (i, 0))] * 2,
        out_shape=[jax.ShapeDtypeStruct((n_p * TM, 512), BF)] * 2,
        compiler_params=params,
        name="attn_prompt",
    )(q, k, v, qm, km, vm)
    ck, cv, ckm, cvm = cache
    past = ck.shape[2]
    q_spec = lambda w: pl.BlockSpec((TM, w), lambda b, j: (n_p + b * tps + j, 0))
    kv_spec = lambda w: pl.BlockSpec((ls, w), lambda b, j: (off + b, 0))
    c_spec = lambda w: pl.BlockSpec((None, None, past, w), lambda b, j: (b, l, 0, 0))
    params2 = pltpu.CompilerParams(dimension_semantics=("arbitrary", "arbitrary"), vmem_limit_bytes=VMEM_LIMIT)
    o_spec = pl.BlockSpec((TM, 512), lambda b, j: (b * tps + j, 0))
    oc_s, od_s = pl.pallas_call(
        _attn_sample_kernel,
        grid=(bs, tps),
        in_specs=[q_spec(512), kv_spec(LANES), kv_spec(LANES), q_spec(MLA_HEADS * LANES),
                  kv_spec(MLA_HEADS * LANES), kv_spec(MLA_HEADS * MLA_V),
                  c_spec(LANES), c_spec(LANES), c_spec(MLA_HEADS * LANES), c_spec(MLA_HEADS * MLA_V)],
        out_specs=[o_spec, o_spec],
        out_shape=[jax.ShapeDtypeStruct((bs * ls, 512), BF)] * 2,
        compiler_params=params2,
        name="attn_sample",
    )(q, k, v, qm, km, vm, ck, cv, ckm, cvm)
    return oc_p, od_p, oc_s, od_s


def _merge_kernel(geo, x_ref, mod_ref, a_ref, ap_ref, an_ref, b_ref, bp_ref, bn_ref,
                  ocp_ref, odp_ref, ocs_ref, ods_ref,
                  g1_ref, wg_ref, cw_ref, cb_ref, lng_ref, lnb_ref, wco_ref, pw_ref, ps_ref, wpo_ref,
                  wgo_ref, wmo_ref, wo_ref, g2_ref, wr_ref, br_ref,
                  ltri_ref, upper_ref,
                  xo_ref, xs_ref, rl_ref, tw_ref, meta_ref, abuf, bbuf, ashift):
    n_p, tps = geo["np"], geo["tps"]
    i = pl.program_id(0)
    j = jnp.where(i < n_p, 0, (i - n_p) % tps)
    n_seq_tiles = jnp.where(i < n_p, 1, tps)
    has_prev = j > 0
    has_next = j < n_seq_tiles - 1

    def fill(buf, cur, prev, nxt):
        buf[0:HALO, :] = jnp.where(has_prev, prev[...].astype(F32), 0.0)
        buf[HALO:HALO + TM, :] = cur[...].astype(F32)
        buf[HALO + TM:, :] = jnp.where(has_next, nxt[...].astype(F32), 0.0)

    fill(abuf, a_ref, ap_ref, an_ref)
    fill(bbuf, b_ref, bp_ref, bn_ref)

    sh_rows = TM + 2 * HALO - SUBLANES
    for sh in range(1, SUBLANES):
        ashift[sh - 1] = abuf[sh:sh + sh_rows, :]
    rows = 32
    conv = []
    for r0 in range(0, TM, rows):
        acc = jnp.zeros((rows, CONV_W), F32)
        for t in range(CONV_K):
            s = r0 + t + HALO - CONV_K // 2
            sh = s % SUBLANES
            tap = abuf[s:s + rows, :] if sh == 0 else ashift[sh - 1, s - sh:s - sh + rows, :]
            acc = acc + tap * cw_ref[t:t + 1, :]
        conv.append(acc)
    ca = jnp.concatenate(conv, axis=0) + cb_ref[...]
    mu = jnp.mean(ca, axis=-1, keepdims=True)
    xc = ca - mu
    ln = xc * lax.rsqrt(jnp.mean(xc * xc, axis=-1, keepdims=True) + EPS) * lng_ref[...] + lnb_ref[...]
    br_a = _dot((ln * jax.nn.sigmoid(ln)).astype(BF), wco_ref[...])

    pos = j * TM + lax.broadcasted_iota(jnp.int32, (TM, 1), 0)
    seq_len = n_seq_tiles * TM
    mixed = []
    for g, w in enumerate(POOL_WINDOWS):
        lo, hi = g * POOL_G, (g + 1) * POOL_G
        acc = jnp.zeros((TM, POOL_G), F32)
        for o in range(-(w // 2), w - w // 2):
            acc = acc + bbuf[HALO + o:HALO + o + TM, lo:hi]
        cnt = jnp.clip(pos - w // 2 + w, 0, seq_len) - jnp.clip(pos - w // 2, 0, seq_len)
        pooled = acc / cnt.astype(F32) - bbuf[HALO:HALO + TM, lo:hi]
        mixed.append(_dot(pooled.astype(BF), pw_ref[g]))
    pb = jnp.concatenate(mixed, axis=-1) * ps_ref[...]
    br_b = _dot(pb.astype(BF), wpo_ref[...])

    is_prompt = i < n_p
    br_c = _dot(jnp.where(is_prompt, ocp_ref[...], ocs_ref[...]), wgo_ref[...])
    br_d = _dot(jnp.where(is_prompt, odp_ref[...], ods_ref[...]), wmo_ref[...])

    x = x_ref[...]
    h = _modulated_norm(x, g1_ref[...], mod_ref[0:1, :], mod_ref[1:2, :])
    gates = jax.nn.sigmoid(_dot(h.astype(BF), wg_ref[...]).astype(BF))
    merged = (gates[:, 0:D_MODEL] * br_a.astype(BF) + gates[:, D_MODEL:2 * D_MODEL] * br_b.astype(BF)
              + gates[:, 2 * D_MODEL:3 * D_MODEL] * br_c.astype(BF) + gates[:, 3 * D_MODEL:] * br_d.astype(BF))
    x = x + mod_ref[2:3, :] * _dot(merged, wo_ref[...])
    xo_ref[...] = x

    h2 = _modulated_norm(x, g2_ref[...], mod_ref[3:4, :], mod_ref[4:5, :])

    h2_hi = h2.astype(BF)
    h2_lo = (h2 - h2_hi.astype(F32)).astype(BF)
    hi_terms = _dot(h2_hi, wr_ref[...])
    logits = (hi_terms[:, 0:LANES] + hi_terms[:, LANES:] + _dot(h2_lo, wr_ref[:, 0:LANES])) + br_ref[...]
    lane = lax.broadcasted_iota(jnp.int32, (TM, LANES), 1).astype(F32)
    neg = jnp.float32(-jnp.inf)
    lg = jnp.where(lane < N_EXPERTS, logits, neg)
    vals, idxs = [], []
    for _ in range(TOP_K):
        m = jnp.max(lg, axis=-1, keepdims=True)
        idx = jnp.min(jnp.where(lg == m, lane, float(LANES)), axis=-1, keepdims=True)
        vals.append(m)
        idxs.append(idx)
        lg = jnp.where(lane == idx, neg, lg)
    exps = [jnp.exp(v - vals[0]) for v in vals]
    denom = exps[0] + exps[1] + exps[2] + exps[3]

    onehot = [(lane == idxs[kk]).astype(F32) for kk in range(TOP_K)]
    colsum = [jnp.sum(o, axis=0, keepdims=True) for o in onehot]
    cnt = colsum[0] + colsum[1] + colsum[2] + colsum[3]
    pad8 = jnp.floor((cnt + 7.0) * 0.125) * 8.0
    run_off = jnp.dot(jnp.broadcast_to(pad8, (8, LANES)), upper_ref[...], preferred_element_type=F32,
                      precision=lax.Precision.HIGHEST)[0:1, :]
    base = jnp.zeros((1, LANES), F32)
    rloc = []
    for kk in range(TOP_K):
        before = _dot(ltri_ref[...], onehot[kk].astype(BF))
        rloc.append(jnp.sum(onehot[kk] * (run_off + base + before), axis=-1, keepdims=True))
        base = base + colsum[kk]

    eye = (lax.broadcasted_iota(jnp.int32, (TM, TM), 0) == lax.broadcasted_iota(jnp.int32, (TM, TM), 1))
    r_sub = lax.broadcasted_iota(jnp.int32, (R_TILE, TM), 0).astype(F32)
    ones8 = jnp.ones((8, TM), F32)
    sel = jnp.zeros((R_TILE, TM), F32)
    for kk in range(TOP_K):
        row = jnp.dot(ones8, jnp.where(eye, rloc[kk], 0.0), preferred_element_type=F32,
                      precision=lax.Precision.HIGHEST)[0:1, :]
        sel = jnp.where(r_sub == row, 1.0, sel)
    xs_ref[...] = _dot(sel.astype(BF), h2.astype(BF))

    rl = jnp.zeros((TM, LANES), F32)
    tw = jnp.zeros((TM, LANES), F32)
    for kk in range(TOP_K):
        rl = jnp.where(lane == kk, rloc[kk], rl)
        tw = jnp.where(lane == kk, exps[kk] / denom, tw)
    rl_ref[...] = rl.astype(jnp.int32)
    tw_ref[...] = tw
    sub = lax.broadcasted_iota(jnp.int32, (8, LANES), 0)
    meta = jnp.where(sub == 0, jnp.broadcast_to(pad8, (8, LANES)),
                     jnp.where(sub == 1, jnp.broadcast_to(run_off, (8, LANES)), 0.0))
    meta_ref[...] = meta.astype(jnp.int32)


def _merge(l, x, mod, geo, a, b, attn, p):
    nt, n_p = geo["nt"], geo["np"]
    tok = nt * TM
    seq_row = geo["seq_row"]
    hb = TM // HALO
    last_hb = nt * hb - 1
    tile = lambda w: pl.BlockSpec((TM, w), lambda i: (i, 0))
    prev = lambda w: pl.BlockSpec((HALO, w), lambda i: (jnp.maximum(i * hb - 1, 0), 0))
    nxt = lambda w: pl.BlockSpec((HALO, w), lambda i: (jnp.minimum((i + 1) * hb, last_hb), 0))
    p_tile = pl.BlockSpec((TM, 512), lambda i: (jnp.minimum(i, n_p - 1), 0))
    s_tile = pl.BlockSpec((TM, 512), lambda i: (jnp.maximum(i - n_p, 0), 0))
    return pl.pallas_call(
        functools.partial(_merge_kernel, geo),
        grid=(nt,),
        in_specs=[
            tile(D_MODEL),
            pl.BlockSpec((None, None, 6, D_MODEL), lambda i: (l, seq_row(i), 0, 0)),
            tile(CONV_W), prev(CONV_W), nxt(CONV_W),
            tile(POOL_W), prev(POOL_W), nxt(POOL_W),
            p_tile, p_tile, s_tile, s_tile,
            _layer_spec((1, D_MODEL), l),
            _layer_spec((D_MODEL, N_GATE), l),
            _layer_spec((CONV_K, CONV_W), l),
            _layer_spec((1, CONV_W), l),
            _layer_spec((1, CONV_W), l),
            _layer_spec((1, CONV_W), l),
            _layer_spec((CONV_W, D_MODEL), l),
            _layer_spec((len(POOL_WINDOWS), POOL_G, POOL_G), l),
            _layer_spec((1, POOL_W), l),
            _layer_spec((POOL_W, D_MODEL), l),
            _layer_spec((512, D_MODEL), l),
            _layer_spec((512, D_MODEL), l),
            _layer_spec((D_MODEL, D_MODEL), l),
            _layer_spec((1, D_MODEL), l),
            _layer_spec((D_MODEL, 2 * LANES), l),
            _layer_spec((1, LANES), l),
            _const_spec((TM, TM)),
            _const_spec((LANES, LANES)),
        ],
        out_specs=[tile(D_MODEL), pl.BlockSpec((R_TILE, D_MODEL), lambda i: (i, 0)), tile(LANES), tile(LANES),
                   pl.BlockSpec((None, 8, LANES), lambda i: (i, 0, 0))],
        out_shape=[jax.ShapeDtypeStruct((tok, D_MODEL), F32), jax.ShapeDtypeStruct((nt * R_TILE, D_MODEL), F32),
                   jax.ShapeDtypeStruct((tok, LANES), jnp.int32), jax.ShapeDtypeStruct((tok, LANES), F32),
                   jax.ShapeDtypeStruct((nt, 8, LANES), jnp.int32)],
        scratch_shapes=[pltpu.VMEM((TM + 2 * HALO, CONV_W), F32), pltpu.VMEM((TM + 2 * HALO, POOL_W), F32),
                        pltpu.VMEM((SUBLANES - 1, TM + 2 * HALO - SUBLANES, CONV_W), F32)],
        compiler_params=pltpu.CompilerParams(dimension_semantics=("arbitrary",), vmem_limit_bytes=VMEM_LIMIT),
        name="merge",
    )(x, mod, a, a, a, b, b, b, *attn,
      p["norm1_g"], p["wgate"], p["conv_dw"], p["conv_dw_b"], p["conv_ln_g"], p["conv_ln_b"], p["w_conv_out"],
      p["pool_w"], p["pool_scale"], p["w_pool_out"], p["w_gqa_out"], p["w_mla_out"], p["w_o"],
      p["norm2_g"], p["w_router"], p["b_router"], p["ltri"], p["upper"])


def _expert_kernel(nt, blk_e, blk_row0, blk_t0, blk_t1, n_used_ref, pad8_ref, dst_ref, off_ref, tot8_ref,
                   xs_hbm, wgu_ref, bg_ref, bu_ref, wd_ref, bd_ref, perm_ref,
                   y_hbm, xg, yb, zbuf, wg_bf, wu_bf, wd_bf, cnt_smem, gsem, osem, zsem):
    i = pl.program_id(0)
    n_used = n_used_ref[0]
    slot = i % 2
    last = pl.num_programs(0) - 1

    def for_pieces(b, fn):
        e = blk_e[b]
        b0 = blk_row0[b]

        def body(t, tot):
            j = t * N_EXPERTS + e
            run0 = dst_ref[j]
            lo = jnp.maximum(run0, b0)
            hi = jnp.minimum(run0 + pad8_ref[j], b0 + MOE_BM)
            n = pl.multiple_of(jnp.maximum(hi - lo, 0), 8)

            @pl.when(n > 0)
            def _():
                fn(pl.multiple_of(t * R_TILE + off_ref[j] + lo - run0, 8), pl.multiple_of(lo - b0, 8), n)

            return tot + n

        return lax.fori_loop(blk_t0[b], blk_t1[b], body, jnp.int32(0))

    def rows_copy(src, dst, sem, n):
        return pltpu.make_async_copy(src.at[pl.ds(0, n), :], dst.at[pl.ds(0, n), :], sem)

    def start_gather(b, s):
        def piece(row_t, row_b, n):
            pltpu.make_async_copy(xs_hbm.at[pl.ds(row_t, n), :], xg.at[s].at[pl.ds(row_b, n), :], gsem.at[s]).start()

        cnt_smem[s] = for_pieces(b, piece)

    def tail_copy(t):
        n = pl.multiple_of(R_TILE - tot8_ref[t], 8)
        return n, pltpu.make_async_copy(
            zbuf.at[pl.ds(0, n), :], y_hbm.at[pl.ds(pl.multiple_of(t * R_TILE + tot8_ref[t], 8), n), :], zsem)

    @pl.when(i == 0)
    def _():
        xg[...] = jnp.zeros_like(xg)
        zbuf[...] = jnp.zeros_like(zbuf)
        for s in range(4):
            cnt_smem[s] = 0

        def fill(t, c):
            n, cp = tail_copy(t)

            @pl.when(n > 0)
            def _():
                cp.start()

            return c

        lax.fori_loop(0, nt, fill, 0)

        @pl.when(n_used > 0)
        def _():
            start_gather(0, 0)

    @pl.when(i + 1 < n_used)
    def _():
        start_gather(i + 1, 1 - slot)

    @pl.when(i < n_used)
    def _():
        e_changed = jnp.logical_or(i == 0, blk_e[i] != blk_e[jnp.maximum(i - 1, 0)])

        @pl.when(e_changed)
        def _():
            wd_bf[...] = wd_ref[...].astype(BF)
            for c in range(D_FF // LANES):
                pair = _dot(wgu_ref[:, 2 * c * LANES:2 * (c + 1) * LANES].astype(BF), perm_ref[...])
                wg_bf[:, c * LANES:(c + 1) * LANES] = pair[:, 0:LANES].astype(BF)
                wu_bf[:, c * LANES:(c + 1) * LANES] = pair[:, LANES:].astype(BF)

        n_in = pl.multiple_of(cnt_smem[slot], 8)
        rows_copy(xs_hbm, xg.at[slot], gsem.at[slot], n_in).wait()
        xb = xg[slot].astype(BF)
        gate = jnp.minimum(_dot(xb, wg_bf[...]) + bg_ref[...], SWIGLU_LIMIT)
        up = jnp.clip(_dot(xb, wu_bf[...]) + bu_ref[...], -SWIGLU_LIMIT, SWIGLU_LIMIT)
        glu = gate * jax.nn.sigmoid(gate * SWIGLU_ALPHA)
        y = _dot(((up + 1.0) * glu).astype(BF), wd_bf[...]) + bd_ref[...]

        n_prev = pl.multiple_of(cnt_smem[2 + slot], 8)

        @pl.when(n_prev > 0)
        def _():
            rows_copy(yb.at[slot], y_hbm, osem.at[slot], n_prev).wait()

        yb[slot] = y.astype(BF).astype(F32)

        def piece(row_t, row_b, n):
            pltpu.make_async_copy(yb.at[slot].at[pl.ds(row_b, n), :], y_hbm.at[pl.ds(row_t, n), :],
                                  osem.at[slot]).start()

        cnt_smem[2 + slot] = for_pieces(i, piece)

    @pl.when(i == last)
    def _():
        for s in range(2):
            n_out = pl.multiple_of(cnt_smem[2 + s], 8)

            @pl.when(n_out > 0)
            def _():
                rows_copy(yb.at[s], y_hbm, osem.at[s], n_out).wait()

        def drain(t, c):
            n, cp = tail_copy(t)

            @pl.when(n > 0)
            def _():
                cp.wait()

            return c

        lax.fori_loop(0, nt, drain, 0)


def _experts(l, xs, plan, p, nt):
    n_blocks = plan["blk_e"].shape[0]
    n_pref = 9
    by_expert = lambda *lead: (lambda i, be, *_: lead + (be[i], 0, 0))
    b_spec = pl.BlockSpec((None, None, 1, D_FF), by_expert(l))
    grid_spec = pltpu.PrefetchScalarGridSpec(
        num_scalar_prefetch=n_pref,
        grid=(n_blocks,),
        in_specs=[
            pl.BlockSpec(memory_space=pl.ANY),
            pl.BlockSpec((None, None, D_MODEL, 2 * D_FF), by_expert(l)),
            b_spec, b_spec,
            pl.BlockSpec((None, None, D_FF, D_MODEL), by_expert(l)),
            pl.BlockSpec((None, None, 1, D_MODEL), by_expert(l)),
            pl.BlockSpec((2 * LANES, 2 * LANES), lambda i, *_: (0, 0)),
        ],
        out_specs=pl.BlockSpec(memory_space=pl.ANY),
        scratch_shapes=[
            pltpu.VMEM((2, MOE_BM, D_MODEL), F32),
            pltpu.VMEM((2, MOE_BM, D_MODEL), F32),
            pltpu.VMEM((R_TILE - TOP_K * TM, D_MODEL), F32),
            pltpu.VMEM((D_MODEL, D_FF), BF),
            pltpu.VMEM((D_MODEL, D_FF), BF),
            pltpu.VMEM((D_FF, D_MODEL), BF),
            pltpu.SMEM((4,), jnp.int32),
            pltpu.SemaphoreType.DMA((2,)),
            pltpu.SemaphoreType.DMA((2,)),
            pltpu.SemaphoreType.DMA,
        ],
    )
    return pl.pallas_call(
        functools.partial(_expert_kernel, nt),
        grid_spec=grid_spec,
        out_shape=jax.ShapeDtypeStruct((nt * R_TILE, D_MODEL), F32),
        compiler_params=pltpu.CompilerParams(dimension_semantics=("arbitrary",), vmem_limit_bytes=VMEM_LIMIT),
        name="experts",
    )(plan["blk_e"], plan["blk_row0"], plan["blk_t0"], plan["blk_t1"], plan["n_used"], plan["pad8"], plan["dst"],
      plan["off"], plan["tot8"],
      xs, p["w_gu"], p["b_gate"], p["b_up"], p["w_dn"], p["b_dn"], p["pair_perm"])


def _pair_perm():
    m = np.zeros((2 * LANES, 2 * LANES), np.float32)
    j = np.arange(LANES)
    m[2 * j, j] = 1.0
    m[2 * j + 1, LANES + j] = 1.0
    return jnp.asarray(m, BF)


def _combine_kernel(final, y_ref, x_ref, mod_ref, rl_ref, tw_ref, fg_ref, o_ref):
    r_lane = lax.broadcasted_iota(jnp.int32, (TM, R_TILE), 1)
    rl = rl_ref[...]
    tw = tw_ref[...]
    sel = jnp.zeros((TM, R_TILE), F32)
    for k in range(TOP_K):
        sel = jnp.where(r_lane == rl[:, k:k + 1], tw[:, k:k + 1], sel)
    ffn = _dot(sel.astype(BF), y_ref[...].astype(BF))
    x = x_ref[...] + mod_ref[5:6, :] * ffn
    if final:
        x = _rms(x) * fg_ref[...]
    o_ref[...] = x


def _combine(l, final, y, x, mod, rl, tw, final_g, geo, tile0, n_tiles):
    seq_row = geo["seq_row"]
    tile = lambda w: pl.BlockSpec((TM, w), lambda i: (i + tile0, 0))
    return pl.pallas_call(
        functools.partial(_combine_kernel, final),
        grid=(n_tiles,),
        in_specs=[
            pl.BlockSpec((R_TILE, D_MODEL), lambda i: (i + tile0, 0)),
            tile(D_MODEL),
            pl.BlockSpec((None, None, 6, D_MODEL), lambda i: (l, seq_row(i + tile0), 0, 0)),
            tile(LANES),
            tile(LANES),
            _const_spec((1, D_MODEL)),
        ],
        out_specs=pl.BlockSpec((TM, D_MODEL), lambda i: (i, 0)),
        out_shape=jax.ShapeDtypeStruct((n_tiles * TM, D_MODEL), F32),
        compiler_params=pltpu.CompilerParams(dimension_semantics=("arbitrary",), vmem_limit_bytes=VMEM_LIMIT),
        name="combine",
    )(y, x, mod, rl, tw, final_g)


def _expert_plan(meta, nt):
    pad8 = meta[:, 0, :N_EXPERTS]
    off = meta[:, 1, :N_EXPERTS]
    ends = jnp.cumsum(pad8, axis=0)
    dst = ends - pad8
    tot = ends[-1]
    nb = (tot + MOE_BM - 1) // MOE_BM
    nb_end = jnp.cumsum(nb)
    n_blocks = (nt * TM * TOP_K + nt * N_EXPERTS * 7) // MOE_BM + N_EXPERTS
    b = jnp.arange(n_blocks, dtype=jnp.int32)
    blk_e = jnp.minimum(jnp.sum(nb_end[None, :] <= b[:, None], axis=1), N_EXPERTS - 1).astype(jnp.int32)
    blk_row0 = (b - (nb_end - nb)[blk_e]) * MOE_BM
    ends_b = ends[:, blk_e]
    dst_b = dst[:, blk_e]
    blk_t0 = jnp.sum(ends_b <= blk_row0[None, :], axis=0)
    blk_t1 = jnp.sum(dst_b < blk_row0[None, :] + MOE_BM, axis=0)
    i32 = lambda v: v.astype(jnp.int32)
    return {"blk_e": blk_e, "blk_row0": i32(blk_row0), "blk_t0": i32(blk_t0), "blk_t1": i32(blk_t1),
            "n_used": i32(nb_end[-1]).reshape(1), "pad8": i32(pad8.reshape(-1)), "dst": i32(dst.reshape(-1)),
            "off": i32(off.reshape(-1)), "tot8": i32(jnp.sum(pad8, axis=1))}


def _rope_tables(n_pos):
    pos = np.arange(n_pos)
    row, col = pos // GRID_W, pos % GRID_W
    lane = np.arange(LANES)

    def build(active, r, half):
        n_rot = 4 * half
        is_col = (r % n_rot) >= 2 * half
        rr = r % (2 * half)
        freq = np.power(ROPE_BASE, -(rr % half).astype(np.float64) / half)
        p = np.where(is_col[None, :], col[:, None], row[:, None]).astype(np.float64)
        ang = p * freq[None, :]
        first = rr < half
        cos = np.where(active[None, :], np.cos(ang), 1.0)
        sin_a = np.where((active & first)[None, :], -np.sin(ang), 0.0)
        sin_b = np.where((active & ~first)[None, :], np.sin(ang), 0.0)
        return [cos, sin_a, sin_b]

    tabs = build(np.ones(LANES, bool), lane % GQA_HEAD_DIM, GQA_HEAD_DIM // 4)
    in_rope = (lane >= MLA_NOPE) & (lane < MLA_NOPE + MLA_ROPE)
    tabs += build(in_rope, np.maximum(lane - MLA_NOPE, 0) % MLA_ROPE, MLA_ROPE // 4)
    tabs += build(lane < MLA_ROPE, lane % MLA_ROPE, MLA_ROPE // 4)
    table = np.concatenate(tabs, axis=1)
    ident = np.concatenate([np.ones((TM, LANES)), np.zeros((TM, LANES)), np.zeros((TM, LANES))] * 3, axis=1)
    return jnp.asarray(np.concatenate([ident, table], axis=0), F32)


def _placement():
    e = np.zeros((LANES, MLA_HEADS * LANES), np.float32)
    for h in range(MLA_HEADS):
        for r in range(MLA_ROPE):
            e[r, h * LANES + MLA_NOPE + r] = 1.0
    return jnp.asarray(e, BF)


def _split_hi_lo(w):
    hi = w.astype(BF)
    lo = (w - hi.astype(F32)).astype(BF)
    return jnp.concatenate([hi, lo], axis=-1)


def _block_diag_ones(n, g):
    idx = np.arange(n) // g
    return jnp.asarray((idx[:, None] == idx[None, :]).astype(np.float32), BF)


def kernel(x_prompt, x_sample, cache_gqa_k, cache_gqa_v, cache_mla_ckv, cache_mla_krope, c, c_ctx, norm1_g, norm2_g, w_mod, b_mod, w_in, conv_dw, conv_dw_b, conv_ln_g, conv_ln_b, w_conv_out, pool_w, pool_scale, w_pool_out, gqa_qn_g, gqa_kn_g, w_gqa_out, mla_qn_g, w_mla_q_up, mla_kvn_g, w_mla_kv_up, w_mla_out, w_o, w_router, b_router, w_gu, b_gu, w_dn, b_dn, final_g):
    bp, seq, d = x_prompt.shape
    bs, ls, _ = x_sample.shape
    depth = w_in.shape[0]
    past = cache_gqa_k.shape[2]
    assert seq == TM and d == D_MODEL and ls % TM == 0 and (bp * seq) % ls == 0
    tps = ls // TM
    n_p = bp
    nt = n_p + bs * tps
    geo = {
        "np": n_p, "bs": bs, "tps": tps, "nt": nt,
        "seq_row": lambda i: jnp.where(i < n_p, 0, 1 + (i - n_p) // tps),
        "rope_blk": lambda i: jnp.where(i < n_p, 0, 1 + (i - n_p) % tps),
    }

    n_cond = -(-(1 + bs) // 8) * 8
    cond = jnp.zeros((n_cond, d), F32).at[0].set(c_ctx).at[1:1 + bs].set(c)
    mod = _modulation(cond, w_mod, b_mod).reshape(depth, n_cond, 6, d)

    row = lambda v: v.reshape(depth, 1, -1)
    w1 = jnp.pad(w_in[:, :, :_SPLIT_GATE], ((0, 0), (0, 0), (0, W1_COLS - _SPLIT_GATE))).astype(BF)
    wqup = jnp.pad(w_mla_q_up.reshape(depth, MLA_Q_RANK, MLA_HEADS, MLA_NOPE + MLA_ROPE),
                   ((0, 0), (0, 0), (0, 0), (0, LANES - MLA_NOPE - MLA_ROPE)))
    wkv = w_mla_kv_up.reshape(depth, MLA_KV_RANK, MLA_HEADS, MLA_NOPE + MLA_V)
    wk_pad = jnp.pad(wkv[..., :MLA_NOPE], ((0, 0), (0, 0), (0, 0), (0, LANES - MLA_NOPE)))
    wkvup = jnp.concatenate([wk_pad.reshape(depth, MLA_KV_RANK, MLA_HEADS * LANES),
                             wkv[..., MLA_NOPE:].reshape(depth, MLA_KV_RANK, MLA_HEADS * MLA_V)], axis=-1)
    bgu = b_gu.reshape(depth, N_EXPERTS, 1, D_FF, 2)
    p = {
        "norm1_g": row(norm1_g), "norm2_g": row(norm2_g),
        "w1": w1, "wgate": w_in[:, :, _SPLIT_GATE:].astype(BF),
        "rope": _rope_tables(ls), "place": _placement(), "ones_bd": _block_diag_ones(512, GQA_HEAD_DIM),
        "ltri": jnp.asarray(np.tril(np.ones((TM, TM), np.float32), -1), BF),
        "upper": jnp.asarray(np.triu(np.ones((LANES, LANES), np.float32), 1), F32),
        "qn_g": row(jnp.tile(gqa_qn_g, (1, GQA_HEADS))), "kn_g": row(jnp.tile(gqa_kn_g, (1, GQA_KV_HEADS))),
        "cqn_g": row(mla_qn_g), "kvn_g": row(mla_kvn_g),
        "wqup": wqup.reshape(depth, MLA_Q_RANK, MLA_HEADS * LANES).astype(BF), "wkvup": wkvup.astype(BF),
        "conv_dw": conv_dw, "conv_dw_b": row(conv_dw_b), "conv_ln_g": row(conv_ln_g), "conv_ln_b": row(conv_ln_b),
        "w_conv_out": w_conv_out.astype(BF), "pool_w": pool_w.astype(BF), "pool_scale": row(pool_scale),
        "w_pool_out": w_pool_out.astype(BF), "w_gqa_out": w_gqa_out.astype(BF), "w_mla_out": w_mla_out.astype(BF),
        "w_o": w_o.astype(BF),
        "w_router": _split_hi_lo(jnp.pad(w_router, ((0, 0), (0, 0), (0, LANES - N_EXPERTS)))),
        "b_router": row(jnp.pad(b_router, ((0, 0), (0, LANES - N_EXPERTS)))),
        "w_gu": w_gu, "b_gate": bgu[..., 0], "b_up": bgu[..., 1], "pair_perm": _pair_perm(),
        "w_dn": w_dn, "b_dn": b_dn.reshape(depth, N_EXPERTS, 1, d),
    }

    ckm, cvm = _cache_prep(cache_mla_ckv, jnp.pad(cache_mla_krope, ((0, 0), (0, 0), (0, 0), (0, LANES - MLA_ROPE))), p)
    cache = (cache_gqa_k.reshape(bs, depth, past, LANES).astype(BF),
             cache_gqa_v.reshape(bs, depth, past, LANES).astype(BF), ckm, cvm)

    x = jnp.concatenate([x_prompt.reshape(bp * seq, d), x_sample.reshape(bs * ls, d)], axis=0)
    n_ptok = bp * seq
    states = []
    for l in range(depth):
        a, b, q, k, v, qm, km, vm, ks, vs, ckvs, krs = _proj(l, x, mod, geo, p)
        states.append((ks[:n_ptok], vs[:n_ptok], ckvs[:n_ptok], krs[:n_ptok]))
        attn = _attention(l, geo, q, k, v, qm, km, vm, cache)
        x_mid, xs, slot_row, top_w, meta = _merge(l, x, mod, geo, a, b, attn, p)
        y = _experts(l, xs, _expert_plan(meta, nt), p, nt)
        combine = functools.partial(_combine, l, l == depth - 1, y, x_mid, mod, slot_row, top_w,
                                    final_g.reshape(1, d), geo)
        if l < depth - 1:
            x = combine(0, nt)

    y_prompt = combine(0, n_p).reshape(bp, seq, d)
    y_sample = combine(n_p, nt - n_p).reshape(bs, ls, d)
    st = lambda j, shape: jnp.stack([s[j].reshape(shape) for s in states], axis=1)
    return (y_prompt, y_sample,
            st(0, (bp, seq, GQA_KV_HEADS, GQA_HEAD_DIM)), st(1, (bp, seq, GQA_KV_HEADS, GQA_HEAD_DIM)),
            st(2, (bp, seq, MLA_KV_RANK)), st(3, (bp, seq, MLA_ROPE)))
```

```python
import functools

import jax
import jax.numpy as jnp
import numpy as np
from jax import lax
from jax.experimental import pallas as pl
from jax.experimental.pallas import tpu as pltpu

D_MODEL = 1024
GRID_W = 64
CONV_W = 512
CONV_K = 31
POOL_W = 512
POOL_WINDOWS = (2, 4, 8, 16)
POOL_G = 128
GQA_HEADS = 8
GQA_KV_HEADS = 2
GQA_HEAD_DIM = 64
MLA_HEADS = 8
MLA_Q_RANK = 384
MLA_KV_RANK = 256
MLA_NOPE = 64
MLA_ROPE = 32
MLA_V = 64
ROPE_BASE = 10000.0
N_EXPERTS = 32
TOP_K = 4
D_FF = 1024
SWIGLU_LIMIT = 7.0
SWIGLU_ALPHA = 1.702
EPS = 1e-6
GQA_SCALE = GQA_HEAD_DIM ** -0.5
MLA_SCALE = (MLA_NOPE + MLA_ROPE) ** -0.5

LANES = 128
SUBLANES = 8
TM = 256
HALO = 16
MOE_BM = 512
MOE_SUB = 256
R_TILE = 1280
W1_COLS = 3072
N_GATE = 4 * D_MODEL
VMEM_LIMIT = 56 * 1024 * 1024

BF = jnp.bfloat16
F32 = jnp.float32

_C_A, _C_B, _C_Q, _C_K, _C_V, _C_CQ, _C_CKV, _C_KR = 0, 1024, 1536, 2048, 2176, 2304, 2688, 2944
_SPLIT_GATE = 2976


def _dot(a, b):
    return jnp.dot(a, b, preferred_element_type=F32)


def _dot_nt(a, b):
    return lax.dot_general(a, b, (((1,), (1,)), ((), ())), preferred_element_type=F32)


def _rms(x):
    return x * lax.rsqrt(jnp.mean(x * x, axis=-1, keepdims=True) + EPS)


def _group_mean_sq(x, ones_bd, width):
    xx = x * x
    hi = xx.astype(BF)
    lo = (xx - hi.astype(F32)).astype(BF)
    return (_dot(hi, ones_bd) + _dot(lo, ones_bd)) * (1.0 / width)


def _tile_lanes(t, width):
    reps = width // LANES
    return t if reps == 1 else jnp.concatenate([t] * reps, axis=-1)


def _rope(x, cos, sin_a, sin_b, shift):
    w = x.shape[-1]
    return (x * _tile_lanes(cos, w) + pltpu.roll(x, w - shift, 1) * _tile_lanes(sin_a, w)
            + pltpu.roll(x, shift, 1) * _tile_lanes(sin_b, w))


def _modulated_norm(x, g, shift, scale):
    return _rms(x) * g * (1.0 + scale) + shift


def _mod_kernel(cond_ref, w_ref, b_ref, o_ref):
    c = cond_ref[...]
    s = (c * jax.nn.sigmoid(c)).astype(BF)
    o_ref[...] = _dot(s, w_ref[...].astype(BF)) + b_ref[...]


def _modulation(cond, w_mod, b_mod):
    depth, d, n = w_mod.shape
    rows = cond.shape[0]
    return pl.pallas_call(
        _mod_kernel,
        grid=(depth, n // D_MODEL),
        in_specs=[
            pl.BlockSpec((rows, d), lambda l, j: (0, 0)),
            pl.BlockSpec((None, d, D_MODEL), lambda l, j: (l, 0, j)),
            pl.BlockSpec((None, 1, D_MODEL), lambda l, j: (l, 0, j)),
        ],
        out_specs=pl.BlockSpec((None, rows, D_MODEL), lambda l, j: (l, 0, j)),
        out_shape=jax.ShapeDtypeStruct((depth, rows, n), F32),
        name="modulation",
    )(cond, w_mod, b_mod.reshape(depth, 1, n))


def _proj_kernel(x_ref, mod_ref, g1_ref, w1_ref, rope_ref, qn_ref, kn_ref, cqn_ref, kvn_ref,
                 wqup_ref, wkvup_ref, place_ref, ones_ref,
                 a_ref, b_ref, q_ref, k_ref, v_ref, qm_ref, km_ref, vm_ref,
                 ks_ref, vs_ref, ckvs_ref, krs_ref):
    x = x_ref[...]
    h = _modulated_norm(x, g1_ref[...], mod_ref[0:1, :], mod_ref[1:2, :])
    y = _dot(h.astype(BF), w1_ref[...])

    a_ref[...] = (y[:, _C_A:_C_A + CONV_W] * jax.nn.sigmoid(y[:, _C_A + CONV_W:_C_B])).astype(BF)
    b_ref[...] = y[:, _C_B:_C_Q].astype(BF)

    def tab(j):
        return rope_ref[:, j * LANES:(j + 1) * LANES]

    q = y[:, _C_Q:_C_K]
    q = q * lax.rsqrt(_group_mean_sq(q, ones_ref[...], GQA_HEAD_DIM) + EPS) * qn_ref[...]
    q = _rope(q, tab(0), tab(1), tab(2), GQA_HEAD_DIM // 4)
    q_ref[...] = (q * GQA_SCALE).astype(BF)

    k = y[:, _C_K:_C_V]
    k = k * lax.rsqrt(_group_mean_sq(k, ones_ref[0:LANES, 0:LANES], GQA_HEAD_DIM) + EPS) * kn_ref[...]
    ks_ref[...] = k
    k_ref[...] = _rope(k, tab(0), tab(1), tab(2), GQA_HEAD_DIM // 4).astype(BF)

    v = y[:, _C_V:_C_CQ]
    vs_ref[...] = v
    v_ref[...] = v.astype(BF)

    cq = _rms(y[:, _C_CQ:_C_CKV]) * cqn_ref[...]
    qm = _dot(cq.astype(BF), wqup_ref[...])
    qm = _rope(qm, tab(3), tab(4), tab(5), MLA_ROPE // 4)
    qm_ref[...] = (qm * MLA_SCALE).astype(BF)

    ckv = _rms(y[:, _C_CKV:_C_KR]) * kvn_ref[...]
    ckvs_ref[...] = ckv
    kv = _dot(ckv.astype(BF), wkvup_ref[...])
    kr = y[:, _C_KR:W1_COLS]
    krs_ref[...] = kr[:, 0:MLA_ROPE]
    kr_rot = _rope(kr, tab(6), tab(7), tab(8), MLA_ROPE // 4)
    km = kv[:, 0:MLA_HEADS * LANES] + _dot(kr_rot.astype(BF), place_ref[...])
    km_ref[...] = km.astype(BF)
    vm_ref[...] = kv[:, MLA_HEADS * LANES:].astype(BF)


def _const_spec(shape):
    nd = len(shape)
    return pl.BlockSpec(shape, lambda *_: (0,) * nd)


def _layer_spec(shape, l):
    nd = len(shape)
    return pl.BlockSpec((None,) + shape, lambda *_: (l,) + (0,) * nd)


def _proj(l, x, mod, geo, p):
    nt = geo["nt"]
    tok = nt * TM
    seq_row, rope_blk = geo["seq_row"], geo["rope_blk"]

    def tile(width, dtype):
        return pl.BlockSpec((TM, width), lambda i: (i, 0)), jax.ShapeDtypeStruct((tok, width), dtype)

    outs = [tile(CONV_W, BF), tile(POOL_W, BF), tile(512, BF), tile(LANES, BF), tile(LANES, BF),
            tile(MLA_HEADS * LANES, BF), tile(MLA_HEADS * LANES, BF), tile(MLA_HEADS * MLA_V, BF),
            tile(LANES, F32), tile(LANES, F32), tile(MLA_KV_RANK, F32), tile(MLA_ROPE, F32)]
    return pl.pallas_call(
        _proj_kernel,
        grid=(nt,),
        in_specs=[
            pl.BlockSpec((TM, D_MODEL), lambda i: (i, 0)),
            pl.BlockSpec((None, None, 6, D_MODEL), lambda i: (l, seq_row(i), 0, 0)),
            _layer_spec((1, D_MODEL), l),
            _layer_spec((D_MODEL, W1_COLS), l),
            pl.BlockSpec((TM, 9 * LANES), lambda i: (rope_blk(i), 0)),
            _layer_spec((1, 512), l),
            _layer_spec((1, LANES), l),
            _layer_spec((1, MLA_Q_RANK), l),
            _layer_spec((1, MLA_KV_RANK), l),
            _layer_spec((MLA_Q_RANK, MLA_HEADS * LANES), l),
            _layer_spec((MLA_KV_RANK, MLA_HEADS * (LANES + MLA_V)), l),
            _const_spec((LANES, MLA_HEADS * LANES)),
            _const_spec((512, 512)),
        ],
        out_specs=[o[0] for o in outs],
        out_shape=[o[1] for o in outs],
        compiler_params=pltpu.CompilerParams(dimension_semantics=("arbitrary",), vmem_limit_bytes=VMEM_LIMIT),
        name="proj",
    )(x, mod, p["norm1_g"], p["w1"], p["rope"], p["qn_g"], p["kn_g"], p["cqn_g"], p["kvn_g"],
      p["wqup"], p["wkvup"], p["place"], p["ones_bd"])


def _cache_kernel(ckv_ref, kr_ref, wkvup_ref, place_ref, km_ref, vm_ref):
    kv = _dot(ckv_ref[...].astype(BF), wkvup_ref[...])
    km = kv[:, 0:MLA_HEADS * LANES] + _dot(kr_ref[...].astype(BF), place_ref[...])
    km_ref[...] = km.astype(BF)
    vm_ref[...] = kv[:, MLA_HEADS * LANES:].astype(BF)


def _cache_prep(ckv, kr_pad, p):
    bs, depth, past, _ = ckv.shape
    return pl.pallas_call(
        _cache_kernel,
        grid=(bs, depth),
        in_specs=[
            pl.BlockSpec((None, None, past, MLA_KV_RANK), lambda b, l: (b, l, 0, 0)),
            pl.BlockSpec((None, None, past, LANES), lambda b, l: (b, l, 0, 0)),
            pl.BlockSpec((None, MLA_KV_RANK, MLA_HEADS * (LANES + MLA_V)), lambda b, l: (l, 0, 0)),
            pl.BlockSpec((LANES, MLA_HEADS * LANES), lambda b, l: (0, 0)),
        ],
        out_specs=[
            pl.BlockSpec((None, None, past, MLA_HEADS * LANES), lambda b, l: (b, l, 0, 0)),
            pl.BlockSpec((None, None, past, MLA_HEADS * MLA_V), lambda b, l: (b, l, 0, 0)),
        ],
        out_shape=[jax.ShapeDtypeStruct((bs, depth, past, MLA_HEADS * LANES), BF),
                   jax.ShapeDtypeStruct((bs, depth, past, MLA_HEADS * MLA_V), BF)],
        name="cache_prep",
    )(ckv, kr_pad, p["wkvup"], p["place"])


def _softmax_pv(s, v):
    m = jnp.max(s, axis=-1, keepdims=True)
    e = jnp.exp(s - m)
    l = jnp.sum(e, axis=-1, keepdims=True)
    return _dot(e.astype(BF), v) / l


def _attention_body(q, k, v, qm, km, vm, oc_ref, od_ref):
    lane_k = lax.broadcasted_iota(jnp.int32, k.shape, 1)
    lane_q = lax.broadcasted_iota(jnp.int32, (TM, LANES), 1)
    lo_k = lane_k < GQA_HEAD_DIM
    lo_q = lane_q < GQA_HEAD_DIM
    k32, v32 = k.astype(F32), v.astype(F32)
    k_sw = pltpu.roll(k32, GQA_HEAD_DIM, 1)
    v_sw = pltpu.roll(v32, GQA_HEAD_DIM, 1)
    k_dup = [jnp.where(lo_k, k32, k_sw).astype(BF), jnp.where(lo_k, k_sw, k32).astype(BF)]
    v_dup = [jnp.where(lo_k, v32, v_sw).astype(BF), jnp.where(lo_k, v_sw, v32).astype(BF)]
    zero = jnp.zeros((TM, LANES), BF)
    group = GQA_HEADS // GQA_KV_HEADS
    for j in range(GQA_HEADS // 2):
        qs = q[:, j * LANES:(j + 1) * LANES]
        g = (2 * j) // group
        o_lo = _softmax_pv(_dot_nt(jnp.where(lo_q, qs, zero), k_dup[g]), v_dup[g])
        o_hi = _softmax_pv(_dot_nt(jnp.where(lo_q, zero, qs), k_dup[g]), v_dup[g])
        oc_ref[:, j * LANES:(j + 1) * LANES] = jnp.where(lo_q, o_lo, o_hi).astype(BF)
    for j in range(MLA_HEADS // 2):
        vs = vm[:, j * LANES:(j + 1) * LANES]
        outs = []
        for h in (2 * j, 2 * j + 1):
            s = _dot_nt(qm[:, h * LANES:(h + 1) * LANES], km[:, h * LANES:(h + 1) * LANES])
            outs.append(_softmax_pv(s, vs))
        od_ref[:, j * LANES:(j + 1) * LANES] = jnp.where(lo_q, outs[0], outs[1]).astype(BF)


def _attn_prompt_kernel(q_ref, k_ref, v_ref, qm_ref, km_ref, vm_ref, oc_ref, od_ref):
    _attention_body(q_ref[...], k_ref[...], v_ref[...], qm_ref[...], km_ref[...], vm_ref[...], oc_ref, od_ref)


def _attn_sample_kernel(q_ref, k_ref, v_ref, qm_ref, km_ref, vm_ref, ck_ref, cv_ref, ckm_ref, cvm_ref,
                        oc_ref, od_ref):
    k = jnp.concatenate([ck_ref[...], k_ref[...]], axis=0)
    v = jnp.concatenate([cv_ref[...], v_ref[...]], axis=0)
    km = jnp.concatenate([ckm_ref[...], km_ref[...]], axis=0)
    vm = jnp.concatenate([cvm_ref[...], vm_ref[...]], axis=0)
    _attention_body(q_ref[...], k, v, qm_ref[...], km, vm, oc_ref, od_ref)


def _attention(l, geo, q, k, v, qm, km, vm, cache):
    n_p, bs, tps, nt = geo["np"], geo["bs"], geo["tps"], geo["nt"]
    tok = nt * TM
    ls = tps * TM
    off = (n_p * TM) // ls
    widths = (512, LANES, LANES, MLA_HEADS * LANES, MLA_HEADS * LANES, MLA_HEADS * MLA_V)
    params = pltpu.CompilerParams(dimension_semantics=("arbitrary",), vmem_limit_bytes=VMEM_LIMIT)
    oc_p, od_p = pl.pallas_call(
        _attn_prompt_kernel,
        grid=(n_p,),
        in_specs=[pl.BlockSpec((TM, w), lambda i: (i, 0)) for w in widths],
        out_specs=[pl.BlockSpec((TM, 512), lambda i: (i, 0))] * 2,
        out_shape=[jax.ShapeDtypeStruct((n_p * TM, 512), BF)] * 2,
        compiler_params=params,
        name="attn_prompt",
    )(q, k, v, qm, km, vm)
    ck, cv, ckm, cvm = cache
    past = ck.shape[2]
    q_spec = lambda w: pl.BlockSpec((TM, w), lambda b, j: (n_p + b * tps + j, 0))
    kv_spec = lambda w: pl.BlockSpec((ls, w), lambda b, j: (off + b, 0))
    c_spec = lambda w: pl.BlockSpec((None, None, past, w), lambda b, j: (b, l, 0, 0))
    params2 = pltpu.CompilerParams(dimension_semantics=("arbitrary", "arbitrary"), vmem_limit_bytes=VMEM_LIMIT)
    o_spec = pl.BlockSpec((TM, 512), lambda b, j: (b * tps + j, 0))
    oc_s, od_s = pl.pallas_call(
        _attn_sample_kernel,
        grid=(bs, tps),
        in_specs=[q_spec(512), kv_spec(LANES), kv_spec(LANES), q_spec(MLA_HEADS * LANES),
                  kv_spec(MLA_HEADS * LANES), kv_spec(MLA_HEADS * MLA_V),
                  c_spec(LANES), c_spec(LANES), c_spec(MLA_HEADS * LANES), c_spec(MLA_HEADS * MLA_V)],
        out_specs=[o_spec, o_spec],
        out_shape=[jax.ShapeDtypeStruct((bs * ls, 512), BF)] * 2,
        compiler_params=params2,
        name="attn_sample",
    )(q, k, v, qm, km, vm, ck, cv, ckm, cvm)
    return oc_p, od_p, oc_s, od_s


def _merge_kernel(geo, x_ref, mod_ref, a_ref, ap_ref, an_ref, b_ref, bp_ref, bn_ref,
                  ocp_ref, odp_ref, ocs_ref, ods_ref,
                  g1_ref, wg_ref, cw_ref, cb_ref, lng_ref, lnb_ref, wco_ref, pw_ref, ps_ref, wpo_ref,
                  wgo_ref, wmo_ref, wo_ref, g2_ref, wr_ref, br_ref,
                  ltri_ref, upper_ref,
                  xo_ref, xs_ref, rl_ref, tw_ref, meta_ref, abuf, bbuf, ashift):
    n_p, tps = geo["np"], geo["tps"]
    i = pl.program_id(0)
    j = jnp.where(i < n_p, 0, (i - n_p) % tps)
    n_seq_tiles = jnp.where(i < n_p, 1, tps)
    has_prev = j > 0
    has_next = j < n_seq_tiles - 1

    def fill(buf, cur, prev, nxt):
        buf[0:HALO, :] = jnp.where(has_prev, prev[...].astype(F32), 0.0)
        buf[HALO:HALO + TM, :] = cur[...].astype(F32)
        buf[HALO + TM:, :] = jnp.where(has_next, nxt[...].astype(F32), 0.0)

    fill(abuf, a_ref, ap_ref, an_ref)
    fill(bbuf, b_ref, bp_ref, bn_ref)

    sh_rows = TM + 2 * HALO - SUBLANES
    for sh in range(1, SUBLANES):
        ashift[sh - 1] = abuf[sh:sh + sh_rows, :]
    rows = 32
    conv = []
    for r0 in range(0, TM, rows):
        acc = jnp.zeros((rows, CONV_W), F32)
        for t in range(CONV_K):
            s = r0 + t + HALO - CONV_K // 2
            sh = s % SUBLANES
            tap = abuf[s:s + rows, :] if sh == 0 else ashift[sh - 1, s - sh:s - sh + rows, :]
            acc = acc + tap * cw_ref[t:t + 1, :]
        conv.append(acc)
    ca = jnp.concatenate(conv, axis=0) + cb_ref[...]
    mu = jnp.mean(ca, axis=-1, keepdims=True)
    xc = ca - mu
    ln = xc * lax.rsqrt(jnp.mean(xc * xc, axis=-1, keepdims=True) + EPS) * lng_ref[...] + lnb_ref[...]
    br_a = _dot((ln * jax.nn.sigmoid(ln)).astype(BF), wco_ref[...])

    pos = j * TM + lax.broadcasted_iota(jnp.int32, (TM, 1), 0)
    seq_len = n_seq_tiles * TM
    mixed = []
    for g, w in enumerate(POOL_WINDOWS):
        lo, hi = g * POOL_G, (g + 1) * POOL_G
        acc = jnp.zeros((TM, POOL_G), F32)
        for o in range(-(w // 2), w - w // 2):
            acc = acc + bbuf[HALO + o:HALO + o + TM, lo:hi]
        cnt = jnp.clip(pos - w // 2 + w, 0, seq_len) - jnp.clip(pos - w // 2, 0, seq_len)
        pooled = acc / cnt.astype(F32) - bbuf[HALO:HALO + TM, lo:hi]
        mixed.append(_dot(pooled.astype(BF), pw_ref[g]))
    pb = jnp.concatenate(mixed, axis=-1) * ps_ref[...]
    br_b = _dot(pb.astype(BF), wpo_ref[...])

    is_prompt = i < n_p
    br_c = _dot(jnp.where(is_prompt, ocp_ref[...], ocs_ref[...]), wgo_ref[...])
    br_d = _dot(jnp.where(is_prompt, odp_ref[...], ods_ref[...]), wmo_ref[...])

    x = x_ref[...]
    h = _modulated_norm(x, g1_ref[...], mod_ref[0:1, :], mod_ref[1:2, :])
    gates = jax.nn.sigmoid(_dot(h.astype(BF), wg_ref[...]).astype(BF))
    merged = (gates[:, 0:D_MODEL] * br_a.astype(BF) + gates[:, D_MODEL:2 * D_MODEL] * br_b.astype(BF)
              + gates[:, 2 * D_MODEL:3 * D_MODEL] * br_c.astype(BF) + gates[:, 3 * D_MODEL:] * br_d.astype(BF))
    x = x + mod_ref[2:3, :] * _dot(merged, wo_ref[...])
    xo_ref[...] = x

    h2 = _modulated_norm(x, g2_ref[...], mod_ref[3:4, :], mod_ref[4:5, :])

    h2_hi = h2.astype(BF)
    h2_lo = (h2 - h2_hi.astype(F32)).astype(BF)
    hi_terms = _dot(h2_hi, wr_ref[...])
    logits = (hi_terms[:, 0:LANES] + hi_terms[:, LANES:] + _dot(h2_lo, wr_ref[:, 0:LANES])) + br_ref[...]
    lane = lax.broadcasted_iota(jnp.int32, (TM, LANES), 1).astype(F32)
    neg = jnp.float32(-jnp.inf)
    lg = jnp.where(lane < N_EXPERTS, logits, neg)
    vals, idxs = [], []
    for _ in range(TOP_K):
        m = jnp.max(lg, axis=-1, keepdims=True)
        idx = jnp.min(jnp.where(lg == m, lane, float(LANES)), axis=-1, keepdims=True)
        vals.append(m)
        idxs.append(idx)
        lg = jnp.where(lane == idx, neg, lg)
    exps = [jnp.exp(v - vals[0]) for v in vals]
    denom = exps[0] + exps[1] + exps[2] + exps[3]

    onehot = [(lane == idxs[kk]).astype(F32) for kk in range(TOP_K)]
    colsum = [jnp.sum(o, axis=0, keepdims=True) for o in onehot]
    cnt = colsum[0] + colsum[1] + colsum[2] + colsum[3]
    pad8 = jnp.floor((cnt + 7.0) * 0.125) * 8.0
    run_off = jnp.dot(jnp.broadcast_to(pad8, (8, LANES)), upper_ref[...], preferred_element_type=F32,
                      precision=lax.Precision.HIGHEST)[0:1, :]
    base = jnp.zeros((1, LANES), F32)
    rloc = []
    for kk in range(TOP_K):
        before = _dot(ltri_ref[...], onehot[kk].astype(BF))
        rloc.append(jnp.sum(onehot[kk] * (run_off + base + before), axis=-1, keepdims=True))
        base = base + colsum[kk]

    eye = (lax.broadcasted_iota(jnp.int32, (TM, TM), 0) == lax.broadcasted_iota(jnp.int32, (TM, TM), 1))
    r_sub = lax.broadcasted_iota(jnp.int32, (R_TILE, TM), 0).astype(F32)
    ones8 = jnp.ones((8, TM), BF)
    sel = jnp.zeros((R_TILE, TM), F32)
    for kk in range(TOP_K):
        hi = jnp.floor(rloc[kk] * (1.0 / TM))
        lo = rloc[kk] - hi * TM
        row = (_dot(ones8, jnp.where(eye, hi, 0.0).astype(BF)) * TM
               + _dot(ones8, jnp.where(eye, lo, 0.0).astype(BF)))[0:1, :]
        sel = jnp.where(r_sub == row, 1.0, sel)
    xs_ref[...] = _dot(sel.astype(BF), h2.astype(BF))

    rl = jnp.zeros((TM, LANES), F32)
    tw = jnp.zeros((TM, LANES), F32)
    for kk in range(TOP_K):
        rl = jnp.where(lane == kk, rloc[kk], rl)
        tw = jnp.where(lane == kk, exps[kk] / denom, tw)
    rl_ref[...] = rl.astype(jnp.int32)
    tw_ref[...] = tw
    sub = lax.broadcasted_iota(jnp.int32, (8, LANES), 0)
    meta = jnp.where(sub == 0, jnp.broadcast_to(pad8, (8, LANES)),
                     jnp.where(sub == 1, jnp.broadcast_to(run_off, (8, LANES)), 0.0))
    meta_ref[...] = meta.astype(jnp.int32)


def _merge(l, x, mod, geo, a, b, attn, p):
    nt, n_p = geo["nt"], geo["np"]
    tok = nt * TM
    seq_row = geo["seq_row"]
    hb = TM // HALO
    last_hb = nt * hb - 1
    tile = lambda w: pl.BlockSpec((TM, w), lambda i: (i, 0))
    prev = lambda w: pl.BlockSpec((HALO, w), lambda i: (jnp.maximum(i * hb - 1, 0), 0))
    nxt = lambda w: pl.BlockSpec((HALO, w), lambda i: (jnp.minimum((i + 1) * hb, last_hb), 0))
    p_tile = pl.BlockSpec((TM, 512), lambda i: (jnp.minimum(i, n_p - 1), 0))
    s_tile = pl.BlockSpec((TM, 512), lambda i: (jnp.maximum(i - n_p, 0), 0))
    return pl.pallas_call(
        functools.partial(_merge_kernel, geo),
        grid=(nt,),
        in_specs=[
            tile(D_MODEL),
            pl.BlockSpec((None, None, 6, D_MODEL), lambda i: (l, seq_row(i), 0, 0)),
            tile(CONV_W), prev(CONV_W), nxt(CONV_W),
            tile(POOL_W), prev(POOL_W), nxt(POOL_W),
            p_tile, p_tile, s_tile, s_tile,
            _layer_spec((1, D_MODEL), l),
            _layer_spec((D_MODEL, N_GATE), l),
            _layer_spec((CONV_K, CONV_W), l),
            _layer_spec((1, CONV_W), l),
            _layer_spec((1, CONV_W), l),
            _layer_spec((1, CONV_W), l),
            _layer_spec((CONV_W, D_MODEL), l),
            _layer_spec((len(POOL_WINDOWS), POOL_G, POOL_G), l),
            _layer_spec((1, POOL_W), l),
            _layer_spec((POOL_W, D_MODEL), l),
            _layer_spec((512, D_MODEL), l),
            _layer_spec((512, D_MODEL), l),
            _layer_spec((D_MODEL, D_MODEL), l),
            _layer_spec((1, D_MODEL), l),
            _layer_spec((D_MODEL, 2 * LANES), l),
            _layer_spec((1, LANES), l),
            _const_spec((TM, TM)),
            _const_spec((LANES, LANES)),
        ],
        out_specs=[tile(D_MODEL), pl.BlockSpec((R_TILE, D_MODEL), lambda i: (i, 0)), tile(LANES), tile(LANES),
                   pl.BlockSpec((None, 8, LANES), lambda i: (i, 0, 0))],
        out_shape=[jax.ShapeDtypeStruct((tok, D_MODEL), F32), jax.ShapeDtypeStruct((nt * R_TILE, D_MODEL), F32),
                   jax.ShapeDtypeStruct((tok, LANES), jnp.int32), jax.ShapeDtypeStruct((tok, LANES), F32),
                   jax.ShapeDtypeStruct((nt, 8, LANES), jnp.int32)],
        scratch_shapes=[pltpu.VMEM((TM + 2 * HALO, CONV_W), F32), pltpu.VMEM((TM + 2 * HALO, POOL_W), F32),
                        pltpu.VMEM((SUBLANES - 1, TM + 2 * HALO - SUBLANES, CONV_W), F32)],
        compiler_params=pltpu.CompilerParams(dimension_semantics=("arbitrary",), vmem_limit_bytes=VMEM_LIMIT),
        name="merge",
    )(x, mod, a, a, a, b, b, b, *attn,
      p["norm1_g"], p["wgate"], p["conv_dw"], p["conv_dw_b"], p["conv_ln_g"], p["conv_ln_b"], p["w_conv_out"],
      p["pool_w"], p["pool_scale"], p["w_pool_out"], p["w_gqa_out"], p["w_mla_out"], p["w_o"],
      p["norm2_g"], p["w_router"], p["b_router"], p["ltri"], p["upper"])


def _expert_kernel(nt, blk_e, blk_row0, blk_t0, blk_t1, n_used_ref, pad8_ref, dst_ref, off_ref, tot8_ref,
                   xs_hbm, wgu_ref, bg_ref, bu_ref, wd_ref, bd_ref, perm_ref,
                   y_hbm, xg, yb, zbuf, wg_bf, wu_bf, wd_bf, cnt_smem, gsem, osem, zsem):
    i = pl.program_id(0)
    n_used = n_used_ref[0]
    slot = i % 2
    last = pl.num_programs(0) - 1

    def for_pieces(b, fn):
        e = blk_e[b]
        b0 = blk_row0[b]

        def body(t, tot):
            j = t * N_EXPERTS + e
            run0 = dst_ref[j]
            lo = jnp.maximum(run0, b0)
            hi = jnp.minimum(run0 + pad8_ref[j], b0 + MOE_BM)
            n = pl.multiple_of(jnp.maximum(hi - lo, 0), 8)

            @pl.when(n > 0)
            def _():
                fn(pl.multiple_of(t * R_TILE + off_ref[j] + lo - run0, 8), pl.multiple_of(lo - b0, 8), n)

            return tot + n

        return lax.fori_loop(blk_t0[b], blk_t1[b], body, jnp.int32(0))

    def rows_copy(src, dst, sem, n):
        return pltpu.make_async_copy(src.at[pl.ds(0, n), :], dst.at[pl.ds(0, n), :], sem)

    def start_gather(b, s):
        def piece(row_t, row_b, n):
            pltpu.make_async_copy(xs_hbm.at[pl.ds(row_t, n), :], xg.at[s].at[pl.ds(row_b, n), :], gsem.at[s]).start()

        cnt_smem[s] = for_pieces(b, piece)

    def tail_copy(t):
        n = pl.multiple_of(R_TILE - tot8_ref[t], 8)
        return n, pltpu.make_async_copy(
            zbuf.at[pl.ds(0, n), :], y_hbm.at[pl.ds(pl.multiple_of(t * R_TILE + tot8_ref[t], 8), n), :], zsem)

    @pl.when(i == 0)
    def _():
        xg[...] = jnp.zeros_like(xg)
        zbuf[...] = jnp.zeros_like(zbuf)
        for s in range(4):
            cnt_smem[s] = 0

        def fill(t, c):
            n, cp = tail_copy(t)

            @pl.when(n > 0)
            def _():
                cp.start()

            return c

        lax.fori_loop(0, nt, fill, 0)

        @pl.when(n_used > 0)
        def _():
            start_gather(0, 0)

    @pl.when(i + 1 < n_used)
    def _():
        start_gather(i + 1, 1 - slot)

    @pl.when(i < n_used)
    def _():
        e_changed = jnp.logical_or(i == 0, blk_e[i] != blk_e[jnp.maximum(i - 1, 0)])

        @pl.when(e_changed)
        def _():
            wd_bf[...] = wd_ref[...].astype(BF)
            for c in range(D_FF // LANES):
                pair = _dot(wgu_ref[:, 2 * c * LANES:2 * (c + 1) * LANES].astype(BF), perm_ref[...])
                wg_bf[:, c * LANES:(c + 1) * LANES] = pair[:, 0:LANES].astype(BF)
                wu_bf[:, c * LANES:(c + 1) * LANES] = pair[:, LANES:].astype(BF)

        n_prev = pl.multiple_of(cnt_smem[2 + slot], 8)

        @pl.when(n_prev > 0)
        def _():
            rows_copy(yb.at[slot], y_hbm, osem.at[slot], n_prev).wait()

        n_in = pl.multiple_of(cnt_smem[slot], 8)
        rows_copy(xs_hbm, xg.at[slot], gsem.at[slot], n_in).wait()

        def expert_mlp(r0):
            xb = xg[slot, r0:r0 + MOE_SUB, :].astype(BF)
            gate = jnp.minimum(_dot(xb, wg_bf[...]) + bg_ref[...], SWIGLU_LIMIT)
            up = jnp.clip(_dot(xb, wu_bf[...]) + bu_ref[...], -SWIGLU_LIMIT, SWIGLU_LIMIT)
            glu = gate * jax.nn.sigmoid(gate * SWIGLU_ALPHA)
            y = _dot(((up + 1.0) * glu).astype(BF), wd_bf[...]) + bd_ref[...]
            yb[slot, r0:r0 + MOE_SUB, :] = y.astype(BF).astype(F32)

        expert_mlp(0)
        for r0 in range(MOE_SUB, MOE_BM, MOE_SUB):
            @pl.when(n_in > r0)
            def _():
                expert_mlp(r0)

        def piece(row_t, row_b, n):
            pltpu.make_async_copy(yb.at[slot].at[pl.ds(row_b, n), :], y_hbm.at[pl.ds(row_t, n), :],
                                  osem.at[slot]).start()

        cnt_smem[2 + slot] = for_pieces(i, piece)

    @pl.when(i == last)
    def _():
        for s in range(2):
            n_out = pl.multiple_of(cnt_smem[2 + s], 8)

            @pl.when(n_out > 0)
            def _():
                rows_copy(yb.at[s], y_hbm, osem.at[s], n_out).wait()

        def drain(t, c):
            n, cp = tail_copy(t)

            @pl.when(n > 0)
            def _():
                cp.wait()

            return c

        lax.fori_loop(0, nt, drain, 0)


def _experts(l, xs, plan, p, nt):
    n_blocks = plan["blk_e"].shape[0]
    n_pref = 9
    by_expert = lambda *lead: (lambda i, be, *_: lead + (be[i], 0, 0))
    b_spec = pl.BlockSpec((None, None, 1, D_FF), by_expert(l))
    grid_spec = pltpu.PrefetchScalarGridSpec(
        num_scalar_prefetch=n_pref,
        grid=(n_blocks,),
        in_specs=[
            pl.BlockSpec(memory_space=pl.ANY),
            pl.BlockSpec((None, None, D_MODEL, 2 * D_FF), by_expert(l)),
            b_spec, b_spec,
            pl.BlockSpec((None, None, D_FF, D_MODEL), by_expert(l)),
            pl.BlockSpec((None, None, 1, D_MODEL), by_expert(l)),
            pl.BlockSpec((2 * LANES, 2 * LANES), lambda i, *_: (0, 0)),
        ],
        out_specs=pl.BlockSpec(memory_space=pl.ANY),
        scratch_shapes=[
            pltpu.VMEM((2, MOE_BM, D_MODEL), F32),
            pltpu.VMEM((2, MOE_BM, D_MODEL), F32),
            pltpu.VMEM((R_TILE - TOP_K * TM, D_MODEL), F32),
            pltpu.VMEM((D_MODEL, D_FF), BF),
            pltpu.VMEM((D_MODEL, D_FF), BF),
            pltpu.VMEM((D_FF, D_MODEL), BF),
            pltpu.SMEM((4,), jnp.int32),
            pltpu.SemaphoreType.DMA((2,)),
            pltpu.SemaphoreType.DMA((2,)),
            pltpu.SemaphoreType.DMA,
        ],
    )
    return pl.pallas_call(
        functools.partial(_expert_kernel, nt),
        grid_spec=grid_spec,
        out_shape=jax.ShapeDtypeStruct((nt * R_TILE, D_MODEL), F32),
        compiler_params=pltpu.CompilerParams(dimension_semantics=("arbitrary",), vmem_limit_bytes=VMEM_LIMIT),
        name="experts",
    )(plan["blk_e"], plan["blk_row0"], plan["blk_t0"], plan["blk_t1"], plan["n_used"], plan["pad8"], plan["dst"],
      plan["off"], plan["tot8"],
      xs, p["w_gu"], p["b_gate"], p["b_up"], p["w_dn"], p["b_dn"], p["pair_perm"])


def _pair_perm():
    m = np.zeros((2 * LANES, 2 * LANES), np.float32)
    j = np.arange(LANES)
    m[2 * j, j] = 1.0
    m[2 * j + 1, LANES + j] = 1.0
    return jnp.asarray(m, BF)


def _combine_kernel(final, y_ref, x_ref, mod_ref, rl_ref, tw_ref, fg_ref, o_ref):
    r_lane = lax.broadcasted_iota(jnp.int32, (TM, R_TILE), 1)
    rl = rl_ref[...]
    tw = tw_ref[...]
    sel = jnp.zeros((TM, R_TILE), F32)
    for k in range(TOP_K):
        sel = jnp.where(r_lane == rl[:, k:k + 1], tw[:, k:k + 1], sel)
    ffn = _dot(sel.astype(BF), y_ref[...].astype(BF))
    x = x_ref[...] + mod_ref[5:6, :] * ffn
    if final:
        x = _rms(x) * fg_ref[...]
    o_ref[...] = x


def _combine(l, final, y, x, mod, rl, tw, final_g, geo, tile0, n_tiles):
    seq_row = geo["seq_row"]
    tile = lambda w: pl.BlockSpec((TM, w), lambda i: (i + tile0, 0))
    return pl.pallas_call(
        functools.partial(_combine_kernel, final),
        grid=(n_tiles,),
        in_specs=[
            pl.BlockSpec((R_TILE, D_MODEL), lambda i: (i + tile0, 0)),
            tile(D_MODEL),
            pl.BlockSpec((None, None, 6, D_MODEL), lambda i: (l, seq_row(i + tile0), 0, 0)),
            tile(LANES),
            tile(LANES),
            _const_spec((1, D_MODEL)),
        ],
        out_specs=pl.BlockSpec((TM, D_MODEL), lambda i: (i, 0)),
        out_shape=jax.ShapeDtypeStruct((n_tiles * TM, D_MODEL), F32),
        compiler_params=pltpu.CompilerParams(dimension_semantics=("arbitrary",), vmem_limit_bytes=VMEM_LIMIT),
        name="combine",
    )(y, x, mod, rl, tw, final_g)


def _expert_plan(meta, nt):
    pad8 = meta[:, 0, :N_EXPERTS]
    off = meta[:, 1, :N_EXPERTS]
    ends = jnp.cumsum(pad8, axis=0)
    dst = ends - pad8
    tot = ends[-1]
    nb = (tot + MOE_BM - 1) // MOE_BM
    nb_end = jnp.cumsum(nb)
    n_blocks = (nt * TM * TOP_K + nt * N_EXPERTS * 7) // MOE_BM + N_EXPERTS
    b = jnp.arange(n_blocks, dtype=jnp.int32)
    blk_e = jnp.minimum(jnp.sum(nb_end[None, :] <= b[:, None], axis=1), N_EXPERTS - 1).astype(jnp.int32)
    blk_row0 = (b - (nb_end - nb)[blk_e]) * MOE_BM
    ends_b = ends[:, blk_e]
    dst_b = dst[:, blk_e]
    blk_t0 = jnp.sum(ends_b <= blk_row0[None, :], axis=0)
    blk_t1 = jnp.sum(dst_b < blk_row0[None, :] + MOE_BM, axis=0)
    i32 = lambda v: v.astype(jnp.int32)
    return {"blk_e": blk_e, "blk_row0": i32(blk_row0), "blk_t0": i32(blk_t0), "blk_t1": i32(blk_t1),
            "n_used": i32(nb_end[-1]).reshape(1), "pad8": i32(pad8.reshape(-1)), "dst": i32(dst.reshape(-1)),
            "off": i32(off.reshape(-1)), "tot8": i32(jnp.sum(pad8, axis=1))}


def _rope_tables(n_pos):
    pos = np.arange(n_pos)
    row, col = pos // GRID_W, pos % GRID_W
    lane = np.arange(LANES)

    def build(active, r, half):
        n_rot = 4 * half
        is_col = (r % n_rot) >= 2 * half
        rr = r % (2 * half)
        freq = np.power(ROPE_BASE, -(rr % half).astype(np.float64) / half)
        p = np.where(is_col[None, :], col[:, None], row[:, None]).astype(np.float64)
        ang = p * freq[None, :]
        first = rr < half
        cos = np.where(active[None, :], np.cos(ang), 1.0)
        sin_a = np.where((active & first)[None, :], -np.sin(ang), 0.0)
        sin_b = np.where((active & ~first)[None, :], np.sin(ang), 0.0)
        return [cos, sin_a, sin_b]

    tabs = build(np.ones(LANES, bool), lane % GQA_HEAD_DIM, GQA_HEAD_DIM // 4)
    in_rope = (lane >= MLA_NOPE) & (lane < MLA_NOPE + MLA_ROPE)
    tabs += build(in_rope, np.maximum(lane - MLA_NOPE, 0) % MLA_ROPE, MLA_ROPE // 4)
    tabs += build(lane < MLA_ROPE, lane % MLA_ROPE, MLA_ROPE // 4)
    table = np.concatenate(tabs, axis=1)
    ident = np.concatenate([np.ones((TM, LANES)), np.zeros((TM, LANES)), np.zeros((TM, LANES))] * 3, axis=1)
    return jnp.asarray(np.concatenate([ident, table], axis=0), F32)


def _placement():
    e = np.zeros((LANES, MLA_HEADS * LANES), np.float32)
    for h in range(MLA_HEADS):
        for r in range(MLA_ROPE):
            e[r, h * LANES + MLA_NOPE + r] = 1.0
    return jnp.asarray(e, BF)


def _split_hi_lo(w):
    hi = w.astype(BF)
    lo = (w - hi.astype(F32)).astype(BF)
    return jnp.concatenate([hi, lo], axis=-1)


def _block_diag_ones(n, g):
    idx = np.arange(n) // g
    return jnp.asarray((idx[:, None] == idx[None, :]).astype(np.float32), BF)


def kernel(x_prompt, x_sample, cache_gqa_k, cache_gqa_v, cache_mla_ckv, cache_mla_krope, c, c_ctx, norm1_g, norm2_g, w_mod, b_mod, w_in, conv_dw, conv_dw_b, conv_ln_g, conv_ln_b, w_conv_out, pool_w, pool_scale, w_pool_out, gqa_qn_g, gqa_kn_g, w_gqa_out, mla_qn_g, w_mla_q_up, mla_kvn_g, w_mla_kv_up, w_mla_out, w_o, w_router, b_router, w_gu, b_gu, w_dn, b_dn, final_g):
    bp, seq, d = x_prompt.shape
    bs, ls, _ = x_sample.shape
    depth = w_in.shape[0]
    past = cache_gqa_k.shape[2]
    assert seq == TM and d == D_MODEL and ls % TM == 0 and (bp * seq) % ls == 0
    tps = ls // TM
    n_p = bp
    nt = n_p + bs * tps
    geo = {
        "np": n_p, "bs": bs, "tps": tps, "nt": nt,
        "seq_row": lambda i: jnp.where(i < n_p, 0, 1 + (i - n_p) // tps),
        "rope_blk": lambda i: jnp.where(i < n_p, 0, 1 + (i - n_p) % tps),
    }

    n_cond = -(-(1 + bs) // 8) * 8
    cond = jnp.zeros((n_cond, d), F32).at[0].set(c_ctx).at[1:1 + bs].set(c)
    mod = _modulation(cond, w_mod, b_mod).reshape(depth, n_cond, 6, d)

    row = lambda v: v.reshape(depth, 1, -1)
    w1 = jnp.pad(w_in[:, :, :_SPLIT_GATE], ((0, 0), (0, 0), (0, W1_COLS - _SPLIT_GATE))).astype(BF)
    wqup = jnp.pad(w_mla_q_up.reshape(depth, MLA_Q_RANK, MLA_HEADS, MLA_NOPE + MLA_ROPE),
                   ((0, 0), (0, 0), (0, 0), (0, LANES - MLA_NOPE - MLA_ROPE)))
    wkv = w_mla_kv_up.reshape(depth, MLA_KV_RANK, MLA_HEADS, MLA_NOPE + MLA_V)
    wk_pad = jnp.pad(wkv[..., :MLA_NOPE], ((0, 0), (0, 0), (0, 0), (0, LANES - MLA_NOPE)))
    wkvup = jnp.concatenate([wk_pad.reshape(depth, MLA_KV_RANK, MLA_HEADS * LANES),
                             wkv[..., MLA_NOPE:].reshape(depth, MLA_KV_RANK, MLA_HEADS * MLA_V)], axis=-1)
    bgu = b_gu.reshape(depth, N_EXPERTS, 1, D_FF, 2)
    p = {
        "norm1_g": row(norm1_g), "norm2_g": row(norm2_g),
        "w1": w1, "wgate": w_in[:, :, _SPLIT_GATE:].astype(BF),
        "rope": _rope_tables(ls), "place": _placement(), "ones_bd": _block_diag_ones(512, GQA_HEAD_DIM),
        "ltri": jnp.asarray(np.tril(np.ones((TM, TM), np.float32), -1), BF),
        "upper": jnp.asarray(np.triu(np.ones((LANES, LANES), np.float32), 1), F32),
        "qn_g": row(jnp.tile(gqa_qn_g, (1, GQA_HEADS))), "kn_g": row(jnp.tile(gqa_kn_g, (1, GQA_KV_HEADS))),
        "cqn_g": row(mla_qn_g), "kvn_g": row(mla_kvn_g),
        "wqup": wqup.reshape(depth, MLA_Q_RANK, MLA_HEADS * LANES).astype(BF), "wkvup": wkvup.astype(BF),
        "conv_dw": conv_dw, "conv_dw_b": row(conv_dw_b), "conv_ln_g": row(conv_ln_g), "conv_ln_b": row(conv_ln_b),
        "w_conv_out": w_conv_out.astype(BF), "pool_w": pool_w.astype(BF), "pool_scale": row(pool_scale),
        "w_pool_out": w_pool_out.astype(BF), "w_gqa_out": w_gqa_out.astype(BF), "w_mla_out": w_mla_out.astype(BF),
        "w_o": w_o.astype(BF),
        "w_router": _split_hi_lo(jnp.pad(w_router, ((0, 0), (0, 0), (0, LANES - N_EXPERTS)))),
        "b_router": row(jnp.pad(b_router, ((0, 0), (0, LANES - N_EXPERTS)))),
        "w_gu": w_gu, "b_gate": bgu[..., 0], "b_up": bgu[..., 1], "pair_perm": _pair_perm(),
        "w_dn": w_dn, "b_dn": b_dn.reshape(depth, N_EXPERTS, 1, d),
    }

    ckm, cvm = _cache_prep(cache_mla_ckv, jnp.pad(cache_mla_krope, ((0, 0), (0, 0), (0, 0), (0, LANES - MLA_ROPE))), p)
    cache = (cache_gqa_k.reshape(bs, depth, past, LANES).astype(BF),
             cache_gqa_v.reshape(bs, depth, past, LANES).astype(BF), ckm, cvm)

    x = jnp.concatenate([x_prompt.reshape(bp * seq, d), x_sample.reshape(bs * ls, d)], axis=0)
    n_ptok = bp * seq
    states = []
    for l in range(depth):
        a, b, q, k, v, qm, km, vm, ks, vs, ckvs, krs = _proj(l, x, mod, geo, p)
        states.append((ks[:n_ptok], vs[:n_ptok], ckvs[:n_ptok], krs[:n_ptok]))
        attn = _attention(l, geo, q, k, v, qm, km, vm, cache)
        x_mid, xs, slot_row, top_w, meta = _merge(l, x, mod, geo, a, b, attn, p)
        y = _experts(l, xs, _expert_plan(meta, nt), p, nt)
        combine = functools.partial(_combine, l, l == depth - 1, y, x_mid, mod, slot_row, top_w,
                                    final_g.reshape(1, d), geo)
        if l < depth - 1:
            x = combine(0, nt)

    y_prompt = combine(0, n_p).reshape(bp, seq, d)
    y_sample = combine(n_p, nt - n_p).reshape(bs, ls, d)
    st = lambda j, shape: jnp.stack([s[j].reshape(shape) for s in states], axis=1)
    return (y_prompt, y_sample,
            st(0, (bp, seq, GQA_KV_HEADS, GQA_HEAD_DIM)), st(1, (bp, seq, GQA_KV_HEADS, GQA_HEAD_DIM)),
            st(2, (bp, seq, MLA_KV_RANK)), st(3, (bp, seq, MLA_ROPE)))
```

```python
import functools

import jax
import jax.numpy as jnp
import numpy as np
from jax import lax
from jax.experimental import pallas as pl
from jax.experimental.pallas import tpu as pltpu

D_MODEL = 1024
GRID_W = 64
CONV_W = 512
CONV_K = 31
POOL_W = 512
POOL_WINDOWS = (2, 4, 8, 16)
POOL_G = 128
GQA_HEADS = 8
GQA_KV_HEADS = 2
GQA_HEAD_DIM = 64
MLA_HEADS = 8
MLA_Q_RANK = 384
MLA_KV_RANK = 256
MLA_NOPE = 64
MLA_ROPE = 32
MLA_V = 64
ROPE_BASE = 10000.0
N_EXPERTS = 32
TOP_K = 4
D_FF = 1024
SWIGLU_LIMIT = 7.0
SWIGLU_ALPHA = 1.702
EPS = 1e-6
GQA_SCALE = GQA_HEAD_DIM ** -0.5
MLA_SCALE = (MLA_NOPE + MLA_ROPE) ** -0.5
LOG2E = 1.4426950408889634

LANES = 128
SUBLANES = 8
TM = 256
HALO = 16
MOE_BM = 512
R_TILE = 1280
W1_COLS = 3072
N_GATE = 4 * D_MODEL
VMEM_LIMIT = 56 * 1024 * 1024

BF = jnp.bfloat16
F32 = jnp.float32

_C_A, _C_B, _C_Q, _C_K, _C_V, _C_CQ, _C_CKV, _C_KR = 0, 1024, 1536, 2048, 2176, 2304, 2688, 2944
_SPLIT_GATE = 2976


def _dot(a, b):
    return jnp.dot(a, b, preferred_element_type=F32)


def _dot_nt(a, b):
    return lax.dot_general(a, b, (((1,), (1,)), ((), ())), preferred_element_type=F32)


def _rms(x):
    return x * lax.rsqrt(jnp.mean(x * x, axis=-1, keepdims=True) + EPS)


def _group_mean_sq(x, ones_bd, width):
    xx = x * x
    hi = xx.astype(BF)
    lo = (xx - hi.astype(F32)).astype(BF)
    return (_dot(hi, ones_bd) + _dot(lo, ones_bd)) * (1.0 / width)


def _tile_lanes(t, width):
    reps = width // LANES
    return t if reps == 1 else jnp.concatenate([t] * reps, axis=-1)


def _rope(x, cos, sin_a, sin_b, shift):
    w = x.shape[-1]
    return (x * _tile_lanes(cos, w) + pltpu.roll(x, w - shift, 1) * _tile_lanes(sin_a, w)
            + pltpu.roll(x, shift, 1) * _tile_lanes(sin_b, w))


def _modulated_norm(x, g, shift, scale):
    return _rms(x) * g * (1.0 + scale) + shift


def _mod_kernel(cond_ref, w_ref, b_ref, o_ref):
    c = cond_ref[...]
    s = (c * jax.nn.sigmoid(c)).astype(BF)
    o_ref[...] = _dot(s, w_ref[...].astype(BF)) + b_ref[...]


def _modulation(cond, w_mod, b_mod):
    depth, d, n = w_mod.shape
    rows = cond.shape[0]
    return pl.pallas_call(
        _mod_kernel,
        grid=(depth, n // D_MODEL),
        in_specs=[
            pl.BlockSpec((rows, d), lambda l, j: (0, 0)),
            pl.BlockSpec((None, d, D_MODEL), lambda l, j: (l, 0, j)),
            pl.BlockSpec((None, 1, D_MODEL), lambda l, j: (l, 0, j)),
        ],
        out_specs=pl.BlockSpec((None, rows, D_MODEL), lambda l, j: (l, 0, j)),
        out_shape=jax.ShapeDtypeStruct((depth, rows, n), F32),
        name="modulation",
    )(cond, w_mod, b_mod.reshape(depth, 1, n))


def _proj_kernel(x_ref, mod_ref, g1_ref, w1_ref, rope_ref, qn_ref, kn_ref, cqn_ref, kvn_ref,
                 wqup_ref, wkvup_ref, place_ref, ones_ref,
                 a_ref, b_ref, q_ref, k_ref, v_ref, qm_ref, km_ref, vm_ref,
                 ks_ref, vs_ref, ckvs_ref, krs_ref):
    x = x_ref[...]
    h = _modulated_norm(x, g1_ref[...], mod_ref[0:1, :], mod_ref[1:2, :])
    y = _dot(h.astype(BF), w1_ref[...])

    a_ref[...] = (y[:, _C_A:_C_A + CONV_W] * jax.nn.sigmoid(y[:, _C_A + CONV_W:_C_B])).astype(BF)
    b_ref[...] = y[:, _C_B:_C_Q].astype(BF)

    def tab(j):
        return rope_ref[:, j * LANES:(j + 1) * LANES]

    q = y[:, _C_Q:_C_K]
    q = q * lax.rsqrt(_group_mean_sq(q, ones_ref[...], GQA_HEAD_DIM) + EPS) * qn_ref[...]
    q = _rope(q, tab(0), tab(1), tab(2), GQA_HEAD_DIM // 4)
    q_ref[...] = (q * (GQA_SCALE * LOG2E)).astype(BF)

    k = y[:, _C_K:_C_V]
    k = k * lax.rsqrt(_group_mean_sq(k, ones_ref[0:LANES, 0:LANES], GQA_HEAD_DIM) + EPS) * kn_ref[...]
    ks_ref[...] = k
    k_ref[...] = _rope(k, tab(0), tab(1), tab(2), GQA_HEAD_DIM // 4).astype(BF)

    v = y[:, _C_V:_C_CQ]
    vs_ref[...] = v
    v_ref[...] = v.astype(BF)

    cq = _rms(y[:, _C_CQ:_C_CKV]) * cqn_ref[...]
    qm = _dot(cq.astype(BF), wqup_ref[...])
    qm = _rope(qm, tab(3), tab(4), tab(5), MLA_ROPE // 4)
    qm_ref[...] = (qm * (MLA_SCALE * LOG2E)).astype(BF)

    ckv = _rms(y[:, _C_CKV:_C_KR]) * kvn_ref[...]
    ckvs_ref[...] = ckv
    kv = _dot(ckv.astype(BF), wkvup_ref[...])
    kr = y[:, _C_KR:W1_COLS]
    krs_ref[...] = kr[:, 0:MLA_ROPE]
    kr_rot = _rope(kr, tab(6), tab(7), tab(8), MLA_ROPE // 4)
    km = kv[:, 0:MLA_HEADS * LANES] + _dot(kr_rot.astype(BF), place_ref[...])
    km_ref[...] = km.astype(BF)
    vm_ref[...] = kv[:, MLA_HEADS * LANES:].astype(BF)


def _const_spec(shape):
    nd = len(shape)
    return pl.BlockSpec(shape, lambda *_: (0,) * nd)


def _layer_spec(shape, l):
    nd = len(shape)
    return pl.BlockSpec((None,) + shape, lambda *_: (l,) + (0,) * nd)


def _proj(l, x, mod, geo, p):
    nt = geo["nt"]
    tok = nt * TM
    seq_row, rope_blk = geo["seq_row"], geo["rope_blk"]

    def tile(width, dtype):
        return pl.BlockSpec((TM, width), lambda i: (i, 0)), jax.ShapeDtypeStruct((tok, width), dtype)

    outs = [tile(CONV_W, BF), tile(POOL_W, BF), tile(512, BF), tile(LANES, BF), tile(LANES, BF),
            tile(MLA_HEADS * LANES, BF), tile(MLA_HEADS * LANES, BF), tile(MLA_HEADS * MLA_V, BF),
            tile(LANES, F32), tile(LANES, F32), tile(MLA_KV_RANK, F32), tile(MLA_ROPE, F32)]
    return pl.pallas_call(
        _proj_kernel,
        grid=(nt,),
        in_specs=[
            pl.BlockSpec((TM, D_MODEL), lambda i: (i, 0)),
            pl.BlockSpec((None, None, 6, D_MODEL), lambda i: (l, seq_row(i), 0, 0)),
            _layer_spec((1, D_MODEL), l),
            _layer_spec((D_MODEL, W1_COLS), l),
            pl.BlockSpec((TM, 9 * LANES), lambda i: (rope_blk(i), 0)),
            _layer_spec((1, 512), l),
            _layer_spec((1, LANES), l),
            _layer_spec((1, MLA_Q_RANK), l),
            _layer_spec((1, MLA_KV_RANK), l),
            _layer_spec((MLA_Q_RANK, MLA_HEADS * LANES), l),
            _layer_spec((MLA_KV_RANK, MLA_HEADS * (LANES + MLA_V)), l),
            _const_spec((LANES, MLA_HEADS * LANES)),
            _const_spec((512, 512)),
        ],
        out_specs=[o[0] for o in outs],
        out_shape=[o[1] for o in outs],
        compiler_params=pltpu.CompilerParams(dimension_semantics=("arbitrary",), vmem_limit_bytes=VMEM_LIMIT),
        name="proj",
    )(x, mod, p["norm1_g"], p["w1"], p["rope"], p["qn_g"], p["kn_g"], p["cqn_g"], p["kvn_g"],
      p["wqup"], p["wkvup"], p["place"], p["ones_bd"])


def _cache_kernel(ckv_ref, kr_ref, wkvup_ref, place_ref, km_ref, vm_ref):
    kv = _dot(ckv_ref[...].astype(BF), wkvup_ref[...])
    km = kv[:, 0:MLA_HEADS * LANES] + _dot(kr_ref[...].astype(BF), place_ref[...])
    km_ref[...] = km.astype(BF)
    vm_ref[...] = kv[:, MLA_HEADS * LANES:].astype(BF)


def _cache_prep(ckv, kr_pad, p):
    bs, depth, past, _ = ckv.shape
    return pl.pallas_call(
        _cache_kernel,
        grid=(bs, depth),
        in_specs=[
            pl.BlockSpec((None, None, past, MLA_KV_RANK), lambda b, l: (b, l, 0, 0)),
            pl.BlockSpec((None, None, past, LANES), lambda b, l: (b, l, 0, 0)),
            pl.BlockSpec((None, MLA_KV_RANK, MLA_HEADS * (LANES + MLA_V)), lambda b, l: (l, 0, 0)),
            pl.BlockSpec((LANES, MLA_HEADS * LANES), lambda b, l: (0, 0)),
        ],
        out_specs=[
            pl.BlockSpec((None, None, past, MLA_HEADS * LANES), lambda b, l: (b, l, 0, 0)),
            pl.BlockSpec((None, None, past, MLA_HEADS * MLA_V), lambda b, l: (b, l, 0, 0)),
        ],
        out_shape=[jax.ShapeDtypeStruct((bs, depth, past, MLA_HEADS * LANES), BF),
                   jax.ShapeDtypeStruct((bs, depth, past, MLA_HEADS * MLA_V), BF)],
        name="cache_prep",
    )(ckv, kr_pad, p["wkvup"], p["place"])


def _softmax_pv(s, v):
    m = jnp.max(s, axis=-1, keepdims=True)
    e = jnp.exp2(s - m)
    l = jnp.sum(e, axis=-1, keepdims=True)
    return _dot(e.astype(BF), v) / l


def _attention_body(q, k, v, qm, km, vm, oc_ref, od_ref):
    lane_k = lax.broadcasted_iota(jnp.int32, k.shape, 1)
    lane_q = lax.broadcasted_iota(jnp.int32, (TM, LANES), 1)
    lo_k = lane_k < GQA_HEAD_DIM
    lo_q = lane_q < GQA_HEAD_DIM
    k32, v32 = k.astype(F32), v.astype(F32)
    k_sw = pltpu.roll(k32, GQA_HEAD_DIM, 1)
    v_sw = pltpu.roll(v32, GQA_HEAD_DIM, 1)
    k_dup = [jnp.where(lo_k, k32, k_sw).astype(BF), jnp.where(lo_k, k_sw, k32).astype(BF)]
    v_dup = [jnp.where(lo_k, v32, v_sw).astype(BF), jnp.where(lo_k, v_sw, v32).astype(BF)]
    zero = jnp.zeros((TM, LANES), BF)
    group = GQA_HEADS // GQA_KV_HEADS
    for j in range(GQA_HEADS // 2):
        qs = q[:, j * LANES:(j + 1) * LANES]
        g = (2 * j) // group
        o_lo = _softmax_pv(_dot_nt(jnp.where(lo_q, qs, zero), k_dup[g]), v_dup[g])
        o_hi = _softmax_pv(_dot_nt(jnp.where(lo_q, zero, qs), k_dup[g]), v_dup[g])
        oc_ref[:, j * LANES:(j + 1) * LANES] = jnp.where(lo_q, o_lo, o_hi).astype(BF)
    for j in range(MLA_HEADS // 2):
        vs = vm[:, j * LANES:(j + 1) * LANES]
        outs = []
        for h in (2 * j, 2 * j + 1):
            s = _dot_nt(qm[:, h * LANES:(h + 1) * LANES], km[:, h * LANES:(h + 1) * LANES])
            outs.append(_softmax_pv(s, vs))
        od_ref[:, j * LANES:(j + 1) * LANES] = jnp.where(lo_q, outs[0], outs[1]).astype(BF)


def _attn_prompt_kernel(q_ref, k_ref, v_ref, qm_ref, km_ref, vm_ref, oc_ref, od_ref):
    _attention_body(q_ref[...], k_ref[...], v_ref[...], qm_ref[...], km_ref[...], vm_ref[...], oc_ref, od_ref)


def _attn_sample_kernel(q_ref, k_ref, v_ref, qm_ref, km_ref, vm_ref, ck_ref, cv_ref, ckm_ref, cvm_ref,
                        oc_ref, od_ref):
    k = jnp.concatenate([ck_ref[...], k_ref[...]], axis=0)
    v = jnp.concatenate([cv_ref[...], v_ref[...]], axis=0)
    km = jnp.concatenate([ckm_ref[...], km_ref[...]], axis=0)
    vm = jnp.concatenate([cvm_ref[...], vm_ref[...]], axis=0)
    _attention_body(q_ref[...], k, v, qm_ref[...], km, vm, oc_ref, od_ref)


def _attention(l, geo, q, k, v, qm, km, vm, cache):
    n_p, bs, tps, nt = geo["np"], geo["bs"], geo["tps"], geo["nt"]
    tok = nt * TM
    ls = tps * TM
    off = (n_p * TM) // ls
    widths = (512, LANES, LANES, MLA_HEADS * LANES, MLA_HEADS * LANES, MLA_HEADS * MLA_V)
    params = pltpu.CompilerParams(dimension_semantics=("arbitrary",), vmem_limit_bytes=VMEM_LIMIT)
    oc_p, od_p = pl.pallas_call(
        _attn_prompt_kernel,
        grid=(n_p,),
        in_specs=[pl.BlockSpec((TM, w), lambda i: (i, 0)) for w in widths],
        out_specs=[pl.BlockSpec((TM, 512), lambda i: (i, 0))] * 2,
        out_shape=[jax.ShapeDtypeStruct((n_p * TM, 512), BF)] * 2,
        compiler_params=params,
        name="attn_prompt",
    )(q, k, v, qm, km, vm)
    ck, cv, ckm, cvm = cache
    past = ck.shape[2]
    q_spec = lambda w: pl.BlockSpec((TM, w), lambda b, j: (n_p + b * tps + j, 0))
    kv_spec = lambda w: pl.BlockSpec((ls, w), lambda b, j: (off + b, 0))
    c_spec = lambda w: pl.BlockSpec((None, None, past, w), lambda b, j: (b, l, 0, 0))
    params2 = pltpu.CompilerParams(dimension_semantics=("arbitrary", "arbitrary"), vmem_limit_bytes=VMEM_LIMIT)
    o_spec = pl.BlockSpec((TM, 512), lambda b, j: (b * tps + j, 0))
    oc_s, od_s = pl.pallas_call(
        _attn_sample_kernel,
        grid=(bs, tps),
        in_specs=[q_spec(512), kv_spec(LANES), kv_spec(LANES), q_spec(MLA_HEADS * LANES),
                  kv_spec(MLA_HEADS * LANES), kv_spec(MLA_HEADS * MLA_V),
                  c_spec(LANES), c_spec(LANES), c_spec(MLA_HEADS * LANES), c_spec(MLA_HEADS * MLA_V)],
        out_specs=[o_spec, o_spec],
        out_shape=[jax.ShapeDtypeStruct((bs * ls, 512), BF)] * 2,
        compiler_params=params2,
        name="attn_sample",
    )(q, k, v, qm, km, vm, ck, cv, ckm, cvm)
    return oc_p, od_p, oc_s, od_s


def _merge_kernel(geo, x_ref, mod_ref, a_ref, ap_ref, an_ref, b_ref, bp_ref, bn_ref,
                  ocp_ref, odp_ref, ocs_ref, ods_ref,
                  g1_ref, wg_ref, cw_ref, cb_ref, lng_ref, lnb_ref, wco_ref, pw_ref, ps_ref, wpo_ref,
                  wgo_ref, wmo_ref, wo_ref, g2_ref, wr_ref, br_ref,
                  ltri_ref, upper_ref,
                  xo_ref, xs_ref, rl_ref, tw_ref, meta_ref, abuf, bbuf, ashift):
    n_p, tps = geo["np"], geo["tps"]
    i = pl.program_id(0)
    j = jnp.where(i < n_p, 0, (i - n_p) % tps)
    n_seq_tiles = jnp.where(i < n_p, 1, tps)
    has_prev = j > 0
    has_next = j < n_seq_tiles - 1

    def fill(buf, cur, prev, nxt):
        buf[0:HALO, :] = jnp.where(has_prev, prev[...].astype(F32), 0.0)
        buf[HALO:HALO + TM, :] = cur[...].astype(F32)
        buf[HALO + TM:, :] = jnp.where(has_next, nxt[...].astype(F32), 0.0)

    fill(abuf, a_ref, ap_ref, an_ref)
    fill(bbuf, b_ref, bp_ref, bn_ref)

    sh_rows = TM + 2 * HALO - SUBLANES
    for sh in range(1, SUBLANES):
        ashift[sh - 1] = abuf[sh:sh + sh_rows, :]
    rows = 32
    conv = []
    for r0 in range(0, TM, rows):
        acc = jnp.zeros((rows, CONV_W), F32)
        for t in range(CONV_K):
            s = r0 + t + HALO - CONV_K // 2
            sh = s % SUBLANES
            tap = abuf[s:s + rows, :] if sh == 0 else ashift[sh - 1, s - sh:s - sh + rows, :]
            acc = acc + tap * cw_ref[t:t + 1, :]
        conv.append(acc)
    ca = jnp.concatenate(conv, axis=0) + cb_ref[...]
    mu = jnp.mean(ca, axis=-1, keepdims=True)
    xc = ca - mu
    ln = xc * lax.rsqrt(jnp.mean(xc * xc, axis=-1, keepdims=True) + EPS) * lng_ref[...] + lnb_ref[...]
    br_a = _dot((ln * jax.nn.sigmoid(ln)).astype(BF), wco_ref[...])

    pos = j * TM + lax.broadcasted_iota(jnp.int32, (TM, 1), 0)
    seq_len = n_seq_tiles * TM
    mixed = []
    for g, w in enumerate(POOL_WINDOWS):
        lo, hi = g * POOL_G, (g + 1) * POOL_G
        acc = jnp.zeros((TM, POOL_G), F32)
        for o in range(-(w // 2), w - w // 2):
            acc = acc + bbuf[HALO + o:HALO + o + TM, lo:hi]
        cnt = jnp.clip(pos - w // 2 + w, 0, seq_len) - jnp.clip(pos - w // 2, 0, seq_len)
        pooled = acc / cnt.astype(F32) - bbuf[HALO:HALO + TM, lo:hi]
        mixed.append(_dot(pooled.astype(BF), pw_ref[g]))
    pb = jnp.concatenate(mixed, axis=-1) * ps_ref[...]
    br_b = _dot(pb.astype(BF), wpo_ref[...])

    is_prompt = i < n_p
    br_c = _dot(jnp.where(is_prompt, ocp_ref[...], ocs_ref[...]), wgo_ref[...])
    br_d = _dot(jnp.where(is_prompt, odp_ref[...], ods_ref[...]), wmo_ref[...])

    x = x_ref[...]
    h = _modulated_norm(x, g1_ref[...], mod_ref[0:1, :], mod_ref[1:2, :])
    gates = jax.nn.sigmoid(_dot(h.astype(BF), wg_ref[...]).astype(BF))
    merged = (gates[:, 0:D_MODEL] * br_a.astype(BF) + gates[:, D_MODEL:2 * D_MODEL] * br_b.astype(BF)
              + gates[:, 2 * D_MODEL:3 * D_MODEL] * br_c.astype(BF) + gates[:, 3 * D_MODEL:] * br_d.astype(BF))
    x = x + mod_ref[2:3, :] * _dot(merged, wo_ref[...])
    xo_ref[...] = x

    h2 = _modulated_norm(x, g2_ref[...], mod_ref[3:4, :], mod_ref[4:5, :])

    h2_hi = h2.astype(BF)
    h2_lo = (h2 - h2_hi.astype(F32)).astype(BF)
    hi_terms = _dot(h2_hi, wr_ref[...])
    logits = (hi_terms[:, 0:LANES] + hi_terms[:, LANES:] + _dot(h2_lo, wr_ref[:, 0:LANES])) + br_ref[...]
    lane = lax.broadcasted_iota(jnp.int32, (TM, LANES), 1).astype(F32)
    neg = jnp.float32(-jnp.inf)
    lg = jnp.where(lane < N_EXPERTS, logits, neg)
    vals, idxs = [], []
    for _ in range(TOP_K):
        m = jnp.max(lg, axis=-1, keepdims=True)
        idx = jnp.min(jnp.where(lg == m, lane, float(LANES)), axis=-1, keepdims=True)
        vals.append(m)
        idxs.append(idx)
        lg = jnp.where(lane == idx, neg, lg)
    exps = [jnp.exp(v - vals[0]) for v in vals]
    denom = exps[0] + exps[1] + exps[2] + exps[3]

    onehot = [(lane == idxs[kk]).astype(F32) for kk in range(TOP_K)]
    colsum = [jnp.sum(o, axis=0, keepdims=True) for o in onehot]
    cnt = colsum[0] + colsum[1] + colsum[2] + colsum[3]
    pad8 = jnp.floor((cnt + 7.0) * 0.125) * 8.0
    run_off = jnp.dot(jnp.broadcast_to(pad8, (8, LANES)), upper_ref[...], preferred_element_type=F32,
                      precision=lax.Precision.HIGHEST)[0:1, :]
    base = jnp.zeros((1, LANES), F32)
    rloc = []
    for kk in range(TOP_K):
        before = _dot(ltri_ref[...], onehot[kk].astype(BF))
        rloc.append(jnp.sum(onehot[kk] * (run_off + base + before), axis=-1, keepdims=True))
        base = base + colsum[kk]

    eye = (lax.broadcasted_iota(jnp.int32, (TM, TM), 0) == lax.broadcasted_iota(jnp.int32, (TM, TM), 1))
    r_sub = lax.broadcasted_iota(jnp.int32, (R_TILE, TM), 0).astype(F32)
    ones8 = jnp.ones((8, TM), BF)
    sel = jnp.zeros((R_TILE, TM), F32)
    for kk in range(TOP_K):
        hi = jnp.floor(rloc[kk] * (1.0 / TM))
        lo = rloc[kk] - hi * TM
        row = (_dot(ones8, jnp.where(eye, hi, 0.0).astype(BF)) * TM
               + _dot(ones8, jnp.where(eye, lo, 0.0).astype(BF)))[0:1, :]
        sel = jnp.where(r_sub == row, 1.0, sel)
    xs_ref[...] = _dot(sel.astype(BF), h2.astype(BF))

    rl = jnp.zeros((TM, LANES), F32)
    tw = jnp.zeros((TM, LANES), F32)
    for kk in range(TOP_K):
        rl = jnp.where(lane == kk, rloc[kk], rl)
        tw = jnp.where(lane == kk, exps[kk] / denom, tw)
    rl_ref[...] = rl.astype(jnp.int32)
    tw_ref[...] = tw
    sub = lax.broadcasted_iota(jnp.int32, (8, LANES), 0)
    meta = jnp.where(sub == 0, jnp.broadcast_to(pad8, (8, LANES)),
                     jnp.where(sub == 1, jnp.broadcast_to(run_off, (8, LANES)), 0.0))
    meta_ref[...] = meta.astype(jnp.int32)


def _merge(l, x, mod, geo, a, b, attn, p):
    nt, n_p = geo["nt"], geo["np"]
    tok = nt * TM
    seq_row = geo["seq_row"]
    hb = TM // HALO
    last_hb = nt * hb - 1
    tile = lambda w: pl.BlockSpec((TM, w), lambda i: (i, 0))
    prev = lambda w: pl.BlockSpec((HALO, w), lambda i: (jnp.maximum(i * hb - 1, 0), 0))
    nxt = lambda w: pl.BlockSpec((HALO, w), lambda i: (jnp.minimum((i + 1) * hb, last_hb), 0))
    p_tile = pl.BlockSpec((TM, 512), lambda i: (jnp.minimum(i, n_p - 1), 0))
    s_tile = pl.BlockSpec((TM, 512), lambda i: (jnp.maximum(i - n_p, 0), 0))
    return pl.pallas_call(
        functools.partial(_merge_kernel, geo),
        grid=(nt,),
        in_specs=[
            tile(D_MODEL),
            pl.BlockSpec((None, None, 6, D_MODEL), lambda i: (l, seq_row(i), 0, 0)),
            tile(CONV_W), prev(CONV_W), nxt(CONV_W),
            tile(POOL_W), prev(POOL_W), nxt(POOL_W),
            p_tile, p_tile, s_tile, s_tile,
            _layer_spec((1, D_MODEL), l),
            _layer_spec((D_MODEL, N_GATE), l),
            _layer_spec((CONV_K, CONV_W), l),
            _layer_spec((1, CONV_W), l),
            _layer_spec((1, CONV_W), l),
            _layer_spec((1, CONV_W), l),
            _layer_spec((CONV_W, D_MODEL), l),
            _layer_spec((len(POOL_WINDOWS), POOL_G, POOL_G), l),
            _layer_spec((1, POOL_W), l),
            _layer_spec((POOL_W, D_MODEL), l),
            _layer_spec((512, D_MODEL), l),
            _layer_spec((512, D_MODEL), l),
            _layer_spec((D_MODEL, D_MODEL), l),
            _layer_spec((1, D_MODEL), l),
            _layer_spec((D_MODEL, 2 * LANES), l),
            _layer_spec((1, LANES), l),
            _const_spec((TM, TM)),
            _const_spec((LANES, LANES)),
        ],
        out_specs=[tile(D_MODEL), pl.BlockSpec((R_TILE, D_MODEL), lambda i: (i, 0)), tile(LANES), tile(LANES),
                   pl.BlockSpec((None, 8, LANES), lambda i: (i, 0, 0))],
        out_shape=[jax.ShapeDtypeStruct((tok, D_MODEL), F32), jax.ShapeDtypeStruct((nt * R_TILE, D_MODEL), F32),
                   jax.ShapeDtypeStruct((tok, LANES), jnp.int32), jax.ShapeDtypeStruct((tok, LANES), F32),
                   jax.ShapeDtypeStruct((nt, 8, LANES), jnp.int32)],
        scratch_shapes=[pltpu.VMEM((TM + 2 * HALO, CONV_W), F32), pltpu.VMEM((TM + 2 * HALO, POOL_W), F32),
                        pltpu.VMEM((SUBLANES - 1, TM + 2 * HALO - SUBLANES, CONV_W), F32)],
        compiler_params=pltpu.CompilerParams(dimension_semantics=("arbitrary",), vmem_limit_bytes=VMEM_LIMIT),
        name="merge",
    )(x, mod, a, a, a, b, b, b, *attn,
      p["norm1_g"], p["wgate"], p["conv_dw"], p["conv_dw_b"], p["conv_ln_g"], p["conv_ln_b"], p["w_conv_out"],
      p["pool_w"], p["pool_scale"], p["w_pool_out"], p["w_gqa_out"], p["w_mla_out"], p["w_o"],
      p["norm2_g"], p["w_router"], p["b_router"], p["ltri"], p["upper"])


def _expert_kernel(nt, blk_e, blk_row0, blk_t0, blk_t1, n_used_ref, pad8_ref, dst_ref, off_ref, tot8_ref,
                   xs_hbm, wgu_ref, bg_ref, bu_ref, wd_ref, bd_ref, perm_ref,
                   y_hbm, xg, yb, zbuf, wg_bf, wu_bf, wd_bf, cnt_smem, gsem, osem, zsem):
    i = pl.program_id(0)
    n_used = n_used_ref[0]
    slot = i % 2
    last = pl.num_programs(0) - 1

    def for_pieces(b, fn):
        e = blk_e[b]
        b0 = blk_row0[b]

        def body(t, tot):
            j = t * N_EXPERTS + e
            run0 = dst_ref[j]
            lo = jnp.maximum(run0, b0)
            hi = jnp.minimum(run0 + pad8_ref[j], b0 + MOE_BM)
            n = pl.multiple_of(jnp.maximum(hi - lo, 0), 8)

            @pl.when(n > 0)
            def _():
                fn(pl.multiple_of(t * R_TILE + off_ref[j] + lo - run0, 8), pl.multiple_of(lo - b0, 8), n)

            return tot + n

        return lax.fori_loop(blk_t0[b], blk_t1[b], body, jnp.int32(0))

    def rows_copy(src, dst, sem, n):
        return pltpu.make_async_copy(src.at[pl.ds(0, n), :], dst.at[pl.ds(0, n), :], sem)

    def start_gather(b, s):
        def piece(row_t, row_b, n):
            pltpu.make_async_copy(xs_hbm.at[pl.ds(row_t, n), :], xg.at[s].at[pl.ds(row_b, n), :], gsem.at[s]).start()

        cnt_smem[s] = for_pieces(b, piece)

    def tail_copy(t):
        n = pl.multiple_of(R_TILE - tot8_ref[t], 8)
        return n, pltpu.make_async_copy(
            zbuf.at[pl.ds(0, n), :], y_hbm.at[pl.ds(pl.multiple_of(t * R_TILE + tot8_ref[t], 8), n), :], zsem)

    @pl.when(i == 0)
    def _():
        xg[...] = jnp.zeros_like(xg)
        zbuf[...] = jnp.zeros_like(zbuf)
        for s in range(4):
            cnt_smem[s] = 0

        def fill(t, c):
            n, cp = tail_copy(t)

            @pl.when(n > 0)
            def _():
                cp.start()

            return c

        lax.fori_loop(0, nt, fill, 0)

        @pl.when(n_used > 0)
        def _():
            start_gather(0, 0)

    @pl.when(i + 1 < n_used)
    def _():
        start_gather(i + 1, 1 - slot)

    @pl.when(i < n_used)
    def _():
        e_changed = jnp.logical_or(i == 0, blk_e[i] != blk_e[jnp.maximum(i - 1, 0)])

        @pl.when(e_changed)
        def _():
            wd_bf[...] = wd_ref[...].astype(BF)
            for c in range(D_FF // LANES):
                pair = _dot(wgu_ref[:, 2 * c * LANES:2 * (c + 1) * LANES].astype(BF), perm_ref[...])
                wg_bf[:, c * LANES:(c + 1) * LANES] = pair[:, 0:LANES].astype(BF)
                wu_bf[:, c * LANES:(c + 1) * LANES] = pair[:, LANES:].astype(BF)

        n_prev = pl.multiple_of(cnt_smem[2 + slot], 8)

        @pl.when(n_prev > 0)
        def _():
            rows_copy(yb.at[slot], y_hbm, osem.at[slot], n_prev).wait()

        n_in = pl.multiple_of(cnt_smem[slot], 8)
        rows_copy(xs_hbm, xg.at[slot], gsem.at[slot], n_in).wait()

        def expert_mlp(rows):
            xb = xg[slot, 0:rows, :].astype(BF)
            gate = jnp.minimum(_dot(xb, wg_bf[...]) + bg_ref[...], SWIGLU_LIMIT)
            up = jnp.clip(_dot(xb, wu_bf[...]) + bu_ref[...], -SWIGLU_LIMIT, SWIGLU_LIMIT)
            glu = gate * jax.nn.sigmoid(gate * SWIGLU_ALPHA)
            y = _dot(((up + 1.0) * glu).astype(BF), wd_bf[...]) + bd_ref[...]
            yb[slot, 0:rows, :] = y.astype(BF).astype(F32)

        @pl.when(n_in > MOE_BM // 2)
        def _():
            expert_mlp(MOE_BM)

        @pl.when(n_in <= MOE_BM // 2)
        def _():
            expert_mlp(MOE_BM // 2)

        def piece(row_t, row_b, n):
            pltpu.make_async_copy(yb.at[slot].at[pl.ds(row_b, n), :], y_hbm.at[pl.ds(row_t, n), :],
                                  osem.at[slot]).start()

        cnt_smem[2 + slot] = for_pieces(i, piece)

    @pl.when(i == last)
    def _():
        for s in range(2):
            n_out = pl.multiple_of(cnt_smem[2 + s], 8)

            @pl.when(n_out > 0)
            def _():
                rows_copy(yb.at[s], y_hbm, osem.at[s], n_out).wait()

        def drain(t, c):
            n, cp = tail_copy(t)

            @pl.when(n > 0)
            def _():
                cp.wait()

            return c

        lax.fori_loop(0, nt, drain, 0)


def _experts(l, xs, plan, p, nt):
    n_blocks = plan["blk_e"].shape[0]
    n_pref = 9
    by_expert = lambda *lead: (lambda i, be, *_: lead + (be[i], 0, 0))
    b_spec = pl.BlockSpec((None, None, 1, D_FF), by_expert(l))
    grid_spec = pltpu.PrefetchScalarGridSpec(
        num_scalar_prefetch=n_pref,
        grid=(n_blocks,),
        in_specs=[
            pl.BlockSpec(memory_space=pl.ANY),
            pl.BlockSpec((None, None, D_MODEL, 2 * D_FF), by_expert(l)),
            b_spec, b_spec,
            pl.BlockSpec((None, None, D_FF, D_MODEL), by_expert(l)),
            pl.BlockSpec((None, None, 1, D_MODEL), by_expert(l)),
            pl.BlockSpec((2 * LANES, 2 * LANES), lambda i, *_: (0, 0)),
        ],
        out_specs=pl.BlockSpec(memory_space=pl.ANY),
        scratch_shapes=[
            pltpu.VMEM((2, MOE_BM, D_MODEL), F32),
            pltpu.VMEM((2, MOE_BM, D_MODEL), F32),
            pltpu.VMEM((R_TILE - TOP_K * TM, D_MODEL), F32),
            pltpu.VMEM((D_MODEL, D_FF), BF),
            pltpu.VMEM((D_MODEL, D_FF), BF),
            pltpu.VMEM((D_FF, D_MODEL), BF),
            pltpu.SMEM((4,), jnp.int32),
            pltpu.SemaphoreType.DMA((2,)),
            pltpu.SemaphoreType.DMA((2,)),
            pltpu.SemaphoreType.DMA,
        ],
    )
    return pl.pallas_call(
        functools.partial(_expert_kernel, nt),
        grid_spec=grid_spec,
        out_shape=jax.ShapeDtypeStruct((nt * R_TILE, D_MODEL), F32),
        compiler_params=pltpu.CompilerParams(dimension_semantics=("arbitrary",), vmem_limit_bytes=VMEM_LIMIT),
        name="experts",
    )(plan["blk_e"], plan["blk_row0"], plan["blk_t0"], plan["blk_t1"], plan["n_used"], plan["pad8"], plan["dst"],
      plan["off"], plan["tot8"],
      xs, p["w_gu"], p["b_gate"], p["b_up"], p["w_dn"], p["b_dn"], p["pair_perm"])


def _pair_perm():
    m = np.zeros((2 * LANES, 2 * LANES), np.float32)
    j = np.arange(LANES)
    m[2 * j, j] = 1.0
    m[2 * j + 1, LANES + j] = 1.0
    return jnp.asarray(m, BF)


def _combine_kernel(final, y_ref, x_ref, mod_ref, rl_ref, tw_ref, fg_ref, o_ref):
    r_lane = lax.broadcasted_iota(jnp.int32, (TM, R_TILE), 1)
    rl = rl_ref[...]
    tw = tw_ref[...]
    sel = jnp.zeros((TM, R_TILE), F32)
    for k in range(TOP_K):
        sel = jnp.where(r_lane == rl[:, k:k + 1], tw[:, k:k + 1], sel)
    ffn = _dot(sel.astype(BF), y_ref[...].astype(BF))
    x = x_ref[...] + mod_ref[5:6, :] * ffn
    if final:
        x = _rms(x) * fg_ref[...]
    o_ref[...] = x


def _combine(l, final, y, x, mod, rl, tw, final_g, geo, tile0, n_tiles):
    seq_row = geo["seq_row"]
    tile = lambda w: pl.BlockSpec((TM, w), lambda i: (i + tile0, 0))
    return pl.pallas_call(
        functools.partial(_combine_kernel, final),
        grid=(n_tiles,),
        in_specs=[
            pl.BlockSpec((R_TILE, D_MODEL), lambda i: (i + tile0, 0)),
            tile(D_MODEL),
            pl.BlockSpec((None, None, 6, D_MODEL), lambda i: (l, seq_row(i + tile0), 0, 0)),
            tile(LANES),
            tile(LANES),
            _const_spec((1, D_MODEL)),
        ],
        out_specs=pl.BlockSpec((TM, D_MODEL), lambda i: (i, 0)),
        out_shape=jax.ShapeDtypeStruct((n_tiles * TM, D_MODEL), F32),
        compiler_params=pltpu.CompilerParams(dimension_semantics=("arbitrary",), vmem_limit_bytes=VMEM_LIMIT),
        name="combine",
    )(y, x, mod, rl, tw, final_g)


def _expert_plan(meta, nt):
    pad8 = meta[:, 0, :N_EXPERTS]
    off = meta[:, 1, :N_EXPERTS]
    ends = jnp.cumsum(pad8, axis=0)
    dst = ends - pad8
    tot = ends[-1]
    nb = (tot + MOE_BM - 1) // MOE_BM
    nb_end = jnp.cumsum(nb)
    n_blocks = (nt * TM * TOP_K + nt * N_EXPERTS * 7) // MOE_BM + N_EXPERTS
    b = jnp.arange(n_blocks, dtype=jnp.int32)
    blk_e = jnp.minimum(jnp.sum(nb_end[None, :] <= b[:, None], axis=1), N_EXPERTS - 1).astype(jnp.int32)
    blk_row0 = (b - (nb_end - nb)[blk_e]) * MOE_BM
    ends_b = ends[:, blk_e]
    dst_b = dst[:, blk_e]
    blk_t0 = jnp.sum(ends_b <= blk_row0[None, :], axis=0)
    blk_t1 = jnp.sum(dst_b < blk_row0[None, :] + MOE_BM, axis=0)
    i32 = lambda v: v.astype(jnp.int32)
    return {"blk_e": blk_e, "blk_row0": i32(blk_row0), "blk_t0": i32(blk_t0), "blk_t1": i32(blk_t1),
            "n_used": i32(nb_end[-1]).reshape(1), "pad8": i32(pad8.reshape(-1)), "dst": i32(dst.reshape(-1)),
            "off": i32(off.reshape(-1)), "tot8": i32(jnp.sum(pad8, axis=1))}


def _rope_tables(n_pos):
    pos = np.arange(n_pos)
    row, col = pos // GRID_W, pos % GRID_W
    lane = np.arange(LANES)

    def build(active, r, half):
        n_rot = 4 * half
        is_col = (r % n_rot) >= 2 * half
        rr = r % (2 * half)
        freq = np.power(ROPE_BASE, -(rr % half).astype(np.float64) / half)
        p = np.where(is_col[None, :], col[:, None], row[:, None]).astype(np.float64)
        ang = p * freq[None, :]
        first = rr < half
        cos = np.where(active[None, :], np.cos(ang), 1.0)
        sin_a = np.where((active & first)[None, :], -np.sin(ang), 0.0)
        sin_b = np.where((active & ~first)[None, :], np.sin(ang), 0.0)
        return [cos, sin_a, sin_b]

    tabs = build(np.ones(LANES, bool), lane % GQA_HEAD_DIM, GQA_HEAD_DIM // 4)
    in_rope = (lane >= MLA_NOPE) & (lane < MLA_NOPE + MLA_ROPE)
    tabs += build(in_rope, np.maximum(lane - MLA_NOPE, 0) % MLA_ROPE, MLA_ROPE // 4)
    tabs += build(lane < MLA_ROPE, lane % MLA_ROPE, MLA_ROPE // 4)
    table = np.concatenate(tabs, axis=1)
    ident = np.concatenate([np.ones((TM, LANES)), np.zeros((TM, LANES)), np.zeros((TM, LANES))] * 3, axis=1)
    return jnp.asarray(np.concatenate([ident, table], axis=0), F32)


def _placement():
    e = np.zeros((LANES, MLA_HEADS * LANES), np.float32)
    for h in range(MLA_HEADS):
        for r in range(MLA_ROPE):
            e[r, h * LANES + MLA_NOPE + r] = 1.0
    return jnp.asarray(e, BF)


def _split_hi_lo(w):
    hi = w.astype(BF)
    lo = (w - hi.astype(F32)).astype(BF)
    return jnp.concatenate([hi, lo], axis=-1)


def _block_diag_ones(n, g):
    idx = np.arange(n) // g
    return jnp.asarray((idx[:, None] == idx[None, :]).astype(np.float32), BF)


def kernel(x_prompt, x_sample, cache_gqa_k, cache_gqa_v, cache_mla_ckv, cache_mla_krope, c, c_ctx, norm1_g, norm2_g, w_mod, b_mod, w_in, conv_dw, conv_dw_b, conv_ln_g, conv_ln_b, w_conv_out, pool_w, pool_scale, w_pool_out, gqa_qn_g, gqa_kn_g, w_gqa_out, mla_qn_g, w_mla_q_up, mla_kvn_g, w_mla_kv_up, w_mla_out, w_o, w_router, b_router, w_gu, b_gu, w_dn, b_dn, final_g):
    bp, seq, d = x_prompt.shape
    bs, ls, _ = x_sample.shape
    depth = w_in.shape[0]
    past = cache_gqa_k.shape[2]
    assert seq == TM and d == D_MODEL and ls % TM == 0 and (bp * seq) % ls == 0
    tps = ls // TM
    n_p = bp
    nt = n_p + bs * tps
    geo = {
        "np": n_p, "bs": bs, "tps": tps, "nt": nt,
        "seq_row": lambda i: jnp.where(i < n_p, 0, 1 + (i - n_p) // tps),
        "rope_blk": lambda i: jnp.where(i < n_p, 0, 1 + (i - n_p) % tps),
    }

    n_cond = -(-(1 + bs) // 8) * 8
    cond = jnp.zeros((n_cond, d), F32).at[0].set(c_ctx).at[1:1 + bs].set(c)
    mod = _modulation(cond, w_mod, b_mod).reshape(depth, n_cond, 6, d)

    row = lambda v: v.reshape(depth, 1, -1)
    w1 = jnp.pad(w_in[:, :, :_SPLIT_GATE], ((0, 0), (0, 0), (0, W1_COLS - _SPLIT_GATE))).astype(BF)
    wqup = jnp.pad(w_mla_q_up.reshape(depth, MLA_Q_RANK, MLA_HEADS, MLA_NOPE + MLA_ROPE),
                   ((0, 0), (0, 0), (0, 0), (0, LANES - MLA_NOPE - MLA_ROPE)))
    wkv = w_mla_kv_up.reshape(depth, MLA_KV_RANK, MLA_HEADS, MLA_NOPE + MLA_V)
    wk_pad = jnp.pad(wkv[..., :MLA_NOPE], ((0, 0), (0, 0), (0, 0), (0, LANES - MLA_NOPE)))
    wkvup = jnp.concatenate([wk_pad.reshape(depth, MLA_KV_RANK, MLA_HEADS * LANES),
                             wkv[..., MLA_NOPE:].reshape(depth, MLA_KV_RANK, MLA_HEADS * MLA_V)], axis=-1)
    bgu = b_gu.reshape(depth, N_EXPERTS, 1, D_FF, 2)
    p = {
        "norm1_g": row(norm1_g), "norm2_g": row(norm2_g),
        "w1": w1, "wgate": w_in[:, :, _SPLIT_GATE:].astype(BF),
        "rope": _rope_tables(ls), "place": _placement(), "ones_bd": _block_diag_ones(512, GQA_HEAD_DIM),
        "ltri": jnp.asarray(np.tril(np.ones((TM, TM), np.float32), -1), BF),
        "upper": jnp.asarray(np.triu(np.ones((LANES, LANES), np.float32), 1), F32),
        "qn_g": row(jnp.tile(gqa_qn_g, (1, GQA_HEADS))), "kn_g": row(jnp.tile(gqa_kn_g, (1, GQA_KV_HEADS))),
        "cqn_g": row(mla_qn_g), "kvn_g": row(mla_kvn_g),
        "wqup": wqup.reshape(depth, MLA_Q_RANK, MLA_HEADS * LANES).astype(BF), "wkvup": wkvup.astype(BF),
        "conv_dw": conv_dw, "conv_dw_b": row(conv_dw_b), "conv_ln_g": row(conv_ln_g), "conv_ln_b": row(conv_ln_b),
        "w_conv_out": w_conv_out.astype(BF), "pool_w": pool_w.astype(BF), "pool_scale": row(pool_scale),
        "w_pool_out": w_pool_out.astype(BF), "w_gqa_out": w_gqa_out.astype(BF), "w_mla_out": w_mla_out.astype(BF),
        "w_o": w_o.astype(BF),
        "w_router": _split_hi_lo(jnp.pad(w_router, ((0, 0), (0, 0), (0, LANES - N_EXPERTS)))),
        "b_router": row(jnp.pad(b_router, ((0, 0), (0, LANES - N_EXPERTS)))),
        "w_gu": w_gu, "b_gate": bgu[..., 0], "b_up": bgu[..., 1], "pair_perm": _pair_perm(),
        "w_dn": w_dn, "b_dn": b_dn.reshape(depth, N_EXPERTS, 1, d),
    }

    ckm, cvm = _cache_prep(cache_mla_ckv, jnp.pad(cache_mla_krope, ((0, 0), (0, 0), (0, 0), (0, LANES - MLA_ROPE))), p)
    cache = (cache_gqa_k.reshape(bs, depth, past, LANES).astype(BF),
             cache_gqa_v.reshape(bs, depth, past, LANES).astype(BF), ckm, cvm)

    x = jnp.concatenate([x_prompt.reshape(bp * seq, d), x_sample.reshape(bs * ls, d)], axis=0)
    n_ptok = bp * seq
    states = []
    for l in range(depth):
        a, b, q, k, v, qm, km, vm, ks, vs, ckvs, krs = _proj(l, x, mod, geo, p)
        states.append((ks[:n_ptok], vs[:n_ptok], ckvs[:n_ptok], krs[:n_ptok]))
        attn = _attention(l, geo, q, k, v, qm, km, vm, cache)
        x_mid, xs, slot_row, top_w, meta = _merge(l, x, mod, geo, a, b, attn, p)
        y = _experts(l, xs, _expert_plan(meta, nt), p, nt)
        combine = functools.partial(_combine, l, l == depth - 1, y, x_mid, mod, slot_row, top_w,
                                    final_g.reshape(1, d), geo)
        if l < depth - 1:
            x = combine(0, nt)

    y_prompt = combine(0, n_p).reshape(bp, seq, d)
    y_sample = combine(n_p, nt - n_p).reshape(bs, ls, d)
    st = lambda j, shape: jnp.stack([s[j].reshape(shape) for s in states], axis=1)
    return (y_prompt, y_sample,
            st(0, (bp, seq, GQA_KV_HEADS, GQA_HEAD_DIM)), st(1, (bp, seq, GQA_KV_HEADS, GQA_HEAD_DIM)),
            st(2, (bp, seq, MLA_KV_RANK)), st(3, (bp, seq, MLA_ROPE)))
```

```python
import functools

import jax
import jax.numpy as jnp
import numpy as np
from jax import lax
from jax.experimental import pallas as pl
from jax.experimental.pallas import tpu as pltpu

D_MODEL = 1024
GRID_W = 64
CONV_W = 512
CONV_K = 31
POOL_W = 512
POOL_WINDOWS = (2, 4, 8, 16)
POOL_G = 128
GQA_HEADS = 8
GQA_KV_HEADS = 2
GQA_HEAD_DIM = 64
MLA_HEADS = 8
MLA_Q_RANK = 384
MLA_KV_RANK = 256
MLA_NOPE = 64
MLA_ROPE = 32
MLA_V = 64
ROPE_BASE = 10000.0
N_EXPERTS = 32
TOP_K = 4
D_FF = 1024
SWIGLU_LIMIT = 7.0
SWIGLU_ALPHA = 1.702
EPS = 1e-6
GQA_SCALE = GQA_HEAD_DIM ** -0.5
MLA_SCALE = (MLA_NOPE + MLA_ROPE) ** -0.5
LOG2E = 1.4426950408889634

LANES = 128
SUBLANES = 8
TM = 256
HALO = 16
MOE_BM = 512
R_TILE = 1280
W1_COLS = 3072
N_GATE = 4 * D_MODEL
VMEM_LIMIT = 56 * 1024 * 1024

BF = jnp.bfloat16
F32 = jnp.float32

_C_A, _C_B, _C_Q, _C_K, _C_V, _C_CQ, _C_CKV, _C_KR = 0, 1024, 1536, 2048, 2176, 2304, 2688, 2944
_SPLIT_GATE = 2976


def _dot(a, b):
    return jnp.dot(a, b, preferred_element_type=F32)


def _dot_nt(a, b):
    return lax.dot_general(a, b, (((1,), (1,)), ((), ())), preferred_element_type=F32)


def _rms(x):
    return x * lax.rsqrt(jnp.mean(x * x, axis=-1, keepdims=True) + EPS)


def _group_mean_sq(x, ones_bd, width):
    xx = x * x
    hi = xx.astype(BF)
    lo = (xx - hi.astype(F32)).astype(BF)
    return (_dot(hi, ones_bd) + _dot(lo, ones_bd)) * (1.0 / width)


def _tile_lanes(t, width):
    reps = width // LANES
    return t if reps == 1 else jnp.concatenate([t] * reps, axis=-1)


def _rope(x, cos, sin_a, sin_b, shift):
    w = x.shape[-1]
    return (x * _tile_lanes(cos, w) + pltpu.roll(x, w - shift, 1) * _tile_lanes(sin_a, w)
            + pltpu.roll(x, shift, 1) * _tile_lanes(sin_b, w))


def _modulated_norm(x, g, shift, scale):
    return _rms(x) * g * (1.0 + scale) + shift


def _mod_kernel(cond_ref, w_ref, b_ref, o_ref):
    c = cond_ref[...]
    s = (c * jax.nn.sigmoid(c)).astype(BF)
    o_ref[...] = _dot(s, w_ref[...].astype(BF)) + b_ref[...]


def _modulation(cond, w_mod, b_mod):
    depth, d, n = w_mod.shape
    rows = cond.shape[0]
    return pl.pallas_call(
        _mod_kernel,
        grid=(depth, n // D_MODEL),
        in_specs=[
            pl.BlockSpec((rows, d), lambda l, j: (0, 0)),
            pl.BlockSpec((None, d, D_MODEL), lambda l, j: (l, 0, j)),
            pl.BlockSpec((None, 1, D_MODEL), lambda l, j: (l, 0, j)),
        ],
        out_specs=pl.BlockSpec((None, rows, D_MODEL), lambda l, j: (l, 0, j)),
        out_shape=jax.ShapeDtypeStruct((depth, rows, n), F32),
        name="modulation",
    )(cond, w_mod, b_mod.reshape(depth, 1, n))


def _proj_kernel(x_ref, mod_ref, g1_ref, w1_ref, rope_ref, qn_ref, kn_ref, cqn_ref, kvn_ref,
                 wqup_ref, wkvup_ref, place_ref, ones_ref,
                 a_ref, b_ref, q_ref, k_ref, v_ref, qm_ref, km_ref, vm_ref,
                 ks_ref, vs_ref, ckvs_ref, krs_ref):
    x = x_ref[...]
    h = _modulated_norm(x, g1_ref[...], mod_ref[0:1, :], mod_ref[1:2, :])
    y = _dot(h.astype(BF), w1_ref[...])

    a_ref[...] = (y[:, _C_A:_C_A + CONV_W] * jax.nn.sigmoid(y[:, _C_A + CONV_W:_C_B])).astype(BF)
    b_ref[...] = y[:, _C_B:_C_Q].astype(BF)

    def tab(j):
        return rope_ref[:, j * LANES:(j + 1) * LANES]

    q = y[:, _C_Q:_C_K]
    q = q * lax.rsqrt(_group_mean_sq(q, ones_ref[...], GQA_HEAD_DIM) + EPS) * qn_ref[...]
    q = _rope(q, tab(0), tab(1), tab(2), GQA_HEAD_DIM // 4)
    q_ref[...] = (q * (GQA_SCALE * LOG2E)).astype(BF)

    k = y[:, _C_K:_C_V]
    k = k * lax.rsqrt(_group_mean_sq(k, ones_ref[0:LANES, 0:LANES], GQA_HEAD_DIM) + EPS) * kn_ref[...]
    ks_ref[...] = k
    k_ref[...] = _rope(k, tab(0), tab(1), tab(2), GQA_HEAD_DIM // 4).astype(BF)

    v = y[:, _C_V:_C_CQ]
    vs_ref[...] = v
    v_ref[...] = v.astype(BF)

    cq = _rms(y[:, _C_CQ:_C_CKV]) * cqn_ref[...]
    qm = _dot(cq.astype(BF), wqup_ref[...])
    qm = _rope(qm, tab(3), tab(4), tab(5), MLA_ROPE // 4)
    qm_ref[...] = (qm * (MLA_SCALE * LOG2E)).astype(BF)

    ckv = _rms(y[:, _C_CKV:_C_KR]) * kvn_ref[...]
    ckvs_ref[...] = ckv
    kv = _dot(ckv.astype(BF), wkvup_ref[...])
    kr = y[:, _C_KR:W1_COLS]
    krs_ref[...] = kr[:, 0:MLA_ROPE]
    kr_rot = _rope(kr, tab(6), tab(7), tab(8), MLA_ROPE // 4)
    km = kv[:, 0:MLA_HEADS * LANES] + _dot(kr_rot.astype(BF), place_ref[...])
    km_ref[...] = km.astype(BF)
    vm_ref[...] = kv[:, MLA_HEADS * LANES:].astype(BF)


def _const_spec(shape):
    nd = len(shape)
    return pl.BlockSpec(shape, lambda *_: (0,) * nd)


def _layer_spec(shape, l):
    nd = len(shape)
    return pl.BlockSpec((None,) + shape, lambda *_: (l,) + (0,) * nd)


def _proj(l, x, mod, geo, p):
    nt = geo["nt"]
    tok = nt * TM
    seq_row, rope_blk = geo["seq_row"], geo["rope_blk"]

    def tile(width, dtype):
        return pl.BlockSpec((TM, width), lambda i: (i, 0)), jax.ShapeDtypeStruct((tok, width), dtype)

    outs = [tile(CONV_W, BF), tile(POOL_W, BF), tile(512, BF), tile(LANES, BF), tile(LANES, BF),
            tile(MLA_HEADS * LANES, BF), tile(MLA_HEADS * LANES, BF), tile(MLA_HEADS * MLA_V, BF),
            tile(LANES, F32), tile(LANES, F32), tile(MLA_KV_RANK, F32), tile(MLA_ROPE, F32)]
    return pl.pallas_call(
        _proj_kernel,
        grid=(nt,),
        in_specs=[
            pl.BlockSpec((TM, D_MODEL), lambda i: (i, 0)),
            pl.BlockSpec((None, None, 6, D_MODEL), lambda i: (l, seq_row(i), 0, 0)),
            _layer_spec((1, D_MODEL), l),
            _layer_spec((D_MODEL, W1_COLS), l),
            pl.BlockSpec((TM, 9 * LANES), lambda i: (rope_blk(i), 0)),
            _layer_spec((1, 512), l),
            _layer_spec((1, LANES), l),
            _layer_spec((1, MLA_Q_RANK), l),
            _layer_spec((1, MLA_KV_RANK), l),
            _layer_spec((MLA_Q_RANK, MLA_HEADS * LANES), l),
            _layer_spec((MLA_KV_RANK, MLA_HEADS * (LANES + MLA_V)), l),
            _const_spec((LANES, MLA_HEADS * LANES)),
            _const_spec((512, 512)),
        ],
        out_specs=[o[0] for o in outs],
        out_shape=[o[1] for o in outs],
        compiler_params=pltpu.CompilerParams(dimension_semantics=("arbitrary",), vmem_limit_bytes=VMEM_LIMIT),
        name="proj",
    )(x, mod, p["norm1_g"], p["w1"], p["rope"], p["qn_g"], p["kn_g"], p["cqn_g"], p["kvn_g"],
      p["wqup"], p["wkvup"], p["place"], p["ones_bd"])


def _cache_kernel(ckv_ref, kr_ref, wkvup_ref, place_ref, km_ref, vm_ref):
    kv = _dot(ckv_ref[...].astype(BF), wkvup_ref[...])
    km = kv[:, 0:MLA_HEADS * LANES] + _dot(kr_ref[...].astype(BF), place_ref[...])
    km_ref[...] = km.astype(BF)
    vm_ref[...] = kv[:, MLA_HEADS * LANES:].astype(BF)


def _cache_prep(ckv, kr_pad, p):
    bs, depth, past, _ = ckv.shape
    return pl.pallas_call(
        _cache_kernel,
        grid=(bs, depth),
        in_specs=[
            pl.BlockSpec((None, None, past, MLA_KV_RANK), lambda b, l: (b, l, 0, 0)),
            pl.BlockSpec((None, None, past, LANES), lambda b, l: (b, l, 0, 0)),
            pl.BlockSpec((None, MLA_KV_RANK, MLA_HEADS * (LANES + MLA_V)), lambda b, l: (l, 0, 0)),
            pl.BlockSpec((LANES, MLA_HEADS * LANES), lambda b, l: (0, 0)),
        ],
        out_specs=[
            pl.BlockSpec((None, None, past, MLA_HEADS * LANES), lambda b, l: (b, l, 0, 0)),
            pl.BlockSpec((None, None, past, MLA_HEADS * MLA_V), lambda b, l: (b, l, 0, 0)),
        ],
        out_shape=[jax.ShapeDtypeStruct((bs, depth, past, MLA_HEADS * LANES), BF),
                   jax.ShapeDtypeStruct((bs, depth, past, MLA_HEADS * MLA_V), BF)],
        name="cache_prep",
    )(ckv, kr_pad, p["wkvup"], p["place"])


def _softmax_pv(s, v):
    m = jnp.max(s, axis=-1, keepdims=True)
    e = jnp.exp2(s - m)
    l = jnp.sum(e, axis=-1, keepdims=True)
    return _dot(e.astype(BF), v) / l


def _attention_body(q, k, v, qm, km, vm, oc_ref, od_ref):
    lane_k = lax.broadcasted_iota(jnp.int32, k.shape, 1)
    lane_q = lax.broadcasted_iota(jnp.int32, (TM, LANES), 1)
    lo_k = lane_k < GQA_HEAD_DIM
    lo_q = lane_q < GQA_HEAD_DIM
    k32, v32 = k.astype(F32), v.astype(F32)
    k_sw = pltpu.roll(k32, GQA_HEAD_DIM, 1)
    v_sw = pltpu.roll(v32, GQA_HEAD_DIM, 1)
    k_dup = [jnp.where(lo_k, k32, k_sw).astype(BF), jnp.where(lo_k, k_sw, k32).astype(BF)]
    v_dup = [jnp.where(lo_k, v32, v_sw).astype(BF), jnp.where(lo_k, v_sw, v32).astype(BF)]
    zero = jnp.zeros((TM, LANES), BF)
    group = GQA_HEADS // GQA_KV_HEADS
    for j in range(GQA_HEADS // 2):
        qs = q[:, j * LANES:(j + 1) * LANES]
        g = (2 * j) // group
        o_lo = _softmax_pv(_dot_nt(jnp.where(lo_q, qs, zero), k_dup[g]), v_dup[g])
        o_hi = _softmax_pv(_dot_nt(jnp.where(lo_q, zero, qs), k_dup[g]), v_dup[g])
        oc_ref[:, j * LANES:(j + 1) * LANES] = jnp.where(lo_q, o_lo, o_hi).astype(BF)
    for j in range(MLA_HEADS // 2):
        vs = vm[:, j * LANES:(j + 1) * LANES]
        outs = []
        for h in (2 * j, 2 * j + 1):
            s = _dot_nt(qm[:, h * LANES:(h + 1) * LANES], km[:, h * LANES:(h + 1) * LANES])
            outs.append(_softmax_pv(s, vs))
        od_ref[:, j * LANES:(j + 1) * LANES] = jnp.where(lo_q, outs[0], outs[1]).astype(BF)


def _attn_prompt_kernel(q_ref, k_ref, v_ref, qm_ref, km_ref, vm_ref, oc_ref, od_ref):
    _attention_body(q_ref[...], k_ref[...], v_ref[...], qm_ref[...], km_ref[...], vm_ref[...], oc_ref, od_ref)


def _attn_sample_kernel(q_ref, k_ref, v_ref, qm_ref, km_ref, vm_ref, ck_ref, cv_ref, ckm_ref, cvm_ref,
                        oc_ref, od_ref):
    k = jnp.concatenate([ck_ref[...], k_ref[...]], axis=0)
    v = jnp.concatenate([cv_ref[...], v_ref[...]], axis=0)
    km = jnp.concatenate([ckm_ref[...], km_ref[...]], axis=0)
    vm = jnp.concatenate([cvm_ref[...], vm_ref[...]], axis=0)
    _attention_body(q_ref[...], k, v, qm_ref[...], km, vm, oc_ref, od_ref)


def _attention(l, geo, q, k, v, qm, km, vm, cache):
    n_p, bs, tps, nt = geo["np"], geo["bs"], geo["tps"], geo["nt"]
    tok = nt * TM
    ls = tps * TM
    off = (n_p * TM) // ls
    widths = (512, LANES, LANES, MLA_HEADS * LANES, MLA_HEADS * LANES, MLA_HEADS * MLA_V)
    params = pltpu.CompilerParams(dimension_semantics=("arbitrary",), vmem_limit_bytes=VMEM_LIMIT)
    oc_p, od_p = pl.pallas_call(
        _attn_prompt_kernel,
        grid=(n_p,),
        in_specs=[pl.BlockSpec((TM, w), lambda i: (i, 0)) for w in widths],
        out_specs=[pl.BlockSpec((TM, 512), lambda i: (i, 0))] * 2,
        out_shape=[jax.ShapeDtypeStruct((n_p * TM, 512), BF)] * 2,
        compiler_params=params,
        name="attn_prompt",
    )(q, k, v, qm, km, vm)
    ck, cv, ckm, cvm = cache
    past = ck.shape[2]
    q_spec = lambda w: pl.BlockSpec((TM, w), lambda b, j: (n_p + b * tps + j, 0))
    kv_spec = lambda w: pl.BlockSpec((ls, w), lambda b, j: (off + b, 0))
    c_spec = lambda w: pl.BlockSpec((None, None, past, w), lambda b, j: (b, l, 0, 0))
    params2 = pltpu.CompilerParams(dimension_semantics=("arbitrary", "arbitrary"), vmem_limit_bytes=VMEM_LIMIT)
    o_spec = pl.BlockSpec((TM, 512), lambda b, j: (b * tps + j, 0))
    oc_s, od_s = pl.pallas_call(
        _attn_sample_kernel,
        grid=(bs, tps),
        in_specs=[q_spec(512), kv_spec(LANES), kv_spec(LANES), q_spec(MLA_HEADS * LANES),
                  kv_spec(MLA_HEADS * LANES), kv_spec(MLA_HEADS * MLA_V),
                  c_spec(LANES), c_spec(LANES), c_spec(MLA_HEADS * LANES), c_spec(MLA_HEADS * MLA_V)],
        out_specs=[o_spec, o_spec],
        out_shape=[jax.ShapeDtypeStruct((bs * ls, 512), BF)] * 2,
        compiler_params=params2,
        name="attn_sample",
    )(q, k, v, qm, km, vm, ck, cv, ckm, cvm)
    return oc_p, od_p, oc_s, od_s


def _merge_kernel(geo, x_ref, mod_ref, a_ref, ap_ref, an_ref, b_ref, bp_ref, bn_ref,
                  ocp_ref, odp_ref, ocs_ref, ods_ref,
                  g1_ref, wg_ref, cw_ref, cb_ref, lng_ref, lnb_ref, wco_ref, pw_ref, ps_ref, wpo_ref,
                  wgo_ref, wmo_ref, wo_ref, g2_ref, wr_ref, br_ref,
                  ltri_ref, upper_ref,
                  xo_ref, xs_ref, rl_ref, tw_ref, meta_ref, abuf, bbuf, ashift):
    n_p, tps = geo["np"], geo["tps"]
    i = pl.program_id(0)
    j = jnp.where(i < n_p, 0, (i - n_p) % tps)
    n_seq_tiles = jnp.where(i < n_p, 1, tps)
    has_prev = j > 0
    has_next = j < n_seq_tiles - 1

    def fill(buf, cur, prev, nxt):
        buf[0:HALO, :] = jnp.where(has_prev, prev[...].astype(F32), 0.0)
        buf[HALO:HALO + TM, :] = cur[...].astype(F32)
        buf[HALO + TM:, :] = jnp.where(has_next, nxt[...].astype(F32), 0.0)

    fill(abuf, a_ref, ap_ref, an_ref)
    fill(bbuf, b_ref, bp_ref, bn_ref)

    sh_rows = TM + 2 * HALO - SUBLANES
    for sh in range(1, SUBLANES):
        ashift[sh - 1] = abuf[sh:sh + sh_rows, :]
    rows = 32
    conv = []
    for r0 in range(0, TM, rows):
        acc = jnp.zeros((rows, CONV_W), F32)
        for t in range(CONV_K):
            s = r0 + t + HALO - CONV_K // 2
            sh = s % SUBLANES
            tap = abuf[s:s + rows, :] if sh == 0 else ashift[sh - 1, s - sh:s - sh + rows, :]
            acc = acc + tap * cw_ref[t:t + 1, :]
        conv.append(acc)
    ca = jnp.concatenate(conv, axis=0) + cb_ref[...]
    mu = jnp.mean(ca, axis=-1, keepdims=True)
    xc = ca - mu
    ln = xc * lax.rsqrt(jnp.mean(xc * xc, axis=-1, keepdims=True) + EPS) * lng_ref[...] + lnb_ref[...]
    br_a = _dot((ln * jax.nn.sigmoid(ln)).astype(BF), wco_ref[...])

    pos = j * TM + lax.broadcasted_iota(jnp.int32, (TM, 1), 0)
    seq_len = n_seq_tiles * TM
    mixed = []
    for g, w in enumerate(POOL_WINDOWS):
        lo, hi = g * POOL_G, (g + 1) * POOL_G
        acc = jnp.zeros((TM, POOL_G), F32)
        for o in range(-(w // 2), w - w // 2):
            acc = acc + bbuf[HALO + o:HALO + o + TM, lo:hi]
        cnt = jnp.clip(pos - w // 2 + w, 0, seq_len) - jnp.clip(pos - w // 2, 0, seq_len)
        pooled = acc / cnt.astype(F32) - bbuf[HALO:HALO + TM, lo:hi]
        mixed.append(_dot(pooled.astype(BF), pw_ref[g]))
    pb = jnp.concatenate(mixed, axis=-1) * ps_ref[...]
    br_b = _dot(pb.astype(BF), wpo_ref[...])

    is_prompt = i < n_p
    br_c = _dot(jnp.where(is_prompt, ocp_ref[...], ocs_ref[...]), wgo_ref[...])
    br_d = _dot(jnp.where(is_prompt, odp_ref[...], ods_ref[...]), wmo_ref[...])

    x = x_ref[...]
    h = _modulated_norm(x, g1_ref[...], mod_ref[0:1, :], mod_ref[1:2, :])
    gates = jax.nn.sigmoid(_dot(h.astype(BF), wg_ref[...]).astype(BF))
    merged = (gates[:, 0:D_MODEL] * br_a.astype(BF) + gates[:, D_MODEL:2 * D_MODEL] * br_b.astype(BF)
              + gates[:, 2 * D_MODEL:3 * D_MODEL] * br_c.astype(BF) + gates[:, 3 * D_MODEL:] * br_d.astype(BF))
    x = x + mod_ref[2:3, :] * _dot(merged, wo_ref[...])
    xo_ref[...] = x

    h2 = _modulated_norm(x, g2_ref[...], mod_ref[3:4, :], mod_ref[4:5, :])

    h2_hi = h2.astype(BF)
    h2_lo = (h2 - h2_hi.astype(F32)).astype(BF)
    hi_terms = _dot(h2_hi, wr_ref[...])
    logits = (hi_terms[:, 0:LANES] + hi_terms[:, LANES:] + _dot(h2_lo, wr_ref[:, 0:LANES])) + br_ref[...]
    lane = lax.broadcasted_iota(jnp.int32, (TM, LANES), 1).astype(F32)
    neg = jnp.float32(-jnp.inf)
    lg = jnp.where(lane < N_EXPERTS, logits, neg)
    vals, idxs = [], []
    for _ in range(TOP_K):
        m = jnp.max(lg, axis=-1, keepdims=True)
        idx = jnp.min(jnp.where(lg == m, lane, float(LANES)), axis=-1, keepdims=True)
        vals.append(m)
        idxs.append(idx)
        lg = jnp.where(lane == idx, neg, lg)
    exps = [jnp.exp(v - vals[0]) for v in vals]
    denom = exps[0] + exps[1] + exps[2] + exps[3]

    onehot = [(lane == idxs[kk]).astype(F32) for kk in range(TOP_K)]
    colsum = [jnp.sum(o, axis=0, keepdims=True) for o in onehot]
    cnt = colsum[0] + colsum[1] + colsum[2] + colsum[3]
    pad8 = jnp.floor((cnt + 7.0) * 0.125) * 8.0
    run_off = jnp.dot(jnp.broadcast_to(pad8, (8, LANES)), upper_ref[...], preferred_element_type=F32,
                      precision=lax.Precision.HIGHEST)[0:1, :]
    base = jnp.zeros((1, LANES), F32)
    rloc = []
    for kk in range(TOP_K):
        before = _dot(ltri_ref[...], onehot[kk].astype(BF))
        rloc.append(jnp.sum(onehot[kk] * (run_off + base + before), axis=-1, keepdims=True))
        base = base + colsum[kk]

    eye = (lax.broadcasted_iota(jnp.int32, (TM, TM), 0) == lax.broadcasted_iota(jnp.int32, (TM, TM), 1))
    r_sub = lax.broadcasted_iota(jnp.int32, (R_TILE, TM), 0).astype(F32)
    ones8 = jnp.ones((8, TM), BF)
    sel = jnp.zeros((R_TILE, TM), F32)
    for kk in range(TOP_K):
        hi = jnp.floor(rloc[kk] * (1.0 / TM))
        lo = rloc[kk] - hi * TM
        row = (_dot(ones8, jnp.where(eye, hi, 0.0).astype(BF)) * TM
               + _dot(ones8, jnp.where(eye, lo, 0.0).astype(BF)))[0:1, :]
        sel = jnp.where(r_sub == row, 1.0, sel)
    xs_ref[...] = _dot(sel.astype(BF), h2.astype(BF))

    rl = jnp.zeros((TM, LANES), F32)
    tw = jnp.zeros((TM, LANES), F32)
    for kk in range(TOP_K):
        rl = jnp.where(lane == kk, rloc[kk], rl)
        tw = jnp.where(lane == kk, exps[kk] / denom, tw)
    rl_ref[...] = rl.astype(jnp.int32)
    tw_ref[...] = tw
    sub = lax.broadcasted_iota(jnp.int32, (8, LANES), 0)
    meta = jnp.where(sub == 0, jnp.broadcast_to(pad8, (8, LANES)),
                     jnp.where(sub == 1, jnp.broadcast_to(run_off, (8, LANES)), 0.0))
    meta_ref[...] = meta.astype(jnp.int32)


def _merge(l, x, mod, geo, a, b, attn, p):
    nt, n_p = geo["nt"], geo["np"]
    tok = nt * TM
    seq_row = geo["seq_row"]
    hb = TM // HALO
    last_hb = nt * hb - 1
    tile = lambda w: pl.BlockSpec((TM, w), lambda i: (i, 0))
    prev = lambda w: pl.BlockSpec((HALO, w), lambda i: (jnp.maximum(i * hb - 1, 0), 0))
    nxt = lambda w: pl.BlockSpec((HALO, w), lambda i: (jnp.minimum((i + 1) * hb, last_hb), 0))
    p_tile = pl.BlockSpec((TM, 512), lambda i: (jnp.minimum(i, n_p - 1), 0))
    s_tile = pl.BlockSpec((TM, 512), lambda i: (jnp.maximum(i - n_p, 0), 0))
    return pl.pallas_call(
        functools.partial(_merge_kernel, geo),
        grid=(nt,),
        in_specs=[
            tile(D_MODEL),
            pl.BlockSpec((None, None, 6, D_MODEL), lambda i: (l, seq_row(i), 0, 0)),
            tile(CONV_W), prev(CONV_W), nxt(CONV_W),
            tile(POOL_W), prev(POOL_W), nxt(POOL_W),
            p_tile, p_tile, s_tile, s_tile,
            _layer_spec((1, D_MODEL), l),
            _layer_spec((D_MODEL, N_GATE), l),
            _layer_spec((CONV_K, CONV_W), l),
            _layer_spec((1, CONV_W), l),
            _layer_spec((1, CONV_W), l),
            _layer_spec((1, CONV_W), l),
            _layer_spec((CONV_W, D_MODEL), l),
            _layer_spec((len(POOL_WINDOWS), POOL_G, POOL_G), l),
            _layer_spec((1, POOL_W), l),
            _layer_spec((POOL_W, D_MODEL), l),
            _layer_spec((512, D_MODEL), l),
            _layer_spec((512, D_MODEL), l),
            _layer_spec((D_MODEL, D_MODEL), l),
            _layer_spec((1, D_MODEL), l),
            _layer_spec((D_MODEL, 2 * LANES), l),
            _layer_spec((1, LANES), l),
            _const_spec((TM, TM)),
            _const_spec((LANES, LANES)),
        ],
        out_specs=[tile(D_MODEL), pl.BlockSpec((R_TILE, D_MODEL), lambda i: (i, 0)), tile(LANES), tile(LANES),
                   pl.BlockSpec((None, 8, LANES), lambda i: (i, 0, 0))],
        out_shape=[jax.ShapeDtypeStruct((tok, D_MODEL), F32), jax.ShapeDtypeStruct((nt * R_TILE, D_MODEL), F32),
                   jax.ShapeDtypeStruct((tok, LANES), jnp.int32), jax.ShapeDtypeStruct((tok, LANES), F32),
                   jax.ShapeDtypeStruct((nt, 8, LANES), jnp.int32)],
        scratch_shapes=[pltpu.VMEM((TM + 2 * HALO, CONV_W), F32), pltpu.VMEM((TM + 2 * HALO, POOL_W), F32),
                        pltpu.VMEM((SUBLANES - 1, TM + 2 * HALO - SUBLANES, CONV_W), F32)],
        compiler_params=pltpu.CompilerParams(dimension_semantics=("arbitrary",), vmem_limit_bytes=VMEM_LIMIT),
        name="merge",
    )(x, mod, a, a, a, b, b, b, *attn,
      p["norm1_g"], p["wgate"], p["conv_dw"], p["conv_dw_b"], p["conv_ln_g"], p["conv_ln_b"], p["w_conv_out"],
      p["pool_w"], p["pool_scale"], p["w_pool_out"], p["w_gqa_out"], p["w_mla_out"], p["w_o"],
      p["norm2_g"], p["w_router"], p["b_router"], p["ltri"], p["upper"])


def _expert_kernel(nt, layer, blk_e, blk_row0, blk_t0, blk_t1, n_used_ref, pad8_ref, dst_ref, off_ref, tot8_ref,
                   next_e, xs_hbm, wgu_hbm, bg_ref, bu_ref, wd_hbm, bd_ref, perm_ref,
                   y_hbm, xg, yb, zbuf, wgu_buf, wd_buf, wg_bf, wu_bf, wd_bf, cnt_smem, gsem, osem, zsem, wsem):
    i = pl.program_id(0)
    n_used = n_used_ref[0]
    slot = i % 2
    last = pl.num_programs(0) - 1

    def for_pieces(b, fn):
        e = blk_e[b]
        b0 = blk_row0[b]

        def body(t, tot):
            j = t * N_EXPERTS + e
            run0 = dst_ref[j]
            lo = jnp.maximum(run0, b0)
            hi = jnp.minimum(run0 + pad8_ref[j], b0 + MOE_BM)
            n = pl.multiple_of(jnp.maximum(hi - lo, 0), 8)

            @pl.when(n > 0)
            def _():
                fn(pl.multiple_of(t * R_TILE + off_ref[j] + lo - run0, 8), pl.multiple_of(lo - b0, 8), n)

            return tot + n

        return lax.fori_loop(blk_t0[b], blk_t1[b], body, jnp.int32(0))

    def rows_copy(src, dst, sem, n):
        return pltpu.make_async_copy(src.at[pl.ds(0, n), :], dst.at[pl.ds(0, n), :], sem)

    def start_gather(b, s):
        def piece(row_t, row_b, n):
            pltpu.make_async_copy(xs_hbm.at[pl.ds(row_t, n), :], xg.at[s].at[pl.ds(row_b, n), :], gsem.at[s]).start()

        cnt_smem[s] = for_pieces(b, piece)

    def tail_copy(t):
        n = pl.multiple_of(R_TILE - tot8_ref[t], 8)
        return n, pltpu.make_async_copy(
            zbuf.at[pl.ds(0, n), :], y_hbm.at[pl.ds(pl.multiple_of(t * R_TILE + tot8_ref[t], 8), n), :], zsem)

    @pl.when(i == 0)
    def _():
        xg[...] = jnp.zeros_like(xg)
        zbuf[...] = jnp.zeros_like(zbuf)
        for s in range(4):
            cnt_smem[s] = 0

        def fill(t, c):
            n, cp = tail_copy(t)

            @pl.when(n > 0)
            def _():
                cp.start()

            return c

        lax.fori_loop(0, nt, fill, 0)

        @pl.when(n_used > 0)
        def _():
            start_gather(0, 0)

    @pl.when(i + 1 < n_used)
    def _():
        start_gather(i + 1, 1 - slot)

    @pl.when(i < n_used)
    def _():
        e_changed = jnp.logical_or(i == 0, blk_e[i] != blk_e[jnp.maximum(i - 1, 0)])

        def weight_copies(e):
            return (pltpu.make_async_copy(wgu_hbm.at[layer, e], wgu_buf, wsem.at[0]),
                    pltpu.make_async_copy(wd_hbm.at[layer, e], wd_buf, wsem.at[1]))

        @pl.when(e_changed)
        def _():
            e = blk_e[i]

            @pl.when(i == 0)
            def _():
                for cp in weight_copies(e):
                    cp.start()

            for cp in weight_copies(e):
                cp.wait()
            wd_bf[...] = wd_buf[...].astype(BF)
            for c in range(D_FF // LANES):
                pair = _dot(wgu_buf[:, 2 * c * LANES:2 * (c + 1) * LANES].astype(BF), perm_ref[...])
                wg_bf[:, c * LANES:(c + 1) * LANES] = pair[:, 0:LANES].astype(BF)
                wu_bf[:, c * LANES:(c + 1) * LANES] = pair[:, LANES:].astype(BF)
            e_next = next_e[e]

            @pl.when(e_next >= 0)
            def _():
                for cp in weight_copies(e_next):
                    cp.start()

        n_prev = pl.multiple_of(cnt_smem[2 + slot], 8)

        @pl.when(n_prev > 0)
        def _():
            rows_copy(yb.at[slot], y_hbm, osem.at[slot], n_prev).wait()

        n_in = pl.multiple_of(cnt_smem[slot], 8)
        rows_copy(xs_hbm, xg.at[slot], gsem.at[slot], n_in).wait()

        def expert_mlp(rows):
            xb = xg[slot, 0:rows, :].astype(BF)
            gate = jnp.minimum(_dot(xb, wg_bf[...]) + bg_ref[...], SWIGLU_LIMIT)
            up = jnp.clip(_dot(xb, wu_bf[...]) + bu_ref[...], -SWIGLU_LIMIT, SWIGLU_LIMIT)
            glu = gate * jax.nn.sigmoid(gate * SWIGLU_ALPHA)
            y = _dot(((up + 1.0) * glu).astype(BF), wd_bf[...]) + bd_ref[...]
            yb[slot, 0:rows, :] = y.astype(BF).astype(F32)

        @pl.when(n_in > MOE_BM // 2)
        def _():
            expert_mlp(MOE_BM)

        @pl.when(n_in <= MOE_BM // 2)
        def _():
            expert_mlp(MOE_BM // 2)

        def piece(row_t, row_b, n):
            pltpu.make_async_copy(yb.at[slot].at[pl.ds(row_b, n), :], y_hbm.at[pl.ds(row_t, n), :],
                                  osem.at[slot]).start()

        cnt_smem[2 + slot] = for_pieces(i, piece)

    @pl.when(i == last)
    def _():
        for s in range(2):
            n_out = pl.multiple_of(cnt_smem[2 + s], 8)

            @pl.when(n_out > 0)
            def _():
                rows_copy(yb.at[s], y_hbm, osem.at[s], n_out).wait()

        def drain(t, c):
            n, cp = tail_copy(t)

            @pl.when(n > 0)
            def _():
                cp.wait()

            return c

        lax.fori_loop(0, nt, drain, 0)


def _experts(l, xs, plan, p, nt):
    n_blocks = plan["blk_e"].shape[0]
    n_pref = 10
    by_expert = lambda *lead: (lambda i, be, *_: lead + (be[i], 0, 0))
    b_spec = pl.BlockSpec((None, None, 1, D_FF), by_expert(l))
    grid_spec = pltpu.PrefetchScalarGridSpec(
        num_scalar_prefetch=n_pref,
        grid=(n_blocks,),
        in_specs=[
            pl.BlockSpec(memory_space=pl.ANY),
            pl.BlockSpec(memory_space=pl.ANY),
            b_spec, b_spec,
            pl.BlockSpec(memory_space=pl.ANY),
            pl.BlockSpec((None, None, 1, D_MODEL), by_expert(l)),
            pl.BlockSpec((2 * LANES, 2 * LANES), lambda i, *_: (0, 0)),
        ],
        out_specs=pl.BlockSpec(memory_space=pl.ANY),
        scratch_shapes=[
            pltpu.VMEM((2, MOE_BM, D_MODEL), F32),
            pltpu.VMEM((2, MOE_BM, D_MODEL), F32),
            pltpu.VMEM((R_TILE - TOP_K * TM, D_MODEL), F32),
            pltpu.VMEM((D_MODEL, 2 * D_FF), F32),
            pltpu.VMEM((D_FF, D_MODEL), F32),
            pltpu.VMEM((D_MODEL, D_FF), BF),
            pltpu.VMEM((D_MODEL, D_FF), BF),
            pltpu.VMEM((D_FF, D_MODEL), BF),
            pltpu.SMEM((4,), jnp.int32),
            pltpu.SemaphoreType.DMA((2,)),
            pltpu.SemaphoreType.DMA((2,)),
            pltpu.SemaphoreType.DMA,
            pltpu.SemaphoreType.DMA((2,)),
        ],
    )
    return pl.pallas_call(
        functools.partial(_expert_kernel, nt, l),
        grid_spec=grid_spec,
        out_shape=jax.ShapeDtypeStruct((nt * R_TILE, D_MODEL), F32),
        compiler_params=pltpu.CompilerParams(dimension_semantics=("arbitrary",), vmem_limit_bytes=VMEM_LIMIT),
        name="experts",
    )(plan["blk_e"], plan["blk_row0"], plan["blk_t0"], plan["blk_t1"], plan["n_used"], plan["pad8"], plan["dst"],
      plan["off"], plan["tot8"], plan["next_e"],
      xs, p["w_gu"], p["b_gate"], p["b_up"], p["w_dn"], p["b_dn"], p["pair_perm"])


def _pair_perm():
    m = np.zeros((2 * LANES, 2 * LANES), np.float32)
    j = np.arange(LANES)
    m[2 * j, j] = 1.0
    m[2 * j + 1, LANES + j] = 1.0
    return jnp.asarray(m, BF)


def _combine_kernel(final, y_ref, x_ref, mod_ref, rl_ref, tw_ref, fg_ref, o_ref):
    r_lane = lax.broadcasted_iota(jnp.int32, (TM, R_TILE), 1)
    rl = rl_ref[...]
    tw = tw_ref[...]
    sel = jnp.zeros((TM, R_TILE), F32)
    for k in range(TOP_K):
        sel = jnp.where(r_lane == rl[:, k:k + 1], tw[:, k:k + 1], sel)
    ffn = _dot(sel.astype(BF), y_ref[...].astype(BF))
    x = x_ref[...] + mod_ref[5:6, :] * ffn
    if final:
        x = _rms(x) * fg_ref[...]
    o_ref[...] = x


def _combine(l, final, y, x, mod, rl, tw, final_g, geo, tile0, n_tiles):
    seq_row = geo["seq_row"]
    tile = lambda w: pl.BlockSpec((TM, w), lambda i: (i + tile0, 0))
    return pl.pallas_call(
        functools.partial(_combine_kernel, final),
        grid=(n_tiles,),
        in_specs=[
            pl.BlockSpec((R_TILE, D_MODEL), lambda i: (i + tile0, 0)),
            tile(D_MODEL),
            pl.BlockSpec((None, None, 6, D_MODEL), lambda i: (l, seq_row(i + tile0), 0, 0)),
            tile(LANES),
            tile(LANES),
            _const_spec((1, D_MODEL)),
        ],
        out_specs=pl.BlockSpec((TM, D_MODEL), lambda i: (i, 0)),
        out_shape=jax.ShapeDtypeStruct((n_tiles * TM, D_MODEL), F32),
        compiler_params=pltpu.CompilerParams(dimension_semantics=("arbitrary",), vmem_limit_bytes=VMEM_LIMIT),
        name="combine",
    )(y, x, mod, rl, tw, final_g)


def _expert_plan(meta, nt):
    pad8 = meta[:, 0, :N_EXPERTS]
    off = meta[:, 1, :N_EXPERTS]
    ends = jnp.cumsum(pad8, axis=0)
    dst = ends - pad8
    tot = ends[-1]
    nb = (tot + MOE_BM - 1) // MOE_BM
    nb_end = jnp.cumsum(nb)
    n_blocks = (nt * TM * TOP_K + nt * N_EXPERTS * 7) // MOE_BM + N_EXPERTS
    b = jnp.arange(n_blocks, dtype=jnp.int32)
    blk_e = jnp.minimum(jnp.sum(nb_end[None, :] <= b[:, None], axis=1), N_EXPERTS - 1).astype(jnp.int32)
    blk_row0 = (b - (nb_end - nb)[blk_e]) * MOE_BM
    ends_b = ends[:, blk_e]
    dst_b = dst[:, blk_e]
    blk_t0 = jnp.sum(ends_b <= blk_row0[None, :], axis=0)
    blk_t1 = jnp.sum(dst_b < blk_row0[None, :] + MOE_BM, axis=0)
    i32 = lambda v: v.astype(jnp.int32)
    ids = jnp.arange(N_EXPERTS, dtype=jnp.int32)
    later = jnp.where((nb > 0)[None, :] & (ids[None, :] > ids[:, None]), ids[None, :], N_EXPERTS)
    next_e = jnp.min(later, axis=1)
    next_e = jnp.where(next_e == N_EXPERTS, -1, next_e)
    return {"next_e": i32(next_e), "blk_e": blk_e, "blk_row0": i32(blk_row0), "blk_t0": i32(blk_t0), "blk_t1": i32(blk_t1),
            "n_used": i32(nb_end[-1]).reshape(1), "pad8": i32(pad8.reshape(-1)), "dst": i32(dst.reshape(-1)),
            "off": i32(off.reshape(-1)), "tot8": i32(jnp.sum(pad8, axis=1))}


def _rope_tables(n_pos):
    pos = np.arange(n_pos)
    row, col = pos // GRID_W, pos % GRID_W
    lane = np.arange(LANES)

    def build(active, r, half):
        n_rot = 4 * half
        is_col = (r % n_rot) >= 2 * half
        rr = r % (2 * half)
        freq = np.power(ROPE_BASE, -(rr % half).astype(np.float64) / half)
        p = np.where(is_col[None, :], col[:, None], row[:, None]).astype(np.float64)
        ang = p * freq[None, :]
        first = rr < half
        cos = np.where(active[None, :], np.cos(ang), 1.0)
        sin_a = np.where((active & first)[None, :], -np.sin(ang), 0.0)
        sin_b = np.where((active & ~first)[None, :], np.sin(ang), 0.0)
        return [cos, sin_a, sin_b]

    tabs = build(np.ones(LANES, bool), lane % GQA_HEAD_DIM, GQA_HEAD_DIM // 4)
    in_rope = (lane >= MLA_NOPE) & (lane < MLA_NOPE + MLA_ROPE)
    tabs += build(in_rope, np.maximum(lane - MLA_NOPE, 0) % MLA_ROPE, MLA_ROPE // 4)
    tabs += build(lane < MLA_ROPE, lane % MLA_ROPE, MLA_ROPE // 4)
    table = np.concatenate(tabs, axis=1)
    ident = np.concatenate([np.ones((TM, LANES)), np.zeros((TM, LANES)), np.zeros((TM, LANES))] * 3, axis=1)
    return jnp.asarray(np.concatenate([ident, table], axis=0), F32)


def _placement():
    e = np.zeros((LANES, MLA_HEADS * LANES), np.float32)
    for h in range(MLA_HEADS):
        for r in range(MLA_ROPE):
            e[r, h * LANES + MLA_NOPE + r] = 1.0
    return jnp.asarray(e, BF)


def _split_hi_lo(w):
    hi = w.astype(BF)
    lo = (w - hi.astype(F32)).astype(BF)
    return jnp.concatenate([hi, lo], axis=-1)


def _block_diag_ones(n, g):
    idx = np.arange(n) // g
    return jnp.asarray((idx[:, None] == idx[None, :]).astype(np.float32), BF)


def kernel(x_prompt, x_sample, cache_gqa_k, cache_gqa_v, cache_mla_ckv, cache_mla_krope, c, c_ctx, norm1_g, norm2_g, w_mod, b_mod, w_in, conv_dw, conv_dw_b, conv_ln_g, conv_ln_b, w_conv_out, pool_w, pool_scale, w_pool_out, gqa_qn_g, gqa_kn_g, w_gqa_out, mla_qn_g, w_mla_q_up, mla_kvn_g, w_mla_kv_up, w_mla_out, w_o, w_router, b_router, w_gu, b_gu, w_dn, b_dn, final_g):
    bp, seq, d = x_prompt.shape
    bs, ls, _ = x_sample.shape
    depth = w_in.shape[0]
    past = cache_gqa_k.shape[2]
    assert seq == TM and d == D_MODEL and ls % TM == 0 and (bp * seq) % ls == 0
    tps = ls // TM
    n_p = bp
    nt = n_p + bs * tps
    geo = {
        "np": n_p, "bs": bs, "tps": tps, "nt": nt,
        "seq_row": lambda i: jnp.where(i < n_p, 0, 1 + (i - n_p) // tps),
        "rope_blk": lambda i: jnp.where(i < n_p, 0, 1 + (i - n_p) % tps),
    }

    n_cond = -(-(1 + bs) // 8) * 8
    cond = jnp.zeros((n_cond, d), F32).at[0].set(c_ctx).at[1:1 + bs].set(c)
    mod = _modulation(cond, w_mod, b_mod).reshape(depth, n_cond, 6, d)

    row = lambda v: v.reshape(depth, 1, -1)
    w1 = jnp.pad(w_in[:, :, :_SPLIT_GATE], ((0, 0), (0, 0), (0, W1_COLS - _SPLIT_GATE))).astype(BF)
    wqup = jnp.pad(w_mla_q_up.reshape(depth, MLA_Q_RANK, MLA_HEADS, MLA_NOPE + MLA_ROPE),
                   ((0, 0), (0, 0), (0, 0), (0, LANES - MLA_NOPE - MLA_ROPE)))
    wkv = w_mla_kv_up.reshape(depth, MLA_KV_RANK, MLA_HEADS, MLA_NOPE + MLA_V)
    wk_pad = jnp.pad(wkv[..., :MLA_NOPE], ((0, 0), (0, 0), (0, 0), (0, LANES - MLA_NOPE)))
    wkvup = jnp.concatenate([wk_pad.reshape(depth, MLA_KV_RANK, MLA_HEADS * LANES),
                             wkv[..., MLA_NOPE:].reshape(depth, MLA_KV_RANK, MLA_HEADS * MLA_V)], axis=-1)
    bgu = b_gu.reshape(depth, N_EXPERTS, 1, D_FF, 2)
    p = {
        "norm1_g": row(norm1_g), "norm2_g": row(norm2_g),
        "w1": w1, "wgate": w_in[:, :, _SPLIT_GATE:].astype(BF),
        "rope": _rope_tables(ls), "place": _placement(), "ones_bd": _block_diag_ones(512, GQA_HEAD_DIM),
        "ltri": jnp.asarray(np.tril(np.ones((TM, TM), np.float32), -1), BF),
        "upper": jnp.asarray(np.triu(np.ones((LANES, LANES), np.float32), 1), F32),
        "qn_g": row(jnp.tile(gqa_qn_g, (1, GQA_HEADS))), "kn_g": row(jnp.tile(gqa_kn_g, (1, GQA_KV_HEADS))),
        "cqn_g": row(mla_qn_g), "kvn_g": row(mla_kvn_g),
        "wqup": wqup.reshape(depth, MLA_Q_RANK, MLA_HEADS * LANES).astype(BF), "wkvup": wkvup.astype(BF),
        "conv_dw": conv_dw, "conv_dw_b": row(conv_dw_b), "conv_ln_g": row(conv_ln_g), "conv_ln_b": row(conv_ln_b),
        "w_conv_out": w_conv_out.astype(BF), "pool_w": pool_w.astype(BF), "pool_scale": row(pool_scale),
        "w_pool_out": w_pool_out.astype(BF), "w_gqa_out": w_gqa_out.astype(BF), "w_mla_out": w_mla_out.astype(BF),
        "w_o": w_o.astype(BF),
        "w_router": _split_hi_lo(jnp.pad(w_router, ((0, 0), (0, 0), (0, LANES - N_EXPERTS)))),
        "b_router": row(jnp.pad(b_router, ((0, 0), (0, LANES - N_EXPERTS)))),
        "w_gu": w_gu, "b_gate": bgu[..., 0], "b_up": bgu[..., 1], "pair_perm": _pair_perm(),
        "w_dn": w_dn, "b_dn": b_dn.reshape(depth, N_EXPERTS, 1, d),
    }

    ckm, cvm = _cache_prep(cache_mla_ckv, jnp.pad(cache_mla_krope, ((0, 0), (0, 0), (0, 0), (0, LANES - MLA_ROPE))), p)
    cache = (cache_gqa_k.reshape(bs, depth, past, LANES).astype(BF),
             cache_gqa_v.reshape(bs, depth, past, LANES).astype(BF), ckm, cvm)

    x = jnp.concatenate([x_prompt.reshape(bp * seq, d), x_sample.reshape(bs * ls, d)], axis=0)
    n_ptok = bp * seq
    states = []
    for l in range(depth):
        a, b, q, k, v, qm, km, vm, ks, vs, ckvs, krs = _proj(l, x, mod, geo, p)
        states.append((ks[:n_ptok], vs[:n_ptok], ckvs[:n_ptok], krs[:n_ptok]))
        attn = _attention(l, geo, q, k, v, qm, km, vm, cache)
        x_mid, xs, slot_row, top_w, meta = _merge(l, x, mod, geo, a, b, attn, p)
        y = _experts(l, xs, _expert_plan(meta, nt), p, nt)
        combine = functools.partial(_combine, l, l == depth - 1, y, x_mid, mod, slot_row, top_w,
                                    final_g.reshape(1, d), geo)
        if l < depth - 1:
            x = combine(0, nt)

    y_prompt = combine(0, n_p).reshape(bp, seq, d)
    y_sample = combine(n_p, nt - n_p).reshape(bs, ls, d)
    st = lambda j, shape: jnp.stack([s[j].reshape(shape) for s in states], axis=1)
    return (y_prompt, y_sample,
            st(0, (bp, seq, GQA_KV_HEADS, GQA_HEAD_DIM)), st(1, (bp, seq, GQA_KV_HEADS, GQA_HEAD_DIM)),
            st(2, (bp, seq, MLA_KV_RANK)), st(3, (bp, seq, MLA_ROPE)))
```

```python
import functools

import jax
import jax.numpy as jnp
import numpy as np
from jax import lax
from jax.experimental import pallas as pl
from jax.experimental.pallas import tpu as pltpu

D_MODEL = 1024
GRID_W = 64
CONV_W = 512
CONV_K = 31
POOL_W = 512
POOL_WINDOWS = (2, 4, 8, 16)
POOL_G = 128
GQA_HEADS = 8
GQA_KV_HEADS = 2
GQA_HEAD_DIM = 64
MLA_HEADS = 8
MLA_Q_RANK = 384
MLA_KV_RANK = 256
MLA_NOPE = 64
MLA_ROPE = 32
MLA_V = 64
ROPE_BASE = 10000.0
N_EXPERTS = 32
TOP_K = 4
D_FF = 1024
SWIGLU_LIMIT = 7.0
SWIGLU_ALPHA = 1.702
EPS = 1e-6
GQA_SCALE = GQA_HEAD_DIM ** -0.5
MLA_SCALE = (MLA_NOPE + MLA_ROPE) ** -0.5
LOG2E = 1.4426950408889634

LANES = 128
SUBLANES = 8
TM = 256
HALO = 16
MOE_BM = 512
R_TILE = 1280
W1_COLS = 3072
N_GATE = 4 * D_MODEL
VMEM_LIMIT = 56 * 1024 * 1024

BF = jnp.bfloat16
F32 = jnp.float32

_C_A, _C_B, _C_Q, _C_K, _C_V, _C_CQ, _C_CKV, _C_KR = 0, 1024, 1536, 2048, 2176, 2304, 2688, 2944
_SPLIT_GATE = 2976


def _dot(a, b):
    return jnp.dot(a, b, preferred_element_type=F32)


def _dot_nt(a, b):
    return lax.dot_general(a, b, (((1,), (1,)), ((), ())), preferred_element_type=F32)


def _rms(x):
    return x * lax.rsqrt(jnp.mean(x * x, axis=-1, keepdims=True) + EPS)


def _group_mean_sq(x, ones_bd, width):
    xx = x * x
    hi = xx.astype(BF)
    lo = (xx - hi.astype(F32)).astype(BF)
    return (_dot(hi, ones_bd) + _dot(lo, ones_bd)) * (1.0 / width)


def _tile_lanes(t, width):
    reps = width // LANES
    return t if reps == 1 else jnp.concatenate([t] * reps, axis=-1)


def _rope(x, cos, sin_a, sin_b, shift):
    w = x.shape[-1]
    return (x * _tile_lanes(cos, w) + pltpu.roll(x, w - shift, 1) * _tile_lanes(sin_a, w)
            + pltpu.roll(x, shift, 1) * _tile_lanes(sin_b, w))


def _modulated_norm(x, g, shift, scale):
    return _rms(x) * g * (1.0 + scale) + shift


def _mod_kernel(cond_ref, w_ref, b_ref, o_ref):
    c = cond_ref[...]
    s = (c * jax.nn.sigmoid(c)).astype(BF)
    o_ref[...] = _dot(s, w_ref[...].astype(BF)) + b_ref[...]


def _modulation(cond, w_mod, b_mod):
    depth, d, n = w_mod.shape
    rows = cond.shape[0]
    return pl.pallas_call(
        _mod_kernel,
        grid=(depth, n // D_MODEL),
        in_specs=[
            pl.BlockSpec((rows, d), lambda l, j: (0, 0)),
            pl.BlockSpec((None, d, D_MODEL), lambda l, j: (l, 0, j)),
            pl.BlockSpec((None, 1, D_MODEL), lambda l, j: (l, 0, j)),
        ],
        out_specs=pl.BlockSpec((None, rows, D_MODEL), lambda l, j: (l, 0, j)),
        out_shape=jax.ShapeDtypeStruct((depth, rows, n), F32),
        name="modulation",
    )(cond, w_mod, b_mod.reshape(depth, 1, n))


def _proj_kernel(x_ref, mod_ref, g1_ref, w1_ref, rope_ref, qn_ref, kn_ref, cqn_ref, kvn_ref,
                 wqup_ref, wkvup_ref, place_ref, ones_ref,
                 a_ref, b_ref, q_ref, k_ref, v_ref, qm_ref, km_ref, vm_ref,
                 ks_ref, vs_ref, ckvs_ref, krs_ref):
    x = x_ref[...]
    h = _modulated_norm(x, g1_ref[...], mod_ref[0:1, :], mod_ref[1:2, :])
    y = _dot(h.astype(BF), w1_ref[...])

    a_ref[...] = (y[:, _C_A:_C_A + CONV_W] * jax.nn.sigmoid(y[:, _C_A + CONV_W:_C_B])).astype(BF)
    b_ref[...] = y[:, _C_B:_C_Q].astype(BF)

    def tab(j):
        return rope_ref[:, j * LANES:(j + 1) * LANES]

    q = y[:, _C_Q:_C_K]
    q = q * lax.rsqrt(_group_mean_sq(q, ones_ref[...], GQA_HEAD_DIM) + EPS) * qn_ref[...]
    q = _rope(q, tab(0), tab(1), tab(2), GQA_HEAD_DIM // 4)
    q_ref[...] = (q * (GQA_SCALE * LOG2E)).astype(BF)

    k = y[:, _C_K:_C_V]
    k = k * lax.rsqrt(_group_mean_sq(k, ones_ref[0:LANES, 0:LANES], GQA_HEAD_DIM) + EPS) * kn_ref[...]
    ks_ref[...] = k
    k_ref[...] = _rope(k, tab(0), tab(1), tab(2), GQA_HEAD_DIM // 4).astype(BF)

    v = y[:, _C_V:_C_CQ]
    vs_ref[...] = v
    v_ref[...] = v.astype(BF)

    cq = _rms(y[:, _C_CQ:_C_CKV]) * cqn_ref[...]
    qm = _dot(cq.astype(BF), wqup_ref[...])
    qm = _rope(qm, tab(3), tab(4), tab(5), MLA_ROPE // 4)
    qm_ref[...] = (qm * (MLA_SCALE * LOG2E)).astype(BF)

    ckv = _rms(y[:, _C_CKV:_C_KR]) * kvn_ref[...]
    ckvs_ref[...] = ckv
    kv = _dot(ckv.astype(BF), wkvup_ref[...])
    kr = y[:, _C_KR:W1_COLS]
    krs_ref[...] = kr[:, 0:MLA_ROPE]
    kr_rot = _rope(kr, tab(6), tab(7), tab(8), MLA_ROPE // 4)
    km = kv[:, 0:MLA_HEADS * LANES] + _dot(kr_rot.astype(BF), place_ref[...])
    km_ref[...] = km.astype(BF)
    vm_ref[...] = kv[:, MLA_HEADS * LANES:].astype(BF)


def _const_spec(shape):
    nd = len(shape)
    return pl.BlockSpec(shape, lambda *_: (0,) * nd)


def _layer_spec(shape, l):
    nd = len(shape)
    return pl.BlockSpec((None,) + shape, lambda *_: (l,) + (0,) * nd)


def _proj(l, x, mod, geo, p):
    nt = geo["nt"]
    tok = nt * TM
    seq_row, rope_blk = geo["seq_row"], geo["rope_blk"]

    def tile(width, dtype):
        return pl.BlockSpec((TM, width), lambda i: (i, 0)), jax.ShapeDtypeStruct((tok, width), dtype)

    outs = [tile(CONV_W, BF), tile(POOL_W, BF), tile(512, BF), tile(LANES, BF), tile(LANES, BF),
            tile(MLA_HEADS * LANES, BF), tile(MLA_HEADS * LANES, BF), tile(MLA_HEADS * MLA_V, BF),
            tile(LANES, F32), tile(LANES, F32), tile(MLA_KV_RANK, F32), tile(MLA_ROPE, F32)]
    return pl.pallas_call(
        _proj_kernel,
        grid=(nt,),
        in_specs=[
            pl.BlockSpec((TM, D_MODEL), lambda i: (i, 0)),
            pl.BlockSpec((None, None, 6, D_MODEL), lambda i: (l, seq_row(i), 0, 0)),
            _layer_spec((1, D_MODEL), l),
            _layer_spec((D_MODEL, W1_COLS), l),
            pl.BlockSpec((TM, 9 * LANES), lambda i: (rope_blk(i), 0)),
            _layer_spec((1, 512), l),
            _layer_spec((1, LANES), l),
            _layer_spec((1, MLA_Q_RANK), l),
            _layer_spec((1, MLA_KV_RANK), l),
            _layer_spec((MLA_Q_RANK, MLA_HEADS * LANES), l),
            _layer_spec((MLA_KV_RANK, MLA_HEADS * (LANES + MLA_V)), l),
            _const_spec((LANES, MLA_HEADS * LANES)),
            _const_spec((512, 512)),
        ],
        out_specs=[o[0] for o in outs],
        out_shape=[o[1] for o in outs],
        compiler_params=pltpu.CompilerParams(dimension_semantics=("arbitrary",), vmem_limit_bytes=VMEM_LIMIT),
        name="proj",
    )(x, mod, p["norm1_g"], p["w_in"], p["rope"], p["qn_g"], p["kn_g"], p["cqn_g"], p["kvn_g"],
      p["wqup"], p["wkvup"], p["place"], p["ones_bd"])


def _cache_kernel(ckv_ref, kr_ref, wkvup_ref, place_ref, km_ref, vm_ref):
    kv = _dot(ckv_ref[...].astype(BF), wkvup_ref[...])
    km = kv[:, 0:MLA_HEADS * LANES] + _dot(kr_ref[...].astype(BF), place_ref[...])
    km_ref[...] = km.astype(BF)
    vm_ref[...] = kv[:, MLA_HEADS * LANES:].astype(BF)


def _cache_prep(ckv, kr_pad, p):
    bs, depth, past, _ = ckv.shape
    return pl.pallas_call(
        _cache_kernel,
        grid=(bs, depth),
        in_specs=[
            pl.BlockSpec((None, None, past, MLA_KV_RANK), lambda b, l: (b, l, 0, 0)),
            pl.BlockSpec((None, None, past, LANES), lambda b, l: (b, l, 0, 0)),
            pl.BlockSpec((None, MLA_KV_RANK, MLA_HEADS * (LANES + MLA_V)), lambda b, l: (l, 0, 0)),
            pl.BlockSpec((LANES, MLA_HEADS * LANES), lambda b, l: (0, 0)),
        ],
        out_specs=[
            pl.BlockSpec((None, None, past, MLA_HEADS * LANES), lambda b, l: (b, l, 0, 0)),
            pl.BlockSpec((None, None, past, MLA_HEADS * MLA_V), lambda b, l: (b, l, 0, 0)),
        ],
        out_shape=[jax.ShapeDtypeStruct((bs, depth, past, MLA_HEADS * LANES), BF),
                   jax.ShapeDtypeStruct((bs, depth, past, MLA_HEADS * MLA_V), BF)],
        name="cache_prep",
    )(ckv, kr_pad, p["wkvup"], p["place"])


def _attend(qs, k, v):
    s = _dot_nt(qs, k)
    m = jnp.max(s, axis=-1, keepdims=True)
    e = jnp.exp2(s - m)
    l = jnp.sum(e, axis=-1, keepdims=True)
    return _dot(e.astype(BF), v) / l


def _attention_body(q, k, v, qm, km, vm, oc_ref, od_ref):
    lane_k = lax.broadcasted_iota(jnp.int32, k.shape, 1)
    lane_q = lax.broadcasted_iota(jnp.int32, (TM, LANES), 1)
    lo_k = lane_k < GQA_HEAD_DIM
    lo_q = lane_q < GQA_HEAD_DIM
    k32, v32 = k.astype(F32), v.astype(F32)
    k_sw = pltpu.roll(k32, GQA_HEAD_DIM, 1)
    v_sw = pltpu.roll(v32, GQA_HEAD_DIM, 1)
    k_dup = [jnp.where(lo_k, k32, k_sw).astype(BF), jnp.where(lo_k, k_sw, k32).astype(BF)]
    v_dup = [jnp.where(lo_k, v32, v_sw).astype(BF), jnp.where(lo_k, v_sw, v32).astype(BF)]
    zero = jnp.zeros((TM, LANES), BF)
    group = GQA_HEADS // GQA_KV_HEADS
    for j in range(GQA_HEADS // 2):
        qs = q[:, j * LANES:(j + 1) * LANES]
        g = (2 * j) // group
        o_lo = _attend(jnp.where(lo_q, qs, zero), k_dup[g], v_dup[g])
        o_hi = _attend(jnp.where(lo_q, zero, qs), k_dup[g], v_dup[g])
        oc_ref[:, j * LANES:(j + 1) * LANES] = jnp.where(lo_q, o_lo, o_hi).astype(BF)
    for j in range(MLA_HEADS // 2):
        vs = vm[:, j * LANES:(j + 1) * LANES]
        outs = []
        for h in (2 * j, 2 * j + 1):
            outs.append(_attend(qm[:, h * LANES:(h + 1) * LANES], km[:, h * LANES:(h + 1) * LANES], vs))
        od_ref[:, j * LANES:(j + 1) * LANES] = jnp.where(lo_q, outs[0], outs[1]).astype(BF)


def _attn_prompt_kernel(q_ref, k_ref, v_ref, qm_ref, km_ref, vm_ref, oc_ref, od_ref):
    _attention_body(q_ref[...], k_ref[...], v_ref[...], qm_ref[...], km_ref[...], vm_ref[...], oc_ref, od_ref)


def _attn_sample_kernel(q_ref, k_ref, v_ref, qm_ref, km_ref, vm_ref, ck_ref, cv_ref, ckm_ref, cvm_ref,
                        oc_ref, od_ref):
    k = jnp.concatenate([ck_ref[...], k_ref[...]], axis=0)
    v = jnp.concatenate([cv_ref[...], v_ref[...]], axis=0)
    km = jnp.concatenate([ckm_ref[...], km_ref[...]], axis=0)
    vm = jnp.concatenate([cvm_ref[...], vm_ref[...]], axis=0)
    _attention_body(q_ref[...], k, v, qm_ref[...], km, vm, oc_ref, od_ref)


def _attention(l, geo, q, k, v, qm, km, vm, cache):
    n_p, bs, tps, nt = geo["np"], geo["bs"], geo["tps"], geo["nt"]
    tok = nt * TM
    ls = tps * TM
    off = (n_p * TM) // ls
    widths = (512, LANES, LANES, MLA_HEADS * LANES, MLA_HEADS * LANES, MLA_HEADS * MLA_V)
    params = pltpu.CompilerParams(dimension_semantics=("arbitrary",), vmem_limit_bytes=VMEM_LIMIT)
    oc_p, od_p = pl.pallas_call(
        _attn_prompt_kernel,
        grid=(n_p,),
        in_specs=[pl.BlockSpec((TM, w), lambda i: (i, 0)) for w in widths],
        out_specs=[pl.BlockSpec((TM, 512), lambda i: (i, 0))] * 2,
        out_shape=[jax.ShapeDtypeStruct((n_p * TM, 512), BF)] * 2,
        compiler_params=params,
        name="attn_prompt",
    )(q, k, v, qm, km, vm)
    ck, cv, ckm, cvm = cache
    past = ck.shape[2]
    q_spec = lambda w: pl.BlockSpec((TM, w), lambda b, j: (n_p + b * tps + j, 0))
    kv_spec = lambda w: pl.BlockSpec((ls, w), lambda b, j: (off + b, 0))
    c_spec = lambda w: pl.BlockSpec((None, None, past, w), lambda b, j: (b, l, 0, 0))
    params2 = pltpu.CompilerParams(dimension_semantics=("arbitrary", "arbitrary"), vmem_limit_bytes=VMEM_LIMIT)
    o_spec = pl.BlockSpec((TM, 512), lambda b, j: (b * tps + j, 0))
    oc_s, od_s = pl.pallas_call(
        _attn_sample_kernel,
        grid=(bs, tps),
        in_specs=[q_spec(512), kv_spec(LANES), kv_spec(LANES), q_spec(MLA_HEADS * LANES),
                  kv_spec(MLA_HEADS * LANES), kv_spec(MLA_HEADS * MLA_V),
                  c_spec(LANES), c_spec(LANES), c_spec(MLA_HEADS * LANES), c_spec(MLA_HEADS * MLA_V)],
        out_specs=[o_spec, o_spec],
        out_shape=[jax.ShapeDtypeStruct((bs * ls, 512), BF)] * 2,
        compiler_params=params2,
        name="attn_sample",
    )(q, k, v, qm, km, vm, ck, cv, ckm, cvm)
    return oc_p, od_p, oc_s, od_s


def _merge_kernel(geo, x_ref, mod_ref, a_ref, ap_ref, an_ref, b_ref, bp_ref, bn_ref,
                  ocp_ref, odp_ref, ocs_ref, ods_ref,
                  g1_ref, wg_ref, cw_ref, cb_ref, lng_ref, lnb_ref, wco_ref, pw_ref, ps_ref, wpo_ref,
                  wgo_ref, wmo_ref, wo_ref, g2_ref, wr_ref, br_ref,
                  ltri_ref, upper_ref,
                  xo_ref, xs_ref, rl_ref, tw_ref, meta_ref, abuf, bbuf, ashift):
    n_p, tps = geo["np"], geo["tps"]
    i = pl.program_id(0)
    j = jnp.where(i < n_p, 0, (i - n_p) % tps)
    n_seq_tiles = jnp.where(i < n_p, 1, tps)
    has_prev = j > 0
    has_next = j < n_seq_tiles - 1

    def fill(buf, cur, prev, nxt):
        buf[0:HALO, :] = jnp.where(has_prev, prev[...].astype(F32), 0.0)
        buf[HALO:HALO + TM, :] = cur[...].astype(F32)
        buf[HALO + TM:, :] = jnp.where(has_next, nxt[...].astype(F32), 0.0)

    fill(abuf, a_ref, ap_ref, an_ref)
    fill(bbuf, b_ref, bp_ref, bn_ref)

    sh_rows = TM + 2 * HALO - SUBLANES
    for sh in range(1, SUBLANES):
        ashift[sh - 1] = abuf[sh:sh + sh_rows, :]
    rows = 32
    conv = []
    for r0 in range(0, TM, rows):
        acc = jnp.zeros((rows, CONV_W), F32)
        for t in range(CONV_K):
            s = r0 + t + HALO - CONV_K // 2
            sh = s % SUBLANES
            tap = abuf[s:s + rows, :] if sh == 0 else ashift[sh - 1, s - sh:s - sh + rows, :]
            acc = acc + tap * cw_ref[t:t + 1, :]
        conv.append(acc)
    ca = jnp.concatenate(conv, axis=0) + cb_ref[...]
    mu = jnp.mean(ca, axis=-1, keepdims=True)
    xc = ca - mu
    ln = xc * lax.rsqrt(jnp.mean(xc * xc, axis=-1, keepdims=True) + EPS) * lng_ref[...] + lnb_ref[...]
    br_a = _dot((ln * jax.nn.sigmoid(ln)).astype(BF), wco_ref[...])

    pos = j * TM + lax.broadcasted_iota(jnp.int32, (TM, 1), 0)
    seq_len = n_seq_tiles * TM
    mixed = []
    for g, w in enumerate(POOL_WINDOWS):
        lo, hi = g * POOL_G, (g + 1) * POOL_G
        acc = jnp.zeros((TM, POOL_G), F32)
        for o in range(-(w // 2), w - w // 2):
            acc = acc + bbuf[HALO + o:HALO + o + TM, lo:hi]
        cnt = jnp.clip(pos - w // 2 + w, 0, seq_len) - jnp.clip(pos - w // 2, 0, seq_len)
        pooled = acc / cnt.astype(F32) - bbuf[HALO:HALO + TM, lo:hi]
        mixed.append(_dot(pooled.astype(BF), pw_ref[g]))
    pb = jnp.concatenate(mixed, axis=-1) * ps_ref[...]
    br_b = _dot(pb.astype(BF), wpo_ref[...])

    is_prompt = i < n_p
    br_c = _dot(jnp.where(is_prompt, ocp_ref[...], ocs_ref[...]), wgo_ref[...])
    br_d = _dot(jnp.where(is_prompt, odp_ref[...], ods_ref[...]), wmo_ref[...])

    x = x_ref[...]
    h = _modulated_norm(x, g1_ref[...], mod_ref[0:1, :], mod_ref[1:2, :])
    gates = jax.nn.sigmoid(_dot(h.astype(BF), wg_ref[...]).astype(BF))
    merged = (gates[:, 0:D_MODEL] * br_a.astype(BF) + gates[:, D_MODEL:2 * D_MODEL] * br_b.astype(BF)
              + gates[:, 2 * D_MODEL:3 * D_MODEL] * br_c.astype(BF) + gates[:, 3 * D_MODEL:] * br_d.astype(BF))
    x = x + mod_ref[2:3, :] * _dot(merged, wo_ref[...])
    xo_ref[...] = x

    h2 = _modulated_norm(x, g2_ref[...], mod_ref[3:4, :], mod_ref[4:5, :])

    h2_hi = h2.astype(BF)
    h2_lo = (h2 - h2_hi.astype(F32)).astype(BF)
    hi_terms = _dot(h2_hi, wr_ref[...])
    logits = (hi_terms[:, 0:LANES] + hi_terms[:, LANES:] + _dot(h2_lo, wr_ref[:, 0:LANES])) + br_ref[...]
    lane = lax.broadcasted_iota(jnp.int32, (TM, LANES), 1).astype(F32)
    neg = jnp.float32(-jnp.inf)
    lg = jnp.where(lane < N_EXPERTS, logits, neg)
    vals, idxs = [], []
    for _ in range(TOP_K):
        m = jnp.max(lg, axis=-1, keepdims=True)
        idx = jnp.min(jnp.where(lg == m, lane, float(LANES)), axis=-1, keepdims=True)
        vals.append(m)
        idxs.append(idx)
        lg = jnp.where(lane == idx, neg, lg)
    exps = [jnp.exp(v - vals[0]) for v in vals]
    denom = exps[0] + exps[1] + exps[2] + exps[3]

    onehot = [(lane == idxs[kk]).astype(F32) for kk in range(TOP_K)]
    colsum = [jnp.sum(o, axis=0, keepdims=True) for o in onehot]
    cnt = colsum[0] + colsum[1] + colsum[2] + colsum[3]
    pad8 = jnp.floor((cnt + 7.0) * 0.125) * 8.0
    run_off = jnp.dot(jnp.broadcast_to(pad8, (8, LANES)), upper_ref[...], preferred_element_type=F32,
                      precision=lax.Precision.HIGHEST)[0:1, :]
    base = jnp.zeros((1, LANES), F32)
    rloc = []
    for kk in range(TOP_K):
        before = _dot(ltri_ref[...], onehot[kk].astype(BF))
        rloc.append(jnp.sum(onehot[kk] * (run_off + base + before), axis=-1, keepdims=True))
        base = base + colsum[kk]

    eye = (lax.broadcasted_iota(jnp.int32, (TM, TM), 0) == lax.broadcasted_iota(jnp.int32, (TM, TM), 1))
    r_sub = lax.broadcasted_iota(jnp.int32, (R_TILE, TM), 0).astype(F32)
    ones8 = jnp.ones((8, TM), BF)
    sel = jnp.zeros((R_TILE, TM), F32)
    for kk in range(TOP_K):
        hi = jnp.floor(rloc[kk] * (1.0 / TM))
        lo = rloc[kk] - hi * TM
        row = (_dot(ones8, jnp.where(eye, hi, 0.0).astype(BF)) * TM
               + _dot(ones8, jnp.where(eye, lo, 0.0).astype(BF)))[0:1, :]
        sel = jnp.where(r_sub == row, 1.0, sel)
    xs_ref[...] = _dot(sel.astype(BF), h2.astype(BF))

    rl = jnp.zeros((TM, LANES), F32)
    tw = jnp.zeros((TM, LANES), F32)
    for kk in range(TOP_K):
        rl = jnp.where(lane == kk, rloc[kk], rl)
        tw = jnp.where(lane == kk, exps[kk] / denom, tw)
    rl_ref[...] = rl.astype(jnp.int32)
    tw_ref[...] = tw
    sub = lax.broadcasted_iota(jnp.int32, (8, LANES), 0)
    meta = jnp.where(sub == 0, jnp.broadcast_to(pad8, (8, LANES)),
                     jnp.where(sub == 1, jnp.broadcast_to(run_off, (8, LANES)), 0.0))
    meta_ref[...] = meta.astype(jnp.int32)


def _merge(l, x, mod, geo, a, b, attn, p):
    nt, n_p = geo["nt"], geo["np"]
    tok = nt * TM
    seq_row = geo["seq_row"]
    hb = TM // HALO
    last_hb = nt * hb - 1
    tile = lambda w: pl.BlockSpec((TM, w), lambda i: (i, 0))
    prev = lambda w: pl.BlockSpec((HALO, w), lambda i: (jnp.maximum(i * hb - 1, 0), 0))
    nxt = lambda w: pl.BlockSpec((HALO, w), lambda i: (jnp.minimum((i + 1) * hb, last_hb), 0))
    p_tile = pl.BlockSpec((TM, 512), lambda i: (jnp.minimum(i, n_p - 1), 0))
    s_tile = pl.BlockSpec((TM, 512), lambda i: (jnp.maximum(i - n_p, 0), 0))
    return pl.pallas_call(
        functools.partial(_merge_kernel, geo),
        grid=(nt,),
        in_specs=[
            tile(D_MODEL),
            pl.BlockSpec((None, None, 6, D_MODEL), lambda i: (l, seq_row(i), 0, 0)),
            tile(CONV_W), prev(CONV_W), nxt(CONV_W),
            tile(POOL_W), prev(POOL_W), nxt(POOL_W),
            p_tile, p_tile, s_tile, s_tile,
            _layer_spec((1, D_MODEL), l),
            pl.BlockSpec((None, D_MODEL, N_GATE), lambda i: (l, 0, 1)),
            _layer_spec((CONV_K, CONV_W), l),
            _layer_spec((1, CONV_W), l),
            _layer_spec((1, CONV_W), l),
            _layer_spec((1, CONV_W), l),
            _layer_spec((CONV_W, D_MODEL), l),
            _layer_spec((len(POOL_WINDOWS), POOL_G, POOL_G), l),
            _layer_spec((1, POOL_W), l),
            _layer_spec((POOL_W, D_MODEL), l),
            _layer_spec((512, D_MODEL), l),
            _layer_spec((512, D_MODEL), l),
            _layer_spec((D_MODEL, D_MODEL), l),
            _layer_spec((1, D_MODEL), l),
            _layer_spec((D_MODEL, 2 * LANES), l),
            _layer_spec((1, LANES), l),
            _const_spec((TM, TM)),
            _const_spec((LANES, LANES)),
        ],
        out_specs=[tile(D_MODEL), pl.BlockSpec((R_TILE, D_MODEL), lambda i: (i, 0)), tile(LANES), tile(LANES),
                   pl.BlockSpec((None, 8, LANES), lambda i: (i, 0, 0))],
        out_shape=[jax.ShapeDtypeStruct((tok, D_MODEL), F32), jax.ShapeDtypeStruct((nt * R_TILE, D_MODEL), F32),
                   jax.ShapeDtypeStruct((tok, LANES), jnp.int32), jax.ShapeDtypeStruct((tok, LANES), F32),
                   jax.ShapeDtypeStruct((nt, 8, LANES), jnp.int32)],
        scratch_shapes=[pltpu.VMEM((TM + 2 * HALO, CONV_W), F32), pltpu.VMEM((TM + 2 * HALO, POOL_W), F32),
                        pltpu.VMEM((SUBLANES - 1, TM + 2 * HALO - SUBLANES, CONV_W), F32)],
        compiler_params=pltpu.CompilerParams(dimension_semantics=("arbitrary",), vmem_limit_bytes=VMEM_LIMIT),
        name="merge",
    )(x, mod, a, a, a, b, b, b, *attn,
      p["norm1_g"], p["w_in"], p["conv_dw"], p["conv_dw_b"], p["conv_ln_g"], p["conv_ln_b"], p["w_conv_out"],
      p["pool_w"], p["pool_scale"], p["w_pool_out"], p["w_gqa_out"], p["w_mla_out"], p["w_o"],
      p["norm2_g"], p["w_router"], p["b_router"], p["ltri"], p["upper"])


def _expert_kernel(nt, layer, blk_e, blk_row0, blk_t0, blk_t1, n_used_ref, pad8_ref, dst_ref, off_ref, tot8_ref,
                   next_e, xs_hbm, wgu_hbm, bg_ref, bu_ref, wd_hbm, bd_ref, perm_ref,
                   y_hbm, xg, yb, zbuf, wgu_buf, wd_buf, wg_bf, wu_bf, wd_bf, cnt_smem, gsem, osem, zsem, wsem):
    i = pl.program_id(0)
    n_used = n_used_ref[0]
    slot = i % 2
    last = pl.num_programs(0) - 1

    def for_pieces(b, fn):
        e = blk_e[b]
        b0 = blk_row0[b]

        def body(t, tot):
            j = t * N_EXPERTS + e
            run0 = dst_ref[j]
            lo = jnp.maximum(run0, b0)
            hi = jnp.minimum(run0 + pad8_ref[j], b0 + MOE_BM)
            n = pl.multiple_of(jnp.maximum(hi - lo, 0), 8)

            @pl.when(n > 0)
            def _():
                fn(pl.multiple_of(t * R_TILE + off_ref[j] + lo - run0, 8), pl.multiple_of(lo - b0, 8), n)

            return tot + n

        return lax.fori_loop(blk_t0[b], blk_t1[b], body, jnp.int32(0))

    def rows_copy(src, dst, sem, n):
        return pltpu.make_async_copy(src.at[pl.ds(0, n), :], dst.at[pl.ds(0, n), :], sem)

    def start_gather(b, s):
        def piece(row_t, row_b, n):
            pltpu.make_async_copy(xs_hbm.at[pl.ds(row_t, n), :], xg.at[s].at[pl.ds(row_b, n), :], gsem.at[s]).start()

        cnt_smem[s] = for_pieces(b, piece)

    def tail_copy(t):
        n = pl.multiple_of(R_TILE - tot8_ref[t], 8)
        return n, pltpu.make_async_copy(
            zbuf.at[pl.ds(0, n), :], y_hbm.at[pl.ds(pl.multiple_of(t * R_TILE + tot8_ref[t], 8), n), :], zsem)

    @pl.when(i == 0)
    def _():
        xg[...] = jnp.zeros_like(xg)
        zbuf[...] = jnp.zeros_like(zbuf)
        for s in range(4):
            cnt_smem[s] = 0

        def fill(t, c):
            n, cp = tail_copy(t)

            @pl.when(n > 0)
            def _():
                cp.start()

            return c

        lax.fori_loop(0, nt, fill, 0)

        @pl.when(n_used > 0)
        def _():
            start_gather(0, 0)

    @pl.when(i + 1 < n_used)
    def _():
        start_gather(i + 1, 1 - slot)

    @pl.when(i < n_used)
    def _():
        e_changed = jnp.logical_or(i == 0, blk_e[i] != blk_e[jnp.maximum(i - 1, 0)])

        def weight_copies(e):
            return (pltpu.make_async_copy(wgu_hbm.at[layer, e], wgu_buf, wsem.at[0]),
                    pltpu.make_async_copy(wd_hbm.at[layer, e], wd_buf, wsem.at[1]))

        @pl.when(e_changed)
        def _():
            e = blk_e[i]

            @pl.when(i == 0)
            def _():
                for cp in weight_copies(e):
                    cp.start()

            for cp in weight_copies(e):
                cp.wait()
            wd_bf[...] = wd_buf[...].astype(BF)
            for c in range(D_FF // LANES):
                pair = _dot(wgu_buf[:, 2 * c * LANES:2 * (c + 1) * LANES].astype(BF), perm_ref[...])
                wg_bf[:, c * LANES:(c + 1) * LANES] = pair[:, 0:LANES].astype(BF)
                wu_bf[:, c * LANES:(c + 1) * LANES] = pair[:, LANES:].astype(BF)
            e_next = next_e[e]

            @pl.when(e_next >= 0)
            def _():
                for cp in weight_copies(e_next):
                    cp.start()

        n_prev = pl.multiple_of(cnt_smem[2 + slot], 8)

        @pl.when(n_prev > 0)
        def _():
            rows_copy(yb.at[slot], y_hbm, osem.at[slot], n_prev).wait()

        n_in = pl.multiple_of(cnt_smem[slot], 8)
        rows_copy(xs_hbm, xg.at[slot], gsem.at[slot], n_in).wait()

        def expert_mlp(rows):
            xb = xg[slot, 0:rows, :].astype(BF)
            gate = jnp.minimum(_dot(xb, wg_bf[...]) + bg_ref[...], SWIGLU_LIMIT)
            up = jnp.clip(_dot(xb, wu_bf[...]) + bu_ref[...], -SWIGLU_LIMIT, SWIGLU_LIMIT)
            glu = gate * jax.nn.sigmoid(gate * SWIGLU_ALPHA)
            y = _dot(((up + 1.0) * glu).astype(BF), wd_bf[...]) + bd_ref[...]
            yb[slot, 0:rows, :] = y.astype(BF).astype(F32)

        @pl.when(n_in > MOE_BM // 2)
        def _():
            expert_mlp(MOE_BM)

        @pl.when(n_in <= MOE_BM // 2)
        def _():
            expert_mlp(MOE_BM // 2)

        def piece(row_t, row_b, n):
            pltpu.make_async_copy(yb.at[slot].at[pl.ds(row_b, n), :], y_hbm.at[pl.ds(row_t, n), :],
                                  osem.at[slot]).start()

        cnt_smem[2 + slot] = for_pieces(i, piece)

    @pl.when(i == last)
    def _():
        for s in range(2):
            n_out = pl.multiple_of(cnt_smem[2 + s], 8)

            @pl.when(n_out > 0)
            def _():
                rows_copy(yb.at[s], y_hbm, osem.at[s], n_out).wait()

        def drain(t, c):
            n, cp = tail_copy(t)

            @pl.when(n > 0)
            def _():
                cp.wait()

            return c

        lax.fori_loop(0, nt, drain, 0)


def _experts(l, xs, plan, p, nt):
    n_blocks = plan["blk_e"].shape[0]
    n_pref = 10
    by_expert = lambda *lead: (lambda i, be, *_: lead + (be[i], 0, 0))
    b_spec = pl.BlockSpec((None, None, 1, D_FF), by_expert(l))
    grid_spec = pltpu.PrefetchScalarGridSpec(
        num_scalar_prefetch=n_pref,
        grid=(n_blocks,),
        in_specs=[
            pl.BlockSpec(memory_space=pl.ANY),
            pl.BlockSpec(memory_space=pl.ANY),
            b_spec, b_spec,
            pl.BlockSpec(memory_space=pl.ANY),
            pl.BlockSpec((None, None, 1, D_MODEL), by_expert(l)),
            pl.BlockSpec((2 * LANES, 2 * LANES), lambda i, *_: (0, 0)),
        ],
        out_specs=pl.BlockSpec(memory_space=pl.ANY),
        scratch_shapes=[
            pltpu.VMEM((2, MOE_BM, D_MODEL), F32),
            pltpu.VMEM((2, MOE_BM, D_MODEL), F32),
            pltpu.VMEM((R_TILE - TOP_K * TM, D_MODEL), F32),
            pltpu.VMEM((D_MODEL, 2 * D_FF), F32),
            pltpu.VMEM((D_FF, D_MODEL), F32),
            pltpu.VMEM((D_MODEL, D_FF), BF),
            pltpu.VMEM((D_MODEL, D_FF), BF),
            pltpu.VMEM((D_FF, D_MODEL), BF),
            pltpu.SMEM((4,), jnp.int32),
            pltpu.SemaphoreType.DMA((2,)),
            pltpu.SemaphoreType.DMA((2,)),
            pltpu.SemaphoreType.DMA,
            pltpu.SemaphoreType.DMA((2,)),
        ],
    )
    return pl.pallas_call(
        functools.partial(_expert_kernel, nt, l),
        grid_spec=grid_spec,
        out_shape=jax.ShapeDtypeStruct((nt * R_TILE, D_MODEL), F32),
        compiler_params=pltpu.CompilerParams(dimension_semantics=("arbitrary",), vmem_limit_bytes=VMEM_LIMIT),
        name="experts",
    )(plan["blk_e"], plan["blk_row0"], plan["blk_t0"], plan["blk_t1"], plan["n_used"], plan["pad8"], plan["dst"],
      plan["off"], plan["tot8"], plan["next_e"],
      xs, p["w_gu"], p["b_gate"], p["b_up"], p["w_dn"], p["b_dn"], p["pair_perm"])


def _pair_perm():
    m = np.zeros((2 * LANES, 2 * LANES), np.float32)
    j = np.arange(LANES)
    m[2 * j, j] = 1.0
    m[2 * j + 1, LANES + j] = 1.0
    return jnp.asarray(m, BF)


def _combine_kernel(final, y_ref, x_ref, mod_ref, rl_ref, tw_ref, fg_ref, o_ref):
    r_lane = lax.broadcasted_iota(jnp.int32, (TM, R_TILE), 1)
    rl = rl_ref[...]
    tw = tw_ref[...]
    sel = jnp.zeros((TM, R_TILE), F32)
    for k in range(TOP_K):
        sel = jnp.where(r_lane == rl[:, k:k + 1], tw[:, k:k + 1], sel)
    ffn = _dot(sel.astype(BF), y_ref[...].astype(BF))
    x = x_ref[...] + mod_ref[5:6, :] * ffn
    if final:
        x = _rms(x) * fg_ref[...]
    o_ref[...] = x


def _combine(l, final, y, x, mod, rl, tw, final_g, geo, tile0, n_tiles):
    seq_row = geo["seq_row"]
    tile = lambda w: pl.BlockSpec((TM, w), lambda i: (i + tile0, 0))
    return pl.pallas_call(
        functools.partial(_combine_kernel, final),
        grid=(n_tiles,),
        in_specs=[
            pl.BlockSpec((R_TILE, D_MODEL), lambda i: (i + tile0, 0)),
            tile(D_MODEL),
            pl.BlockSpec((None, None, 6, D_MODEL), lambda i: (l, seq_row(i + tile0), 0, 0)),
            tile(LANES),
            tile(LANES),
            _const_spec((1, D_MODEL)),
        ],
        out_specs=pl.BlockSpec((TM, D_MODEL), lambda i: (i, 0)),
        out_shape=jax.ShapeDtypeStruct((n_tiles * TM, D_MODEL), F32),
        compiler_params=pltpu.CompilerParams(dimension_semantics=("arbitrary",), vmem_limit_bytes=VMEM_LIMIT),
        name="combine",
    )(y, x, mod, rl, tw, final_g)


def _expert_plan(meta, nt):
    pad8 = meta[:, 0, :N_EXPERTS]
    off = meta[:, 1, :N_EXPERTS]
    ends = jnp.cumsum(pad8, axis=0)
    dst = ends - pad8
    tot = ends[-1]
    nb = (tot + MOE_BM - 1) // MOE_BM
    nb_end = jnp.cumsum(nb)
    n_blocks = (nt * TM * TOP_K + nt * N_EXPERTS * 7) // MOE_BM + N_EXPERTS
    b = jnp.arange(n_blocks, dtype=jnp.int32)
    blk_e = jnp.minimum(jnp.sum(nb_end[None, :] <= b[:, None], axis=1), N_EXPERTS - 1).astype(jnp.int32)
    blk_row0 = (b - (nb_end - nb)[blk_e]) * MOE_BM
    ends_b = ends[:, blk_e]
    dst_b = dst[:, blk_e]
    blk_t0 = jnp.sum(ends_b <= blk_row0[None, :], axis=0)
    blk_t1 = jnp.sum(dst_b < blk_row0[None, :] + MOE_BM, axis=0)
    i32 = lambda v: v.astype(jnp.int32)
    ids = jnp.arange(N_EXPERTS, dtype=jnp.int32)
    later = jnp.where((nb > 0)[None, :] & (ids[None, :] > ids[:, None]), ids[None, :], N_EXPERTS)
    next_e = jnp.min(later, axis=1)
    next_e = jnp.where(next_e == N_EXPERTS, -1, next_e)
    return {"next_e": i32(next_e), "blk_e": blk_e, "blk_row0": i32(blk_row0), "blk_t0": i32(blk_t0), "blk_t1": i32(blk_t1),
            "n_used": i32(nb_end[-1]).reshape(1), "pad8": i32(pad8.reshape(-1)), "dst": i32(dst.reshape(-1)),
            "off": i32(off.reshape(-1)), "tot8": i32(jnp.sum(pad8, axis=1))}


def _rope_tables(n_pos):
    pos = np.arange(n_pos)
    row, col = pos // GRID_W, pos % GRID_W
    lane = np.arange(LANES)

    def build(active, r, half):
        n_rot = 4 * half
        is_col = (r % n_rot) >= 2 * half
        rr = r % (2 * half)
        freq = np.power(ROPE_BASE, -(rr % half).astype(np.float64) / half)
        p = np.where(is_col[None, :], col[:, None], row[:, None]).astype(np.float64)
        ang = p * freq[None, :]
        first = rr < half
        cos = np.where(active[None, :], np.cos(ang), 1.0)
        sin_a = np.where((active & first)[None, :], -np.sin(ang), 0.0)
        sin_b = np.where((active & ~first)[None, :], np.sin(ang), 0.0)
        return [cos, sin_a, sin_b]

    tabs = build(np.ones(LANES, bool), lane % GQA_HEAD_DIM, GQA_HEAD_DIM // 4)
    in_rope = (lane >= MLA_NOPE) & (lane < MLA_NOPE + MLA_ROPE)
    tabs += build(in_rope, np.maximum(lane - MLA_NOPE, 0) % MLA_ROPE, MLA_ROPE // 4)
    tabs += build(lane < MLA_ROPE, lane % MLA_ROPE, MLA_ROPE // 4)
    table = np.concatenate(tabs, axis=1)
    ident = np.concatenate([np.ones((TM, LANES)), np.zeros((TM, LANES)), np.zeros((TM, LANES))] * 3, axis=1)
    return jnp.asarray(np.concatenate([ident, table], axis=0), F32)


def _placement():
    e = np.zeros((LANES, MLA_HEADS * LANES), np.float32)
    for h in range(MLA_HEADS):
        for r in range(MLA_ROPE):
            e[r, h * LANES + MLA_NOPE + r] = 1.0
    return jnp.asarray(e, BF)


def _split_hi_lo(w):
    hi = w.astype(BF)
    lo = (w - hi.astype(F32)).astype(BF)
    return jnp.concatenate([hi, lo], axis=-1)


def _block_diag_ones(n, g):
    idx = np.arange(n) // g
    return jnp.asarray((idx[:, None] == idx[None, :]).astype(np.float32), BF)


def kernel(x_prompt, x_sample, cache_gqa_k, cache_gqa_v, cache_mla_ckv, cache_mla_krope, c, c_ctx, norm1_g, norm2_g, w_mod, b_mod, w_in, conv_dw, conv_dw_b, conv_ln_g, conv_ln_b, w_conv_out, pool_w, pool_scale, w_pool_out, gqa_qn_g, gqa_kn_g, w_gqa_out, mla_qn_g, w_mla_q_up, mla_kvn_g, w_mla_kv_up, w_mla_out, w_o, w_router, b_router, w_gu, b_gu, w_dn, b_dn, final_g):
    bp, seq, d = x_prompt.shape
    bs, ls, _ = x_sample.shape
    depth = w_in.shape[0]
    past = cache_gqa_k.shape[2]
    assert seq == TM and d == D_MODEL and ls % TM == 0 and (bp * seq) % ls == 0
    tps = ls // TM
    n_p = bp
    nt = n_p + bs * tps
    geo = {
        "np": n_p, "bs": bs, "tps": tps, "nt": nt,
        "seq_row": lambda i: jnp.where(i < n_p, 0, 1 + (i - n_p) // tps),
        "rope_blk": lambda i: jnp.where(i < n_p, 0, 1 + (i - n_p) % tps),
    }

    n_cond = -(-(1 + bs) // 8) * 8
    cond = jnp.zeros((n_cond, d), F32).at[0].set(c_ctx).at[1:1 + bs].set(c)
    mod = _modulation(cond, w_mod, b_mod).reshape(depth, n_cond, 6, d)

    row = lambda v: v.reshape(depth, 1, -1)
    w_in_bf = jnp.concatenate([w_in[:, :, :_SPLIT_GATE], jnp.zeros((depth, d, N_GATE - _SPLIT_GATE), F32),
                               w_in[:, :, _SPLIT_GATE:]], axis=-1).astype(BF)
    wqup = jnp.pad(w_mla_q_up.reshape(depth, MLA_Q_RANK, MLA_HEADS, MLA_NOPE + MLA_ROPE),
                   ((0, 0), (0, 0), (0, 0), (0, LANES - MLA_NOPE - MLA_ROPE)))
    wkv = w_mla_kv_up.reshape(depth, MLA_KV_RANK, MLA_HEADS, MLA_NOPE + MLA_V)
    wk_pad = jnp.pad(wkv[..., :MLA_NOPE], ((0, 0), (0, 0), (0, 0), (0, LANES - MLA_NOPE)))
    wkvup = jnp.concatenate([wk_pad.reshape(depth, MLA_KV_RANK, MLA_HEADS * LANES),
                             wkv[..., MLA_NOPE:].reshape(depth, MLA_KV_RANK, MLA_HEADS * MLA_V)], axis=-1)
    bgu = b_gu.reshape(depth, N_EXPERTS, 1, D_FF, 2)
    p = {
        "norm1_g": row(norm1_g), "norm2_g": row(norm2_g),
        "w_in": w_in_bf,
        "rope": _rope_tables(ls), "place": _placement(), "ones_bd": _block_diag_ones(512, GQA_HEAD_DIM),
        "ltri": jnp.asarray(np.tril(np.ones((TM, TM), np.float32), -1), BF),
        "upper": jnp.asarray(np.triu(np.ones((LANES, LANES), np.float32), 1), F32),
        "qn_g": row(jnp.tile(gqa_qn_g, (1, GQA_HEADS))), "kn_g": row(jnp.tile(gqa_kn_g, (1, GQA_KV_HEADS))),
        "cqn_g": row(mla_qn_g), "kvn_g": row(mla_kvn_g),
        "wqup": wqup.reshape(depth, MLA_Q_RANK, MLA_HEADS * LANES).astype(BF), "wkvup": wkvup.astype(BF),
        "conv_dw": conv_dw, "conv_dw_b": row(conv_dw_b), "conv_ln_g": row(conv_ln_g), "conv_ln_b": row(conv_ln_b),
        "w_conv_out": w_conv_out.astype(BF), "pool_w": pool_w.astype(BF), "pool_scale": row(pool_scale),
        "w_pool_out": w_pool_out.astype(BF), "w_gqa_out": w_gqa_out.astype(BF), "w_mla_out": w_mla_out.astype(BF),
        "w_o": w_o.astype(BF),
        "w_router": _split_hi_lo(jnp.pad(w_router, ((0, 0), (0, 0), (0, LANES - N_EXPERTS)))),
        "b_router": row(jnp.pad(b_router, ((0, 0), (0, LANES - N_EXPERTS)))),
        "w_gu": w_gu, "b_gate": bgu[..., 0], "b_up": bgu[..., 1], "pair_perm": _pair_perm(),
        "w_dn": w_dn, "b_dn": b_dn.reshape(depth, N_EXPERTS, 1, d),
    }

    ckm, cvm = _cache_prep(cache_mla_ckv, jnp.pad(cache_mla_krope, ((0, 0), (0, 0), (0, 0), (0, LANES - MLA_ROPE))), p)
    cache = (cache_gqa_k.reshape(bs, depth, past, LANES).astype(BF),
             cache_gqa_v.reshape(bs, depth, past, LANES).astype(BF), ckm, cvm)

    x = jnp.concatenate([x_prompt.reshape(bp * seq, d), x_sample.reshape(bs * ls, d)], axis=0)
    n_ptok = bp * seq
    states = []
    for l in range(depth):
        a, b, q, k, v, qm, km, vm, ks, vs, ckvs, krs = _proj(l, x, mod, geo, p)
        states.append((ks[:n_ptok], vs[:n_ptok], ckvs[:n_ptok], krs[:n_ptok]))
        attn = _attention(l, geo, q, k, v, qm, km, vm, cache)
        x_mid, xs, slot_row, top_w, meta = _merge(l, x, mod, geo, a, b, attn, p)
        y = _experts(l, xs, _expert_plan(meta, nt), p, nt)
        combine = functools.partial(_combine, l, l == depth - 1, y, x_mid, mod, slot_row, top_w,
                                    final_g.reshape(1, d), geo)
        if l < depth - 1:
            x = combine(0, nt)

    y_prompt = combine(0, n_p).reshape(bp, seq, d)
    y_sample = combine(n_p, nt - n_p).reshape(bs, ls, d)
    st = lambda j, shape: jnp.stack([s[j].reshape(shape) for s in states], axis=1)
    return (y_prompt, y_sample,
            st(0, (bp, seq, GQA_KV_HEADS, GQA_HEAD_DIM)), st(1, (bp, seq, GQA_KV_HEADS, GQA_HEAD_DIM)),
            st(2, (bp, seq, MLA_KV_RANK)), st(3, (bp, seq, MLA_ROPE)))
```

```python
import functools

import jax
import jax.numpy as jnp
import numpy as np
from jax import lax
from jax.experimental import pallas as pl
from jax.experimental.pallas import tpu as pltpu

D_MODEL = 1024
GRID_W = 64
CONV_W = 512
CONV_K = 31
POOL_W = 512
POOL_WINDOWS = (2, 4, 8, 16)
POOL_G = 128
GQA_HEADS = 8
GQA_KV_HEADS = 2
GQA_HEAD_DIM = 64
MLA_HEADS = 8
MLA_Q_RANK = 384
MLA_KV_RANK = 256
MLA_NOPE = 64
MLA_ROPE = 32
MLA_V = 64
ROPE_BASE = 10000.0
N_EXPERTS = 32
TOP_K = 4
D_FF = 1024
SWIGLU_LIMIT = 7.0
SWIGLU_ALPHA = 1.702
EPS = 1e-6
GQA_SCALE = GQA_HEAD_DIM ** -0.5
MLA_SCALE = (MLA_NOPE + MLA_ROPE) ** -0.5
LOG2E = 1.4426950408889634

LANES = 128
SUBLANES = 8
TM = 256
HALO = 16
MOE_BM = 512
R_TILE = 1280
W1_COLS = 3072
N_GATE = 4 * D_MODEL
VMEM_LIMIT = 56 * 1024 * 1024

BF = jnp.bfloat16
F32 = jnp.float32

_C_A, _C_B, _C_Q, _C_K, _C_V, _C_CQ, _C_CKV, _C_KR = 0, 1024, 1536, 2048, 2176, 2304, 2688, 2944
_SPLIT_GATE = 2976


def _dot(a, b):
    return jnp.dot(a, b, preferred_element_type=F32)


def _dot_nt(a, b):
    return lax.dot_general(a, b, (((1,), (1,)), ((), ())), preferred_element_type=F32)


def _rms(x):
    return x * lax.rsqrt(jnp.mean(x * x, axis=-1, keepdims=True) + EPS)


def _group_mean_sq(x, ones_bd, width):
    xx = x * x
    hi = xx.astype(BF)
    lo = (xx - hi.astype(F32)).astype(BF)
    return (_dot(hi, ones_bd) + _dot(lo, ones_bd)) * (1.0 / width)


def _tile_lanes(t, width):
    reps = width // LANES
    return t if reps == 1 else jnp.concatenate([t] * reps, axis=-1)


def _rope(x, cos, sin_a, sin_b, shift):
    w = x.shape[-1]
    return (x * _tile_lanes(cos, w) + pltpu.roll(x, w - shift, 1) * _tile_lanes(sin_a, w)
            + pltpu.roll(x, shift, 1) * _tile_lanes(sin_b, w))


def _modulated_norm(x, g, shift, scale):
    return _rms(x) * g * (1.0 + scale) + shift


def _mod_kernel(cond_ref, w_ref, b_ref, o_ref):
    c = cond_ref[...]
    s = (c * jax.nn.sigmoid(c)).astype(BF)
    o_ref[...] = _dot(s, w_ref[...].astype(BF)) + b_ref[...]


def _modulation(cond, w_mod, b_mod):
    depth, d, n = w_mod.shape
    rows = cond.shape[0]
    return pl.pallas_call(
        _mod_kernel,
        grid=(depth, n // D_MODEL),
        in_specs=[
            pl.BlockSpec((rows, d), lambda l, j: (0, 0)),
            pl.BlockSpec((None, d, D_MODEL), lambda l, j: (l, 0, j)),
            pl.BlockSpec((None, 1, D_MODEL), lambda l, j: (l, 0, j)),
        ],
        out_specs=pl.BlockSpec((None, rows, D_MODEL), lambda l, j: (l, 0, j)),
        out_shape=jax.ShapeDtypeStruct((depth, rows, n), F32),
        name="modulation",
    )(cond, w_mod, b_mod.reshape(depth, 1, n))


def _proj_kernel(x_ref, mod_ref, g1_ref, w1_ref, rope_ref, qn_ref, kn_ref, cqn_ref, kvn_ref,
                 wqup_ref, wkvup_ref, place_ref, ones_ref,
                 a_ref, b_ref, q_ref, k_ref, v_ref, qm_ref, km_ref, vm_ref,
                 ks_ref, vs_ref, ckvs_ref, krs_ref):
    x = x_ref[...]
    h = _modulated_norm(x, g1_ref[...], mod_ref[0:1, :], mod_ref[1:2, :])
    y = _dot(h.astype(BF), w1_ref[...])

    a_ref[...] = (y[:, _C_A:_C_A + CONV_W] * jax.nn.sigmoid(y[:, _C_A + CONV_W:_C_B])).astype(BF)
    b_ref[...] = y[:, _C_B:_C_Q].astype(BF)

    def tab(j):
        return rope_ref[:, j * LANES:(j + 1) * LANES]

    q = y[:, _C_Q:_C_K]
    q = q * lax.rsqrt(_group_mean_sq(q, ones_ref[...], GQA_HEAD_DIM) + EPS) * qn_ref[...]
    q = _rope(q, tab(0), tab(1), tab(2), GQA_HEAD_DIM // 4)
    q_ref[...] = (q * (GQA_SCALE * LOG2E)).astype(BF)

    k = y[:, _C_K:_C_V]
    k = k * lax.rsqrt(_group_mean_sq(k, ones_ref[0:LANES, 0:LANES], GQA_HEAD_DIM) + EPS) * kn_ref[...]
    ks_ref[...] = k
    k_ref[...] = _rope(k, tab(0), tab(1), tab(2), GQA_HEAD_DIM // 4).astype(BF)

    v = y[:, _C_V:_C_CQ]
    vs_ref[...] = v
    v_ref[...] = v.astype(BF)

    cq = _rms(y[:, _C_CQ:_C_CKV]) * cqn_ref[...]
    qm = _dot(cq.astype(BF), wqup_ref[...])
    qm = _rope(qm, tab(3), tab(4), tab(5), MLA_ROPE // 4)
    qm_ref[...] = (qm * (MLA_SCALE * LOG2E)).astype(BF)

    ckv = _rms(y[:, _C_CKV:_C_KR]) * kvn_ref[...]
    ckvs_ref[...] = ckv
    kv = _dot(ckv.astype(BF), wkvup_ref[...])
    kr = y[:, _C_KR:W1_COLS]
    krs_ref[...] = kr[:, 0:MLA_ROPE]
    kr_rot = _rope(kr, tab(6), tab(7), tab(8), MLA_ROPE // 4)
    km = kv[:, 0:MLA_HEADS * LANES] + _dot(kr_rot.astype(BF), place_ref[...])
    km_ref[...] = km.astype(BF)
    vm_ref[...] = kv[:, MLA_HEADS * LANES:].astype(BF)


def _const_spec(shape):
    nd = len(shape)
    return pl.BlockSpec(shape, lambda *_: (0,) * nd)


def _layer_spec(shape, l):
    nd = len(shape)
    return pl.BlockSpec((None,) + shape, lambda *_: (l,) + (0,) * nd)


def _proj(l, x, mod, geo, p):
    nt = geo["nt"]
    tok = nt * TM
    seq_row, rope_blk = geo["seq_row"], geo["rope_blk"]

    def tile(width, dtype):
        return pl.BlockSpec((TM, width), lambda i: (i, 0)), jax.ShapeDtypeStruct((tok, width), dtype)

    outs = [tile(CONV_W, BF), tile(POOL_W, BF), tile(512, BF), tile(LANES, BF), tile(LANES, BF),
            tile(MLA_HEADS * LANES, BF), tile(MLA_HEADS * LANES, BF), tile(MLA_HEADS * MLA_V, BF),
            tile(LANES, F32), tile(LANES, F32), tile(MLA_KV_RANK, F32), tile(MLA_ROPE, F32)]
    return pl.pallas_call(
        _proj_kernel,
        grid=(nt,),
        in_specs=[
            pl.BlockSpec((TM, D_MODEL), lambda i: (i, 0)),
            pl.BlockSpec((None, None, 6, D_MODEL), lambda i: (l, seq_row(i), 0, 0)),
            _layer_spec((1, D_MODEL), l),
            _layer_spec((D_MODEL, W1_COLS), l),
            pl.BlockSpec((TM, 9 * LANES), lambda i: (rope_blk(i), 0)),
            _layer_spec((1, 512), l),
            _layer_spec((1, LANES), l),
            _layer_spec((1, MLA_Q_RANK), l),
            _layer_spec((1, MLA_KV_RANK), l),
            _layer_spec((MLA_Q_RANK, MLA_HEADS * LANES), l),
            _layer_spec((MLA_KV_RANK, MLA_HEADS * (LANES + MLA_V)), l),
            _const_spec((LANES, MLA_HEADS * LANES)),
            _const_spec((512, 512)),
        ],
        out_specs=[o[0] for o in outs],
        out_shape=[o[1] for o in outs],
        compiler_params=pltpu.CompilerParams(dimension_semantics=("arbitrary",), vmem_limit_bytes=VMEM_LIMIT),
        name="proj",
    )(x, mod, p["norm1_g"], p["w1"], p["rope"], p["qn_g"], p["kn_g"], p["cqn_g"], p["kvn_g"],
      p["wqup"], p["wkvup"], p["place"], p["ones_bd"])


def _cache_kernel(ckv_ref, kr_ref, wkvup_ref, place_ref, km_ref, vm_ref):
    kv = _dot(ckv_ref[...].astype(BF), wkvup_ref[...])
    km = kv[:, 0:MLA_HEADS * LANES] + _dot(kr_ref[...].astype(BF), place_ref[...])
    km_ref[...] = km.astype(BF)
    vm_ref[...] = kv[:, MLA_HEADS * LANES:].astype(BF)


def _cache_prep(ckv, kr_pad, p):
    bs, depth, past, _ = ckv.shape
    return pl.pallas_call(
        _cache_kernel,
        grid=(bs, depth),
        in_specs=[
            pl.BlockSpec((None, None, past, MLA_KV_RANK), lambda b, l: (b, l, 0, 0)),
            pl.BlockSpec((None, None, past, LANES), lambda b, l: (b, l, 0, 0)),
            pl.BlockSpec((None, MLA_KV_RANK, MLA_HEADS * (LANES + MLA_V)), lambda b, l: (l, 0, 0)),
            pl.BlockSpec((LANES, MLA_HEADS * LANES), lambda b, l: (0, 0)),
        ],
        out_specs=[
            pl.BlockSpec((None, None, past, MLA_HEADS * LANES), lambda b, l: (b, l, 0, 0)),
            pl.BlockSpec((None, None, past, MLA_HEADS * MLA_V), lambda b, l: (b, l, 0, 0)),
        ],
        out_shape=[jax.ShapeDtypeStruct((bs, depth, past, MLA_HEADS * LANES), BF),
                   jax.ShapeDtypeStruct((bs, depth, past, MLA_HEADS * MLA_V), BF)],
        name="cache_prep",
    )(ckv, kr_pad, p["wkvup"], p["place"])


def _attend(qs, k, v):
    s = _dot_nt(qs, k)
    m = jnp.max(s, axis=-1, keepdims=True)
    e = jnp.exp2(s - m)
    l = jnp.sum(e, axis=-1, keepdims=True)
    return _dot(e.astype(BF), v) / l


def _attention_body(q, k, v, qm, km, vm, oc_ref, od_ref):
    lane_k = lax.broadcasted_iota(jnp.int32, k.shape, 1)
    lane_q = lax.broadcasted_iota(jnp.int32, (TM, LANES), 1)
    lo_k = lane_k < GQA_HEAD_DIM
    lo_q = lane_q < GQA_HEAD_DIM
    k32, v32 = k.astype(F32), v.astype(F32)
    k_sw = pltpu.roll(k32, GQA_HEAD_DIM, 1)
    v_sw = pltpu.roll(v32, GQA_HEAD_DIM, 1)
    k_dup = [jnp.where(lo_k, k32, k_sw).astype(BF), jnp.where(lo_k, k_sw, k32).astype(BF)]
    v_dup = [jnp.where(lo_k, v32, v_sw).astype(BF), jnp.where(lo_k, v_sw, v32).astype(BF)]
    zero = jnp.zeros((TM, LANES), BF)
    group = GQA_HEADS // GQA_KV_HEADS
    for j in range(GQA_HEADS // 2):
        qs = q[:, j * LANES:(j + 1) * LANES]
        g = (2 * j) // group
        o_lo = _attend(jnp.where(lo_q, qs, zero), k_dup[g], v_dup[g])
        o_hi = _attend(jnp.where(lo_q, zero, qs), k_dup[g], v_dup[g])
        oc_ref[:, j * LANES:(j + 1) * LANES] = jnp.where(lo_q, o_lo, o_hi).astype(BF)
    for j in range(MLA_HEADS // 2):
        vs = vm[:, j * LANES:(j + 1) * LANES]
        outs = []
        for h in (2 * j, 2 * j + 1):
            outs.append(_attend(qm[:, h * LANES:(h + 1) * LANES], km[:, h * LANES:(h + 1) * LANES], vs))
        od_ref[:, j * LANES:(j + 1) * LANES] = jnp.where(lo_q, outs[0], outs[1]).astype(BF)


def _attn_prompt_kernel(q_ref, k_ref, v_ref, qm_ref, km_ref, vm_ref, oc_ref, od_ref):
    _attention_body(q_ref[...], k_ref[...], v_ref[...], qm_ref[...], km_ref[...], vm_ref[...], oc_ref, od_ref)


def _attn_sample_kernel(q_ref, k_ref, v_ref, qm_ref, km_ref, vm_ref, ck_ref, cv_ref, ckm_ref, cvm_ref,
                        oc_ref, od_ref):
    k = jnp.concatenate([ck_ref[...], k_ref[...]], axis=0)
    v = jnp.concatenate([cv_ref[...], v_ref[...]], axis=0)
    km = jnp.concatenate([ckm_ref[...], km_ref[...]], axis=0)
    vm = jnp.concatenate([cvm_ref[...], vm_ref[...]], axis=0)
    _attention_body(q_ref[...], k, v, qm_ref[...], km, vm, oc_ref, od_ref)


def _attention(l, geo, q, k, v, qm, km, vm, cache):
    n_p, bs, tps, nt = geo["np"], geo["bs"], geo["tps"], geo["nt"]
    tok = nt * TM
    ls = tps * TM
    off = (n_p * TM) // ls
    widths = (512, LANES, LANES, MLA_HEADS * LANES, MLA_HEADS * LANES, MLA_HEADS * MLA_V)
    params = pltpu.CompilerParams(dimension_semantics=("arbitrary",), vmem_limit_bytes=VMEM_LIMIT)
    oc_p, od_p = pl.pallas_call(
        _attn_prompt_kernel,
        grid=(n_p,),
        in_specs=[pl.BlockSpec((TM, w), lambda i: (i, 0)) for w in widths],
        out_specs=[pl.BlockSpec((TM, 512), lambda i: (i, 0))] * 2,
        out_shape=[jax.ShapeDtypeStruct((n_p * TM, 512), BF)] * 2,
        compiler_params=params,
        name="attn_prompt",
    )(q, k, v, qm, km, vm)
    ck, cv, ckm, cvm = cache
    past = ck.shape[2]
    q_spec = lambda w: pl.BlockSpec((TM, w), lambda b, j: (n_p + b * tps + j, 0))
    kv_spec = lambda w: pl.BlockSpec((ls, w), lambda b, j: (off + b, 0))
    c_spec = lambda w: pl.BlockSpec((None, None, past, w), lambda b, j: (b, l, 0, 0))
    params2 = pltpu.CompilerParams(dimension_semantics=("arbitrary", "arbitrary"), vmem_limit_bytes=VMEM_LIMIT)
    o_spec = pl.BlockSpec((TM, 512), lambda b, j: (b * tps + j, 0))
    oc_s, od_s = pl.pallas_call(
        _attn_sample_kernel,
        grid=(bs, tps),
        in_specs=[q_spec(512), kv_spec(LANES), kv_spec(LANES), q_spec(MLA_HEADS * LANES),
                  kv_spec(MLA_HEADS * LANES), kv_spec(MLA_HEADS * MLA_V),
                  c_spec(LANES), c_spec(LANES), c_spec(MLA_HEADS * LANES), c_spec(MLA_HEADS * MLA_V)],
        out_specs=[o_spec, o_spec],
        out_shape=[jax.ShapeDtypeStruct((bs * ls, 512), BF)] * 2,
        compiler_params=params2,
        name="attn_sample",
    )(q, k, v, qm, km, vm, ck, cv, ckm, cvm)
    return oc_p, od_p, oc_s, od_s


def _route(h2, wr_ref, br_ref, ltri_ref, upper_ref, xs_ref, rl_ref, tw_ref, meta_ref):
    h2_hi = h2.astype(BF)
    h2_lo = (h2 - h2_hi.astype(F32)).astype(BF)
    hi_terms = _dot(h2_hi, wr_ref[...])
    logits = (hi_terms[:, 0:LANES] + hi_terms[:, LANES:] + _dot(h2_lo, wr_ref[:, 0:LANES])) + br_ref[...]
    lane = lax.broadcasted_iota(jnp.int32, (TM, LANES), 1).astype(F32)
    neg = jnp.float32(-jnp.inf)
    lg = jnp.where(lane < N_EXPERTS, logits, neg)
    vals, idxs = [], []
    for _ in range(TOP_K):
        m = jnp.max(lg, axis=-1, keepdims=True)
        idx = jnp.min(jnp.where(lg == m, lane, float(LANES)), axis=-1, keepdims=True)
        vals.append(m)
        idxs.append(idx)
        lg = jnp.where(lane == idx, neg, lg)
    exps = [jnp.exp(v - vals[0]) for v in vals]
    denom = exps[0] + exps[1] + exps[2] + exps[3]

    onehot = [(lane == idxs[kk]).astype(F32) for kk in range(TOP_K)]
    colsum = [jnp.sum(o, axis=0, keepdims=True) for o in onehot]
    cnt = colsum[0] + colsum[1] + colsum[2] + colsum[3]
    pad8 = jnp.floor((cnt + 7.0) * 0.125) * 8.0
    run_off = jnp.dot(jnp.broadcast_to(pad8, (8, LANES)), upper_ref[...], preferred_element_type=F32,
                      precision=lax.Precision.HIGHEST)[0:1, :]
    base = jnp.zeros((1, LANES), F32)
    rloc = []
    for kk in range(TOP_K):
        before = _dot(ltri_ref[...], onehot[kk].astype(BF))
        rloc.append(jnp.sum(onehot[kk] * (run_off + base + before), axis=-1, keepdims=True))
        base = base + colsum[kk]

    eye = (lax.broadcasted_iota(jnp.int32, (TM, TM), 0) == lax.broadcasted_iota(jnp.int32, (TM, TM), 1))
    r_sub = lax.broadcasted_iota(jnp.int32, (R_TILE, TM), 0).astype(F32)
    ones8 = jnp.ones((8, TM), BF)
    sel = jnp.zeros((R_TILE, TM), F32)
    for kk in range(TOP_K):
        hi = jnp.floor(rloc[kk] * (1.0 / TM))
        lo = rloc[kk] - hi * TM
        row = (_dot(ones8, jnp.where(eye, hi, 0.0).astype(BF)) * TM
               + _dot(ones8, jnp.where(eye, lo, 0.0).astype(BF)))[0:1, :]
        sel = jnp.where(r_sub == row, 1.0, sel)
    xs_ref[...] = _dot(sel.astype(BF), h2.astype(BF))

    rl = jnp.zeros((TM, LANES), F32)
    tw = jnp.zeros((TM, LANES), F32)
    for kk in range(TOP_K):
        rl = jnp.where(lane == kk, rloc[kk], rl)
        tw = jnp.where(lane == kk, exps[kk] / denom, tw)
    rl_ref[...] = rl.astype(jnp.int32)
    tw_ref[...] = tw
    sub = lax.broadcasted_iota(jnp.int32, (8, LANES), 0)
    meta = jnp.where(sub == 0, jnp.broadcast_to(pad8, (8, LANES)),
                     jnp.where(sub == 1, jnp.broadcast_to(run_off, (8, LANES)), 0.0))
    meta_ref[...] = meta.astype(jnp.int32)


def _merge_kernel(geo, x_ref, mod_ref, a_ref, ap_ref, an_ref, b_ref, bp_ref, bn_ref,
                  ocp_ref, odp_ref, ocs_ref, ods_ref,
                  g1_ref, wg_ref, cw_ref, cb_ref, lng_ref, lnb_ref, wco_ref, pw_ref, ps_ref, wpo_ref,
                  wgo_ref, wmo_ref, wo_ref, g2_ref, wr_ref, br_ref,
                  ltri_ref, upper_ref,
                  xo_ref, xs_ref, rl_ref, tw_ref, meta_ref, abuf, bbuf, ashift):
    n_p, tps = geo["np"], geo["tps"]
    i = pl.program_id(0)
    j = jnp.where(i < n_p, 0, (i - n_p) % tps)
    n_seq_tiles = jnp.where(i < n_p, 1, tps)
    has_prev = j > 0
    has_next = j < n_seq_tiles - 1

    def fill(buf, cur, prev, nxt):
        buf[0:HALO, :] = jnp.where(has_prev, prev[...].astype(F32), 0.0)
        buf[HALO:HALO + TM, :] = cur[...].astype(F32)
        buf[HALO + TM:, :] = jnp.where(has_next, nxt[...].astype(F32), 0.0)

    fill(abuf, a_ref, ap_ref, an_ref)
    fill(bbuf, b_ref, bp_ref, bn_ref)

    sh_rows = TM + 2 * HALO - SUBLANES
    for sh in range(1, SUBLANES):
        ashift[sh - 1] = abuf[sh:sh + sh_rows, :]
    rows = 32
    conv = []
    for r0 in range(0, TM, rows):
        acc = jnp.zeros((rows, CONV_W), F32)
        for t in range(CONV_K):
            s = r0 + t + HALO - CONV_K // 2
            sh = s % SUBLANES
            tap = abuf[s:s + rows, :] if sh == 0 else ashift[sh - 1, s - sh:s - sh + rows, :]
            acc = acc + tap * cw_ref[t:t + 1, :]
        conv.append(acc)
    ca = jnp.concatenate(conv, axis=0) + cb_ref[...]
    mu = jnp.mean(ca, axis=-1, keepdims=True)
    xc = ca - mu
    ln = xc * lax.rsqrt(jnp.mean(xc * xc, axis=-1, keepdims=True) + EPS) * lng_ref[...] + lnb_ref[...]
    br_a = _dot((ln * jax.nn.sigmoid(ln)).astype(BF), wco_ref[...])

    pos = j * TM + lax.broadcasted_iota(jnp.int32, (TM, 1), 0)
    seq_len = n_seq_tiles * TM
    mixed = []
    for g, w in enumerate(POOL_WINDOWS):
        lo, hi = g * POOL_G, (g + 1) * POOL_G
        acc = jnp.zeros((TM, POOL_G), F32)
        for o in range(-(w // 2), w - w // 2):
            acc = acc + bbuf[HALO + o:HALO + o + TM, lo:hi]
        cnt = jnp.clip(pos - w // 2 + w, 0, seq_len) - jnp.clip(pos - w // 2, 0, seq_len)
        pooled = acc / cnt.astype(F32) - bbuf[HALO:HALO + TM, lo:hi]
        mixed.append(_dot(pooled.astype(BF), pw_ref[g]))
    pb = jnp.concatenate(mixed, axis=-1) * ps_ref[...]
    br_b = _dot(pb.astype(BF), wpo_ref[...])

    is_prompt = i < n_p
    br_c = _dot(jnp.where(is_prompt, ocp_ref[...], ocs_ref[...]), wgo_ref[...])
    br_d = _dot(jnp.where(is_prompt, odp_ref[...], ods_ref[...]), wmo_ref[...])

    x = x_ref[...]
    h = _modulated_norm(x, g1_ref[...], mod_ref[0:1, :], mod_ref[1:2, :])
    gates = jax.nn.sigmoid(_dot(h.astype(BF), wg_ref[...]).astype(BF))
    merged = (gates[:, 0:D_MODEL] * br_a.astype(BF) + gates[:, D_MODEL:2 * D_MODEL] * br_b.astype(BF)
              + gates[:, 2 * D_MODEL:3 * D_MODEL] * br_c.astype(BF) + gates[:, 3 * D_MODEL:] * br_d.astype(BF))
    x = x + mod_ref[2:3, :] * _dot(merged, wo_ref[...])
    xo_ref[...] = x

    h2 = _modulated_norm(x, g2_ref[...], mod_ref[3:4, :], mod_ref[4:5, :])
    _route(h2, wr_ref, br_ref, ltri_ref, upper_ref, xs_ref, rl_ref, tw_ref, meta_ref)


def _merge(l, x, mod, geo, a, b, attn, p):
    nt, n_p = geo["nt"], geo["np"]
    tok = nt * TM
    seq_row = geo["seq_row"]
    hb = TM // HALO
    last_hb = nt * hb - 1
    tile = lambda w: pl.BlockSpec((TM, w), lambda i: (i, 0))
    prev = lambda w: pl.BlockSpec((HALO, w), lambda i: (jnp.maximum(i * hb - 1, 0), 0))
    nxt = lambda w: pl.BlockSpec((HALO, w), lambda i: (jnp.minimum((i + 1) * hb, last_hb), 0))
    p_tile = pl.BlockSpec((TM, 512), lambda i: (jnp.minimum(i, n_p - 1), 0))
    s_tile = pl.BlockSpec((TM, 512), lambda i: (jnp.maximum(i - n_p, 0), 0))
    return pl.pallas_call(
        functools.partial(_merge_kernel, geo),
        grid=(nt,),
        in_specs=[
            tile(D_MODEL),
            pl.BlockSpec((None, None, 6, D_MODEL), lambda i: (l, seq_row(i), 0, 0)),
            tile(CONV_W), prev(CONV_W), nxt(CONV_W),
            tile(POOL_W), prev(POOL_W), nxt(POOL_W),
            p_tile, p_tile, s_tile, s_tile,
            _layer_spec((1, D_MODEL), l),
            _layer_spec((D_MODEL, N_GATE), l),
            _layer_spec((CONV_K, CONV_W), l),
            _layer_spec((1, CONV_W), l),
            _layer_spec((1, CONV_W), l),
            _layer_spec((1, CONV_W), l),
            _layer_spec((CONV_W, D_MODEL), l),
            _layer_spec((len(POOL_WINDOWS), POOL_G, POOL_G), l),
            _layer_spec((1, POOL_W), l),
            _layer_spec((POOL_W, D_MODEL), l),
            _layer_spec((512, D_MODEL), l),
            _layer_spec((512, D_MODEL), l),
            _layer_spec((D_MODEL, D_MODEL), l),
            _layer_spec((1, D_MODEL), l),
            _layer_spec((D_MODEL, 2 * LANES), l),
            _layer_spec((1, LANES), l),
            _const_spec((TM, TM)),
            _const_spec((LANES, LANES)),
        ],
        out_specs=[tile(D_MODEL), pl.BlockSpec((R_TILE, D_MODEL), lambda i: (i, 0)), tile(LANES), tile(LANES),
                   pl.BlockSpec((None, 8, LANES), lambda i: (i, 0, 0))],
        out_shape=[jax.ShapeDtypeStruct((tok, D_MODEL), F32), jax.ShapeDtypeStruct((nt * R_TILE, D_MODEL), F32),
                   jax.ShapeDtypeStruct((tok, LANES), jnp.int32), jax.ShapeDtypeStruct((tok, LANES), F32),
                   jax.ShapeDtypeStruct((nt, 8, LANES), jnp.int32)],
        scratch_shapes=[pltpu.VMEM((TM + 2 * HALO, CONV_W), F32), pltpu.VMEM((TM + 2 * HALO, POOL_W), F32),
                        pltpu.VMEM((SUBLANES - 1, TM + 2 * HALO - SUBLANES, CONV_W), F32)],
        compiler_params=pltpu.CompilerParams(dimension_semantics=("arbitrary",), vmem_limit_bytes=VMEM_LIMIT),
        name="merge",
    )(x, mod, a, a, a, b, b, b, *attn,
      p["norm1_g"], p["wgate"], p["conv_dw"], p["conv_dw_b"], p["conv_ln_g"], p["conv_ln_b"], p["w_conv_out"],
      p["pool_w"], p["pool_scale"], p["w_pool_out"], p["w_gqa_out"], p["w_mla_out"], p["w_o"],
      p["norm2_g"], p["w_router"], p["b_router"], p["ltri"], p["upper"])


def _expert_kernel(nt, layer, blk_e, blk_row0, blk_t0, blk_t1, n_used_ref, pad8_ref, dst_ref, off_ref, tot8_ref,
                   next_e, xs_hbm, wgu_hbm, bg_ref, bu_ref, wd_hbm, bd_ref, perm_ref,
                   y_hbm, xg, yb, zbuf, wgu_buf, wd_buf, wg_bf, wu_bf, wd_bf, cnt_smem, gsem, osem, zsem, wsem):
    i = pl.program_id(0)
    n_used = n_used_ref[0]
    slot = i % 2
    last = pl.num_programs(0) - 1

    def for_pieces(b, fn):
        e = blk_e[b]
        b0 = blk_row0[b]

        def body(t, tot):
            j = t * N_EXPERTS + e
            run0 = dst_ref[j]
            lo = jnp.maximum(run0, b0)
            hi = jnp.minimum(run0 + pad8_ref[j], b0 + MOE_BM)
            n = pl.multiple_of(jnp.maximum(hi - lo, 0), 8)

            @pl.when(n > 0)
            def _():
                fn(pl.multiple_of(t * R_TILE + off_ref[j] + lo - run0, 8), pl.multiple_of(lo - b0, 8), n)

            return tot + n

        return lax.fori_loop(blk_t0[b], blk_t1[b], body, jnp.int32(0))

    def rows_copy(src, dst, sem, n):
        return pltpu.make_async_copy(src.at[pl.ds(0, n), :], dst.at[pl.ds(0, n), :], sem)

    def start_gather(b, s):
        def piece(row_t, row_b, n):
            pltpu.make_async_copy(xs_hbm.at[pl.ds(row_t, n), :], xg.at[s].at[pl.ds(row_b, n), :], gsem.at[s]).start()

        cnt_smem[s] = for_pieces(b, piece)

    def tail_copy(t):
        n = pl.multiple_of(R_TILE - tot8_ref[t], 8)
        return n, pltpu.make_async_copy(
            zbuf.at[pl.ds(0, n), :], y_hbm.at[pl.ds(pl.multiple_of(t * R_TILE + tot8_ref[t], 8), n), :], zsem)

    @pl.when(i == 0)
    def _():
        xg[...] = jnp.zeros_like(xg)
        zbuf[...] = jnp.zeros_like(zbuf)
        for s in range(4):
            cnt_smem[s] = 0

        def fill(t, c):
            n, cp = tail_copy(t)

            @pl.when(n > 0)
            def _():
                cp.start()

            return c

        lax.fori_loop(0, nt, fill, 0)

        @pl.when(n_used > 0)
        def _():
            start_gather(0, 0)

    @pl.when(i + 1 < n_used)
    def _():
        start_gather(i + 1, 1 - slot)

    @pl.when(i < n_used)
    def _():
        e_changed = jnp.logical_or(i == 0, blk_e[i] != blk_e[jnp.maximum(i - 1, 0)])

        def weight_copies(e):
            return (pltpu.make_async_copy(wgu_hbm.at[layer, e], wgu_buf, wsem.at[0]),
                    pltpu.make_async_copy(wd_hbm.at[layer, e], wd_buf, wsem.at[1]))

        @pl.when(e_changed)
        def _():
            e = blk_e[i]

            @pl.when(i == 0)
            def _():
                for cp in weight_copies(e):
                    cp.start()

            for cp in weight_copies(e):
                cp.wait()
            wd_bf[...] = wd_buf[...].astype(BF)
            for c in range(D_FF // LANES):
                pair = _dot(wgu_buf[:, 2 * c * LANES:2 * (c + 1) * LANES].astype(BF), perm_ref[...])
                wg_bf[:, c * LANES:(c + 1) * LANES] = pair[:, 0:LANES].astype(BF)
                wu_bf[:, c * LANES:(c + 1) * LANES] = pair[:, LANES:].astype(BF)
            e_next = next_e[e]

            @pl.when(e_next >= 0)
            def _():
                for cp in weight_copies(e_next):
                    cp.start()

        n_prev = pl.multiple_of(cnt_smem[2 + slot], 8)

        @pl.when(n_prev > 0)
        def _():
            rows_copy(yb.at[slot], y_hbm, osem.at[slot], n_prev).wait()

        n_in = pl.multiple_of(cnt_smem[slot], 8)
        rows_copy(xs_hbm, xg.at[slot], gsem.at[slot], n_in).wait()

        def expert_mlp(rows):
            xb = xg[slot, 0:rows, :].astype(BF)
            gate = jnp.minimum(_dot(xb, wg_bf[...]) + bg_ref[...], SWIGLU_LIMIT)
            up = jnp.clip(_dot(xb, wu_bf[...]) + bu_ref[...], -SWIGLU_LIMIT, SWIGLU_LIMIT)
            glu = gate * jax.nn.sigmoid(gate * SWIGLU_ALPHA)
            y = _dot(((up + 1.0) * glu).astype(BF), wd_bf[...]) + bd_ref[...]
            yb[slot, 0:rows, :] = y.astype(BF).astype(F32)

        @pl.when(n_in > MOE_BM // 2)
        def _():
            expert_mlp(MOE_BM)

        @pl.when(n_in <= MOE_BM // 2)
        def _():
            expert_mlp(MOE_BM // 2)

        def piece(row_t, row_b, n):
            pltpu.make_async_copy(yb.at[slot].at[pl.ds(row_b, n), :], y_hbm.at[pl.ds(row_t, n), :],
                                  osem.at[slot]).start()

        cnt_smem[2 + slot] = for_pieces(i, piece)

    @pl.when(i == last)
    def _():
        for s in range(2):
            n_out = pl.multiple_of(cnt_smem[2 + s], 8)

            @pl.when(n_out > 0)
            def _():
                rows_copy(yb.at[s], y_hbm, osem.at[s], n_out).wait()

        def drain(t, c):
            n, cp = tail_copy(t)

            @pl.when(n > 0)
            def _():
                cp.wait()

            return c

        lax.fori_loop(0, nt, drain, 0)


def _experts(l, xs, plan, p, nt):
    n_blocks = plan["blk_e"].shape[0]
    n_pref = 10
    by_expert = lambda *lead: (lambda i, be, *_: lead + (be[i], 0, 0))
    b_spec = pl.BlockSpec((None, None, 1, D_FF), by_expert(l))
    grid_spec = pltpu.PrefetchScalarGridSpec(
        num_scalar_prefetch=n_pref,
        grid=(n_blocks,),
        in_specs=[
            pl.BlockSpec(memory_space=pl.ANY),
            pl.BlockSpec(memory_space=pl.ANY),
            b_spec, b_spec,
            pl.BlockSpec(memory_space=pl.ANY),
            pl.BlockSpec((None, None, 1, D_MODEL), by_expert(l)),
            pl.BlockSpec((2 * LANES, 2 * LANES), lambda i, *_: (0, 0)),
        ],
        out_specs=pl.BlockSpec(memory_space=pl.ANY),
        scratch_shapes=[
            pltpu.VMEM((2, MOE_BM, D_MODEL), F32),
            pltpu.VMEM((2, MOE_BM, D_MODEL), F32),
            pltpu.VMEM((R_TILE - TOP_K * TM, D_MODEL), F32),
            pltpu.VMEM((D_MODEL, 2 * D_FF), F32),
            pltpu.VMEM((D_FF, D_MODEL), F32),
            pltpu.VMEM((D_MODEL, D_FF), BF),
            pltpu.VMEM((D_MODEL, D_FF), BF),
            pltpu.VMEM((D_FF, D_MODEL), BF),
            pltpu.SMEM((4,), jnp.int32),
            pltpu.SemaphoreType.DMA((2,)),
            pltpu.SemaphoreType.DMA((2,)),
            pltpu.SemaphoreType.DMA,
            pltpu.SemaphoreType.DMA((2,)),
        ],
    )
    return pl.pallas_call(
        functools.partial(_expert_kernel, nt, l),
        grid_spec=grid_spec,
        out_shape=jax.ShapeDtypeStruct((nt * R_TILE, D_MODEL), F32),
        compiler_params=pltpu.CompilerParams(dimension_semantics=("arbitrary",), vmem_limit_bytes=VMEM_LIMIT),
        name="experts",
    )(plan["blk_e"], plan["blk_row0"], plan["blk_t0"], plan["blk_t1"], plan["n_used"], plan["pad8"], plan["dst"],
      plan["off"], plan["tot8"], plan["next_e"],
      xs, p["w_gu"], p["b_gate"], p["b_up"], p["w_dn"], p["b_dn"], p["pair_perm"])


def _pair_perm():
    m = np.zeros((2 * LANES, 2 * LANES), np.float32)
    j = np.arange(LANES)
    m[2 * j, j] = 1.0
    m[2 * j + 1, LANES + j] = 1.0
    return jnp.asarray(m, BF)


def _combine_kernel(final, y_ref, x_ref, mod_ref, rl_ref, tw_ref, fg_ref, o_ref):
    r_lane = lax.broadcasted_iota(jnp.int32, (TM, R_TILE), 1)
    rl = rl_ref[...]
    tw = tw_ref[...]
    sel = jnp.zeros((TM, R_TILE), F32)
    for k in range(TOP_K):
        sel = jnp.where(r_lane == rl[:, k:k + 1], tw[:, k:k + 1], sel)
    ffn = _dot(sel.astype(BF), y_ref[...].astype(BF))
    x = x_ref[...] + mod_ref[5:6, :] * ffn
    if final:
        x = _rms(x) * fg_ref[...]
    o_ref[...] = x


def _combine(l, final, y, x, mod, rl, tw, final_g, geo, tile0, n_tiles):
    seq_row = geo["seq_row"]
    tile = lambda w: pl.BlockSpec((TM, w), lambda i: (i + tile0, 0))
    return pl.pallas_call(
        functools.partial(_combine_kernel, final),
        grid=(n_tiles,),
        in_specs=[
            pl.BlockSpec((R_TILE, D_MODEL), lambda i: (i + tile0, 0)),
            tile(D_MODEL),
            pl.BlockSpec((None, None, 6, D_MODEL), lambda i: (l, seq_row(i + tile0), 0, 0)),
            tile(LANES),
            tile(LANES),
            _const_spec((1, D_MODEL)),
        ],
        out_specs=pl.BlockSpec((TM, D_MODEL), lambda i: (i, 0)),
        out_shape=jax.ShapeDtypeStruct((n_tiles * TM, D_MODEL), F32),
        compiler_params=pltpu.CompilerParams(dimension_semantics=("arbitrary",), vmem_limit_bytes=VMEM_LIMIT),
        name="combine",
    )(y, x, mod, rl, tw, final_g)


def _expert_plan(meta, nt):
    pad8 = meta[:, 0, :N_EXPERTS]
    off = meta[:, 1, :N_EXPERTS]
    ends = jnp.cumsum(pad8, axis=0)
    dst = ends - pad8
    tot = ends[-1]
    nb = (tot + MOE_BM - 1) // MOE_BM
    nb_end = jnp.cumsum(nb)
    n_blocks = (nt * TM * TOP_K + nt * N_EXPERTS * 7) // MOE_BM + N_EXPERTS
    b = jnp.arange(n_blocks, dtype=jnp.int32)
    blk_e = jnp.minimum(jnp.sum(nb_end[None, :] <= b[:, None], axis=1), N_EXPERTS - 1).astype(jnp.int32)
    blk_row0 = (b - (nb_end - nb)[blk_e]) * MOE_BM
    ends_b = ends[:, blk_e]
    dst_b = dst[:, blk_e]
    blk_t0 = jnp.sum(ends_b <= blk_row0[None, :], axis=0)
    blk_t1 = jnp.sum(dst_b < blk_row0[None, :] + MOE_BM, axis=0)
    i32 = lambda v: v.astype(jnp.int32)
    ids = jnp.arange(N_EXPERTS, dtype=jnp.int32)
    later = jnp.where((nb > 0)[None, :] & (ids[None, :] > ids[:, None]), ids[None, :], N_EXPERTS)
    next_e = jnp.min(later, axis=1)
    next_e = jnp.where(next_e == N_EXPERTS, -1, next_e)
    return {"next_e": i32(next_e), "blk_e": blk_e, "blk_row0": i32(blk_row0), "blk_t0": i32(blk_t0), "blk_t1": i32(blk_t1),
            "n_used": i32(nb_end[-1]).reshape(1), "pad8": i32(pad8.reshape(-1)), "dst": i32(dst.reshape(-1)),
            "off": i32(off.reshape(-1)), "tot8": i32(jnp.sum(pad8, axis=1))}


def _rope_tables(n_pos):
    pos = np.arange(n_pos)
    row, col = pos // GRID_W, pos % GRID_W
    lane = np.arange(LANES)

    def build(active, r, half):
        n_rot = 4 * half
        is_col = (r % n_rot) >= 2 * half
        rr = r % (2 * half)
        freq = np.power(ROPE_BASE, -(rr % half).astype(np.float64) / half)
        p = np.where(is_col[None, :], col[:, None], row[:, None]).astype(np.float64)
        ang = p * freq[None, :]
        first = rr < half
        cos = np.where(active[None, :], np.cos(ang), 1.0)
        sin_a = np.where((active & first)[None, :], -np.sin(ang), 0.0)
        sin_b = np.where((active & ~first)[None, :], np.sin(ang), 0.0)
        return [cos, sin_a, sin_b]

    tabs = build(np.ones(LANES, bool), lane % GQA_HEAD_DIM, GQA_HEAD_DIM // 4)
    in_rope = (lane >= MLA_NOPE) & (lane < MLA_NOPE + MLA_ROPE)
    tabs += build(in_rope, np.maximum(lane - MLA_NOPE, 0) % MLA_ROPE, MLA_ROPE // 4)
    tabs += build(lane < MLA_ROPE, lane % MLA_ROPE, MLA_ROPE // 4)
    table = np.concatenate(tabs, axis=1)
    ident = np.concatenate([np.ones((TM, LANES)), np.zeros((TM, LANES)), np.zeros((TM, LANES))] * 3, axis=1)
    return jnp.asarray(np.concatenate([ident, table], axis=0), F32)


def _placement():
    e = np.zeros((LANES, MLA_HEADS * LANES), np.float32)
    for h in range(MLA_HEADS):
        for r in range(MLA_ROPE):
            e[r, h * LANES + MLA_NOPE + r] = 1.0
    return jnp.asarray(e, BF)


def _split_hi_lo(w):
    hi = w.astype(BF)
    lo = (w - hi.astype(F32)).astype(BF)
    return jnp.concatenate([hi, lo], axis=-1)


def _block_diag_ones(n, g):
    idx = np.arange(n) // g
    return jnp.asarray((idx[:, None] == idx[None, :]).astype(np.float32), BF)


def kernel(x_prompt, x_sample, cache_gqa_k, cache_gqa_v, cache_mla_ckv, cache_mla_krope, c, c_ctx, norm1_g, norm2_g, w_mod, b_mod, w_in, conv_dw, conv_dw_b, conv_ln_g, conv_ln_b, w_conv_out, pool_w, pool_scale, w_pool_out, gqa_qn_g, gqa_kn_g, w_gqa_out, mla_qn_g, w_mla_q_up, mla_kvn_g, w_mla_kv_up, w_mla_out, w_o, w_router, b_router, w_gu, b_gu, w_dn, b_dn, final_g):
    bp, seq, d = x_prompt.shape
    bs, ls, _ = x_sample.shape
    depth = w_in.shape[0]
    past = cache_gqa_k.shape[2]
    assert seq == TM and d == D_MODEL and ls % TM == 0 and (bp * seq) % ls == 0
    tps = ls // TM
    n_p = bp
    nt = n_p + bs * tps
    geo = {
        "np": n_p, "bs": bs, "tps": tps, "nt": nt,
        "seq_row": lambda i: jnp.where(i < n_p, 0, 1 + (i - n_p) // tps),
        "rope_blk": lambda i: jnp.where(i < n_p, 0, 1 + (i - n_p) % tps),
    }

    n_cond = -(-(1 + bs) // 8) * 8
    cond = jnp.zeros((n_cond, d), F32).at[0].set(c_ctx).at[1:1 + bs].set(c)
    mod = _modulation(cond, w_mod, b_mod).reshape(depth, n_cond, 6, d)

    row = lambda v: v.reshape(depth, 1, -1)
    w1 = jnp.pad(w_in[:, :, :_SPLIT_GATE], ((0, 0), (0, 0), (0, W1_COLS - _SPLIT_GATE))).astype(BF)
    wqup = jnp.pad(w_mla_q_up.reshape(depth, MLA_Q_RANK, MLA_HEADS, MLA_NOPE + MLA_ROPE),
                   ((0, 0), (0, 0), (0, 0), (0, LANES - MLA_NOPE - MLA_ROPE)))
    wkv = w_mla_kv_up.reshape(depth, MLA_KV_RANK, MLA_HEADS, MLA_NOPE + MLA_V)
    wk_pad = jnp.pad(wkv[..., :MLA_NOPE], ((0, 0), (0, 0), (0, 0), (0, LANES - MLA_NOPE)))
    wkvup = jnp.concatenate([wk_pad.reshape(depth, MLA_KV_RANK, MLA_HEADS * LANES),
                             wkv[..., MLA_NOPE:].reshape(depth, MLA_KV_RANK, MLA_HEADS * MLA_V)], axis=-1)
    bgu = b_gu.reshape(depth, N_EXPERTS, 1, D_FF, 2)
    p = {
        "norm1_g": row(norm1_g), "norm2_g": row(norm2_g),
        "w1": w1, "wgate": w_in[:, :, _SPLIT_GATE:].astype(BF),
        "rope": _rope_tables(ls), "place": _placement(), "ones_bd": _block_diag_ones(512, GQA_HEAD_DIM),
        "ltri": jnp.asarray(np.tril(np.ones((TM, TM), np.float32), -1), BF),
        "upper": jnp.asarray(np.triu(np.ones((LANES, LANES), np.float32), 1), F32),
        "qn_g": row(jnp.tile(gqa_qn_g, (1, GQA_HEADS))), "kn_g": row(jnp.tile(gqa_kn_g, (1, GQA_KV_HEADS))),
        "cqn_g": row(mla_qn_g), "kvn_g": row(mla_kvn_g),
        "wqup": wqup.reshape(depth, MLA_Q_RANK, MLA_HEADS * LANES).astype(BF), "wkvup": wkvup.astype(BF),
        "conv_dw": conv_dw, "conv_dw_b": row(conv_dw_b), "conv_ln_g": row(conv_ln_g), "conv_ln_b": row(conv_ln_b),
        "w_conv_out": w_conv_out.astype(BF), "pool_w": pool_w.astype(BF), "pool_scale": row(pool_scale),
        "w_pool_out": w_pool_out.astype(BF), "w_gqa_out": w_gqa_out.astype(BF), "w_mla_out": w_mla_out.astype(BF),
        "w_o": w_o.astype(BF),
        "w_router": _split_hi_lo(jnp.pad(w_router, ((0, 0), (0, 0), (0, LANES - N_EXPERTS)))),
        "b_router": row(jnp.pad(b_router, ((0, 0), (0, LANES - N_EXPERTS)))),
        "w_gu": w_gu, "b_gate": bgu[..., 0], "b_up": bgu[..., 1], "pair_perm": _pair_perm(),
        "w_dn": w_dn, "b_dn": b_dn.reshape(depth, N_EXPERTS, 1, d),
    }

    ckm, cvm = _cache_prep(cache_mla_ckv, jnp.pad(cache_mla_krope, ((0, 0), (0, 0), (0, 0), (0, LANES - MLA_ROPE))), p)
    cache = (cache_gqa_k.reshape(bs, depth, past, LANES).astype(BF),
             cache_gqa_v.reshape(bs, depth, past, LANES).astype(BF), ckm, cvm)

    x = jnp.concatenate([x_prompt.reshape(bp * seq, d), x_sample.reshape(bs * ls, d)], axis=0)
    n_ptok = bp * seq
    states = []
    for l in range(depth):
        a, b, q, k, v, qm, km, vm, ks, vs, ckvs, krs = _proj(l, x, mod, geo, p)
        states.append((ks[:n_ptok], vs[:n_ptok], ckvs[:n_ptok], krs[:n_ptok]))
        attn = _attention(l, geo, q, k, v, qm, km, vm, cache)
        x_mid, xs, slot_row, top_w, meta = _merge(l, x, mod, geo, a, b, attn, p)
        y = _experts(l, xs, _expert_plan(meta, nt), p, nt)
        combine = functools.partial(_combine, l, l == depth - 1, y, x_mid, mod, slot_row, top_w,
                                    final_g.reshape(1, d), geo)
        if l < depth - 1:
            x = combine(0, nt)

    y_prompt = combine(0, n_p).reshape(bp, seq, d)
    y_sample = combine(n_p, nt - n_p).reshape(bs, ls, d)
    st = lambda j, shape: jnp.stack([s[j].reshape(shape) for s in states], axis=1)
    return (y_prompt, y_sample,
            st(0, (bp, seq, GQA_KV_HEADS, GQA_HEAD_DIM)), st(1, (bp, seq, GQA_KV_HEADS, GQA_HEAD_DIM)),
            st(2, (bp, seq, MLA_KV_RANK)), st(3, (bp, seq, MLA_ROPE)))
```

```python
import functools

import jax
import jax.numpy as jnp
import numpy as np
from jax import lax
from jax.experimental import pallas as pl
from jax.experimental.pallas import tpu as pltpu

D_MODEL = 1024
GRID_W = 64
CONV_W = 512
CONV_K = 31
POOL_W = 512
POOL_WINDOWS = (2, 4, 8, 16)
POOL_G = 128
GQA_HEADS = 8
GQA_KV_HEADS = 2
GQA_HEAD_DIM = 64
MLA_HEADS = 8
MLA_Q_RANK = 384
MLA_KV_RANK = 256
MLA_NOPE = 64
MLA_ROPE = 32
MLA_V = 64
ROPE_BASE = 10000.0
N_EXPERTS = 32
TOP_K = 4
D_FF = 1024
SWIGLU_LIMIT = 7.0
SWIGLU_ALPHA = 1.702
EPS = 1e-6
GQA_SCALE = GQA_HEAD_DIM ** -0.5
MLA_SCALE = (MLA_NOPE + MLA_ROPE) ** -0.5
LOG2E = 1.4426950408889634

LANES = 128
SUBLANES = 8
TM = 256
HALO = 16
MOE_BM = 512
R_TILE = 1280
W1_COLS = 3072
N_GATE = 4 * D_MODEL
VMEM_LIMIT = 56 * 1024 * 1024

BF = jnp.bfloat16
F32 = jnp.float32

_C_A, _C_B, _C_Q, _C_K, _C_V, _C_CQ, _C_CKV, _C_KR = 0, 1024, 1536, 2048, 2176, 2304, 2688, 2944
_SPLIT_GATE = 2976


def _dot(a, b):
    return jnp.dot(a, b, preferred_element_type=F32)


def _dot_nt(a, b):
    return lax.dot_general(a, b, (((1,), (1,)), ((), ())), preferred_element_type=F32)


def _rms(x):
    return x * lax.rsqrt(jnp.mean(x * x, axis=-1, keepdims=True) + EPS)


def _group_mean_sq(x, ones_bd, width):
    xx = x * x
    hi = xx.astype(BF)
    lo = (xx - hi.astype(F32)).astype(BF)
    return (_dot(hi, ones_bd) + _dot(lo, ones_bd)) * (1.0 / width)


def _tile_lanes(t, width):
    reps = width // LANES
    return t if reps == 1 else jnp.concatenate([t] * reps, axis=-1)


def _rope(x, cos, sin_a, sin_b, shift):
    w = x.shape[-1]
    return (x * _tile_lanes(cos, w) + pltpu.roll(x, w - shift, 1) * _tile_lanes(sin_a, w)
            + pltpu.roll(x, shift, 1) * _tile_lanes(sin_b, w))


def _modulated_norm(x, g, shift, scale):
    return _rms(x) * g * (1.0 + scale) + shift


def _mod_kernel(cond_ref, w_ref, b_ref, o_ref):
    c = cond_ref[...]
    s = (c * jax.nn.sigmoid(c)).astype(BF)
    o_ref[...] = _dot(s, w_ref[...].astype(BF)) + b_ref[...]


def _modulation(cond, w_mod, b_mod):
    depth, d, n = w_mod.shape
    rows = cond.shape[0]
    return pl.pallas_call(
        _mod_kernel,
        grid=(depth, n // D_MODEL),
        in_specs=[
            pl.BlockSpec((rows, d), lambda l, j: (0, 0)),
            pl.BlockSpec((None, d, D_MODEL), lambda l, j: (l, 0, j)),
            pl.BlockSpec((None, 1, D_MODEL), lambda l, j: (l, 0, j)),
        ],
        out_specs=pl.BlockSpec((None, rows, D_MODEL), lambda l, j: (l, 0, j)),
        out_shape=jax.ShapeDtypeStruct((depth, rows, n), F32),
        name="modulation",
    )(cond, w_mod, b_mod.reshape(depth, 1, n))


def _proj_kernel(x_ref, mod_ref, g1_ref, w1_ref, rope_ref, qn_ref, kn_ref, cqn_ref, kvn_ref,
                 wqup_ref, wkvup_ref, place_ref, ones_ref,
                 a_ref, b_ref, q_ref, k_ref, v_ref, qm_ref, km_ref, vm_ref,
                 ks_ref, vs_ref, ckvs_ref, krs_ref):
    x = x_ref[...]
    h = _modulated_norm(x, g1_ref[...], mod_ref[0:1, :], mod_ref[1:2, :])
    y = _dot(h.astype(BF), w1_ref[...])

    a_ref[...] = (y[:, _C_A:_C_A + CONV_W] * jax.nn.sigmoid(y[:, _C_A + CONV_W:_C_B])).astype(BF)
    b_ref[...] = y[:, _C_B:_C_Q].astype(BF)

    def tab(j):
        return rope_ref[:, j * LANES:(j + 1) * LANES]

    q = y[:, _C_Q:_C_K]
    q = q * lax.rsqrt(_group_mean_sq(q, ones_ref[...], GQA_HEAD_DIM) + EPS) * qn_ref[...]
    q = _rope(q, tab(0), tab(1), tab(2), GQA_HEAD_DIM // 4)
    q_ref[...] = (q * (GQA_SCALE * LOG2E)).astype(BF)

    k = y[:, _C_K:_C_V]
    k = k * lax.rsqrt(_group_mean_sq(k, ones_ref[0:LANES, 0:LANES], GQA_HEAD_DIM) + EPS) * kn_ref[...]
    ks_ref[...] = k
    k_ref[...] = _rope(k, tab(0), tab(1), tab(2), GQA_HEAD_DIM // 4).astype(BF)

    v = y[:, _C_V:_C_CQ]
    vs_ref[...] = v
    v_ref[...] = v.astype(BF)

    cq = _rms(y[:, _C_CQ:_C_CKV]) * cqn_ref[...]
    qm = _dot(cq.astype(BF), wqup_ref[...])
    qm = _rope(qm, tab(3), tab(4), tab(5), MLA_ROPE // 4)
    qm_ref[...] = (qm * (MLA_SCALE * LOG2E)).astype(BF)

    ckv = _rms(y[:, _C_CKV:_C_KR]) * kvn_ref[...]
    ckvs_ref[...] = ckv
    kv = _dot(ckv.astype(BF), wkvup_ref[...])
    kr = y[:, _C_KR:W1_COLS]
    krs_ref[...] = kr[:, 0:MLA_ROPE]
    kr_rot = _rope(kr, tab(6), tab(7), tab(8), MLA_ROPE // 4)
    km = kv[:, 0:MLA_HEADS * LANES] + _dot(kr_rot.astype(BF), place_ref[...])
    km_ref[...] = km.astype(BF)
    vm_ref[...] = kv[:, MLA_HEADS * LANES:].astype(BF)


def _const_spec(shape):
    nd = len(shape)
    return pl.BlockSpec(shape, lambda *_: (0,) * nd)


def _layer_spec(shape, l):
    nd = len(shape)
    return pl.BlockSpec((None,) + shape, lambda *_: (l,) + (0,) * nd)


def _proj(l, x, mod, geo, p):
    nt = geo["nt"]
    tok = nt * TM
    seq_row, rope_blk = geo["seq_row"], geo["rope_blk"]

    def tile(width, dtype):
        return pl.BlockSpec((TM, width), lambda i: (i, 0)), jax.ShapeDtypeStruct((tok, width), dtype)

    outs = [tile(CONV_W, BF), tile(POOL_W, BF), tile(512, BF), tile(LANES, BF), tile(LANES, BF),
            tile(MLA_HEADS * LANES, BF), tile(MLA_HEADS * LANES, BF), tile(MLA_HEADS * MLA_V, BF),
            tile(LANES, F32), tile(LANES, F32), tile(MLA_KV_RANK, F32), tile(MLA_ROPE, F32)]
    return pl.pallas_call(
        _proj_kernel,
        grid=(nt,),
        in_specs=[
            pl.BlockSpec((TM, D_MODEL), lambda i: (i, 0)),
            pl.BlockSpec((None, None, 6, D_MODEL), lambda i: (l, seq_row(i), 0, 0)),
            _layer_spec((1, D_MODEL), l),
            _layer_spec((D_MODEL, W1_COLS), l),
            pl.BlockSpec((TM, 9 * LANES), lambda i: (rope_blk(i), 0)),
            _layer_spec((1, 512), l),
            _layer_spec((1, LANES), l),
            _layer_spec((1, MLA_Q_RANK), l),
            _layer_spec((1, MLA_KV_RANK), l),
            _layer_spec((MLA_Q_RANK, MLA_HEADS * LANES), l),
            _layer_spec((MLA_KV_RANK, MLA_HEADS * (LANES + MLA_V)), l),
            _const_spec((LANES, MLA_HEADS * LANES)),
            _const_spec((512, 512)),
        ],
        out_specs=[o[0] for o in outs],
        out_shape=[o[1] for o in outs],
        compiler_params=pltpu.CompilerParams(dimension_semantics=("arbitrary",), vmem_limit_bytes=VMEM_LIMIT),
        name="proj",
    )(x, mod, p["norm1_g"], p["w1"], p["rope"], p["qn_g"], p["kn_g"], p["cqn_g"], p["kvn_g"],
      p["wqup"], p["wkvup"], p["place"], p["ones_bd"])


def _cache_kernel(ckv_ref, kr_ref, wkvup_ref, place_ref, km_ref, vm_ref):
    kv = _dot(ckv_ref[...].astype(BF), wkvup_ref[...])
    km = kv[:, 0:MLA_HEADS * LANES] + _dot(kr_ref[...].astype(BF), place_ref[...])
    km_ref[...] = km.astype(BF)
    vm_ref[...] = kv[:, MLA_HEADS * LANES:].astype(BF)


def _cache_prep(ckv, kr_pad, p):
    bs, depth, past, _ = ckv.shape
    return pl.pallas_call(
        _cache_kernel,
        grid=(bs, depth),
        in_specs=[
            pl.BlockSpec((None, None, past, MLA_KV_RANK), lambda b, l: (b, l, 0, 0)),
            pl.BlockSpec((None, None, past, LANES), lambda b, l: (b, l, 0, 0)),
            pl.BlockSpec((None, MLA_KV_RANK, MLA_HEADS * (LANES + MLA_V)), lambda b, l: (l, 0, 0)),
            pl.BlockSpec((LANES, MLA_HEADS * LANES), lambda b, l: (0, 0)),
        ],
        out_specs=[
            pl.BlockSpec((None, None, past, MLA_HEADS * LANES), lambda b, l: (b, l, 0, 0)),
            pl.BlockSpec((None, None, past, MLA_HEADS * MLA_V), lambda b, l: (b, l, 0, 0)),
        ],
        out_shape=[jax.ShapeDtypeStruct((bs, depth, past, MLA_HEADS * LANES), BF),
                   jax.ShapeDtypeStruct((bs, depth, past, MLA_HEADS * MLA_V), BF)],
        name="cache_prep",
    )(ckv, kr_pad, p["wkvup"], p["place"])


def _attend(qs, k, v):
    s = _dot_nt(qs, k)
    m = jnp.max(s, axis=-1, keepdims=True)
    e = jnp.exp2(s - m)
    l = jnp.sum(e, axis=-1, keepdims=True)
    return _dot(e.astype(BF), v) / l


def _attention_body(q, k, v, qm, km, vm, oc_ref, od_ref):
    lane_k = lax.broadcasted_iota(jnp.int32, k.shape, 1)
    lane_q = lax.broadcasted_iota(jnp.int32, (TM, LANES), 1)
    lo_k = lane_k < GQA_HEAD_DIM
    lo_q = lane_q < GQA_HEAD_DIM
    k32, v32 = k.astype(F32), v.astype(F32)
    k_sw = pltpu.roll(k32, GQA_HEAD_DIM, 1)
    v_sw = pltpu.roll(v32, GQA_HEAD_DIM, 1)
    k_dup = [jnp.where(lo_k, k32, k_sw).astype(BF), jnp.where(lo_k, k_sw, k32).astype(BF)]
    v_dup = [jnp.where(lo_k, v32, v_sw).astype(BF), jnp.where(lo_k, v_sw, v32).astype(BF)]
    zero = jnp.zeros((TM, LANES), BF)
    group = GQA_HEADS // GQA_KV_HEADS
    for j in range(GQA_HEADS // 2):
        qs = q[:, j * LANES:(j + 1) * LANES]
        g = (2 * j) // group
        o_lo = _attend(jnp.where(lo_q, qs, zero), k_dup[g], v_dup[g])
        o_hi = _attend(jnp.where(lo_q, zero, qs), k_dup[g], v_dup[g])
        oc_ref[:, j * LANES:(j + 1) * LANES] = jnp.where(lo_q, o_lo, o_hi).astype(BF)
    for j in range(MLA_HEADS // 2):
        vs = vm[:, j * LANES:(j + 1) * LANES]
        outs = []
        for h in (2 * j, 2 * j + 1):
            outs.append(_attend(qm[:, h * LANES:(h + 1) * LANES], km[:, h * LANES:(h + 1) * LANES], vs))
        od_ref[:, j * LANES:(j + 1) * LANES] = jnp.where(lo_q, outs[0], outs[1]).astype(BF)


def _attn_prompt_kernel(q_ref, k_ref, v_ref, qm_ref, km_ref, vm_ref, oc_ref, od_ref):
    _attention_body(q_ref[...], k_ref[...], v_ref[...], qm_ref[...], km_ref[...], vm_ref[...], oc_ref, od_ref)


def _attn_sample_kernel(q_ref, k_ref, v_ref, qm_ref, km_ref, vm_ref, ck_ref, cv_ref, ckm_ref, cvm_ref,
                        oc_ref, od_ref):
    k = jnp.concatenate([ck_ref[...], k_ref[...]], axis=0)
    v = jnp.concatenate([cv_ref[...], v_ref[...]], axis=0)
    km = jnp.concatenate([ckm_ref[...], km_ref[...]], axis=0)
    vm = jnp.concatenate([cvm_ref[...], vm_ref[...]], axis=0)
    _attention_body(q_ref[...], k, v, qm_ref[...], km, vm, oc_ref, od_ref)


def _attention(l, geo, q, k, v, qm, km, vm, cache):
    n_p, bs, tps, nt = geo["np"], geo["bs"], geo["tps"], geo["nt"]
    tok = nt * TM
    ls = tps * TM
    off = (n_p * TM) // ls
    widths = (512, LANES, LANES, MLA_HEADS * LANES, MLA_HEADS * LANES, MLA_HEADS * MLA_V)
    params = pltpu.CompilerParams(dimension_semantics=("arbitrary",), vmem_limit_bytes=VMEM_LIMIT)
    oc_p, od_p = pl.pallas_call(
        _attn_prompt_kernel,
        grid=(n_p,),
        in_specs=[pl.BlockSpec((TM, w), lambda i: (i, 0)) for w in widths],
        out_specs=[pl.BlockSpec((TM, 512), lambda i: (i, 0))] * 2,
        out_shape=[jax.ShapeDtypeStruct((n_p * TM, 512), BF)] * 2,
        compiler_params=params,
        name="attn_prompt",
    )(q, k, v, qm, km, vm)
    ck, cv, ckm, cvm = cache
    past = ck.shape[2]
    q_spec = lambda w: pl.BlockSpec((TM, w), lambda b, j: (n_p + b * tps + j, 0))
    kv_spec = lambda w: pl.BlockSpec((ls, w), lambda b, j: (off + b, 0))
    c_spec = lambda w: pl.BlockSpec((None, None, past, w), lambda b, j: (b, l, 0, 0))
    params2 = pltpu.CompilerParams(dimension_semantics=("arbitrary", "arbitrary"), vmem_limit_bytes=VMEM_LIMIT)
    o_spec = pl.BlockSpec((TM, 512), lambda b, j: (b * tps + j, 0))
    oc_s, od_s = pl.pallas_call(
        _attn_sample_kernel,
        grid=(bs, tps),
        in_specs=[q_spec(512), kv_spec(LANES), kv_spec(LANES), q_spec(MLA_HEADS * LANES),
                  kv_spec(MLA_HEADS * LANES), kv_spec(MLA_HEADS * MLA_V),
                  c_spec(LANES), c_spec(LANES), c_spec(MLA_HEADS * LANES), c_spec(MLA_HEADS * MLA_V)],
        out_specs=[o_spec, o_spec],
        out_shape=[jax.ShapeDtypeStruct((bs * ls, 512), BF)] * 2,
        compiler_params=params2,
        name="attn_sample",
    )(q, k, v, qm, km, vm, ck, cv, ckm, cvm)
    return oc_p, od_p, oc_s, od_s


def _route(h2, wr_ref, br_ref, ltri_ref, upper_ref, xs_ref, rl_ref, tw_ref, meta_ref):
    h2_hi = h2.astype(BF)
    h2_lo = (h2 - h2_hi.astype(F32)).astype(BF)
    hi_terms = _dot(h2_hi, wr_ref[...])
    logits = (hi_terms[:, 0:LANES] + hi_terms[:, LANES:] + _dot(h2_lo, wr_ref[:, 0:LANES])) + br_ref[...]
    lane = lax.broadcasted_iota(jnp.int32, (TM, LANES), 1).astype(F32)
    neg = jnp.float32(-jnp.inf)
    lg = jnp.where(lane < N_EXPERTS, logits, neg)
    vals, idxs = [], []
    for _ in range(TOP_K):
        m = jnp.max(lg, axis=-1, keepdims=True)
        idx = jnp.min(jnp.where(lg == m, lane, float(LANES)), axis=-1, keepdims=True)
        vals.append(m)
        idxs.append(idx)
        lg = jnp.where(lane == idx, neg, lg)
    exps = [jnp.exp(v - vals[0]) for v in vals]
    denom = exps[0] + exps[1] + exps[2] + exps[3]

    onehot = [(lane == idxs[kk]).astype(F32) for kk in range(TOP_K)]
    colsum = [jnp.sum(o, axis=0, keepdims=True) for o in onehot]
    cnt = colsum[0] + colsum[1] + colsum[2] + colsum[3]
    pad8 = jnp.floor((cnt + 7.0) * 0.125) * 8.0
    run_off = jnp.dot(jnp.broadcast_to(pad8, (8, LANES)), upper_ref[...], preferred_element_type=F32,
                      precision=lax.Precision.HIGHEST)[0:1, :]
    base = jnp.zeros((1, LANES), F32)
    rloc = []
    for kk in range(TOP_K):
        before = _dot(ltri_ref[...], onehot[kk].astype(BF))
        rloc.append(jnp.sum(onehot[kk] * (run_off + base + before), axis=-1, keepdims=True))
        base = base + colsum[kk]

    eye = (lax.broadcasted_iota(jnp.int32, (TM, TM), 0) == lax.broadcasted_iota(jnp.int32, (TM, TM), 1))
    r_sub = lax.broadcasted_iota(jnp.int32, (R_TILE, TM), 0).astype(F32)
    ones8 = jnp.ones((8, TM), BF)
    sel = jnp.zeros((R_TILE, TM), F32)
    for kk in range(TOP_K):
        hi = jnp.floor(rloc[kk] * (1.0 / TM))
        lo = rloc[kk] - hi * TM
        row = (_dot(ones8, jnp.where(eye, hi, 0.0).astype(BF)) * TM
               + _dot(ones8, jnp.where(eye, lo, 0.0).astype(BF)))[0:1, :]
        sel = jnp.where(r_sub == row, 1.0, sel)
    xs_ref[...] = _dot(sel.astype(BF), h2.astype(BF))

    rl = jnp.zeros((TM, LANES), F32)
    tw = jnp.zeros((TM, LANES), F32)
    for kk in range(TOP_K):
        rl = jnp.where(lane == kk, rloc[kk], rl)
        tw = jnp.where(lane == kk, exps[kk] / denom, tw)
    rl_ref[...] = rl.astype(jnp.int32)
    tw_ref[...] = tw
    sub = lax.broadcasted_iota(jnp.int32, (8, LANES), 0)
    meta = jnp.where(sub == 0, jnp.broadcast_to(pad8, (8, LANES)),
                     jnp.where(sub == 1, jnp.broadcast_to(run_off, (8, LANES)), 0.0))
    meta_ref[...] = meta.astype(jnp.int32)


def _merge_kernel(geo, x_ref, mod_ref, a_ref, ap_ref, an_ref, b_ref, bp_ref, bn_ref,
                  ocp_ref, odp_ref, ocs_ref, ods_ref,
                  g1_ref, wg_ref, cw_ref, cb_ref, lng_ref, lnb_ref, wco_ref, pw_ref, ps_ref, wpo_ref,
                  wgo_ref, wmo_ref, wo_ref, g2_ref, wr_ref, br_ref,
                  ltri_ref, upper_ref,
                  xo_ref, xs_ref, rl_ref, tw_ref, meta_ref, abuf, bbuf, ashift):
    n_p, tps = geo["np"], geo["tps"]
    i = pl.program_id(0)
    j = jnp.where(i < n_p, 0, (i - n_p) % tps)
    n_seq_tiles = jnp.where(i < n_p, 1, tps)
    has_prev = j > 0
    has_next = j < n_seq_tiles - 1

    def fill(buf, cur, prev, nxt):
        buf[0:HALO, :] = jnp.where(has_prev, prev[...].astype(F32), 0.0)
        buf[HALO:HALO + TM, :] = cur[...].astype(F32)
        buf[HALO + TM:, :] = jnp.where(has_next, nxt[...].astype(F32), 0.0)

    fill(abuf, a_ref, ap_ref, an_ref)
    fill(bbuf, b_ref, bp_ref, bn_ref)

    sh_rows = TM + 2 * HALO - SUBLANES
    for sh in range(1, SUBLANES):
        ashift[sh - 1] = abuf[sh:sh + sh_rows, :]
    rows = 32
    conv = []
    for r0 in range(0, TM, rows):
        acc = jnp.zeros((rows, CONV_W), F32)
        for t in range(CONV_K):
            s = r0 + t + HALO - CONV_K // 2
            sh = s % SUBLANES
            tap = abuf[s:s + rows, :] if sh == 0 else ashift[sh - 1, s - sh:s - sh + rows, :]
            acc = acc + tap * cw_ref[t:t + 1, :]
        conv.append(acc)
    ca = jnp.concatenate(conv, axis=0) + cb_ref[...]
    mu = jnp.mean(ca, axis=-1, keepdims=True)
    xc = ca - mu
    ln = xc * lax.rsqrt(jnp.mean(xc * xc, axis=-1, keepdims=True) + EPS) * lng_ref[...] + lnb_ref[...]
    br_a = _dot((ln * jax.nn.sigmoid(ln)).astype(BF), wco_ref[...])

    pos = j * TM + lax.broadcasted_iota(jnp.int32, (TM, 1), 0)
    seq_len = n_seq_tiles * TM
    mixed = []
    for g, w in enumerate(POOL_WINDOWS):
        lo, hi = g * POOL_G, (g + 1) * POOL_G
        acc = jnp.zeros((TM, POOL_G), F32)
        for o in range(-(w // 2), w - w // 2):
            acc = acc + bbuf[HALO + o:HALO + o + TM, lo:hi]
        cnt = jnp.clip(pos - w // 2 + w, 0, seq_len) - jnp.clip(pos - w // 2, 0, seq_len)
        pooled = acc / cnt.astype(F32) - bbuf[HALO:HALO + TM, lo:hi]
        mixed.append(_dot(pooled.astype(BF), pw_ref[g]))
    pb = jnp.concatenate(mixed, axis=-1) * ps_ref[...]
    br_b = _dot(pb.astype(BF), wpo_ref[...])

    is_prompt = i < n_p
    br_c = _dot(jnp.where(is_prompt, ocp_ref[...], ocs_ref[...]), wgo_ref[...])
    br_d = _dot(jnp.where(is_prompt, odp_ref[...], ods_ref[...]), wmo_ref[...])

    x = x_ref[...]
    h = _modulated_norm(x, g1_ref[...], mod_ref[0:1, :], mod_ref[1:2, :])
    gates = jax.nn.sigmoid(_dot(h.astype(BF), wg_ref[...]).astype(BF))
    merged = (gates[:, 0:D_MODEL] * br_a.astype(BF) + gates[:, D_MODEL:2 * D_MODEL] * br_b.astype(BF)
              + gates[:, 2 * D_MODEL:3 * D_MODEL] * br_c.astype(BF) + gates[:, 3 * D_MODEL:] * br_d.astype(BF))
    x = x + mod_ref[2:3, :] * _dot(merged, wo_ref[...])
    xo_ref[...] = x

    h2 = _modulated_norm(x, g2_ref[...], mod_ref[3:4, :], mod_ref[4:5, :])
    _route(h2, wr_ref, br_ref, ltri_ref, upper_ref, xs_ref, rl_ref, tw_ref, meta_ref)


def _merge(l, x, mod, geo, a, b, attn, p):
    nt, n_p = geo["nt"], geo["np"]
    tok = nt * TM
    seq_row = geo["seq_row"]
    hb = TM // HALO
    last_hb = nt * hb - 1
    tile = lambda w: pl.BlockSpec((TM, w), lambda i: (i, 0))
    prev = lambda w: pl.BlockSpec((HALO, w), lambda i: (jnp.maximum(i * hb - 1, 0), 0))
    nxt = lambda w: pl.BlockSpec((HALO, w), lambda i: (jnp.minimum((i + 1) * hb, last_hb), 0))
    p_tile = pl.BlockSpec((TM, 512), lambda i: (jnp.minimum(i, n_p - 1), 0))
    s_tile = pl.BlockSpec((TM, 512), lambda i: (jnp.maximum(i - n_p, 0), 0))
    return pl.pallas_call(
        functools.partial(_merge_kernel, geo),
        grid=(nt,),
        in_specs=[
            tile(D_MODEL),
            pl.BlockSpec((None, None, 6, D_MODEL), lambda i: (l, seq_row(i), 0, 0)),
            tile(CONV_W), prev(CONV_W), nxt(CONV_W),
            tile(POOL_W), prev(POOL_W), nxt(POOL_W),
            p_tile, p_tile, s_tile, s_tile,
            _layer_spec((1, D_MODEL), l),
            _layer_spec((D_MODEL, N_GATE), l),
            _layer_spec((CONV_K, CONV_W), l),
            _layer_spec((1, CONV_W), l),
            _layer_spec((1, CONV_W), l),
            _layer_spec((1, CONV_W), l),
            _layer_spec((CONV_W, D_MODEL), l),
            _layer_spec((len(POOL_WINDOWS), POOL_G, POOL_G), l),
            _layer_spec((1, POOL_W), l),
            _layer_spec((POOL_W, D_MODEL), l),
            _layer_spec((512, D_MODEL), l),
            _layer_spec((512, D_MODEL), l),
            _layer_spec((D_MODEL, D_MODEL), l),
            _layer_spec((1, D_MODEL), l),
            _layer_spec((D_MODEL, 2 * LANES), l),
            _layer_spec((1, LANES), l),
            _const_spec((TM, TM)),
            _const_spec((LANES, LANES)),
        ],
        out_specs=[tile(D_MODEL), pl.BlockSpec((R_TILE, D_MODEL), lambda i: (i, 0)), tile(LANES), tile(LANES),
                   pl.BlockSpec((None, 8, LANES), lambda i: (i, 0, 0))],
        out_shape=[jax.ShapeDtypeStruct((tok, D_MODEL), F32), jax.ShapeDtypeStruct((nt * R_TILE, D_MODEL), F32),
                   jax.ShapeDtypeStruct((tok, LANES), jnp.int32), jax.ShapeDtypeStruct((tok, LANES), F32),
                   jax.ShapeDtypeStruct((nt, 8, LANES), jnp.int32)],
        scratch_shapes=[pltpu.VMEM((TM + 2 * HALO, CONV_W), F32), pltpu.VMEM((TM + 2 * HALO, POOL_W), F32),
                        pltpu.VMEM((SUBLANES - 1, TM + 2 * HALO - SUBLANES, CONV_W), F32)],
        compiler_params=pltpu.CompilerParams(dimension_semantics=("arbitrary",), vmem_limit_bytes=VMEM_LIMIT),
        name="merge",
    )(x, mod, a, a, a, b, b, b, *attn,
      p["norm1_g"], p["wgate"], p["conv_dw"], p["conv_dw_b"], p["conv_ln_g"], p["conv_ln_b"], p["w_conv_out"],
      p["pool_w"], p["pool_scale"], p["w_pool_out"], p["w_gqa_out"], p["w_mla_out"], p["w_o"],
      p["norm2_g"], p["w_router"], p["b_router"], p["ltri"], p["upper"])


def _expert_kernel(nt, layer, blk_e, blk_row0, blk_t0, blk_t1, n_used_ref, pad8_ref, dst_ref, off_ref, tot8_ref,
                   next_e, xs_hbm, wgu_hbm, bg_ref, bu_ref, wd_hbm, bd_ref, perm_ref,
                   y_hbm, xg, yb, zbuf, wgu_buf, wd_buf, wg_bf, wu_bf, wd_bf, cnt_smem, gsem, osem, zsem, wsem):
    i = pl.program_id(0)
    n_used = n_used_ref[0]
    slot = i % 2
    last = pl.num_programs(0) - 1

    def for_pieces(b, fn):
        e = blk_e[b]
        b0 = blk_row0[b]

        def body(t, tot):
            j = t * N_EXPERTS + e
            run0 = dst_ref[j]
            lo = jnp.maximum(run0, b0)
            hi = jnp.minimum(run0 + pad8_ref[j], b0 + MOE_BM)
            n = pl.multiple_of(jnp.maximum(hi - lo, 0), 8)

            @pl.when(n > 0)
            def _():
                fn(pl.multiple_of(t * R_TILE + off_ref[j] + lo - run0, 8), pl.multiple_of(lo - b0, 8), n)

            return tot + n

        return lax.fori_loop(blk_t0[b], blk_t1[b], body, jnp.int32(0))

    def rows_copy(src, dst, sem, n):
        return pltpu.make_async_copy(src.at[pl.ds(0, n), :], dst.at[pl.ds(0, n), :], sem)

    def start_gather(b, s):
        def piece(row_t, row_b, n):
            pltpu.make_async_copy(xs_hbm.at[pl.ds(row_t, n), :], xg.at[s].at[pl.ds(row_b, n), :], gsem.at[s]).start()

        cnt_smem[s] = for_pieces(b, piece)

    def tail_copy(t):
        n = pl.multiple_of(R_TILE - tot8_ref[t], 8)
        return n, pltpu.make_async_copy(
            zbuf.at[pl.ds(0, n), :], y_hbm.at[pl.ds(pl.multiple_of(t * R_TILE + tot8_ref[t], 8), n), :], zsem)

    @pl.when(i == 0)
    def _():
        xg[...] = jnp.zeros_like(xg)
        zbuf[...] = jnp.zeros_like(zbuf)
        for s in range(4):
            cnt_smem[s] = 0

        def fill(t, c):
            n, cp = tail_copy(t)

            @pl.when(n > 0)
            def _():
                cp.start()

            return c

        lax.fori_loop(0, nt, fill, 0)

        @pl.when(n_used > 0)
        def _():
            start_gather(0, 0)

    @pl.when(i + 1 < n_used)
    def _():
        start_gather(i + 1, 1 - slot)

    @pl.when(i < n_used)
    def _():
        e_changed = jnp.logical_or(i == 0, blk_e[i] != blk_e[jnp.maximum(i - 1, 0)])

        def weight_copies(e):
            return (pltpu.make_async_copy(wgu_hbm.at[layer, e], wgu_buf, wsem.at[0]),
                    pltpu.make_async_copy(wd_hbm.at[layer, e], wd_buf, wsem.at[1]))

        @pl.when(e_changed)
        def _():
            e = blk_e[i]

            @pl.when(i == 0)
            def _():
                for cp in weight_copies(e):
                    cp.start()

            for cp in weight_copies(e):
                cp.wait()
            wd_bf[...] = wd_buf[...].astype(BF)
            for c in range(D_FF // LANES):
                pair = _dot(wgu_buf[:, 2 * c * LANES:2 * (c + 1) * LANES].astype(BF), perm_ref[...])
                wg_bf[:, c * LANES:(c + 1) * LANES] = pair[:, 0:LANES].astype(BF)
                wu_bf[:, c * LANES:(c + 1) * LANES] = pair[:, LANES:].astype(BF)
            e_next = next_e[e]

            @pl.when(e_next >= 0)
            def _():
                for cp in weight_copies(e_next):
                    cp.start()

        n_prev = pl.multiple_of(cnt_smem[2 + slot], 8)

        @pl.when(n_prev > 0)
        def _():
            rows_copy(yb.at[slot], y_hbm, osem.at[slot], n_prev).wait()

        n_in = pl.multiple_of(cnt_smem[slot], 8)
        rows_copy(xs_hbm, xg.at[slot], gsem.at[slot], n_in).wait()

        def expert_mlp(rows):
            xb = xg[slot, 0:rows, :].astype(BF)
            gate = jnp.minimum(_dot(xb, wg_bf[...]) + bg_ref[...], SWIGLU_LIMIT)
            up = jnp.clip(_dot(xb, wu_bf[...]) + bu_ref[...], -SWIGLU_LIMIT, SWIGLU_LIMIT)
            glu = gate * jax.nn.sigmoid(gate * SWIGLU_ALPHA)
            y = _dot(((up + 1.0) * glu).astype(BF), wd_bf[...]) + bd_ref[...]
            yb[slot, 0:rows, :] = y

        @pl.when(n_in > MOE_BM // 2)
        def _():
            expert_mlp(MOE_BM)

        @pl.when(n_in <= MOE_BM // 2)
        def _():
            expert_mlp(MOE_BM // 2)

        def piece(row_t, row_b, n):
            pltpu.make_async_copy(yb.at[slot].at[pl.ds(row_b, n), :], y_hbm.at[pl.ds(row_t, n), :],
                                  osem.at[slot]).start()

        cnt_smem[2 + slot] = for_pieces(i, piece)

    @pl.when(i == last)
    def _():
        for s in range(2):
            n_out = pl.multiple_of(cnt_smem[2 + s], 8)

            @pl.when(n_out > 0)
            def _():
                rows_copy(yb.at[s], y_hbm, osem.at[s], n_out).wait()

        def drain(t, c):
            n, cp = tail_copy(t)

            @pl.when(n > 0)
            def _():
                cp.wait()

            return c

        lax.fori_loop(0, nt, drain, 0)


def _experts(l, xs, plan, p, nt):
    n_blocks = plan["blk_e"].shape[0]
    n_pref = 10
    by_expert = lambda *lead: (lambda i, be, *_: lead + (be[i], 0, 0))
    b_spec = pl.BlockSpec((None, None, 1, D_FF), by_expert(l))
    grid_spec = pltpu.PrefetchScalarGridSpec(
        num_scalar_prefetch=n_pref,
        grid=(n_blocks,),
        in_specs=[
            pl.BlockSpec(memory_space=pl.ANY),
            pl.BlockSpec(memory_space=pl.ANY),
            b_spec, b_spec,
            pl.BlockSpec(memory_space=pl.ANY),
            pl.BlockSpec((None, None, 1, D_MODEL), by_expert(l)),
            pl.BlockSpec((2 * LANES, 2 * LANES), lambda i, *_: (0, 0)),
        ],
        out_specs=pl.BlockSpec(memory_space=pl.ANY),
        scratch_shapes=[
            pltpu.VMEM((2, MOE_BM, D_MODEL), F32),
            pltpu.VMEM((2, MOE_BM, D_MODEL), F32),
            pltpu.VMEM((R_TILE - TOP_K * TM, D_MODEL), F32),
            pltpu.VMEM((D_MODEL, 2 * D_FF), F32),
            pltpu.VMEM((D_FF, D_MODEL), F32),
            pltpu.VMEM((D_MODEL, D_FF), BF),
            pltpu.VMEM((D_MODEL, D_FF), BF),
            pltpu.VMEM((D_FF, D_MODEL), BF),
            pltpu.SMEM((4,), jnp.int32),
            pltpu.SemaphoreType.DMA((2,)),
            pltpu.SemaphoreType.DMA((2,)),
            pltpu.SemaphoreType.DMA,
            pltpu.SemaphoreType.DMA((2,)),
        ],
    )
    return pl.pallas_call(
        functools.partial(_expert_kernel, nt, l),
        grid_spec=grid_spec,
        out_shape=jax.ShapeDtypeStruct((nt * R_TILE, D_MODEL), F32),
        compiler_params=pltpu.CompilerParams(dimension_semantics=("arbitrary",), vmem_limit_bytes=VMEM_LIMIT),
        name="experts",
    )(plan["blk_e"], plan["blk_row0"], plan["blk_t0"], plan["blk_t1"], plan["n_used"], plan["pad8"], plan["dst"],
      plan["off"], plan["tot8"], plan["next_e"],
      xs, p["w_gu"], p["b_gate"], p["b_up"], p["w_dn"], p["b_dn"], p["pair_perm"])


def _pair_perm():
    m = np.zeros((2 * LANES, 2 * LANES), np.float32)
    j = np.arange(LANES)
    m[2 * j, j] = 1.0
    m[2 * j + 1, LANES + j] = 1.0
    return jnp.asarray(m, BF)


def _combine_kernel(final, y_ref, x_ref, mod_ref, rl_ref, tw_ref, fg_ref, o_ref):
    r_lane = lax.broadcasted_iota(jnp.int32, (TM, R_TILE), 1)
    rl = rl_ref[...]
    tw = tw_ref[...]
    sel = jnp.zeros((TM, R_TILE), F32)
    for k in range(TOP_K):
        sel = jnp.where(r_lane == rl[:, k:k + 1], tw[:, k:k + 1], sel)
    ffn = _dot(sel.astype(BF), y_ref[...].astype(BF))
    x = x_ref[...] + mod_ref[5:6, :] * ffn
    if final:
        x = _rms(x) * fg_ref[...]
    o_ref[...] = x


def _combine(l, final, y, x, mod, rl, tw, final_g, geo, tile0, n_tiles):
    seq_row = geo["seq_row"]
    tile = lambda w: pl.BlockSpec((TM, w), lambda i: (i + tile0, 0))
    return pl.pallas_call(
        functools.partial(_combine_kernel, final),
        grid=(n_tiles,),
        in_specs=[
            pl.BlockSpec((R_TILE, D_MODEL), lambda i: (i + tile0, 0)),
            tile(D_MODEL),
            pl.BlockSpec((None, None, 6, D_MODEL), lambda i: (l, seq_row(i + tile0), 0, 0)),
            tile(LANES),
            tile(LANES),
            _const_spec((1, D_MODEL)),
        ],
        out_specs=pl.BlockSpec((TM, D_MODEL), lambda i: (i, 0)),
        out_shape=jax.ShapeDtypeStruct((n_tiles * TM, D_MODEL), F32),
        compiler_params=pltpu.CompilerParams(dimension_semantics=("arbitrary",), vmem_limit_bytes=VMEM_LIMIT),
        name="combine",
    )(y, x, mod, rl, tw, final_g)


def _expert_plan(meta, nt):
    pad8 = meta[:, 0, :N_EXPERTS]
    off = meta[:, 1, :N_EXPERTS]
    ends = jnp.cumsum(pad8, axis=0)
    dst = ends - pad8
    tot = ends[-1]
    nb = (tot + MOE_BM - 1) // MOE_BM
    nb_end = jnp.cumsum(nb)
    n_blocks = (nt * TM * TOP_K + nt * N_EXPERTS * 7) // MOE_BM + N_EXPERTS
    b = jnp.arange(n_blocks, dtype=jnp.int32)
    blk_e = jnp.minimum(jnp.sum(nb_end[None, :] <= b[:, None], axis=1), N_EXPERTS - 1).astype(jnp.int32)
    blk_row0 = (b - (nb_end - nb)[blk_e]) * MOE_BM
    ends_b = ends[:, blk_e]
    dst_b = dst[:, blk_e]
    blk_t0 = jnp.sum(ends_b <= blk_row0[None, :], axis=0)
    blk_t1 = jnp.sum(dst_b < blk_row0[None, :] + MOE_BM, axis=0)
    i32 = lambda v: v.astype(jnp.int32)
    ids = jnp.arange(N_EXPERTS, dtype=jnp.int32)
    later = jnp.where((nb > 0)[None, :] & (ids[None, :] > ids[:, None]), ids[None, :], N_EXPERTS)
    next_e = jnp.min(later, axis=1)
    next_e = jnp.where(next_e == N_EXPERTS, -1, next_e)
    return {"next_e": i32(next_e), "blk_e": blk_e, "blk_row0": i32(blk_row0), "blk_t0": i32(blk_t0), "blk_t1": i32(blk_t1),
            "n_used": i32(nb_end[-1]).reshape(1), "pad8": i32(pad8.reshape(-1)), "dst": i32(dst.reshape(-1)),
            "off": i32(off.reshape(-1)), "tot8": i32(jnp.sum(pad8, axis=1))}


def _rope_tables(n_pos):
    pos = np.arange(n_pos)
    row, col = pos // GRID_W, pos % GRID_W
    lane = np.arange(LANES)

    def build(active, r, half):
        n_rot = 4 * half
        is_col = (r % n_rot) >= 2 * half
        rr = r % (2 * half)
        freq = np.power(ROPE_BASE, -(rr % half).astype(np.float64) / half)
        p = np.where(is_col[None, :], col[:, None], row[:, None]).astype(np.float64)
        ang = p * freq[None, :]
        first = rr < half
        cos = np.where(active[None, :], np.cos(ang), 1.0)
        sin_a = np.where((active & first)[None, :], -np.sin(ang), 0.0)
        sin_b = np.where((active & ~first)[None, :], np.sin(ang), 0.0)
        return [cos, sin_a, sin_b]

    tabs = build(np.ones(LANES, bool), lane % GQA_HEAD_DIM, GQA_HEAD_DIM // 4)
    in_rope = (lane >= MLA_NOPE) & (lane < MLA_NOPE + MLA_ROPE)
    tabs += build(in_rope, np.maximum(lane - MLA_NOPE, 0) % MLA_ROPE, MLA_ROPE // 4)
    tabs += build(lane < MLA_ROPE, lane % MLA_ROPE, MLA_ROPE // 4)
    table = np.concatenate(tabs, axis=1)
    ident = np.concatenate([np.ones((TM, LANES)), np.zeros((TM, LANES)), np.zeros((TM, LANES))] * 3, axis=1)
    return jnp.asarray(np.concatenate([ident, table], axis=0), F32)


def _placement():
    e = np.zeros((LANES, MLA_HEADS * LANES), np.float32)
    for h in range(MLA_HEADS):
        for r in range(MLA_ROPE):
            e[r, h * LANES + MLA_NOPE + r] = 1.0
    return jnp.asarray(e, BF)


def _split_hi_lo(w):
    hi = w.astype(BF)
    lo = (w - hi.astype(F32)).astype(BF)
    return jnp.concatenate([hi, lo], axis=-1)


def _block_diag_ones(n, g):
    idx = np.arange(n) // g
    return jnp.asarray((idx[:, None] == idx[None, :]).astype(np.float32), BF)


def kernel(x_prompt, x_sample, cache_gqa_k, cache_gqa_v, cache_mla_ckv, cache_mla_krope, c, c_ctx, norm1_g, norm2_g, w_mod, b_mod, w_in, conv_dw, conv_dw_b, conv_ln_g, conv_ln_b, w_conv_out, pool_w, pool_scale, w_pool_out, gqa_qn_g, gqa_kn_g, w_gqa_out, mla_qn_g, w_mla_q_up, mla_kvn_g, w_mla_kv_up, w_mla_out, w_o, w_router, b_router, w_gu, b_gu, w_dn, b_dn, final_g):
    bp, seq, d = x_prompt.shape
    bs, ls, _ = x_sample.shape
    depth = w_in.shape[0]
    past = cache_gqa_k.shape[2]
    assert seq == TM and d == D_MODEL and ls % TM == 0 and (bp * seq) % ls == 0
    tps = ls // TM
    n_p = bp
    nt = n_p + bs * tps
    geo = {
        "np": n_p, "bs": bs, "tps": tps, "nt": nt,
        "seq_row": lambda i: jnp.where(i < n_p, 0, 1 + (i - n_p) // tps),
        "rope_blk": lambda i: jnp.where(i < n_p, 0, 1 + (i - n_p) % tps),
    }

    n_cond = -(-(1 + bs) // 8) * 8
    cond = jnp.zeros((n_cond, d), F32).at[0].set(c_ctx).at[1:1 + bs].set(c)
    mod = _modulation(cond, w_mod, b_mod).reshape(depth, n_cond, 6, d)

    row = lambda v: v.reshape(depth, 1, -1)
    w1 = jnp.pad(w_in[:, :, :_SPLIT_GATE], ((0, 0), (0, 0), (0, W1_COLS - _SPLIT_GATE))).astype(BF)
    wqup = jnp.pad(w_mla_q_up.reshape(depth, MLA_Q_RANK, MLA_HEADS, MLA_NOPE + MLA_ROPE),
                   ((0, 0), (0, 0), (0, 0), (0, LANES - MLA_NOPE - MLA_ROPE)))
    wkv = w_mla_kv_up.reshape(depth, MLA_KV_RANK, MLA_HEADS, MLA_NOPE + MLA_V)
    wk_pad = jnp.pad(wkv[..., :MLA_NOPE], ((0, 0), (0, 0), (0, 0), (0, LANES - MLA_NOPE)))
    wkvup = jnp.concatenate([wk_pad.reshape(depth, MLA_KV_RANK, MLA_HEADS * LANES),
                             wkv[..., MLA_NOPE:].reshape(depth, MLA_KV_RANK, MLA_HEADS * MLA_V)], axis=-1)
    bgu = b_gu.reshape(depth, N_EXPERTS, 1, D_FF, 2)
    p = {
        "norm1_g": row(norm1_g), "norm2_g": row(norm2_g),
        "w1": w1, "wgate": w_in[:, :, _SPLIT_GATE:].astype(BF),
        "rope": _rope_tables(ls), "place": _placement(), "ones_bd": _block_diag_ones(512, GQA_HEAD_DIM),
        "ltri": jnp.asarray(np.tril(np.ones((TM, TM), np.float32), -1), BF),
        "upper": jnp.asarray(np.triu(np.ones((LANES, LANES), np.float32), 1), F32),
        "qn_g": row(jnp.tile(gqa_qn_g, (1, GQA_HEADS))), "kn_g": row(jnp.tile(gqa_kn_g, (1, GQA_KV_HEADS))),
        "cqn_g": row(mla_qn_g), "kvn_g": row(mla_kvn_g),
        "wqup": wqup.reshape(depth, MLA_Q_RANK, MLA_HEADS * LANES).astype(BF), "wkvup": wkvup.astype(BF),
        "conv_dw": conv_dw, "conv_dw_b": row(conv_dw_b), "conv_ln_g": row(conv_ln_g), "conv_ln_b": row(conv_ln_b),
        "w_conv_out": w_conv_out.astype(BF), "pool_w": pool_w.astype(BF), "pool_scale": row(pool_scale),
        "w_pool_out": w_pool_out.astype(BF), "w_gqa_out": w_gqa_out.astype(BF), "w_mla_out": w_mla_out.astype(BF),
        "w_o": w_o.astype(BF),
        "w_router": _split_hi_lo(jnp.pad(w_router, ((0, 0), (0, 0), (0, LANES - N_EXPERTS)))),
        "b_router": row(jnp.pad(b_router, ((0, 0), (0, LANES - N_EXPERTS)))),
        "w_gu": w_gu, "b_gate": bgu[..., 0], "b_up": bgu[..., 1], "pair_perm": _pair_perm(),
        "w_dn": w_dn, "b_dn": b_dn.reshape(depth, N_EXPERTS, 1, d),
    }

    ckm, cvm = _cache_prep(cache_mla_ckv, jnp.pad(cache_mla_krope, ((0, 0), (0, 0), (0, 0), (0, LANES - MLA_ROPE))), p)
    cache = (cache_gqa_k.reshape(bs, depth, past, LANES).astype(BF),
             cache_gqa_v.reshape(bs, depth, past, LANES).astype(BF), ckm, cvm)

    x = jnp.concatenate([x_prompt.reshape(bp * seq, d), x_sample.reshape(bs * ls, d)], axis=0)
    n_ptok = bp * seq
    states = []
    for l in range(depth):
        a, b, q, k, v, qm, km, vm, ks, vs, ckvs, krs = _proj(l, x, mod, geo, p)
        states.append((ks[:n_ptok], vs[:n_ptok], ckvs[:n_ptok], krs[:n_ptok]))
        attn = _attention(l, geo, q, k, v, qm, km, vm, cache)
        x_mid, xs, slot_row, top_w, meta = _merge(l, x, mod, geo, a, b, attn, p)
        y = _experts(l, xs, _expert_plan(meta, nt), p, nt)
        combine = functools.partial(_combine, l, l == depth - 1, y, x_mid, mod, slot_row, top_w,
                                    final_g.reshape(1, d), geo)
        if l < depth - 1:
            x = combine(0, nt)

    y_prompt = combine(0, n_p).reshape(bp, seq, d)
    y_sample = combine(n_p, nt - n_p).reshape(bs, ls, d)
    st = lambda j, shape: jnp.stack([s[j].reshape(shape) for s in states], axis=1)
    return (y_prompt, y_sample,
            st(0, (bp, seq, GQA_KV_HEADS, GQA_HEAD_DIM)), st(1, (bp, seq, GQA_KV_HEADS, GQA_HEAD_DIM)),
            st(2, (bp, seq, MLA_KV_RANK)), st(3, (bp, seq, MLA_ROPE)))
```

```python
import functools

import jax
import jax.numpy as jnp
import numpy as np
from jax import lax
from jax.experimental import pallas as pl
from jax.experimental.pallas import tpu as pltpu

D_MODEL = 1024
GRID_W = 64
CONV_W = 512
CONV_K = 31
POOL_W = 512
POOL_WINDOWS = (2, 4, 8, 16)
POOL_G = 128
GQA_HEADS = 8
GQA_KV_HEADS = 2
GQA_HEAD_DIM = 64
MLA_HEADS = 8
MLA_Q_RANK = 384
MLA_KV_RANK = 256
MLA_NOPE = 64
MLA_ROPE = 32
MLA_V = 64
ROPE_BASE = 10000.0
N_EXPERTS = 32
TOP_K = 4
D_FF = 1024
SWIGLU_LIMIT = 7.0
SWIGLU_ALPHA = 1.702
EPS = 1e-6
GQA_SCALE = GQA_HEAD_DIM ** -0.5
MLA_SCALE = (MLA_NOPE + MLA_ROPE) ** -0.5
LOG2E = 1.4426950408889634

LANES = 128
SUBLANES = 8
TM = 256
HALO = 16
MOE_BM = 768
R_TILE = 1280
W1_COLS = 3072
N_GATE = 4 * D_MODEL
VMEM_LIMIT = 56 * 1024 * 1024

BF = jnp.bfloat16
F32 = jnp.float32

_C_A, _C_B, _C_Q, _C_K, _C_V, _C_CQ, _C_CKV, _C_KR = 0, 1024, 1536, 2048, 2176, 2304, 2688, 2944
_SPLIT_GATE = 2976


def _dot(a, b):
    return jnp.dot(a, b, preferred_element_type=F32)


def _dot_nt(a, b):
    return lax.dot_general(a, b, (((1,), (1,)), ((), ())), preferred_element_type=F32)


def _rms(x):
    return x * lax.rsqrt(jnp.mean(x * x, axis=-1, keepdims=True) + EPS)


def _group_mean_sq(x, ones_bd, width):
    xx = x * x
    hi = xx.astype(BF)
    lo = (xx - hi.astype(F32)).astype(BF)
    return (_dot(hi, ones_bd) + _dot(lo, ones_bd)) * (1.0 / width)


def _tile_lanes(t, width):
    reps = width // LANES
    return t if reps == 1 else jnp.concatenate([t] * reps, axis=-1)


def _rope(x, cos, sin_a, sin_b, shift):
    w = x.shape[-1]
    return (x * _tile_lanes(cos, w) + pltpu.roll(x, w - shift, 1) * _tile_lanes(sin_a, w)
            + pltpu.roll(x, shift, 1) * _tile_lanes(sin_b, w))


def _modulated_norm(x, g, shift, scale):
    return _rms(x) * g * (1.0 + scale) + shift


def _mod_kernel(cond_ref, w_ref, b_ref, o_ref):
    c = cond_ref[...]
    s = (c * jax.nn.sigmoid(c)).astype(BF)
    o_ref[...] = _dot(s, w_ref[...].astype(BF)) + b_ref[...]


def _modulation(cond, w_mod, b_mod):
    depth, d, n = w_mod.shape
    rows = cond.shape[0]
    return pl.pallas_call(
        _mod_kernel,
        grid=(depth, n // D_MODEL),
        in_specs=[
            pl.BlockSpec((rows, d), lambda l, j: (0, 0)),
            pl.BlockSpec((None, d, D_MODEL), lambda l, j: (l, 0, j)),
            pl.BlockSpec((None, 1, D_MODEL), lambda l, j: (l, 0, j)),
        ],
        out_specs=pl.BlockSpec((None, rows, D_MODEL), lambda l, j: (l, 0, j)),
        out_shape=jax.ShapeDtypeStruct((depth, rows, n), F32),
        name="modulation",
    )(cond, w_mod, b_mod.reshape(depth, 1, n))


def _proj_kernel(x_ref, mod_ref, g1_ref, w1_ref, rope_ref, qn_ref, kn_ref, cqn_ref, kvn_ref,
                 wqup_ref, wkvup_ref, place_ref, ones_ref,
                 a_ref, b_ref, q_ref, k_ref, v_ref, qm_ref, km_ref, vm_ref,
                 ks_ref, vs_ref, ckvs_ref, krs_ref):
    x = x_ref[...]
    h = _modulated_norm(x, g1_ref[...], mod_ref[0:1, :], mod_ref[1:2, :])
    y = _dot(h.astype(BF), w1_ref[...])

    a_ref[...] = (y[:, _C_A:_C_A + CONV_W] * jax.nn.sigmoid(y[:, _C_A + CONV_W:_C_B])).astype(BF)
    b_ref[...] = y[:, _C_B:_C_Q].astype(BF)

    def tab(j):
        return rope_ref[:, j * LANES:(j + 1) * LANES]

    q = y[:, _C_Q:_C_K]
    q = q * lax.rsqrt(_group_mean_sq(q, ones_ref[...], GQA_HEAD_DIM) + EPS) * qn_ref[...]
    q = _rope(q, tab(0), tab(1), tab(2), GQA_HEAD_DIM // 4)
    q_ref[...] = (q * (GQA_SCALE * LOG2E)).astype(BF)

    k = y[:, _C_K:_C_V]
    k = k * lax.rsqrt(_group_mean_sq(k, ones_ref[0:LANES, 0:LANES], GQA_HEAD_DIM) + EPS) * kn_ref[...]
    ks_ref[...] = k
    k_ref[...] = _rope(k, tab(0), tab(1), tab(2), GQA_HEAD_DIM // 4).astype(BF)

    v = y[:, _C_V:_C_CQ]
    vs_ref[...] = v
    v_ref[...] = v.astype(BF)

    cq = _rms(y[:, _C_CQ:_C_CKV]) * cqn_ref[...]
    qm = _dot(cq.astype(BF), wqup_ref[...])
    qm = _rope(qm, tab(3), tab(4), tab(5), MLA_ROPE // 4)
    qm_ref[...] = (qm * (MLA_SCALE * LOG2E)).astype(BF)

    ckv = _rms(y[:, _C_CKV:_C_KR]) * kvn_ref[...]
    ckvs_ref[...] = ckv
    kv = _dot(ckv.astype(BF), wkvup_ref[...])
    kr = y[:, _C_KR:W1_COLS]
    krs_ref[...] = kr[:, 0:MLA_ROPE]
    kr_rot = _rope(kr, tab(6), tab(7), tab(8), MLA_ROPE // 4)
    km = kv[:, 0:MLA_HEADS * LANES] + _dot(kr_rot.astype(BF), place_ref[...])
    km_ref[...] = km.astype(BF)
    vm_ref[...] = kv[:, MLA_HEADS * LANES:].astype(BF)


def _const_spec(shape):
    nd = len(shape)
    return pl.BlockSpec(shape, lambda *_: (0,) * nd)


def _layer_spec(shape, l):
    nd = len(shape)
    return pl.BlockSpec((None,) + shape, lambda *_: (l,) + (0,) * nd)


def _proj(l, x, mod, geo, p):
    nt = geo["nt"]
    tok = nt * TM
    seq_row, rope_blk = geo["seq_row"], geo["rope_blk"]

    def tile(width, dtype):
        return pl.BlockSpec((TM, width), lambda i: (i, 0)), jax.ShapeDtypeStruct((tok, width), dtype)

    outs = [tile(CONV_W, BF), tile(POOL_W, BF), tile(512, BF), tile(LANES, BF), tile(LANES, BF),
            tile(MLA_HEADS * LANES, BF), tile(MLA_HEADS * LANES, BF), tile(MLA_HEADS * MLA_V, BF),
            tile(LANES, F32), tile(LANES, F32), tile(MLA_KV_RANK, F32), tile(MLA_ROPE, F32)]
    return pl.pallas_call(
        _proj_kernel,
        grid=(nt,),
        in_specs=[
            pl.BlockSpec((TM, D_MODEL), lambda i: (i, 0)),
            pl.BlockSpec((None, None, 6, D_MODEL), lambda i: (l, seq_row(i), 0, 0)),
            _layer_spec((1, D_MODEL), l),
            _layer_spec((D_MODEL, W1_COLS), l),
            pl.BlockSpec((TM, 9 * LANES), lambda i: (rope_blk(i), 0)),
            _layer_spec((1, 512), l),
            _layer_spec((1, LANES), l),
            _layer_spec((1, MLA_Q_RANK), l),
            _layer_spec((1, MLA_KV_RANK), l),
            _layer_spec((MLA_Q_RANK, MLA_HEADS * LANES), l),
            _layer_spec((MLA_KV_RANK, MLA_HEADS * (LANES + MLA_V)), l),
            _const_spec((LANES, MLA_HEADS * LANES)),
            _const_spec((512, 512)),
        ],
        out_specs=[o[0] for o in outs],
        out_shape=[o[1] for o in outs],
        compiler_params=pltpu.CompilerParams(dimension_semantics=("arbitrary",), vmem_limit_bytes=VMEM_LIMIT),
        name="proj",
    )(x, mod, p["norm1_g"], p["w1"], p["rope"], p["qn_g"], p["kn_g"], p["cqn_g"], p["kvn_g"],
      p["wqup"], p["wkvup"], p["place"], p["ones_bd"])


def _cache_kernel(ckv_ref, kr_ref, wkvup_ref, place_ref, km_ref, vm_ref):
    kv = _dot(ckv_ref[...].astype(BF), wkvup_ref[...])
    km = kv[:, 0:MLA_HEADS * LANES] + _dot(kr_ref[...].astype(BF), place_ref[...])
    km_ref[...] = km.astype(BF)
    vm_ref[...] = kv[:, MLA_HEADS * LANES:].astype(BF)


def _cache_prep(ckv, kr_pad, p):
    bs, depth, past, _ = ckv.shape
    return pl.pallas_call(
        _cache_kernel,
        grid=(bs, depth),
        in_specs=[
            pl.BlockSpec((None, None, past, MLA_KV_RANK), lambda b, l: (b, l, 0, 0)),
            pl.BlockSpec((None, None, past, LANES), lambda b, l: (b, l, 0, 0)),
            pl.BlockSpec((None, MLA_KV_RANK, MLA_HEADS * (LANES + MLA_V)), lambda b, l: (l, 0, 0)),
            pl.BlockSpec((LANES, MLA_HEADS * LANES), lambda b, l: (0, 0)),
        ],
        out_specs=[
            pl.BlockSpec((None, None, past, MLA_HEADS * LANES), lambda b, l: (b, l, 0, 0)),
            pl.BlockSpec((None, None, past, MLA_HEADS * MLA_V), lambda b, l: (b, l, 0, 0)),
        ],
        out_shape=[jax.ShapeDtypeStruct((bs, depth, past, MLA_HEADS * LANES), BF),
                   jax.ShapeDtypeStruct((bs, depth, past, MLA_HEADS * MLA_V), BF)],
        name="cache_prep",
    )(ckv, kr_pad, p["wkvup"], p["place"])


def _attend(qs, k, v):
    s = _dot_nt(qs, k)
    m = jnp.max(s, axis=-1, keepdims=True)
    e = jnp.exp2(s - m)
    l = jnp.sum(e, axis=-1, keepdims=True)
    return _dot(e.astype(BF), v) / l


def _attention_body(q, k, v, qm, km, vm, oc_ref, od_ref):
    lane_k = lax.broadcasted_iota(jnp.int32, k.shape, 1)
    lane_q = lax.broadcasted_iota(jnp.int32, (TM, LANES), 1)
    lo_k = lane_k < GQA_HEAD_DIM
    lo_q = lane_q < GQA_HEAD_DIM
    k32, v32 = k.astype(F32), v.astype(F32)
    k_sw = pltpu.roll(k32, GQA_HEAD_DIM, 1)
    v_sw = pltpu.roll(v32, GQA_HEAD_DIM, 1)
    k_dup = [jnp.where(lo_k, k32, k_sw).astype(BF), jnp.where(lo_k, k_sw, k32).astype(BF)]
    v_dup = [jnp.where(lo_k, v32, v_sw).astype(BF), jnp.where(lo_k, v_sw, v32).astype(BF)]
    zero = jnp.zeros((TM, LANES), BF)
    group = GQA_HEADS // GQA_KV_HEADS
    for j in range(GQA_HEADS // 2):
        qs = q[:, j * LANES:(j + 1) * LANES]
        g = (2 * j) // group
        o_lo = _attend(jnp.where(lo_q, qs, zero), k_dup[g], v_dup[g])
        o_hi = _attend(jnp.where(lo_q, zero, qs), k_dup[g], v_dup[g])
        oc_ref[:, j * LANES:(j + 1) * LANES] = jnp.where(lo_q, o_lo, o_hi).astype(BF)
    for j in range(MLA_HEADS // 2):
        vs = vm[:, j * LANES:(j + 1) * LANES]
        outs = []
        for h in (2 * j, 2 * j + 1):
            outs.append(_attend(qm[:, h * LANES:(h + 1) * LANES], km[:, h * LANES:(h + 1) * LANES], vs))
        od_ref[:, j * LANES:(j + 1) * LANES] = jnp.where(lo_q, outs[0], outs[1]).astype(BF)


def _attn_prompt_kernel(q_ref, k_ref, v_ref, qm_ref, km_ref, vm_ref, oc_ref, od_ref):
    _attention_body(q_ref[...], k_ref[...], v_ref[...], qm_ref[...], km_ref[...], vm_ref[...], oc_ref, od_ref)


def _attn_sample_kernel(q_ref, k_ref, v_ref, qm_ref, km_ref, vm_ref, ck_ref, cv_ref, ckm_ref, cvm_ref,
                        oc_ref, od_ref):
    k = jnp.concatenate([ck_ref[...], k_ref[...]], axis=0)
    v = jnp.concatenate([cv_ref[...], v_ref[...]], axis=0)
    km = jnp.concatenate([ckm_ref[...], km_ref[...]], axis=0)
    vm = jnp.concatenate([cvm_ref[...], vm_ref[...]], axis=0)
    _attention_body(q_ref[...], k, v, qm_ref[...], km, vm, oc_ref, od_ref)


def _attention(l, geo, q, k, v, qm, km, vm, cache):
    n_p, bs, tps, nt = geo["np"], geo["bs"], geo["tps"], geo["nt"]
    tok = nt * TM
    ls = tps * TM
    off = (n_p * TM) // ls
    widths = (512, LANES, LANES, MLA_HEADS * LANES, MLA_HEADS * LANES, MLA_HEADS * MLA_V)
    params = pltpu.CompilerParams(dimension_semantics=("arbitrary",), vmem_limit_bytes=VMEM_LIMIT)
    oc_p, od_p = pl.pallas_call(
        _attn_prompt_kernel,
        grid=(n_p,),
        in_specs=[pl.BlockSpec((TM, w), lambda i: (i, 0)) for w in widths],
        out_specs=[pl.BlockSpec((TM, 512), lambda i: (i, 0))] * 2,
        out_shape=[jax.ShapeDtypeStruct((n_p * TM, 512), BF)] * 2,
        compiler_params=params,
        name="attn_prompt",
    )(q, k, v, qm, km, vm)
    ck, cv, ckm, cvm = cache
    past = ck.shape[2]
    q_spec = lambda w: pl.BlockSpec((TM, w), lambda b, j: (n_p + b * tps + j, 0))
    kv_spec = lambda w: pl.BlockSpec((ls, w), lambda b, j: (off + b, 0))
    c_spec = lambda w: pl.BlockSpec((None, None, past, w), lambda b, j: (b, l, 0, 0))
    params2 = pltpu.CompilerParams(dimension_semantics=("arbitrary", "arbitrary"), vmem_limit_bytes=VMEM_LIMIT)
    o_spec = pl.BlockSpec((TM, 512), lambda b, j: (b * tps + j, 0))
    oc_s, od_s = pl.pallas_call(
        _attn_sample_kernel,
        grid=(bs, tps),
        in_specs=[q_spec(512), kv_spec(LANES), kv_spec(LANES), q_spec(MLA_HEADS * LANES),
                  kv_spec(MLA_HEADS * LANES), kv_spec(MLA_HEADS * MLA_V),
                  c_spec(LANES), c_spec(LANES), c_spec(MLA_HEADS * LANES), c_spec(MLA_HEADS * MLA_V)],
        out_specs=[o_spec, o_spec],
        out_shape=[jax.ShapeDtypeStruct((bs * ls, 512), BF)] * 2,
        compiler_params=params2,
        name="attn_sample",
    )(q, k, v, qm, km, vm, ck, cv, ckm, cvm)
    return oc_p, od_p, oc_s, od_s


def _route(h2, wr_ref, br_ref, ltri_ref, upper_ref, xs_ref, rl_ref, tw_ref, meta_ref):
    h2_hi = h2.astype(BF)
    h2_lo = (h2 - h2_hi.astype(F32)).astype(BF)
    hi_terms = _dot(h2_hi, wr_ref[...])
    logits = (hi_terms[:, 0:LANES] + hi_terms[:, LANES:] + _dot(h2_lo, wr_ref[:, 0:LANES])) + br_ref[...]
    lane = lax.broadcasted_iota(jnp.int32, (TM, LANES), 1).astype(F32)
    neg = jnp.float32(-jnp.inf)
    lg = jnp.where(lane < N_EXPERTS, logits, neg)
    vals, idxs = [], []
    for _ in range(TOP_K):
        m = jnp.max(lg, axis=-1, keepdims=True)
        idx = jnp.min(jnp.where(lg == m, lane, float(LANES)), axis=-1, keepdims=True)
        vals.append(m)
        idxs.append(idx)
        lg = jnp.where(lane == idx, neg, lg)
    exps = [jnp.exp(v - vals[0]) for v in vals]
    denom = exps[0] + exps[1] + exps[2] + exps[3]

    onehot = [(lane == idxs[kk]).astype(F32) for kk in range(TOP_K)]
    colsum = [jnp.sum(o, axis=0, keepdims=True) for o in onehot]
    cnt = colsum[0] + colsum[1] + colsum[2] + colsum[3]
    pad8 = jnp.floor((cnt + 7.0) * 0.125) * 8.0
    run_off = jnp.dot(jnp.broadcast_to(pad8, (8, LANES)), upper_ref[...], preferred_element_type=F32,
                      precision=lax.Precision.HIGHEST)[0:1, :]
    base = jnp.zeros((1, LANES), F32)
    rloc = []
    for kk in range(TOP_K):
        before = _dot(ltri_ref[...], onehot[kk].astype(BF))
        rloc.append(jnp.sum(onehot[kk] * (run_off + base + before), axis=-1, keepdims=True))
        base = base + colsum[kk]

    eye = (lax.broadcasted_iota(jnp.int32, (TM, TM), 0) == lax.broadcasted_iota(jnp.int32, (TM, TM), 1))
    r_sub = lax.broadcasted_iota(jnp.int32, (R_TILE, TM), 0).astype(F32)
    ones8 = jnp.ones((8, TM), BF)
    sel = jnp.zeros((R_TILE, TM), F32)
    for kk in range(TOP_K):
        hi = jnp.floor(rloc[kk] * (1.0 / TM))
        lo = rloc[kk] - hi * TM
        row = (_dot(ones8, jnp.where(eye, hi, 0.0).astype(BF)) * TM
               + _dot(ones8, jnp.where(eye, lo, 0.0).astype(BF)))[0:1, :]
        sel = jnp.where(r_sub == row, 1.0, sel)
    xs_ref[...] = _dot(sel.astype(BF), h2.astype(BF))

    rl = jnp.zeros((TM, LANES), F32)
    tw = jnp.zeros((TM, LANES), F32)
    for kk in range(TOP_K):
        rl = jnp.where(lane == kk, rloc[kk], rl)
        tw = jnp.where(lane == kk, exps[kk] / denom, tw)
    rl_ref[...] = rl.astype(jnp.int32)
    tw_ref[...] = tw
    sub = lax.broadcasted_iota(jnp.int32, (8, LANES), 0)
    meta = jnp.where(sub == 0, jnp.broadcast_to(pad8, (8, LANES)),
                     jnp.where(sub == 1, jnp.broadcast_to(run_off, (8, LANES)), 0.0))
    meta_ref[...] = meta.astype(jnp.int32)


def _merge_kernel(geo, x_ref, mod_ref, a_ref, ap_ref, an_ref, b_ref, bp_ref, bn_ref,
                  ocp_ref, odp_ref, ocs_ref, ods_ref,
                  g1_ref, wg_ref, cw_ref, cb_ref, lng_ref, lnb_ref, wco_ref, pw_ref, ps_ref, wpo_ref,
                  wgo_ref, wmo_ref, wo_ref, g2_ref, wr_ref, br_ref,
                  ltri_ref, upper_ref,
                  xo_ref, xs_ref, rl_ref, tw_ref, meta_ref, abuf, bbuf, ashift):
    n_p, tps = geo["np"], geo["tps"]
    i = pl.program_id(0)
    j = jnp.where(i < n_p, 0, (i - n_p) % tps)
    n_seq_tiles = jnp.where(i < n_p, 1, tps)
    has_prev = j > 0
    has_next = j < n_seq_tiles - 1

    def fill(buf, cur, prev, nxt):
        buf[0:HALO, :] = jnp.where(has_prev, prev[...].astype(F32), 0.0)
        buf[HALO:HALO + TM, :] = cur[...].astype(F32)
        buf[HALO + TM:, :] = jnp.where(has_next, nxt[...].astype(F32), 0.0)

    fill(abuf, a_ref, ap_ref, an_ref)
    fill(bbuf, b_ref, bp_ref, bn_ref)

    sh_rows = TM + 2 * HALO - SUBLANES
    for sh in range(1, SUBLANES):
        ashift[sh - 1] = abuf[sh:sh + sh_rows, :]
    rows = 32
    conv = []
    for r0 in range(0, TM, rows):
        acc = jnp.zeros((rows, CONV_W), F32)
        for t in range(CONV_K):
            s = r0 + t + HALO - CONV_K // 2
            sh = s % SUBLANES
            tap = abuf[s:s + rows, :] if sh == 0 else ashift[sh - 1, s - sh:s - sh + rows, :]
            acc = acc + tap * cw_ref[t:t + 1, :]
        conv.append(acc)
    ca = jnp.concatenate(conv, axis=0) + cb_ref[...]
    mu = jnp.mean(ca, axis=-1, keepdims=True)
    xc = ca - mu
    ln = xc * lax.rsqrt(jnp.mean(xc * xc, axis=-1, keepdims=True) + EPS) * lng_ref[...] + lnb_ref[...]
    br_a = _dot((ln * jax.nn.sigmoid(ln)).astype(BF), wco_ref[...])

    pos = j * TM + lax.broadcasted_iota(jnp.int32, (TM, 1), 0)
    seq_len = n_seq_tiles * TM
    mixed = []
    for g, w in enumerate(POOL_WINDOWS):
        lo, hi = g * POOL_G, (g + 1) * POOL_G
        acc = jnp.zeros((TM, POOL_G), F32)
        for o in range(-(w // 2), w - w // 2):
            acc = acc + bbuf[HALO + o:HALO + o + TM, lo:hi]
        cnt = jnp.clip(pos - w // 2 + w, 0, seq_len) - jnp.clip(pos - w // 2, 0, seq_len)
        pooled = acc / cnt.astype(F32) - bbuf[HALO:HALO + TM, lo:hi]
        mixed.append(_dot(pooled.astype(BF), pw_ref[g]))
    pb = jnp.concatenate(mixed, axis=-1) * ps_ref[...]
    br_b = _dot(pb.astype(BF), wpo_ref[...])

    is_prompt = i < n_p
    br_c = _dot(jnp.where(is_prompt, ocp_ref[...], ocs_ref[...]), wgo_ref[...])
    br_d = _dot(jnp.where(is_prompt, odp_ref[...], ods_ref[...]), wmo_ref[...])

    x = x_ref[...]
    h = _modulated_norm(x, g1_ref[...], mod_ref[0:1, :], mod_ref[1:2, :])
    gates = jax.nn.sigmoid(_dot(h.astype(BF), wg_ref[...]).astype(BF))
    merged = (gates[:, 0:D_MODEL] * br_a.astype(BF) + gates[:, D_MODEL:2 * D_MODEL] * br_b.astype(BF)
              + gates[:, 2 * D_MODEL:3 * D_MODEL] * br_c.astype(BF) + gates[:, 3 * D_MODEL:] * br_d.astype(BF))
    x = x + mod_ref[2:3, :] * _dot(merged, wo_ref[...])
    xo_ref[...] = x

    h2 = _modulated_norm(x, g2_ref[...], mod_ref[3:4, :], mod_ref[4:5, :])
    _route(h2, wr_ref, br_ref, ltri_ref, upper_ref, xs_ref, rl_ref, tw_ref, meta_ref)


def _merge(l, x, mod, geo, a, b, attn, p):
    nt, n_p = geo["nt"], geo["np"]
    tok = nt * TM
    seq_row = geo["seq_row"]
    hb = TM // HALO
    last_hb = nt * hb - 1
    tile = lambda w: pl.BlockSpec((TM, w), lambda i: (i, 0))
    prev = lambda w: pl.BlockSpec((HALO, w), lambda i: (jnp.maximum(i * hb - 1, 0), 0))
    nxt = lambda w: pl.BlockSpec((HALO, w), lambda i: (jnp.minimum((i + 1) * hb, last_hb), 0))
    p_tile = pl.BlockSpec((TM, 512), lambda i: (jnp.minimum(i, n_p - 1), 0))
    s_tile = pl.BlockSpec((TM, 512), lambda i: (jnp.maximum(i - n_p, 0), 0))
    return pl.pallas_call(
        functools.partial(_merge_kernel, geo),
        grid=(nt,),
        in_specs=[
            tile(D_MODEL),
            pl.BlockSpec((None, None, 6, D_MODEL), lambda i: (l, seq_row(i), 0, 0)),
            tile(CONV_W), prev(CONV_W), nxt(CONV_W),
            tile(POOL_W), prev(POOL_W), nxt(POOL_W),
            p_tile, p_tile, s_tile, s_tile,
            _layer_spec((1, D_MODEL), l),
            _layer_spec((D_MODEL, N_GATE), l),
            _layer_spec((CONV_K, CONV_W), l),
            _layer_spec((1, CONV_W), l),
            _layer_spec((1, CONV_W), l),
            _layer_spec((1, CONV_W), l),
            _layer_spec((CONV_W, D_MODEL), l),
            _layer_spec((len(POOL_WINDOWS), POOL_G, POOL_G), l),
            _layer_spec((1, POOL_W), l),
            _layer_spec((POOL_W, D_MODEL), l),
            _layer_spec((512, D_MODEL), l),
            _layer_spec((512, D_MODEL), l),
            _layer_spec((D_MODEL, D_MODEL), l),
            _layer_spec((1, D_MODEL), l),
            _layer_spec((D_MODEL, 2 * LANES), l),
            _layer_spec((1, LANES), l),
            _const_spec((TM, TM)),
            _const_spec((LANES, LANES)),
        ],
        out_specs=[tile(D_MODEL), pl.BlockSpec((R_TILE, D_MODEL), lambda i: (i, 0)), tile(LANES), tile(LANES),
                   pl.BlockSpec((None, 8, LANES), lambda i: (i, 0, 0))],
        out_shape=[jax.ShapeDtypeStruct((tok, D_MODEL), F32), jax.ShapeDtypeStruct((nt * R_TILE, D_MODEL), F32),
                   jax.ShapeDtypeStruct((tok, LANES), jnp.int32), jax.ShapeDtypeStruct((tok, LANES), F32),
                   jax.ShapeDtypeStruct((nt, 8, LANES), jnp.int32)],
        scratch_shapes=[pltpu.VMEM((TM + 2 * HALO, CONV_W), F32), pltpu.VMEM((TM + 2 * HALO, POOL_W), F32),
                        pltpu.VMEM((SUBLANES - 1, TM + 2 * HALO - SUBLANES, CONV_W), F32)],
        compiler_params=pltpu.CompilerParams(dimension_semantics=("arbitrary",), vmem_limit_bytes=VMEM_LIMIT),
        name="merge",
    )(x, mod, a, a, a, b, b, b, *attn,
      p["norm1_g"], p["wgate"], p["conv_dw"], p["conv_dw_b"], p["conv_ln_g"], p["conv_ln_b"], p["w_conv_out"],
      p["pool_w"], p["pool_scale"], p["w_pool_out"], p["w_gqa_out"], p["w_mla_out"], p["w_o"],
      p["norm2_g"], p["w_router"], p["b_router"], p["ltri"], p["upper"])


def _expert_kernel(nt, layer, blk_e, blk_row0, blk_t0, blk_t1, n_used_ref, pad8_ref, dst_ref, off_ref, tot8_ref,
                   next_e, xs_hbm, wgu_hbm, bg_ref, bu_ref, wd_hbm, bd_ref, perm_ref,
                   y_hbm, xg, yb, zbuf, wgu_buf, wd_buf, wg_bf, wu_bf, wd_bf, cnt_smem, gsem, osem, zsem, wsem):
    i = pl.program_id(0)
    n_used = n_used_ref[0]
    slot = i % 2
    last = pl.num_programs(0) - 1

    def for_pieces(b, fn):
        e = blk_e[b]
        b0 = blk_row0[b]

        def body(t, tot):
            j = t * N_EXPERTS + e
            run0 = dst_ref[j]
            lo = jnp.maximum(run0, b0)
            hi = jnp.minimum(run0 + pad8_ref[j], b0 + MOE_BM)
            n = pl.multiple_of(jnp.maximum(hi - lo, 0), 8)

            @pl.when(n > 0)
            def _():
                fn(pl.multiple_of(t * R_TILE + off_ref[j] + lo - run0, 8), pl.multiple_of(lo - b0, 8), n)

            return tot + n

        return lax.fori_loop(blk_t0[b], blk_t1[b], body, jnp.int32(0))

    def rows_copy(src, dst, sem, n):
        return pltpu.make_async_copy(src.at[pl.ds(0, n), :], dst.at[pl.ds(0, n), :], sem)

    def start_gather(b, s):
        def piece(row_t, row_b, n):
            pltpu.make_async_copy(xs_hbm.at[pl.ds(row_t, n), :], xg.at[s].at[pl.ds(row_b, n), :], gsem.at[s]).start()

        cnt_smem[s] = for_pieces(b, piece)

    def tail_copy(t):
        n = pl.multiple_of(R_TILE - tot8_ref[t], 8)
        return n, pltpu.make_async_copy(
            zbuf.at[pl.ds(0, n), :], y_hbm.at[pl.ds(pl.multiple_of(t * R_TILE + tot8_ref[t], 8), n), :], zsem)

    @pl.when(i == 0)
    def _():
        xg[...] = jnp.zeros_like(xg)
        zbuf[...] = jnp.zeros_like(zbuf)
        for s in range(4):
            cnt_smem[s] = 0

        def fill(t, c):
            n, cp = tail_copy(t)

            @pl.when(n > 0)
            def _():
                cp.start()

            return c

        lax.fori_loop(0, nt, fill, 0)

        @pl.when(n_used > 0)
        def _():
            start_gather(0, 0)

    @pl.when(i + 1 < n_used)
    def _():
        start_gather(i + 1, 1 - slot)

    @pl.when(i < n_used)
    def _():
        e_changed = jnp.logical_or(i == 0, blk_e[i] != blk_e[jnp.maximum(i - 1, 0)])

        def weight_copies(e):
            return (pltpu.make_async_copy(wgu_hbm.at[layer, e], wgu_buf, wsem.at[0]),
                    pltpu.make_async_copy(wd_hbm.at[layer, e], wd_buf, wsem.at[1]))

        @pl.when(e_changed)
        def _():
            e = blk_e[i]

            @pl.when(i == 0)
            def _():
                for cp in weight_copies(e):
                    cp.start()

            for cp in weight_copies(e):
                cp.wait()
            wd_bf[...] = wd_buf[...].astype(BF)
            for c in range(D_FF // LANES):
                pair = _dot(wgu_buf[:, 2 * c * LANES:2 * (c + 1) * LANES].astype(BF), perm_ref[...])
                wg_bf[:, c * LANES:(c + 1) * LANES] = pair[:, 0:LANES].astype(BF)
                wu_bf[:, c * LANES:(c + 1) * LANES] = pair[:, LANES:].astype(BF)
            e_next = next_e[e]

            @pl.when(e_next >= 0)
            def _():
                for cp in weight_copies(e_next):
                    cp.start()

        n_prev = pl.multiple_of(cnt_smem[2 + slot], 8)

        @pl.when(n_prev > 0)
        def _():
            rows_copy(yb.at[slot], y_hbm, osem.at[slot], n_prev).wait()

        n_in = pl.multiple_of(cnt_smem[slot], 8)
        rows_copy(xs_hbm, xg.at[slot], gsem.at[slot], n_in).wait()

        def expert_mlp(rows):
            xb = xg[slot, 0:rows, :].astype(BF)
            gate = jnp.minimum(_dot(xb, wg_bf[...]) + bg_ref[...], SWIGLU_LIMIT)
            up = jnp.clip(_dot(xb, wu_bf[...]) + bu_ref[...], -SWIGLU_LIMIT, SWIGLU_LIMIT)
            glu = gate * jax.nn.sigmoid(gate * SWIGLU_ALPHA)
            y = _dot(((up + 1.0) * glu).astype(BF), wd_bf[...]) + bd_ref[...]
            yb[slot, 0:rows, :] = y

        @pl.when(n_in > MOE_BM // 2)
        def _():
            expert_mlp(MOE_BM)

        @pl.when(n_in <= MOE_BM // 2)
        def _():
            expert_mlp(MOE_BM // 2)

        def piece(row_t, row_b, n):
            pltpu.make_async_copy(yb.at[slot].at[pl.ds(row_b, n), :], y_hbm.at[pl.ds(row_t, n), :],
                                  osem.at[slot]).start()

        cnt_smem[2 + slot] = for_pieces(i, piece)

    @pl.when(i == last)
    def _():
        for s in range(2):
            n_out = pl.multiple_of(cnt_smem[2 + s], 8)

            @pl.when(n_out > 0)
            def _():
                rows_copy(yb.at[s], y_hbm, osem.at[s], n_out).wait()

        def drain(t, c):
            n, cp = tail_copy(t)

            @pl.when(n > 0)
            def _():
                cp.wait()

            return c

        lax.fori_loop(0, nt, drain, 0)


def _experts(l, xs, plan, p, nt):
    n_blocks = plan["blk_e"].shape[0]
    n_pref = 10
    by_expert = lambda *lead: (lambda i, be, *_: lead + (be[i], 0, 0))
    b_spec = pl.BlockSpec((None, None, 1, D_FF), by_expert(l))
    grid_spec = pltpu.PrefetchScalarGridSpec(
        num_scalar_prefetch=n_pref,
        grid=(n_blocks,),
        in_specs=[
            pl.BlockSpec(memory_space=pl.ANY),
            pl.BlockSpec(memory_space=pl.ANY),
            b_spec, b_spec,
            pl.BlockSpec(memory_space=pl.ANY),
            pl.BlockSpec((None, None, 1, D_MODEL), by_expert(l)),
            pl.BlockSpec((2 * LANES, 2 * LANES), lambda i, *_: (0, 0)),
        ],
        out_specs=pl.BlockSpec(memory_space=pl.ANY),
        scratch_shapes=[
            pltpu.VMEM((2, MOE_BM, D_MODEL), F32),
            pltpu.VMEM((2, MOE_BM, D_MODEL), F32),
            pltpu.VMEM((R_TILE - TOP_K * TM, D_MODEL), F32),
            pltpu.VMEM((D_MODEL, 2 * D_FF), F32),
            pltpu.VMEM((D_FF, D_MODEL), F32),
            pltpu.VMEM((D_MODEL, D_FF), BF),
            pltpu.VMEM((D_MODEL, D_FF), BF),
            pltpu.VMEM((D_FF, D_MODEL), BF),
            pltpu.SMEM((4,), jnp.int32),
            pltpu.SemaphoreType.DMA((2,)),
            pltpu.SemaphoreType.DMA((2,)),
            pltpu.SemaphoreType.DMA,
            pltpu.SemaphoreType.DMA((2,)),
        ],
    )
    return pl.pallas_call(
        functools.partial(_expert_kernel, nt, l),
        grid_spec=grid_spec,
        out_shape=jax.ShapeDtypeStruct((nt * R_TILE, D_MODEL), F32),
        compiler_params=pltpu.CompilerParams(dimension_semantics=("arbitrary",), vmem_limit_bytes=VMEM_LIMIT),
        name="experts",
    )(plan["blk_e"], plan["blk_row0"], plan["blk_t0"], plan["blk_t1"], plan["n_used"], plan["pad8"], plan["dst"],
      plan["off"], plan["tot8"], plan["next_e"],
      xs, p["w_gu"], p["b_gate"], p["b_up"], p["w_dn"], p["b_dn"], p["pair_perm"])


def _pair_perm():
    m = np.zeros((2 * LANES, 2 * LANES), np.float32)
    j = np.arange(LANES)
    m[2 * j, j] = 1.0
    m[2 * j + 1, LANES + j] = 1.0
    return jnp.asarray(m, BF)


def _combine_kernel(final, y_ref, x_ref, mod_ref, rl_ref, tw_ref, fg_ref, o_ref):
    r_lane = lax.broadcasted_iota(jnp.int32, (TM, R_TILE), 1)
    rl = rl_ref[...]
    tw = tw_ref[...]
    sel = jnp.zeros((TM, R_TILE), F32)
    for k in range(TOP_K):
        sel = jnp.where(r_lane == rl[:, k:k + 1], tw[:, k:k + 1], sel)
    ffn = _dot(sel.astype(BF), y_ref[...].astype(BF))
    x = x_ref[...] + mod_ref[5:6, :] * ffn
    if final:
        x = _rms(x) * fg_ref[...]
    o_ref[...] = x


def _combine(l, final, y, x, mod, rl, tw, final_g, geo, tile0, n_tiles):
    seq_row = geo["seq_row"]
    tile = lambda w: pl.BlockSpec((TM, w), lambda i: (i + tile0, 0))
    return pl.pallas_call(
        functools.partial(_combine_kernel, final),
        grid=(n_tiles,),
        in_specs=[
            pl.BlockSpec((R_TILE, D_MODEL), lambda i: (i + tile0, 0)),
            tile(D_MODEL),
            pl.BlockSpec((None, None, 6, D_MODEL), lambda i: (l, seq_row(i + tile0), 0, 0)),
            tile(LANES),
            tile(LANES),
            _const_spec((1, D_MODEL)),
        ],
        out_specs=pl.BlockSpec((TM, D_MODEL), lambda i: (i, 0)),
        out_shape=jax.ShapeDtypeStruct((n_tiles * TM, D_MODEL), F32),
        compiler_params=pltpu.CompilerParams(dimension_semantics=("arbitrary",), vmem_limit_bytes=VMEM_LIMIT),
        name="combine",
    )(y, x, mod, rl, tw, final_g)


def _expert_plan(meta, nt):
    pad8 = meta[:, 0, :N_EXPERTS]
    off = meta[:, 1, :N_EXPERTS]
    ends = jnp.cumsum(pad8, axis=0)
    dst = ends - pad8
    tot = ends[-1]
    nb = (tot + MOE_BM - 1) // MOE_BM
    nb_end = jnp.cumsum(nb)
    n_blocks = (nt * TM * TOP_K + nt * N_EXPERTS * 7) // MOE_BM + N_EXPERTS
    b = jnp.arange(n_blocks, dtype=jnp.int32)
    blk_e = jnp.minimum(jnp.sum(nb_end[None, :] <= b[:, None], axis=1), N_EXPERTS - 1).astype(jnp.int32)
    blk_row0 = (b - (nb_end - nb)[blk_e]) * MOE_BM
    ends_b = ends[:, blk_e]
    dst_b = dst[:, blk_e]
    blk_t0 = jnp.sum(ends_b <= blk_row0[None, :], axis=0)
    blk_t1 = jnp.sum(dst_b < blk_row0[None, :] + MOE_BM, axis=0)
    i32 = lambda v: v.astype(jnp.int32)
    ids = jnp.arange(N_EXPERTS, dtype=jnp.int32)
    later = jnp.where((nb > 0)[None, :] & (ids[None, :] > ids[:, None]), ids[None, :], N_EXPERTS)
    next_e = jnp.min(later, axis=1)
    next_e = jnp.where(next_e == N_EXPERTS, -1, next_e)
    return {"next_e": i32(next_e), "blk_e": blk_e, "blk_row0": i32(blk_row0), "blk_t0": i32(blk_t0), "blk_t1": i32(blk_t1),
            "n_used": i32(nb_end[-1]).reshape(1), "pad8": i32(pad8.reshape(-1)), "dst": i32(dst.reshape(-1)),
            "off": i32(off.reshape(-1)), "tot8": i32(jnp.sum(pad8, axis=1))}


def _rope_tables(n_pos):
    pos = np.arange(n_pos)
    row, col = pos // GRID_W, pos % GRID_W
    lane = np.arange(LANES)

    def build(active, r, half):
        n_rot = 4 * half
        is_col = (r % n_rot) >= 2 * half
        rr = r % (2 * half)
        freq = np.power(ROPE_BASE, -(rr % half).astype(np.float64) / half)
        p = np.where(is_col[None, :], col[:, None], row[:, None]).astype(np.float64)
        ang = p * freq[None, :]
        first = rr < half
        cos = np.where(active[None, :], np.cos(ang), 1.0)
        sin_a = np.where((active & first)[None, :], -np.sin(ang), 0.0)
        sin_b = np.where((active & ~first)[None, :], np.sin(ang), 0.0)
        return [cos, sin_a, sin_b]

    tabs = build(np.ones(LANES, bool), lane % GQA_HEAD_DIM, GQA_HEAD_DIM // 4)
    in_rope = (lane >= MLA_NOPE) & (lane < MLA_NOPE + MLA_ROPE)
    tabs += build(in_rope, np.maximum(lane - MLA_NOPE, 0) % MLA_ROPE, MLA_ROPE // 4)
    tabs += build(lane < MLA_ROPE, lane % MLA_ROPE, MLA_ROPE // 4)
    table = np.concatenate(tabs, axis=1)
    ident = np.concatenate([np.ones((TM, LANES)), np.zeros((TM, LANES)), np.zeros((TM, LANES))] * 3, axis=1)
    return jnp.asarray(np.concatenate([ident, table], axis=0), F32)


def _placement():
    e = np.zeros((LANES, MLA_HEADS * LANES), np.float32)
    for h in range(MLA_HEADS):
        for r in range(MLA_ROPE):
            e[r, h * LANES + MLA_NOPE + r] = 1.0
    return jnp.asarray(e, BF)


def _split_hi_lo(w):
    hi = w.astype(BF)
    lo = (w - hi.astype(F32)).astype(BF)
    return jnp.concatenate([hi, lo], axis=-1)


def _block_diag_ones(n, g):
    idx = np.arange(n) // g
    return jnp.asarray((idx[:, None] == idx[None, :]).astype(np.float32), BF)


def kernel(x_prompt, x_sample, cache_gqa_k, cache_gqa_v, cache_mla_ckv, cache_mla_krope, c, c_ctx, norm1_g, norm2_g, w_mod, b_mod, w_in, conv_dw, conv_dw_b, conv_ln_g, conv_ln_b, w_conv_out, pool_w, pool_scale, w_pool_out, gqa_qn_g, gqa_kn_g, w_gqa_out, mla_qn_g, w_mla_q_up, mla_kvn_g, w_mla_kv_up, w_mla_out, w_o, w_router, b_router, w_gu, b_gu, w_dn, b_dn, final_g):
    bp, seq, d = x_prompt.shape
    bs, ls, _ = x_sample.shape
    depth = w_in.shape[0]
    past = cache_gqa_k.shape[2]
    assert seq == TM and d == D_MODEL and ls % TM == 0 and (bp * seq) % ls == 0
    tps = ls // TM
    n_p = bp
    nt = n_p + bs * tps
    geo = {
        "np": n_p, "bs": bs, "tps": tps, "nt": nt,
        "seq_row": lambda i: jnp.where(i < n_p, 0, 1 + (i - n_p) // tps),
        "rope_blk": lambda i: jnp.where(i < n_p, 0, 1 + (i - n_p) % tps),
    }

    n_cond = -(-(1 + bs) // 8) * 8
    cond = jnp.zeros((n_cond, d), F32).at[0].set(c_ctx).at[1:1 + bs].set(c)
    mod = _modulation(cond, w_mod, b_mod).reshape(depth, n_cond, 6, d)

    row = lambda v: v.reshape(depth, 1, -1)
    w1 = jnp.pad(w_in[:, :, :_SPLIT_GATE], ((0, 0), (0, 0), (0, W1_COLS - _SPLIT_GATE))).astype(BF)
    wqup = jnp.pad(w_mla_q_up.reshape(depth, MLA_Q_RANK, MLA_HEADS, MLA_NOPE + MLA_ROPE),
                   ((0, 0), (0, 0), (0, 0), (0, LANES - MLA_NOPE - MLA_ROPE)))
    wkv = w_mla_kv_up.reshape(depth, MLA_KV_RANK, MLA_HEADS, MLA_NOPE + MLA_V)
    wk_pad = jnp.pad(wkv[..., :MLA_NOPE], ((0, 0), (0, 0), (0, 0), (0, LANES - MLA_NOPE)))
    wkvup = jnp.concatenate([wk_pad.reshape(depth, MLA_KV_RANK, MLA_HEADS * LANES),
                             wkv[..., MLA_NOPE:].reshape(depth, MLA_KV_RANK, MLA_HEADS * MLA_V)], axis=-1)
    bgu = b_gu.reshape(depth, N_EXPERTS, 1, D_FF, 2)
    p = {
        "norm1_g": row(norm1_g), "norm2_g": row(norm2_g),
        "w1": w1, "wgate": w_in[:, :, _SPLIT_GATE:].astype(BF),
        "rope": _rope_tables(ls), "place": _placement(), "ones_bd": _block_diag_ones(512, GQA_HEAD_DIM),
        "ltri": jnp.asarray(np.tril(np.ones((TM, TM), np.float32), -1), BF),
        "upper": jnp.asarray(np.triu(np.ones((LANES, LANES), np.float32), 1), F32),
        "qn_g": row(jnp.tile(gqa_qn_g, (1, GQA_HEADS))), "kn_g": row(jnp.tile(gqa_kn_g, (1, GQA_KV_HEADS))),
        "cqn_g": row(mla_qn_g), "kvn_g": row(mla_kvn_g),
        "wqup": wqup.reshape(depth, MLA_Q_RANK, MLA_HEADS * LANES).astype(BF), "wkvup": wkvup.astype(BF),
        "conv_dw": conv_dw, "conv_dw_b": row(conv_dw_b), "conv_ln_g": row(conv_ln_g), "conv_ln_b": row(conv_ln_b),
        "w_conv_out": w_conv_out.astype(BF), "pool_w": pool_w.astype(BF), "pool_scale": row(pool_scale),
        "w_pool_out": w_pool_out.astype(BF), "w_gqa_out": w_gqa_out.astype(BF), "w_mla_out": w_mla_out.astype(BF),
        "w_o": w_o.astype(BF),
        "w_router": _split_hi_lo(jnp.pad(w_router, ((0, 0), (0, 0), (0, LANES - N_EXPERTS)))),
        "b_router": row(jnp.pad(b_router, ((0, 0), (0, LANES - N_EXPERTS)))),
        "w_gu": w_gu, "b_gate": bgu[..., 0], "b_up": bgu[..., 1], "pair_perm": _pair_perm(),
        "w_dn": w_dn, "b_dn": b_dn.reshape(depth, N_EXPERTS, 1, d),
    }

    ckm, cvm = _cache_prep(cache_mla_ckv, jnp.pad(cache_mla_krope, ((0, 0), (0, 0), (0, 0), (0, LANES - MLA_ROPE))), p)
    cache = (cache_gqa_k.reshape(bs, depth, past, LANES).astype(BF),
             cache_gqa_v.reshape(bs, depth, past, LANES).astype(BF), ckm, cvm)

    x = jnp.concatenate([x_prompt.reshape(bp * seq, d), x_sample.reshape(bs * ls, d)], axis=0)
    n_ptok = bp * seq
    states = []
    for l in range(depth):
        a, b, q, k, v, qm, km, vm, ks, vs, ckvs, krs = _proj(l, x, mod, geo, p)
        states.append((ks[:n_ptok], vs[:n_ptok], ckvs[:n_ptok], krs[:n_ptok]))
        attn = _attention(l, geo, q, k, v, qm, km, vm, cache)
        x_mid, xs, slot_row, top_w, meta = _merge(l, x, mod, geo, a, b, attn, p)
        y = _experts(l, xs, _expert_plan(meta, nt), p, nt)
        combine = functools.partial(_combine, l, l == depth - 1, y, x_mid, mod, slot_row, top_w,
                                    final_g.reshape(1, d), geo)
        if l < depth - 1:
            x = combine(0, nt)

    y_prompt = combine(0, n_p).reshape(bp, seq, d)
    y_sample = combine(n_p, nt - n_p).reshape(bs, ls, d)
    st = lambda j, shape: jnp.stack([s[j].reshape(shape) for s in states], axis=1)
    return (y_prompt, y_sample,
            st(0, (bp, seq, GQA_KV_HEADS, GQA_HEAD_DIM)), st(1, (bp, seq, GQA_KV_HEADS, GQA_HEAD_DIM)),
            st(2, (bp, seq, MLA_KV_RANK)), st(3, (bp, seq, MLA_ROPE)))
```

```python
import functools

import jax
import jax.numpy as jnp
import numpy as np
from jax import lax
from jax.experimental import pallas as pl
from jax.experimental.pallas import tpu as pltpu

D_MODEL = 1024
GRID_W = 64
CONV_W = 512
CONV_K = 31
POOL_W = 512
POOL_WINDOWS = (2, 4, 8, 16)
POOL_G = 128
GQA_HEADS = 8
GQA_KV_HEADS = 2
GQA_HEAD_DIM = 64
MLA_HEADS = 8
MLA_Q_RANK = 384
MLA_KV_RANK = 256
MLA_NOPE = 64
MLA_ROPE = 32
MLA_V = 64
ROPE_BASE = 10000.0
N_EXPERTS = 32
TOP_K = 4
D_FF = 1024
SWIGLU_LIMIT = 7.0
SWIGLU_ALPHA = 1.702
EPS = 1e-6
GQA_SCALE = GQA_HEAD_DIM ** -0.5
MLA_SCALE = (MLA_NOPE + MLA_ROPE) ** -0.5
LOG2E = 1.4426950408889634

LANES = 128
SUBLANES = 8
TM = 256
HALO = 16
MOE_BM = 1024
R_TILE = 1280
W1_COLS = 3072
N_GATE = 4 * D_MODEL
VMEM_LIMIT = 56 * 1024 * 1024

BF = jnp.bfloat16
F32 = jnp.float32

_C_A, _C_B, _C_Q, _C_K, _C_V, _C_CQ, _C_CKV, _C_KR = 0, 1024, 1536, 2048, 2176, 2304, 2688, 2944
_SPLIT_GATE = 2976


def _dot(a, b):
    return jnp.dot(a, b, preferred_element_type=F32)


def _dot_nt(a, b):
    return lax.dot_general(a, b, (((1,), (1,)), ((), ())), preferred_element_type=F32)


def _rms(x):
    return x * lax.rsqrt(jnp.mean(x * x, axis=-1, keepdims=True) + EPS)


def _group_mean_sq(x, ones_bd, width):
    xx = x * x
    hi = xx.astype(BF)
    lo = (xx - hi.astype(F32)).astype(BF)
    return (_dot(hi, ones_bd) + _dot(lo, ones_bd)) * (1.0 / width)


def _tile_lanes(t, width):
    reps = width // LANES
    return t if reps == 1 else jnp.concatenate([t] * reps, axis=-1)


def _rope(x, cos, sin_a, sin_b, shift):
    w = x.shape[-1]
    return (x * _tile_lanes(cos, w) + pltpu.roll(x, w - shift, 1) * _tile_lanes(sin_a, w)
            + pltpu.roll(x, shift, 1) * _tile_lanes(sin_b, w))


def _modulated_norm(x, g, shift, scale):
    return _rms(x) * g * (1.0 + scale) + shift


def _mod_kernel(cond_ref, w_ref, b_ref, o_ref):
    c = cond_ref[...]
    s = (c * jax.nn.sigmoid(c)).astype(BF)
    o_ref[...] = _dot(s, w_ref[...].astype(BF)) + b_ref[...]


def _modulation(cond, w_mod, b_mod):
    depth, d, n = w_mod.shape
    rows = cond.shape[0]
    return pl.pallas_call(
        _mod_kernel,
        grid=(depth, n // D_MODEL),
        in_specs=[
            pl.BlockSpec((rows, d), lambda l, j: (0, 0)),
            pl.BlockSpec((None, d, D_MODEL), lambda l, j: (l, 0, j)),
            pl.BlockSpec((None, 1, D_MODEL), lambda l, j: (l, 0, j)),
        ],
        out_specs=pl.BlockSpec((None, rows, D_MODEL), lambda l, j: (l, 0, j)),
        out_shape=jax.ShapeDtypeStruct((depth, rows, n), F32),
        name="modulation",
    )(cond, w_mod, b_mod.reshape(depth, 1, n))


def _proj_kernel(x_ref, mod_ref, g1_ref, w1_ref, rope_ref, qn_ref, kn_ref, cqn_ref, kvn_ref,
                 wqup_ref, wkvup_ref, place_ref, ones_ref,
                 a_ref, b_ref, q_ref, k_ref, v_ref, qm_ref, km_ref, vm_ref,
                 ks_ref, vs_ref, ckvs_ref, krs_ref):
    x = x_ref[...]
    h = _modulated_norm(x, g1_ref[...], mod_ref[0:1, :], mod_ref[1:2, :])
    y = _dot(h.astype(BF), w1_ref[...])

    a_ref[...] = (y[:, _C_A:_C_A + CONV_W] * jax.nn.sigmoid(y[:, _C_A + CONV_W:_C_B])).astype(BF)
    b_ref[...] = y[:, _C_B:_C_Q].astype(BF)

    def tab(j):
        return rope_ref[:, j * LANES:(j + 1) * LANES]

    q = y[:, _C_Q:_C_K]
    q = q * lax.rsqrt(_group_mean_sq(q, ones_ref[...], GQA_HEAD_DIM) + EPS) * qn_ref[...]
    q = _rope(q, tab(0), tab(1), tab(2), GQA_HEAD_DIM // 4)
    q_ref[...] = (q * (GQA_SCALE * LOG2E)).astype(BF)

    k = y[:, _C_K:_C_V]
    k = k * lax.rsqrt(_group_mean_sq(k, ones_ref[0:LANES, 0:LANES], GQA_HEAD_DIM) + EPS) * kn_ref[...]
    ks_ref[...] = k
    k_ref[...] = _rope(k, tab(0), tab(1), tab(2), GQA_HEAD_DIM // 4).astype(BF)

    v = y[:, _C_V:_C_CQ]
    vs_ref[...] = v
    v_ref[...] = v.astype(BF)

    cq = _rms(y[:, _C_CQ:_C_CKV]) * cqn_ref[...]
    qm = _dot(cq.astype(BF), wqup_ref[...])
    qm = _rope(qm, tab(3), tab(4), tab(5), MLA_ROPE // 4)
    qm_ref[...] = (qm * (MLA_SCALE * LOG2E)).astype(BF)

    ckv = _rms(y[:, _C_CKV:_C_KR]) * kvn_ref[...]
    ckvs_ref[...] = ckv
    kv = _dot(ckv.astype(BF), wkvup_ref[...])
    kr = y[:, _C_KR:W1_COLS]
    krs_ref[...] = kr[:, 0:MLA_ROPE]
    kr_rot = _rope(kr, tab(6), tab(7), tab(8), MLA_ROPE // 4)
    km = kv[:, 0:MLA_HEADS * LANES] + _dot(kr_rot.astype(BF), place_ref[...])
    km_ref[...] = km.astype(BF)
    vm_ref[...] = kv[:, MLA_HEADS * LANES:].astype(BF)


def _const_spec(shape):
    nd = len(shape)
    return pl.BlockSpec(shape, lambda *_: (0,) * nd)


def _layer_spec(shape, l):
    nd = len(shape)
    return pl.BlockSpec((None,) + shape, lambda *_: (l,) + (0,) * nd)


def _proj(l, x, mod, geo, p):
    nt = geo["nt"]
    tok = nt * TM
    seq_row, rope_blk = geo["seq_row"], geo["rope_blk"]

    def tile(width, dtype):
        return pl.BlockSpec((TM, width), lambda i: (i, 0)), jax.ShapeDtypeStruct((tok, width), dtype)

    outs = [tile(CONV_W, BF), tile(POOL_W, BF), tile(512, BF), tile(LANES, BF), tile(LANES, BF),
            tile(MLA_HEADS * LANES, BF), tile(MLA_HEADS * LANES, BF), tile(MLA_HEADS * MLA_V, BF),
            tile(LANES, F32), tile(LANES, F32), tile(MLA_KV_RANK, F32), tile(MLA_ROPE, F32)]
    return pl.pallas_call(
        _proj_kernel,
        grid=(nt,),
        in_specs=[
            pl.BlockSpec((TM, D_MODEL), lambda i: (i, 0)),
            pl.BlockSpec((None, None, 6, D_MODEL), lambda i: (l, seq_row(i), 0, 0)),
            _layer_spec((1, D_MODEL), l),
            _layer_spec((D_MODEL, W1_COLS), l),
            pl.BlockSpec((TM, 9 * LANES), lambda i: (rope_blk(i), 0)),
            _layer_spec((1, 512), l),
            _layer_spec((1, LANES), l),
            _layer_spec((1, MLA_Q_RANK), l),
            _layer_spec((1, MLA_KV_RANK), l),
            _layer_spec((MLA_Q_RANK, MLA_HEADS * LANES), l),
            _layer_spec((MLA_KV_RANK, MLA_HEADS * (LANES + MLA_V)), l),
            _const_spec((LANES, MLA_HEADS * LANES)),
            _const_spec((512, 512)),
        ],
        out_specs=[o[0] for o in outs],
        out_shape=[o[1] for o in outs],
        compiler_params=pltpu.CompilerParams(dimension_semantics=("arbitrary",), vmem_limit_bytes=VMEM_LIMIT),
        name="proj",
    )(x, mod, p["norm1_g"], p["w1"], p["rope"], p["qn_g"], p["kn_g"], p["cqn_g"], p["kvn_g"],
      p["wqup"], p["wkvup"], p["place"], p["ones_bd"])


def _cache_kernel(ckv_ref, kr_ref, wkvup_ref, place_ref, km_ref, vm_ref):
    kv = _dot(ckv_ref[...].astype(BF), wkvup_ref[...])
    km = kv[:, 0:MLA_HEADS * LANES] + _dot(kr_ref[...].astype(BF), place_ref[...])
    km_ref[...] = km.astype(BF)
    vm_ref[...] = kv[:, MLA_HEADS * LANES:].astype(BF)


def _cache_prep(ckv, kr_pad, p):
    bs, depth, past, _ = ckv.shape
    return pl.pallas_call(
        _cache_kernel,
        grid=(bs, depth),
        in_specs=[
            pl.BlockSpec((None, None, past, MLA_KV_RANK), lambda b, l: (b, l, 0, 0)),
            pl.BlockSpec((None, None, past, LANES), lambda b, l: (b, l, 0, 0)),
            pl.BlockSpec((None, MLA_KV_RANK, MLA_HEADS * (LANES + MLA_V)), lambda b, l: (l, 0, 0)),
            pl.BlockSpec((LANES, MLA_HEADS * LANES), lambda b, l: (0, 0)),
        ],
        out_specs=[
            pl.BlockSpec((None, None, past, MLA_HEADS * LANES), lambda b, l: (b, l, 0, 0)),
            pl.BlockSpec((None, None, past, MLA_HEADS * MLA_V), lambda b, l: (b, l, 0, 0)),
        ],
        out_shape=[jax.ShapeDtypeStruct((bs, depth, past, MLA_HEADS * LANES), BF),
                   jax.ShapeDtypeStruct((bs, depth, past, MLA_HEADS * MLA_V), BF)],
        name="cache_prep",
    )(ckv, kr_pad, p["wkvup"], p["place"])


def _attend(qs, k, v):
    s = _dot_nt(qs, k)
    m = jnp.max(s, axis=-1, keepdims=True)
    e = jnp.exp2(s - m)
    l = jnp.sum(e, axis=-1, keepdims=True)
    return _dot(e.astype(BF), v) / l


def _attention_body(q, k, v, qm, km, vm, oc_ref, od_ref):
    lane_k = lax.broadcasted_iota(jnp.int32, k.shape, 1)
    lane_q = lax.broadcasted_iota(jnp.int32, (TM, LANES), 1)
    lo_k = lane_k < GQA_HEAD_DIM
    lo_q = lane_q < GQA_HEAD_DIM
    k32, v32 = k.astype(F32), v.astype(F32)
    k_sw = pltpu.roll(k32, GQA_HEAD_DIM, 1)
    v_sw = pltpu.roll(v32, GQA_HEAD_DIM, 1)
    k_dup = [jnp.where(lo_k, k32, k_sw).astype(BF), jnp.where(lo_k, k_sw, k32).astype(BF)]
    v_dup = [jnp.where(lo_k, v32, v_sw).astype(BF), jnp.where(lo_k, v_sw, v32).astype(BF)]
    zero = jnp.zeros((TM, LANES), BF)
    group = GQA_HEADS // GQA_KV_HEADS
    for j in range(GQA_HEADS // 2):
        qs = q[:, j * LANES:(j + 1) * LANES]
        g = (2 * j) // group
        o_lo = _attend(jnp.where(lo_q, qs, zero), k_dup[g], v_dup[g])
        o_hi = _attend(jnp.where(lo_q, zero, qs), k_dup[g], v_dup[g])
        oc_ref[:, j * LANES:(j + 1) * LANES] = jnp.where(lo_q, o_lo, o_hi).astype(BF)
    for j in range(MLA_HEADS // 2):
        vs = vm[:, j * LANES:(j + 1) * LANES]
        outs = []
        for h in (2 * j, 2 * j + 1):
            outs.append(_attend(qm[:, h * LANES:(h + 1) * LANES], km[:, h * LANES:(h + 1) * LANES], vs))
        od_ref[:, j * LANES:(j + 1) * LANES] = jnp.where(lo_q, outs[0], outs[1]).astype(BF)


def _attn_prompt_kernel(q_ref, k_ref, v_ref, qm_ref, km_ref, vm_ref, oc_ref, od_ref):
    _attention_body(q_ref[...], k_ref[...], v_ref[...], qm_ref[...], km_ref[...], vm_ref[...], oc_ref, od_ref)


def _attn_sample_kernel(q_ref, k_ref, v_ref, qm_ref, km_ref, vm_ref, ck_ref, cv_ref, ckm_ref, cvm_ref,
                        oc_ref, od_ref):
    k = jnp.concatenate([ck_ref[...], k_ref[...]], axis=0)
    v = jnp.concatenate([cv_ref[...], v_ref[...]], axis=0)
    km = jnp.concatenate([ckm_ref[...], km_ref[...]], axis=0)
    vm = jnp.concatenate([cvm_ref[...], vm_ref[...]], axis=0)
    _attention_body(q_ref[...], k, v, qm_ref[...], km, vm, oc_ref, od_ref)


def _attention(l, geo, q, k, v, qm, km, vm, cache):
    n_p, bs, tps, nt = geo["np"], geo["bs"], geo["tps"], geo["nt"]
    tok = nt * TM
    ls = tps * TM
    off = (n_p * TM) // ls
    widths = (512, LANES, LANES, MLA_HEADS * LANES, MLA_HEADS * LANES, MLA_HEADS * MLA_V)
    params = pltpu.CompilerParams(dimension_semantics=("arbitrary",), vmem_limit_bytes=VMEM_LIMIT)
    oc_p, od_p = pl.pallas_call(
        _attn_prompt_kernel,
        grid=(n_p,),
        in_specs=[pl.BlockSpec((TM, w), lambda i: (i, 0)) for w in widths],
        out_specs=[pl.BlockSpec((TM, 512), lambda i: (i, 0))] * 2,
        out_shape=[jax.ShapeDtypeStruct((n_p * TM, 512), BF)] * 2,
        compiler_params=params,
        name="attn_prompt",
    )(q, k, v, qm, km, vm)
    ck, cv, ckm, cvm = cache
    past = ck.shape[2]
    q_spec = lambda w: pl.BlockSpec((TM, w), lambda b, j: (n_p + b * tps + j, 0))
    kv_spec = lambda w: pl.BlockSpec((ls, w), lambda b, j: (off + b, 0))
    c_spec = lambda w: pl.BlockSpec((None, None, past, w), lambda b, j: (b, l, 0, 0))
    params2 = pltpu.CompilerParams(dimension_semantics=("arbitrary", "arbitrary"), vmem_limit_bytes=VMEM_LIMIT)
    o_spec = pl.BlockSpec((TM, 512), lambda b, j: (b * tps + j, 0))
    oc_s, od_s = pl.pallas_call(
        _attn_sample_kernel,
        grid=(bs, tps),
        in_specs=[q_spec(512), kv_spec(LANES), kv_spec(LANES), q_spec(MLA_HEADS * LANES),
                  kv_spec(MLA_HEADS * LANES), kv_spec(MLA_HEADS * MLA_V),
                  c_spec(LANES), c_spec(LANES), c_spec(MLA_HEADS * LANES), c_spec(MLA_HEADS * MLA_V)],
        out_specs=[o_spec, o_spec],
        out_shape=[jax.ShapeDtypeStruct((bs * ls, 512), BF)] * 2,
        compiler_params=params2,
        name="attn_sample",
    )(q, k, v, qm, km, vm, ck, cv, ckm, cvm)
    return oc_p, od_p, oc_s, od_s


def _route(h2, wr_ref, br_ref, ltri_ref, upper_ref, xs_ref, rl_ref, tw_ref, meta_ref):
    h2_hi = h2.astype(BF)
    h2_lo = (h2 - h2_hi.astype(F32)).astype(BF)
    hi_terms = _dot(h2_hi, wr_ref[...])
    logits = (hi_terms[:, 0:LANES] + hi_terms[:, LANES:] + _dot(h2_lo, wr_ref[:, 0:LANES])) + br_ref[...]
    lane = lax.broadcasted_iota(jnp.int32, (TM, LANES), 1).astype(F32)
    neg = jnp.float32(-jnp.inf)
    lg = jnp.where(lane < N_EXPERTS, logits, neg)
    vals, idxs = [], []
    for _ in range(TOP_K):
        m = jnp.max(lg, axis=-1, keepdims=True)
        idx = jnp.min(jnp.where(lg == m, lane, float(LANES)), axis=-1, keepdims=True)
        vals.append(m)
        idxs.append(idx)
        lg = jnp.where(lane == idx, neg, lg)
    exps = [jnp.exp(v - vals[0]) for v in vals]
    denom = exps[0] + exps[1] + exps[2] + exps[3]

    onehot = [(lane == idxs[kk]).astype(F32) for kk in range(TOP_K)]
    colsum = [jnp.sum(o, axis=0, keepdims=True) for o in onehot]
    cnt = colsum[0] + colsum[1] + colsum[2] + colsum[3]
    pad8 = jnp.floor((cnt + 7.0) * 0.125) * 8.0
    run_off = jnp.dot(jnp.broadcast_to(pad8, (8, LANES)), upper_ref[...], preferred_element_type=F32,
                      precision=lax.Precision.HIGHEST)[0:1, :]
    base = jnp.zeros((1, LANES), F32)
    rloc = []
    for kk in range(TOP_K):
        before = _dot(ltri_ref[...], onehot[kk].astype(BF))
        rloc.append(jnp.sum(onehot[kk] * (run_off + base + before), axis=-1, keepdims=True))
        base = base + colsum[kk]

    eye = (lax.broadcasted_iota(jnp.int32, (TM, TM), 0) == lax.broadcasted_iota(jnp.int32, (TM, TM), 1))
    r_sub = lax.broadcasted_iota(jnp.int32, (R_TILE, TM), 0).astype(F32)
    ones8 = jnp.ones((8, TM), BF)
    sel = jnp.zeros((R_TILE, TM), F32)
    for kk in range(TOP_K):
        hi = jnp.floor(rloc[kk] * (1.0 / TM))
        lo = rloc[kk] - hi * TM
        row = (_dot(ones8, jnp.where(eye, hi, 0.0).astype(BF)) * TM
               + _dot(ones8, jnp.where(eye, lo, 0.0).astype(BF)))[0:1, :]
        sel = jnp.where(r_sub == row, 1.0, sel)
    xs_ref[...] = _dot(sel.astype(BF), h2.astype(BF))

    rl = jnp.zeros((TM, LANES), F32)
    tw = jnp.zeros((TM, LANES), F32)
    for kk in range(TOP_K):
        rl = jnp.where(lane == kk, rloc[kk], rl)
        tw = jnp.where(lane == kk, exps[kk] / denom, tw)
    rl_ref[...] = rl.astype(jnp.int32)
    tw_ref[...] = tw
    sub = lax.broadcasted_iota(jnp.int32, (8, LANES), 0)
    meta = jnp.where(sub == 0, jnp.broadcast_to(pad8, (8, LANES)),
                     jnp.where(sub == 1, jnp.broadcast_to(run_off, (8, LANES)), 0.0))
    meta_ref[...] = meta.astype(jnp.int32)


def _merge_kernel(geo, x_ref, mod_ref, a_ref, ap_ref, an_ref, b_ref, bp_ref, bn_ref,
                  ocp_ref, odp_ref, ocs_ref, ods_ref,
                  g1_ref, wg_ref, cw_ref, cb_ref, lng_ref, lnb_ref, wco_ref, pw_ref, ps_ref, wpo_ref,
                  wgo_ref, wmo_ref, wo_ref, g2_ref, wr_ref, br_ref,
                  ltri_ref, upper_ref,
                  xo_ref, xs_ref, rl_ref, tw_ref, meta_ref, abuf, bbuf, ashift):
    n_p, tps = geo["np"], geo["tps"]
    i = pl.program_id(0)
    j = jnp.where(i < n_p, 0, (i - n_p) % tps)
    n_seq_tiles = jnp.where(i < n_p, 1, tps)
    has_prev = j > 0
    has_next = j < n_seq_tiles - 1

    def fill(buf, cur, prev, nxt):
        buf[0:HALO, :] = jnp.where(has_prev, prev[...].astype(F32), 0.0)
        buf[HALO:HALO + TM, :] = cur[...].astype(F32)
        buf[HALO + TM:, :] = jnp.where(has_next, nxt[...].astype(F32), 0.0)

    fill(abuf, a_ref, ap_ref, an_ref)
    fill(bbuf, b_ref, bp_ref, bn_ref)

    sh_rows = TM + 2 * HALO - SUBLANES
    for sh in range(1, SUBLANES):
        ashift[sh - 1] = abuf[sh:sh + sh_rows, :]
    rows = 32
    conv = []
    for r0 in range(0, TM, rows):
        acc = jnp.zeros((rows, CONV_W), F32)
        for t in range(CONV_K):
            s = r0 + t + HALO - CONV_K // 2
            sh = s % SUBLANES
            tap = abuf[s:s + rows, :] if sh == 0 else ashift[sh - 1, s - sh:s - sh + rows, :]
            acc = acc + tap * cw_ref[t:t + 1, :]
        conv.append(acc)
    ca = jnp.concatenate(conv, axis=0) + cb_ref[...]
    mu = jnp.mean(ca, axis=-1, keepdims=True)
    xc = ca - mu
    ln = xc * lax.rsqrt(jnp.mean(xc * xc, axis=-1, keepdims=True) + EPS) * lng_ref[...] + lnb_ref[...]
    br_a = _dot((ln * jax.nn.sigmoid(ln)).astype(BF), wco_ref[...])

    pos = j * TM + lax.broadcasted_iota(jnp.int32, (TM, 1), 0)
    seq_len = n_seq_tiles * TM
    mixed = []
    for g, w in enumerate(POOL_WINDOWS):
        lo, hi = g * POOL_G, (g + 1) * POOL_G
        acc = jnp.zeros((TM, POOL_G), F32)
        for o in range(-(w // 2), w - w // 2):
            acc = acc + bbuf[HALO + o:HALO + o + TM, lo:hi]
        cnt = jnp.clip(pos - w // 2 + w, 0, seq_len) - jnp.clip(pos - w // 2, 0, seq_len)
        pooled = acc / cnt.astype(F32) - bbuf[HALO:HALO + TM, lo:hi]
        mixed.append(_dot(pooled.astype(BF), pw_ref[g]))
    pb = jnp.concatenate(mixed, axis=-1) * ps_ref[...]
    br_b = _dot(pb.astype(BF), wpo_ref[...])

    is_prompt = i < n_p
    br_c = _dot(jnp.where(is_prompt, ocp_ref[...], ocs_ref[...]), wgo_ref[...])
    br_d = _dot(jnp.where(is_prompt, odp_ref[...], ods_ref[...]), wmo_ref[...])

    x = x_ref[...]
    h = _modulated_norm(x, g1_ref[...], mod_ref[0:1, :], mod_ref[1:2, :])
    gates = jax.nn.sigmoid(_dot(h.astype(BF), wg_ref[...]).astype(BF))
    merged = (gates[:, 0:D_MODEL] * br_a.astype(BF) + gates[:, D_MODEL:2 * D_MODEL] * br_b.astype(BF)
              + gates[:, 2 * D_MODEL:3 * D_MODEL] * br_c.astype(BF) + gates[:, 3 * D_MODEL:] * br_d.astype(BF))
    x = x + mod_ref[2:3, :] * _dot(merged, wo_ref[...])
    xo_ref[...] = x

    h2 = _modulated_norm(x, g2_ref[...], mod_ref[3:4, :], mod_ref[4:5, :])
    _route(h2, wr_ref, br_ref, ltri_ref, upper_ref, xs_ref, rl_ref, tw_ref, meta_ref)


def _merge(l, x, mod, geo, a, b, attn, p):
    nt, n_p = geo["nt"], geo["np"]
    tok = nt * TM
    seq_row = geo["seq_row"]
    hb = TM // HALO
    last_hb = nt * hb - 1
    tile = lambda w: pl.BlockSpec((TM, w), lambda i: (i, 0))
    prev = lambda w: pl.BlockSpec((HALO, w), lambda i: (jnp.maximum(i * hb - 1, 0), 0))
    nxt = lambda w: pl.BlockSpec((HALO, w), lambda i: (jnp.minimum((i + 1) * hb, last_hb), 0))
    p_tile = pl.BlockSpec((TM, 512), lambda i: (jnp.minimum(i, n_p - 1), 0))
    s_tile = pl.BlockSpec((TM, 512), lambda i: (jnp.maximum(i - n_p, 0), 0))
    return pl.pallas_call(
        functools.partial(_merge_kernel, geo),
        grid=(nt,),
        in_specs=[
            tile(D_MODEL),
            pl.BlockSpec((None, None, 6, D_MODEL), lambda i: (l, seq_row(i), 0, 0)),
            tile(CONV_W), prev(CONV_W), nxt(CONV_W),
            tile(POOL_W), prev(POOL_W), nxt(POOL_W),
            p_tile, p_tile, s_tile, s_tile,
            _layer_spec((1, D_MODEL), l),
            _layer_spec((D_MODEL, N_GATE), l),
            _layer_spec((CONV_K, CONV_W), l),
            _layer_spec((1, CONV_W), l),
            _layer_spec((1, CONV_W), l),
            _layer_spec((1, CONV_W), l),
            _layer_spec((CONV_W, D_MODEL), l),
            _layer_spec((len(POOL_WINDOWS), POOL_G, POOL_G), l),
            _layer_spec((1, POOL_W), l),
            _layer_spec((POOL_W, D_MODEL), l),
            _layer_spec((512, D_MODEL), l),
            _layer_spec((512, D_MODEL), l),
            _layer_spec((D_MODEL, D_MODEL), l),
            _layer_spec((1, D_MODEL), l),
            _layer_spec((D_MODEL, 2 * LANES), l),
            _layer_spec((1, LANES), l),
            _const_spec((TM, TM)),
            _const_spec((LANES, LANES)),
        ],
        out_specs=[tile(D_MODEL), pl.BlockSpec((R_TILE, D_MODEL), lambda i: (i, 0)), tile(LANES), tile(LANES),
                   pl.BlockSpec((None, 8, LANES), lambda i: (i, 0, 0))],
        out_shape=[jax.ShapeDtypeStruct((tok, D_MODEL), F32), jax.ShapeDtypeStruct((nt * R_TILE, D_MODEL), F32),
                   jax.ShapeDtypeStruct((tok, LANES), jnp.int32), jax.ShapeDtypeStruct((tok, LANES), F32),
                   jax.ShapeDtypeStruct((nt, 8, LANES), jnp.int32)],
        scratch_shapes=[pltpu.VMEM((TM + 2 * HALO, CONV_W), F32), pltpu.VMEM((TM + 2 * HALO, POOL_W), F32),
                        pltpu.VMEM((SUBLANES - 1, TM + 2 * HALO - SUBLANES, CONV_W), F32)],
        compiler_params=pltpu.CompilerParams(dimension_semantics=("arbitrary",), vmem_limit_bytes=VMEM_LIMIT),
        name="merge",
    )(x, mod, a, a, a, b, b, b, *attn,
      p["norm1_g"], p["wgate"], p["conv_dw"], p["conv_dw_b"], p["conv_ln_g"], p["conv_ln_b"], p["w_conv_out"],
      p["pool_w"], p["pool_scale"], p["w_pool_out"], p["w_gqa_out"], p["w_mla_out"], p["w_o"],
      p["norm2_g"], p["w_router"], p["b_router"], p["ltri"], p["upper"])


def _expert_kernel(nt, layer, blk_e, blk_row0, blk_t0, blk_t1, n_used_ref, pad8_ref, dst_ref, off_ref, tot8_ref,
                   next_e, xs_hbm, wgu_hbm, bg_ref, bu_ref, wd_hbm, bd_ref, perm_ref,
                   y_hbm, xg, yb, zbuf, wgu_buf, wd_buf, wg_bf, wu_bf, wd_bf, cnt_smem, gsem, osem, zsem, wsem):
    i = pl.program_id(0)
    n_used = n_used_ref[0]
    slot = i % 2
    last = pl.num_programs(0) - 1

    def for_pieces(b, fn):
        e = blk_e[b]
        b0 = blk_row0[b]

        def body(t, tot):
            j = t * N_EXPERTS + e
            run0 = dst_ref[j]
            lo = jnp.maximum(run0, b0)
            hi = jnp.minimum(run0 + pad8_ref[j], b0 + MOE_BM)
            n = pl.multiple_of(jnp.maximum(hi - lo, 0), 8)

            @pl.when(n > 0)
            def _():
                fn(pl.multiple_of(t * R_TILE + off_ref[j] + lo - run0, 8), pl.multiple_of(lo - b0, 8), n)

            return tot + n

        return lax.fori_loop(blk_t0[b], blk_t1[b], body, jnp.int32(0))

    def rows_copy(src, dst, sem, n):
        return pltpu.make_async_copy(src.at[pl.ds(0, n), :], dst.at[pl.ds(0, n), :], sem)

    def start_gather(b, s):
        def piece(row_t, row_b, n):
            pltpu.make_async_copy(xs_hbm.at[pl.ds(row_t, n), :], xg.at[s].at[pl.ds(row_b, n), :], gsem.at[s]).start()

        cnt_smem[s] = for_pieces(b, piece)

    def tail_copy(t):
        n = pl.multiple_of(R_TILE - tot8_ref[t], 8)
        return n, pltpu.make_async_copy(
            zbuf.at[pl.ds(0, n), :], y_hbm.at[pl.ds(pl.multiple_of(t * R_TILE + tot8_ref[t], 8), n), :], zsem)

    @pl.when(i == 0)
    def _():
        xg[...] = jnp.zeros_like(xg)
        zbuf[...] = jnp.zeros_like(zbuf)
        for s in range(4):
            cnt_smem[s] = 0

        def fill(t, c):
            n, cp = tail_copy(t)

            @pl.when(n > 0)
            def _():
                cp.start()

            return c

        lax.fori_loop(0, nt, fill, 0)

        @pl.when(n_used > 0)
        def _():
            start_gather(0, 0)

    @pl.when(i + 1 < n_used)
    def _():
        start_gather(i + 1, 1 - slot)

    @pl.when(i < n_used)
    def _():
        e_changed = jnp.logical_or(i == 0, blk_e[i] != blk_e[jnp.maximum(i - 1, 0)])

        def weight_copies(e):
            return (pltpu.make_async_copy(wgu_hbm.at[layer, e], wgu_buf, wsem.at[0]),
                    pltpu.make_async_copy(wd_hbm.at[layer, e], wd_buf, wsem.at[1]))

        @pl.when(e_changed)
        def _():
            e = blk_e[i]

            @pl.when(i == 0)
            def _():
                for cp in weight_copies(e):
                    cp.start()

            for cp in weight_copies(e):
                cp.wait()
            wd_bf[...] = wd_buf[...].astype(BF)
            for c in range(D_FF // LANES):
                pair = _dot(wgu_buf[:, 2 * c * LANES:2 * (c + 1) * LANES].astype(BF), perm_ref[...])
                wg_bf[:, c * LANES:(c + 1) * LANES] = pair[:, 0:LANES].astype(BF)
                wu_bf[:, c * LANES:(c + 1) * LANES] = pair[:, LANES:].astype(BF)
            e_next = next_e[e]

            @pl.when(e_next >= 0)
            def _():
                for cp in weight_copies(e_next):
                    cp.start()

        n_prev = pl.multiple_of(cnt_smem[2 + slot], 8)

        @pl.when(n_prev > 0)
        def _():
            rows_copy(yb.at[slot], y_hbm, osem.at[slot], n_prev).wait()

        n_in = pl.multiple_of(cnt_smem[slot], 8)
        rows_copy(xs_hbm, xg.at[slot], gsem.at[slot], n_in).wait()

        def expert_mlp(rows):
            xb = xg[slot, 0:rows, :].astype(BF)
            gate = jnp.minimum(_dot(xb, wg_bf[...]) + bg_ref[...], SWIGLU_LIMIT)
            up = jnp.clip(_dot(xb, wu_bf[...]) + bu_ref[...], -SWIGLU_LIMIT, SWIGLU_LIMIT)
            glu = gate * jax.nn.sigmoid(gate * SWIGLU_ALPHA)
            y = _dot(((up + 1.0) * glu).astype(BF), wd_bf[...]) + bd_ref[...]
            yb[slot, 0:rows, :] = y

        @pl.when(n_in > MOE_BM // 2)
        def _():
            expert_mlp(MOE_BM)

        @pl.when(n_in <= MOE_BM // 2)
        def _():
            expert_mlp(MOE_BM // 2)

        def piece(row_t, row_b, n):
            pltpu.make_async_copy(yb.at[slot].at[pl.ds(row_b, n), :], y_hbm.at[pl.ds(row_t, n), :],
                                  osem.at[slot]).start()

        cnt_smem[2 + slot] = for_pieces(i, piece)

    @pl.when(i == last)
    def _():
        for s in range(2):
            n_out = pl.multiple_of(cnt_smem[2 + s], 8)

            @pl.when(n_out > 0)
            def _():
                rows_copy(yb.at[s], y_hbm, osem.at[s], n_out).wait()

        def drain(t, c):
            n, cp = tail_copy(t)

            @pl.when(n > 0)
            def _():
                cp.wait()

            return c

        lax.fori_loop(0, nt, drain, 0)


def _experts(l, xs, plan, p, nt):
    n_blocks = plan["blk_e"].shape[0]
    n_pref = 10
    by_expert = lambda *lead: (lambda i, be, *_: lead + (be[i], 0, 0))
    b_spec = pl.BlockSpec((None, None, 1, D_FF), by_expert(l))
    grid_spec = pltpu.PrefetchScalarGridSpec(
        num_scalar_prefetch=n_pref,
        grid=(n_blocks,),
        in_specs=[
            pl.BlockSpec(memory_space=pl.ANY),
            pl.BlockSpec(memory_space=pl.ANY),
            b_spec, b_spec,
            pl.BlockSpec(memory_space=pl.ANY),
            pl.BlockSpec((None, None, 1, D_MODEL), by_expert(l)),
            pl.BlockSpec((2 * LANES, 2 * LANES), lambda i, *_: (0, 0)),
        ],
        out_specs=pl.BlockSpec(memory_space=pl.ANY),
        scratch_shapes=[
            pltpu.VMEM((2, MOE_BM, D_MODEL), F32),
            pltpu.VMEM((2, MOE_BM, D_MODEL), F32),
            pltpu.VMEM((R_TILE - TOP_K * TM, D_MODEL), F32),
            pltpu.VMEM((D_MODEL, 2 * D_FF), F32),
            pltpu.VMEM((D_FF, D_MODEL), F32),
            pltpu.VMEM((D_MODEL, D_FF), BF),
            pltpu.VMEM((D_MODEL, D_FF), BF),
            pltpu.VMEM((D_FF, D_MODEL), BF),
            pltpu.SMEM((4,), jnp.int32),
            pltpu.SemaphoreType.DMA((2,)),
            pltpu.SemaphoreType.DMA((2,)),
            pltpu.SemaphoreType.DMA,
            pltpu.SemaphoreType.DMA((2,)),
        ],
    )
    return pl.pallas_call(
        functools.partial(_expert_kernel, nt, l),
        grid_spec=grid_spec,
        out_shape=jax.ShapeDtypeStruct((nt * R_TILE, D_MODEL), F32),
        compiler_params=pltpu.CompilerParams(dimension_semantics=("arbitrary",), vmem_limit_bytes=VMEM_LIMIT),
        name="experts",
    )(plan["blk_e"], plan["blk_row0"], plan["blk_t0"], plan["blk_t1"], plan["n_used"], plan["pad8"], plan["dst"],
      plan["off"], plan["tot8"], plan["next_e"],
      xs, p["w_gu"], p["b_gate"], p["b_up"], p["w_dn"], p["b_dn"], p["pair_perm"])


def _pair_perm():
    m = np.zeros((2 * LANES, 2 * LANES), np.float32)
    j = np.arange(LANES)
    m[2 * j, j] = 1.0
    m[2 * j + 1, LANES + j] = 1.0
    return jnp.asarray(m, BF)


def _combine_kernel(final, y_ref, x_ref, mod_ref, rl_ref, tw_ref, fg_ref, o_ref):
    r_lane = lax.broadcasted_iota(jnp.int32, (TM, R_TILE), 1)
    rl = rl_ref[...]
    tw = tw_ref[...]
    sel = jnp.zeros((TM, R_TILE), F32)
    for k in range(TOP_K):
        sel = jnp.where(r_lane == rl[:, k:k + 1], tw[:, k:k + 1], sel)
    ffn = _dot(sel.astype(BF), y_ref[...].astype(BF))
    x = x_ref[...] + mod_ref[5:6, :] * ffn
    if final:
        x = _rms(x) * fg_ref[...]
    o_ref[...] = x


def _combine(l, final, y, x, mod, rl, tw, final_g, geo, tile0, n_tiles):
    seq_row = geo["seq_row"]
    tile = lambda w: pl.BlockSpec((TM, w), lambda i: (i + tile0, 0))
    return pl.pallas_call(
        functools.partial(_combine_kernel, final),
        grid=(n_tiles,),
        in_specs=[
            pl.BlockSpec((R_TILE, D_MODEL), lambda i: (i + tile0, 0)),
            tile(D_MODEL),
            pl.BlockSpec((None, None, 6, D_MODEL), lambda i: (l, seq_row(i + tile0), 0, 0)),
            tile(LANES),
            tile(LANES),
            _const_spec((1, D_MODEL)),
        ],
        out_specs=pl.BlockSpec((TM, D_MODEL), lambda i: (i, 0)),
        out_shape=jax.ShapeDtypeStruct((n_tiles * TM, D_MODEL), F32),
        compiler_params=pltpu.CompilerParams(dimension_semantics=("arbitrary",), vmem_limit_bytes=VMEM_LIMIT),
        name="combine",
    )(y, x, mod, rl, tw, final_g)


def _expert_plan(meta, nt):
    pad8 = meta[:, 0, :N_EXPERTS]
    off = meta[:, 1, :N_EXPERTS]
    ends = jnp.cumsum(pad8, axis=0)
    dst = ends - pad8
    tot = ends[-1]
    nb = (tot + MOE_BM - 1) // MOE_BM
    nb_end = jnp.cumsum(nb)
    n_blocks = (nt * TM * TOP_K + nt * N_EXPERTS * 7) // MOE_BM + N_EXPERTS
    b = jnp.arange(n_blocks, dtype=jnp.int32)
    blk_e = jnp.minimum(jnp.sum(nb_end[None, :] <= b[:, None], axis=1), N_EXPERTS - 1).astype(jnp.int32)
    blk_row0 = (b - (nb_end - nb)[blk_e]) * MOE_BM
    ends_b = ends[:, blk_e]
    dst_b = dst[:, blk_e]
    blk_t0 = jnp.sum(ends_b <= blk_row0[None, :], axis=0)
    blk_t1 = jnp.sum(dst_b < blk_row0[None, :] + MOE_BM, axis=0)
    i32 = lambda v: v.astype(jnp.int32)
    ids = jnp.arange(N_EXPERTS, dtype=jnp.int32)
    later = jnp.where((nb > 0)[None, :] & (ids[None, :] > ids[:, None]), ids[None, :], N_EXPERTS)
    next_e = jnp.min(later, axis=1)
    next_e = jnp.where(next_e == N_EXPERTS, -1, next_e)
    return {"next_e": i32(next_e), "blk_e": blk_e, "blk_row0": i32(blk_row0), "blk_t0": i32(blk_t0), "blk_t1": i32(blk_t1),
            "n_used": i32(nb_end[-1]).reshape(1), "pad8": i32(pad8.reshape(-1)), "dst": i32(dst.reshape(-1)),
            "off": i32(off.reshape(-1)), "tot8": i32(jnp.sum(pad8, axis=1))}


def _rope_tables(n_pos):
    pos = np.arange(n_pos)
    row, col = pos // GRID_W, pos % GRID_W
    lane = np.arange(LANES)

    def build(active, r, half):
        n_rot = 4 * half
        is_col = (r % n_rot) >= 2 * half
        rr = r % (2 * half)
        freq = np.power(ROPE_BASE, -(rr % half).astype(np.float64) / half)
        p = np.where(is_col[None, :], col[:, None], row[:, None]).astype(np.float64)
        ang = p * freq[None, :]
        first = rr < half
        cos = np.where(active[None, :], np.cos(ang), 1.0)
        sin_a = np.where((active & first)[None, :], -np.sin(ang), 0.0)
        sin_b = np.where((active & ~first)[None, :], np.sin(ang), 0.0)
        return [cos, sin_a, sin_b]

    tabs = build(np.ones(LANES, bool), lane % GQA_HEAD_DIM, GQA_HEAD_DIM // 4)
    in_rope = (lane >= MLA_NOPE) & (lane < MLA_NOPE + MLA_ROPE)
    tabs += build(in_rope, np.maximum(lane - MLA_NOPE, 0) % MLA_ROPE, MLA_ROPE // 4)
    tabs += build(lane < MLA_ROPE, lane % MLA_ROPE, MLA_ROPE // 4)
    table = np.concatenate(tabs, axis=1)
    ident = np.concatenate([np.ones((TM, LANES)), np.zeros((TM, LANES)), np.zeros((TM, LANES))] * 3, axis=1)
    return jnp.asarray(np.concatenate([ident, table], axis=0), F32)


def _placement():
    e = np.zeros((LANES, MLA_HEADS * LANES), np.float32)
    for h in range(MLA_HEADS):
        for r in range(MLA_ROPE):
            e[r, h * LANES + MLA_NOPE + r] = 1.0
    return jnp.asarray(e, BF)


def _split_hi_lo(w):
    hi = w.astype(BF)
    lo = (w - hi.astype(F32)).astype(BF)
    return jnp.concatenate([hi, lo], axis=-1)


def _block_diag_ones(n, g):
    idx = np.arange(n) // g
    return jnp.asarray((idx[:, None] == idx[None, :]).astype(np.float32), BF)


def kernel(x_prompt, x_sample, cache_gqa_k, cache_gqa_v, cache_mla_ckv, cache_mla_krope, c, c_ctx, norm1_g, norm2_g, w_mod, b_mod, w_in, conv_dw, conv_dw_b, conv_ln_g, conv_ln_b, w_conv_out, pool_w, pool_scale, w_pool_out, gqa_qn_g, gqa_kn_g, w_gqa_out, mla_qn_g, w_mla_q_up, mla_kvn_g, w_mla_kv_up, w_mla_out, w_o, w_router, b_router, w_gu, b_gu, w_dn, b_dn, final_g):
    bp, seq, d = x_prompt.shape
    bs, ls, _ = x_sample.shape
    depth = w_in.shape[0]
    past = cache_gqa_k.shape[2]
    assert seq == TM and d == D_MODEL and ls % TM == 0 and (bp * seq) % ls == 0
    tps = ls // TM
    n_p = bp
    nt = n_p + bs * tps
    geo = {
        "np": n_p, "bs": bs, "tps": tps, "nt": nt,
        "seq_row": lambda i: jnp.where(i < n_p, 0, 1 + (i - n_p) // tps),
        "rope_blk": lambda i: jnp.where(i < n_p, 0, 1 + (i - n_p) % tps),
    }

    n_cond = -(-(1 + bs) // 8) * 8
    cond = jnp.zeros((n_cond, d), F32).at[0].set(c_ctx).at[1:1 + bs].set(c)
    mod = _modulation(cond, w_mod, b_mod).reshape(depth, n_cond, 6, d)

    row = lambda v: v.reshape(depth, 1, -1)
    w1 = jnp.pad(w_in[:, :, :_SPLIT_GATE], ((0, 0), (0, 0), (0, W1_COLS - _SPLIT_GATE))).astype(BF)
    wqup = jnp.pad(w_mla_q_up.reshape(depth, MLA_Q_RANK, MLA_HEADS, MLA_NOPE + MLA_ROPE),
                   ((0, 0), (0, 0), (0, 0), (0, LANES - MLA_NOPE - MLA_ROPE)))
    wkv = w_mla_kv_up.reshape(depth, MLA_KV_RANK, MLA_HEADS, MLA_NOPE + MLA_V)
    wk_pad = jnp.pad(wkv[..., :MLA_NOPE], ((0, 0), (0, 0), (0, 0), (0, LANES - MLA_NOPE)))
    wkvup = jnp.concatenate([wk_pad.reshape(depth, MLA_KV_RANK, MLA_HEADS * LANES),
                             wkv[..., MLA_NOPE:].reshape(depth, MLA_KV_RANK, MLA_HEADS * MLA_V)], axis=-1)
    bgu = b_gu.reshape(depth, N_EXPERTS, 1, D_FF, 2)
    p = {
        "norm1_g": row(norm1_g), "norm2_g": row(norm2_g),
        "w1": w1, "wgate": w_in[:, :, _SPLIT_GATE:].astype(BF),
        "rope": _rope_tables(ls), "place": _placement(), "ones_bd": _block_diag_ones(512, GQA_HEAD_DIM),
        "ltri": jnp.asarray(np.tril(np.ones((TM, TM), np.float32), -1), BF),
        "upper": jnp.asarray(np.triu(np.ones((LANES, LANES), np.float32), 1), F32),
        "qn_g": row(jnp.tile(gqa_qn_g, (1, GQA_HEADS))), "kn_g": row(jnp.tile(gqa_kn_g, (1, GQA_KV_HEADS))),
        "cqn_g": row(mla_qn_g), "kvn_g": row(mla_kvn_g),
        "wqup": wqup.reshape(depth, MLA_Q_RANK, MLA_HEADS * LANES).astype(BF), "wkvup": wkvup.astype(BF),
        "conv_dw": conv_dw, "conv_dw_b": row(conv_dw_b), "conv_ln_g": row(conv_ln_g), "conv_ln_b": row(conv_ln_b),
        "w_conv_out": w_conv_out.astype(BF), "pool_w": pool_w.astype(BF), "pool_scale": row(pool_scale),
        "w_pool_out": w_pool_out.astype(BF), "w_gqa_out": w_gqa_out.astype(BF), "w_mla_out": w_mla_out.astype(BF),
        "w_o": w_o.astype(BF),
        "w_router": _split_hi_lo(jnp.pad(w_router, ((0, 0), (0, 0), (0, LANES - N_EXPERTS)))),
        "b_router": row(jnp.pad(b_router, ((0, 0), (0, LANES - N_EXPERTS)))),
        "w_gu": w_gu, "b_gate": bgu[..., 0], "b_up": bgu[..., 1], "pair_perm": _pair_perm(),
        "w_dn": w_dn, "b_dn": b_dn.reshape(depth, N_EXPERTS, 1, d),
    }

    ckm, cvm = _cache_prep(cache_mla_ckv, jnp.pad(cache_mla_krope, ((0, 0), (0, 0), (0, 0), (0, LANES - MLA_ROPE))), p)
    cache = (cache_gqa_k.reshape(bs, depth, past, LANES).astype(BF),
             cache_gqa_v.reshape(bs, depth, past, LANES).astype(BF), ckm, cvm)

    x = jnp.concatenate([x_prompt.reshape(bp * seq, d), x_sample.reshape(bs * ls, d)], axis=0)
    n_ptok = bp * seq
    states = []
    for l in range(depth):
        a, b, q, k, v, qm, km, vm, ks, vs, ckvs, krs = _proj(l, x, mod, geo, p)
        states.append((ks[:n_ptok], vs[:n_ptok], ckvs[:n_ptok], krs[:n_ptok]))
        attn = _attention(l, geo, q, k, v, qm, km, vm, cache)
        x_mid, xs, slot_row, top_w, meta = _merge(l, x, mod, geo, a, b, attn, p)
        y = _experts(l, xs, _expert_plan(meta, nt), p, nt)
        combine = functools.partial(_combine, l, l == depth - 1, y, x_mid, mod, slot_row, top_w,
                                    final_g.reshape(1, d), geo)
        if l < depth - 1:
            x = combine(0, nt)

    y_prompt = combine(0, n_p).reshape(bp, seq, d)
    y_sample = combine(n_p, nt - n_p).reshape(bs, ls, d)
    st = lambda j, shape: jnp.stack([s[j].reshape(shape) for s in states], axis=1)
    return (y_prompt, y_sample,
            st(0, (bp, seq, GQA_KV_HEADS, GQA_HEAD_DIM)), st(1, (bp, seq, GQA_KV_HEADS, GQA_HEAD_DIM)),
            st(2, (bp, seq, MLA_KV_RANK)), st(3, (bp, seq, MLA_ROPE)))
```

```python
import functools

import jax
import jax.numpy as jnp
import numpy as np
from jax import lax
from jax.experimental import pallas as pl
from jax.experimental.pallas import tpu as pltpu

D_MODEL = 1024
GRID_W = 64
CONV_W = 512
CONV_K = 31
POOL_W = 512
POOL_WINDOWS = (2, 4, 8, 16)
POOL_G = 128
GQA_HEADS = 8
GQA_KV_HEADS = 2
GQA_HEAD_DIM = 64
MLA_HEADS = 8
MLA_Q_RANK = 384
MLA_KV_RANK = 256
MLA_NOPE = 64
MLA_ROPE = 32
MLA_V = 64
ROPE_BASE = 10000.0
N_EXPERTS = 32
TOP_K = 4
D_FF = 1024
SWIGLU_LIMIT = 7.0
SWIGLU_ALPHA = 1.702
EPS = 1e-6
GQA_SCALE = GQA_HEAD_DIM ** -0.5
MLA_SCALE = (MLA_NOPE + MLA_ROPE) ** -0.5
LOG2E = 1.4426950408889634

LANES = 128
SUBLANES = 8
TM = 256
HALO = 16
MOE_BM = 640
R_TILE = 1280
W1_COLS = 3072
N_GATE = 4 * D_MODEL
VMEM_LIMIT = 56 * 1024 * 1024

BF = jnp.bfloat16
F32 = jnp.float32

_C_A, _C_B, _C_Q, _C_K, _C_V, _C_CQ, _C_CKV, _C_KR = 0, 1024, 1536, 2048, 2176, 2304, 2688, 2944
_SPLIT_GATE = 2976


def _dot(a, b):
    return jnp.dot(a, b, preferred_element_type=F32)


def _dot_nt(a, b):
    return lax.dot_general(a, b, (((1,), (1,)), ((), ())), preferred_element_type=F32)


def _rms(x):
    return x * lax.rsqrt(jnp.mean(x * x, axis=-1, keepdims=True) + EPS)


def _group_mean_sq(x, ones_bd, width):
    xx = x * x
    hi = xx.astype(BF)
    lo = (xx - hi.astype(F32)).astype(BF)
    return (_dot(hi, ones_bd) + _dot(lo, ones_bd)) * (1.0 / width)


def _tile_lanes(t, width):
    reps = width // LANES
    return t if reps == 1 else jnp.concatenate([t] * reps, axis=-1)


def _rope(x, cos, sin_a, sin_b, shift):
    w = x.shape[-1]
    return (x * _tile_lanes(cos, w) + pltpu.roll(x, w - shift, 1) * _tile_lanes(sin_a, w)
            + pltpu.roll(x, shift, 1) * _tile_lanes(sin_b, w))


def _modulated_norm(x, g, shift, scale):
    return _rms(x) * g * (1.0 + scale) + shift


def _mod_kernel(cond_ref, w_ref, b_ref, o_ref):
    c = cond_ref[...]
    s = (c * jax.nn.sigmoid(c)).astype(BF)
    o_ref[...] = _dot(s, w_ref[...].astype(BF)) + b_ref[...]


def _modulation(cond, w_mod, b_mod):
    depth, d, n = w_mod.shape
    rows = cond.shape[0]
    return pl.pallas_call(
        _mod_kernel,
        grid=(depth, n // D_MODEL),
        in_specs=[
            pl.BlockSpec((rows, d), lambda l, j: (0, 0)),
            pl.BlockSpec((None, d, D_MODEL), lambda l, j: (l, 0, j)),
            pl.BlockSpec((None, 1, D_MODEL), lambda l, j: (l, 0, j)),
        ],
        out_specs=pl.BlockSpec((None, rows, D_MODEL), lambda l, j: (l, 0, j)),
        out_shape=jax.ShapeDtypeStruct((depth, rows, n), F32),
        name="modulation",
    )(cond, w_mod, b_mod.reshape(depth, 1, n))


def _proj_kernel(x_ref, mod_ref, g1_ref, w1_ref, rope_ref, qn_ref, kn_ref, cqn_ref, kvn_ref,
                 wqup_ref, wkvup_ref, place_ref, ones_ref,
                 a_ref, b_ref, q_ref, k_ref, v_ref, qm_ref, km_ref, vm_ref,
                 ks_ref, vs_ref, ckvs_ref, krs_ref):
    x = x_ref[...]
    h = _modulated_norm(x, g1_ref[...], mod_ref[0:1, :], mod_ref[1:2, :])
    y = _dot(h.astype(BF), w1_ref[...])

    a_ref[...] = (y[:, _C_A:_C_A + CONV_W] * jax.nn.sigmoid(y[:, _C_A + CONV_W:_C_B])).astype(BF)
    b_ref[...] = y[:, _C_B:_C_Q].astype(BF)

    def tab(j):
        return rope_ref[:, j * LANES:(j + 1) * LANES]

    q = y[:, _C_Q:_C_K]
    q = q * lax.rsqrt(_group_mean_sq(q, ones_ref[...], GQA_HEAD_DIM) + EPS) * qn_ref[...]
    q = _rope(q, tab(0), tab(1), tab(2), GQA_HEAD_DIM // 4)
    q_ref[...] = (q * (GQA_SCALE * LOG2E)).astype(BF)

    k = y[:, _C_K:_C_V]
    k = k * lax.rsqrt(_group_mean_sq(k, ones_ref[0:LANES, 0:LANES], GQA_HEAD_DIM) + EPS) * kn_ref[...]
    ks_ref[...] = k
    k_ref[...] = _rope(k, tab(0), tab(1), tab(2), GQA_HEAD_DIM // 4).astype(BF)

    v = y[:, _C_V:_C_CQ]
    vs_ref[...] = v
    v_ref[...] = v.astype(BF)

    cq = _rms(y[:, _C_CQ:_C_CKV]) * cqn_ref[...]
    qm = _dot(cq.astype(BF), wqup_ref[...])
    qm = _rope(qm, tab(3), tab(4), tab(5), MLA_ROPE // 4)
    qm_ref[...] = (qm * (MLA_SCALE * LOG2E)).astype(BF)

    ckv = _rms(y[:, _C_CKV:_C_KR]) * kvn_ref[...]
    ckvs_ref[...] = ckv
    kv = _dot(ckv.astype(BF), wkvup_ref[...])
    kr = y[:, _C_KR:W1_COLS]
    krs_ref[...] = kr[:, 0:MLA_ROPE]
    kr_rot = _rope(kr, tab(6), tab(7), tab(8), MLA_ROPE // 4)
    km = kv[:, 0:MLA_HEADS * LANES] + _dot(kr_rot.astype(BF), place_ref[...])
    km_ref[...] = km.astype(BF)
    vm_ref[...] = kv[:, MLA_HEADS * LANES:].astype(BF)


def _const_spec(shape):
    nd = len(shape)
    return pl.BlockSpec(shape, lambda *_: (0,) * nd)


def _layer_spec(shape, l):
    nd = len(shape)
    return pl.BlockSpec((None,) + shape, lambda *_: (l,) + (0,) * nd)


def _proj(l, x, mod, geo, p):
    nt = geo["nt"]
    tok = nt * TM
    seq_row, rope_blk = geo["seq_row"], geo["rope_blk"]

    def tile(width, dtype):
        return pl.BlockSpec((TM, width), lambda i: (i, 0)), jax.ShapeDtypeStruct((tok, width), dtype)

    outs = [tile(CONV_W, BF), tile(POOL_W, BF), tile(512, BF), tile(LANES, BF), tile(LANES, BF),
            tile(MLA_HEADS * LANES, BF), tile(MLA_HEADS * LANES, BF), tile(MLA_HEADS * MLA_V, BF),
            tile(LANES, F32), tile(LANES, F32), tile(MLA_KV_RANK, F32), tile(MLA_ROPE, F32)]
    return pl.pallas_call(
        _proj_kernel,
        grid=(nt,),
        in_specs=[
            pl.BlockSpec((TM, D_MODEL), lambda i: (i, 0)),
            pl.BlockSpec((None, None, 6, D_MODEL), lambda i: (l, seq_row(i), 0, 0)),
            _layer_spec((1, D_MODEL), l),
            _layer_spec((D_MODEL, W1_COLS), l),
            pl.BlockSpec((TM, 9 * LANES), lambda i: (rope_blk(i), 0)),
            _layer_spec((1, 512), l),
            _layer_spec((1, LANES), l),
            _layer_spec((1, MLA_Q_RANK), l),
            _layer_spec((1, MLA_KV_RANK), l),
            _layer_spec((MLA_Q_RANK, MLA_HEADS * LANES), l),
            _layer_spec((MLA_KV_RANK, MLA_HEADS * (LANES + MLA_V)), l),
            _const_spec((LANES, MLA_HEADS * LANES)),
            _const_spec((512, 512)),
        ],
        out_specs=[o[0] for o in outs],
        out_shape=[o[1] for o in outs],
        compiler_params=pltpu.CompilerParams(dimension_semantics=("arbitrary",), vmem_limit_bytes=VMEM_LIMIT),
        name="proj",
    )(x, mod, p["norm1_g"], p["w1"], p["rope"], p["qn_g"], p["kn_g"], p["cqn_g"], p["kvn_g"],
      p["wqup"], p["wkvup"], p["place"], p["ones_bd"])


def _cache_kernel(ckv_ref, kr_ref, wkvup_ref, place_ref, km_ref, vm_ref):
    kv = _dot(ckv_ref[...].astype(BF), wkvup_ref[...])
    km = kv[:, 0:MLA_HEADS * LANES] + _dot(kr_ref[...].astype(BF), place_ref[...])
    km_ref[...] = km.astype(BF)
    vm_ref[...] = kv[:, MLA_HEADS * LANES:].astype(BF)


def _cache_prep(ckv, kr_pad, p):
    bs, depth, past, _ = ckv.shape
    return pl.pallas_call(
        _cache_kernel,
        grid=(bs, depth),
        in_specs=[
            pl.BlockSpec((None, None, past, MLA_KV_RANK), lambda b, l: (b, l, 0, 0)),
            pl.BlockSpec((None, None, past, LANES), lambda b, l: (b, l, 0, 0)),
            pl.BlockSpec((None, MLA_KV_RANK, MLA_HEADS * (LANES + MLA_V)), lambda b, l: (l, 0, 0)),
            pl.BlockSpec((LANES, MLA_HEADS * LANES), lambda b, l: (0, 0)),
        ],
        out_specs=[
            pl.BlockSpec((None, None, past, MLA_HEADS * LANES), lambda b, l: (b, l, 0, 0)),
            pl.BlockSpec((None, None, past, MLA_HEADS * MLA_V), lambda b, l: (b, l, 0, 0)),
        ],
        out_shape=[jax.ShapeDtypeStruct((bs, depth, past, MLA_HEADS * LANES), BF),
                   jax.ShapeDtypeStruct((bs, depth, past, MLA_HEADS * MLA_V), BF)],
        name="cache_prep",
    )(ckv, kr_pad, p["wkvup"], p["place"])


def _attend(qs, k, v):
    s = _dot_nt(qs, k)
    m = jnp.max(s, axis=-1, keepdims=True)
    e = jnp.exp2(s - m)
    l = jnp.sum(e, axis=-1, keepdims=True)
    return _dot(e.astype(BF), v) / l


def _attention_body(q, k, v, qm, km, vm, oc_ref, od_ref):
    lane_k = lax.broadcasted_iota(jnp.int32, k.shape, 1)
    lane_q = lax.broadcasted_iota(jnp.int32, (TM, LANES), 1)
    lo_k = lane_k < GQA_HEAD_DIM
    lo_q = lane_q < GQA_HEAD_DIM
    k32, v32 = k.astype(F32), v.astype(F32)
    k_sw = pltpu.roll(k32, GQA_HEAD_DIM, 1)
    v_sw = pltpu.roll(v32, GQA_HEAD_DIM, 1)
    k_dup = [jnp.where(lo_k, k32, k_sw).astype(BF), jnp.where(lo_k, k_sw, k32).astype(BF)]
    v_dup = [jnp.where(lo_k, v32, v_sw).astype(BF), jnp.where(lo_k, v_sw, v32).astype(BF)]
    zero = jnp.zeros((TM, LANES), BF)
    group = GQA_HEADS // GQA_KV_HEADS
    for j in range(GQA_HEADS // 2):
        qs = q[:, j * LANES:(j + 1) * LANES]
        g = (2 * j) // group
        o_lo = _attend(jnp.where(lo_q, qs, zero), k_dup[g], v_dup[g])
        o_hi = _attend(jnp.where(lo_q, zero, qs), k_dup[g], v_dup[g])
        oc_ref[:, j * LANES:(j + 1) * LANES] = jnp.where(lo_q, o_lo, o_hi).astype(BF)
    for j in range(MLA_HEADS // 2):
        vs = vm[:, j * LANES:(j + 1) * LANES]
        outs = []
        for h in (2 * j, 2 * j + 1):
            outs.append(_attend(qm[:, h * LANES:(h + 1) * LANES], km[:, h * LANES:(h + 1) * LANES], vs))
        od_ref[:, j * LANES:(j + 1) * LANES] = jnp.where(lo_q, outs[0], outs[1]).astype(BF)


def _attn_prompt_kernel(q_ref, k_ref, v_ref, qm_ref, km_ref, vm_ref, oc_ref, od_ref):
    _attention_body(q_ref[...], k_ref[...], v_ref[...], qm_ref[...], km_ref[...], vm_ref[...], oc_ref, od_ref)


def _attn_sample_kernel(q_ref, k_ref, v_ref, qm_ref, km_ref, vm_ref, ck_ref, cv_ref, ckm_ref, cvm_ref,
                        oc_ref, od_ref):
    k = jnp.concatenate([ck_ref[...], k_ref[...]], axis=0)
    v = jnp.concatenate([cv_ref[...], v_ref[...]], axis=0)
    km = jnp.concatenate([ckm_ref[...], km_ref[...]], axis=0)
    vm = jnp.concatenate([cvm_ref[...], vm_ref[...]], axis=0)
    _attention_body(q_ref[...], k, v, qm_ref[...], km, vm, oc_ref, od_ref)


def _attention(l, geo, q, k, v, qm, km, vm, cache):
    n_p, bs, tps, nt = geo["np"], geo["bs"], geo["tps"], geo["nt"]
    tok = nt * TM
    ls = tps * TM
    off = (n_p * TM) // ls
    widths = (512, LANES, LANES, MLA_HEADS * LANES, MLA_HEADS * LANES, MLA_HEADS * MLA_V)
    params = pltpu.CompilerParams(dimension_semantics=("arbitrary",), vmem_limit_bytes=VMEM_LIMIT)
    oc_p, od_p = pl.pallas_call(
        _attn_prompt_kernel,
        grid=(n_p,),
        in_specs=[pl.BlockSpec((TM, w), lambda i: (i, 0)) for w in widths],
        out_specs=[pl.BlockSpec((TM, 512), lambda i: (i, 0))] * 2,
        out_shape=[jax.ShapeDtypeStruct((n_p * TM, 512), BF)] * 2,
        compiler_params=params,
        name="attn_prompt",
    )(q, k, v, qm, km, vm)
    ck, cv, ckm, cvm = cache
    past = ck.shape[2]
    q_spec = lambda w: pl.BlockSpec((TM, w), lambda b, j: (n_p + b * tps + j, 0))
    kv_spec = lambda w: pl.BlockSpec((ls, w), lambda b, j: (off + b, 0))
    c_spec = lambda w: pl.BlockSpec((None, None, past, w), lambda b, j: (b, l, 0, 0))
    params2 = pltpu.CompilerParams(dimension_semantics=("arbitrary", "arbitrary"), vmem_limit_bytes=VMEM_LIMIT)
    o_spec = pl.BlockSpec((TM, 512), lambda b, j: (b * tps + j, 0))
    oc_s, od_s = pl.pallas_call(
        _attn_sample_kernel,
        grid=(bs, tps),
        in_specs=[q_spec(512), kv_spec(LANES), kv_spec(LANES), q_spec(MLA_HEADS * LANES),
                  kv_spec(MLA_HEADS * LANES), kv_spec(MLA_HEADS * MLA_V),
                  c_spec(LANES), c_spec(LANES), c_spec(MLA_HEADS * LANES), c_spec(MLA_HEADS * MLA_V)],
        out_specs=[o_spec, o_spec],
        out_shape=[jax.ShapeDtypeStruct((bs * ls, 512), BF)] * 2,
        compiler_params=params2,
        name="attn_sample",
    )(q, k, v, qm, km, vm, ck, cv, ckm, cvm)
    return oc_p, od_p, oc_s, od_s


def _route(h2, wr_ref, br_ref, ltri_ref, upper_ref, xs_ref, rl_ref, tw_ref, meta_ref):
    h2_hi = h2.astype(BF)
    h2_lo = (h2 - h2_hi.astype(F32)).astype(BF)
    hi_terms = _dot(h2_hi, wr_ref[...])
    logits = (hi_terms[:, 0:LANES] + hi_terms[:, LANES:] + _dot(h2_lo, wr_ref[:, 0:LANES])) + br_ref[...]
    lane = lax.broadcasted_iota(jnp.int32, (TM, LANES), 1).astype(F32)
    neg = jnp.float32(-jnp.inf)
    lg = jnp.where(lane < N_EXPERTS, logits, neg)
    vals, idxs = [], []
    for _ in range(TOP_K):
        m = jnp.max(lg, axis=-1, keepdims=True)
        idx = jnp.min(jnp.where(lg == m, lane, float(LANES)), axis=-1, keepdims=True)
        vals.append(m)
        idxs.append(idx)
        lg = jnp.where(lane == idx, neg, lg)
    exps = [jnp.exp(v - vals[0]) for v in vals]
    denom = exps[0] + exps[1] + exps[2] + exps[3]

    onehot = [(lane == idxs[kk]).astype(F32) for kk in range(TOP_K)]
    colsum = [jnp.sum(o, axis=0, keepdims=True) for o in onehot]
    cnt = colsum[0] + colsum[1] + colsum[2] + colsum[3]
    pad8 = jnp.floor((cnt + 7.0) * 0.125) * 8.0
    run_off = jnp.dot(jnp.broadcast_to(pad8, (8, LANES)), upper_ref[...], preferred_element_type=F32,
                      precision=lax.Precision.HIGHEST)[0:1, :]
    base = jnp.zeros((1, LANES), F32)
    rloc = []
    for kk in range(TOP_K):
        before = _dot(ltri_ref[...], onehot[kk].astype(BF))
        rloc.append(jnp.sum(onehot[kk] * (run_off + base + before), axis=-1, keepdims=True))
        base = base + colsum[kk]

    eye = (lax.broadcasted_iota(jnp.int32, (TM, TM), 0) == lax.broadcasted_iota(jnp.int32, (TM, TM), 1))
    r_sub = lax.broadcasted_iota(jnp.int32, (R_TILE, TM), 0).astype(F32)
    ones8 = jnp.ones((8, TM), BF)
    sel = jnp.zeros((R_TILE, TM), F32)
    for kk in range(TOP_K):
        hi = jnp.floor(rloc[kk] * (1.0 / TM))
        lo = rloc[kk] - hi * TM
        row = (_dot(ones8, jnp.where(eye, hi, 0.0).astype(BF)) * TM
               + _dot(ones8, jnp.where(eye, lo, 0.0).astype(BF)))[0:1, :]
        sel = jnp.where(r_sub == row, 1.0, sel)
    xs_ref[...] = _dot(sel.astype(BF), h2.astype(BF))

    rl = jnp.zeros((TM, LANES), F32)
    tw = jnp.zeros((TM, LANES), F32)
    for kk in range(TOP_K):
        rl = jnp.where(lane == kk, rloc[kk], rl)
        tw = jnp.where(lane == kk, exps[kk] / denom, tw)
    rl_ref[...] = rl.astype(jnp.int32)
    tw_ref[...] = tw
    sub = lax.broadcasted_iota(jnp.int32, (8, LANES), 0)
    meta = jnp.where(sub == 0, jnp.broadcast_to(pad8, (8, LANES)),
                     jnp.where(sub == 1, jnp.broadcast_to(run_off, (8, LANES)), 0.0))
    meta_ref[...] = meta.astype(jnp.int32)


def _merge_kernel(geo, x_ref, mod_ref, a_ref, ap_ref, an_ref, b_ref, bp_ref, bn_ref,
                  ocp_ref, odp_ref, ocs_ref, ods_ref,
                  g1_ref, wg_ref, cw_ref, cb_ref, lng_ref, lnb_ref, wco_ref, pw_ref, ps_ref, wpo_ref,
                  wgo_ref, wmo_ref, wo_ref, g2_ref, wr_ref, br_ref,
                  ltri_ref, upper_ref,
                  xo_ref, xs_ref, rl_ref, tw_ref, meta_ref, abuf, bbuf, ashift):
    n_p, tps = geo["np"], geo["tps"]
    i = pl.program_id(0)
    j = jnp.where(i < n_p, 0, (i - n_p) % tps)
    n_seq_tiles = jnp.where(i < n_p, 1, tps)
    has_prev = j > 0
    has_next = j < n_seq_tiles - 1

    def fill(buf, cur, prev, nxt):
        buf[0:HALO, :] = jnp.where(has_prev, prev[...].astype(F32), 0.0)
        buf[HALO:HALO + TM, :] = cur[...].astype(F32)
        buf[HALO + TM:, :] = jnp.where(has_next, nxt[...].astype(F32), 0.0)

    fill(abuf, a_ref, ap_ref, an_ref)
    fill(bbuf, b_ref, bp_ref, bn_ref)

    sh_rows = TM + 2 * HALO - SUBLANES
    for sh in range(1, SUBLANES):
        ashift[sh - 1] = abuf[sh:sh + sh_rows, :]
    rows = 32
    conv = []
    for r0 in range(0, TM, rows):
        acc = jnp.zeros((rows, CONV_W), F32)
        for t in range(CONV_K):
            s = r0 + t + HALO - CONV_K // 2
            sh = s % SUBLANES
            tap = abuf[s:s + rows, :] if sh == 0 else ashift[sh - 1, s - sh:s - sh + rows, :]
            acc = acc + tap * cw_ref[t:t + 1, :]
        conv.append(acc)
    ca = jnp.concatenate(conv, axis=0) + cb_ref[...]
    mu = jnp.mean(ca, axis=-1, keepdims=True)
    xc = ca - mu
    ln = xc * lax.rsqrt(jnp.mean(xc * xc, axis=-1, keepdims=True) + EPS) * lng_ref[...] + lnb_ref[...]
    br_a = _dot((ln * jax.nn.sigmoid(ln)).astype(BF), wco_ref[...])

    pos = j * TM + lax.broadcasted_iota(jnp.int32, (TM, 1), 0)
    seq_len = n_seq_tiles * TM
    mixed = []
    for g, w in enumerate(POOL_WINDOWS):
        lo, hi = g * POOL_G, (g + 1) * POOL_G
        acc = jnp.zeros((TM, POOL_G), F32)
        for o in range(-(w // 2), w - w // 2):
            acc = acc + bbuf[HALO + o:HALO + o + TM, lo:hi]
        cnt = jnp.clip(pos - w // 2 + w, 0, seq_len) - jnp.clip(pos - w // 2, 0, seq_len)
        pooled = acc / cnt.astype(F32) - bbuf[HALO:HALO + TM, lo:hi]
        mixed.append(_dot(pooled.astype(BF), pw_ref[g]))
    pb = jnp.concatenate(mixed, axis=-1) * ps_ref[...]
    br_b = _dot(pb.astype(BF), wpo_ref[...])

    is_prompt = i < n_p
    br_c = _dot(jnp.where(is_prompt, ocp_ref[...], ocs_ref[...]), wgo_ref[...])
    br_d = _dot(jnp.where(is_prompt, odp_ref[...], ods_ref[...]), wmo_ref[...])

    x = x_ref[...]
    h = _modulated_norm(x, g1_ref[...], mod_ref[0:1, :], mod_ref[1:2, :])
    gates = jax.nn.sigmoid(_dot(h.astype(BF), wg_ref[...]).astype(BF))
    merged = (gates[:, 0:D_MODEL] * br_a.astype(BF) + gates[:, D_MODEL:2 * D_MODEL] * br_b.astype(BF)
              + gates[:, 2 * D_MODEL:3 * D_MODEL] * br_c.astype(BF) + gates[:, 3 * D_MODEL:] * br_d.astype(BF))
    x = x + mod_ref[2:3, :] * _dot(merged, wo_ref[...])
    xo_ref[...] = x

    h2 = _modulated_norm(x, g2_ref[...], mod_ref[3:4, :], mod_ref[4:5, :])
    _route(h2, wr_ref, br_ref, ltri_ref, upper_ref, xs_ref, rl_ref, tw_ref, meta_ref)


def _merge(l, x, mod, geo, a, b, attn, p):
    nt, n_p = geo["nt"], geo["np"]
    tok = nt * TM
    seq_row = geo["seq_row"]
    hb = TM // HALO
    last_hb = nt * hb - 1
    tile = lambda w: pl.BlockSpec((TM, w), lambda i: (i, 0))
    prev = lambda w: pl.BlockSpec((HALO, w), lambda i: (jnp.maximum(i * hb - 1, 0), 0))
    nxt = lambda w: pl.BlockSpec((HALO, w), lambda i: (jnp.minimum((i + 1) * hb, last_hb), 0))
    p_tile = pl.BlockSpec((TM, 512), lambda i: (jnp.minimum(i, n_p - 1), 0))
    s_tile = pl.BlockSpec((TM, 512), lambda i: (jnp.maximum(i - n_p, 0), 0))
    return pl.pallas_call(
        functools.partial(_merge_kernel, geo),
        grid=(nt,),
        in_specs=[
            tile(D_MODEL),
            pl.BlockSpec((None, None, 6, D_MODEL), lambda i: (l, seq_row(i), 0, 0)),
            tile(CONV_W), prev(CONV_W), nxt(CONV_W),
            tile(POOL_W), prev(POOL_W), nxt(POOL_W),
            p_tile, p_tile, s_tile, s_tile,
            _layer_spec((1, D_MODEL), l),
            _layer_spec((D_MODEL, N_GATE), l),
            _layer_spec((CONV_K, CONV_W), l),
            _layer_spec((1, CONV_W), l),
            _layer_spec((1, CONV_W), l),
            _layer_spec((1, CONV_W), l),
            _layer_spec((CONV_W, D_MODEL), l),
            _layer_spec((len(POOL_WINDOWS), POOL_G, POOL_G), l),
            _layer_spec((1, POOL_W), l),
            _layer_spec((POOL_W, D_MODEL), l),
            _layer_spec((512, D_MODEL), l),
            _layer_spec((512, D_MODEL), l),
            _layer_spec((D_MODEL, D_MODEL), l),
            _layer_spec((1, D_MODEL), l),
            _layer_spec((D_MODEL, 2 * LANES), l),
            _layer_spec((1, LANES), l),
            _const_spec((TM, TM)),
            _const_spec((LANES, LANES)),
        ],
        out_specs=[tile(D_MODEL), pl.BlockSpec((R_TILE, D_MODEL), lambda i: (i, 0)), tile(LANES), tile(LANES),
                   pl.BlockSpec((None, 8, LANES), lambda i: (i, 0, 0))],
        out_shape=[jax.ShapeDtypeStruct((tok, D_MODEL), F32), jax.ShapeDtypeStruct((nt * R_TILE, D_MODEL), F32),
                   jax.ShapeDtypeStruct((tok, LANES), jnp.int32), jax.ShapeDtypeStruct((tok, LANES), F32),
                   jax.ShapeDtypeStruct((nt, 8, LANES), jnp.int32)],
        scratch_shapes=[pltpu.VMEM((TM + 2 * HALO, CONV_W), F32), pltpu.VMEM((TM + 2 * HALO, POOL_W), F32),
                        pltpu.VMEM((SUBLANES - 1, TM + 2 * HALO - SUBLANES, CONV_W), F32)],
        compiler_params=pltpu.CompilerParams(dimension_semantics=("arbitrary",), vmem_limit_bytes=VMEM_LIMIT),
        name="merge",
    )(x, mod, a, a, a, b, b, b, *attn,
      p["norm1_g"], p["wgate"], p["conv_dw"], p["conv_dw_b"], p["conv_ln_g"], p["conv_ln_b"], p["w_conv_out"],
      p["pool_w"], p["pool_scale"], p["w_pool_out"], p["w_gqa_out"], p["w_mla_out"], p["w_o"],
      p["norm2_g"], p["w_router"], p["b_router"], p["ltri"], p["upper"])


def _expert_kernel(nt, layer, blk_e, blk_row0, blk_t0, blk_t1, n_used_ref, pad8_ref, dst_ref, off_ref, tot8_ref,
                   next_e, xs_hbm, wgu_hbm, bg_ref, bu_ref, wd_hbm, bd_ref, perm_ref,
                   y_hbm, xg, yb, zbuf, wgu_buf, wd_buf, wg_bf, wu_bf, wd_bf, cnt_smem, gsem, osem, zsem, wsem):
    i = pl.program_id(0)
    n_used = n_used_ref[0]
    slot = i % 2
    last = pl.num_programs(0) - 1

    def for_pieces(b, fn):
        e = blk_e[b]
        b0 = blk_row0[b]

        def body(t, tot):
            j = t * N_EXPERTS + e
            run0 = dst_ref[j]
            lo = jnp.maximum(run0, b0)
            hi = jnp.minimum(run0 + pad8_ref[j], b0 + MOE_BM)
            n = pl.multiple_of(jnp.maximum(hi - lo, 0), 8)

            @pl.when(n > 0)
            def _():
                fn(pl.multiple_of(t * R_TILE + off_ref[j] + lo - run0, 8), pl.multiple_of(lo - b0, 8), n)

            return tot + n

        return lax.fori_loop(blk_t0[b], blk_t1[b], body, jnp.int32(0))

    def rows_copy(src, dst, sem, n):
        return pltpu.make_async_copy(src.at[pl.ds(0, n), :], dst.at[pl.ds(0, n), :], sem)

    def start_gather(b, s):
        def piece(row_t, row_b, n):
            pltpu.make_async_copy(xs_hbm.at[pl.ds(row_t, n), :], xg.at[s].at[pl.ds(row_b, n), :], gsem.at[s]).start()

        cnt_smem[s] = for_pieces(b, piece)

    def tail_copy(t):
        n = pl.multiple_of(R_TILE - tot8_ref[t], 8)
        return n, pltpu.make_async_copy(
            zbuf.at[pl.ds(0, n), :], y_hbm.at[pl.ds(pl.multiple_of(t * R_TILE + tot8_ref[t], 8), n), :], zsem)

    @pl.when(i == 0)
    def _():
        xg[...] = jnp.zeros_like(xg)
        zbuf[...] = jnp.zeros_like(zbuf)
        for s in range(4):
            cnt_smem[s] = 0

        def fill(t, c):
            n, cp = tail_copy(t)

            @pl.when(n > 0)
            def _():
                cp.start()

            return c

        lax.fori_loop(0, nt, fill, 0)

        @pl.when(n_used > 0)
        def _():
            start_gather(0, 0)

    @pl.when(i + 1 < n_used)
    def _():
        start_gather(i + 1, 1 - slot)

    @pl.when(i < n_used)
    def _():
        e_changed = jnp.logical_or(i == 0, blk_e[i] != blk_e[jnp.maximum(i - 1, 0)])

        def weight_copies(e):
            return (pltpu.make_async_copy(wgu_hbm.at[layer, e], wgu_buf, wsem.at[0]),
                    pltpu.make_async_copy(wd_hbm.at[layer, e], wd_buf, wsem.at[1]))

        @pl.when(e_changed)
        def _():
            e = blk_e[i]

            @pl.when(i == 0)
            def _():
                for cp in weight_copies(e):
                    cp.start()

            for cp in weight_copies(e):
                cp.wait()
            wd_bf[...] = wd_buf[...].astype(BF)
            for c in range(D_FF // LANES):
                pair = _dot(wgu_buf[:, 2 * c * LANES:2 * (c + 1) * LANES].astype(BF), perm_ref[...])
                wg_bf[:, c * LANES:(c + 1) * LANES] = pair[:, 0:LANES].astype(BF)
                wu_bf[:, c * LANES:(c + 1) * LANES] = pair[:, LANES:].astype(BF)
            e_next = next_e[e]

            @pl.when(e_next >= 0)
            def _():
                for cp in weight_copies(e_next):
                    cp.start()

        n_prev = pl.multiple_of(cnt_smem[2 + slot], 8)

        @pl.when(n_prev > 0)
        def _():
            rows_copy(yb.at[slot], y_hbm, osem.at[slot], n_prev).wait()

        n_in = pl.multiple_of(cnt_smem[slot], 8)
        rows_copy(xs_hbm, xg.at[slot], gsem.at[slot], n_in).wait()

        def expert_mlp(rows):
            xb = xg[slot, 0:rows, :].astype(BF)
            gate = jnp.minimum(_dot(xb, wg_bf[...]) + bg_ref[...], SWIGLU_LIMIT)
            up = jnp.clip(_dot(xb, wu_bf[...]) + bu_ref[...], -SWIGLU_LIMIT, SWIGLU_LIMIT)
            glu = gate * jax.nn.sigmoid(gate * SWIGLU_ALPHA)
            y = _dot(((up + 1.0) * glu).astype(BF), wd_bf[...]) + bd_ref[...]
            yb[slot, 0:rows, :] = y

        @pl.when(n_in > MOE_BM // 2)
        def _():
            expert_mlp(MOE_BM)

        @pl.when(n_in <= MOE_BM // 2)
        def _():
            expert_mlp(MOE_BM // 2)

        def piece(row_t, row_b, n):
            pltpu.make_async_copy(yb.at[slot].at[pl.ds(row_b, n), :], y_hbm.at[pl.ds(row_t, n), :],
                                  osem.at[slot]).start()

        cnt_smem[2 + slot] = for_pieces(i, piece)

    @pl.when(i == last)
    def _():
        for s in range(2):
            n_out = pl.multiple_of(cnt_smem[2 + s], 8)

            @pl.when(n_out > 0)
            def _():
                rows_copy(yb.at[s], y_hbm, osem.at[s], n_out).wait()

        def drain(t, c):
            n, cp = tail_copy(t)

            @pl.when(n > 0)
            def _():
                cp.wait()

            return c

        lax.fori_loop(0, nt, drain, 0)


def _experts(l, xs, plan, p, nt):
    n_blocks = plan["blk_e"].shape[0]
    n_pref = 10
    by_expert = lambda *lead: (lambda i, be, *_: lead + (be[i], 0, 0))
    b_spec = pl.BlockSpec((None, None, 1, D_FF), by_expert(l))
    grid_spec = pltpu.PrefetchScalarGridSpec(
        num_scalar_prefetch=n_pref,
        grid=(n_blocks,),
        in_specs=[
            pl.BlockSpec(memory_space=pl.ANY),
            pl.BlockSpec(memory_space=pl.ANY),
            b_spec, b_spec,
            pl.BlockSpec(memory_space=pl.ANY),
            pl.BlockSpec((None, None, 1, D_MODEL), by_expert(l)),
            pl.BlockSpec((2 * LANES, 2 * LANES), lambda i, *_: (0, 0)),
        ],
        out_specs=pl.BlockSpec(memory_space=pl.ANY),
        scratch_shapes=[
            pltpu.VMEM((2, MOE_BM, D_MODEL), F32),
            pltpu.VMEM((2, MOE_BM, D_MODEL), F32),
            pltpu.VMEM((R_TILE - TOP_K * TM, D_MODEL), F32),
            pltpu.VMEM((D_MODEL, 2 * D_FF), F32),
            pltpu.VMEM((D_FF, D_MODEL), F32),
            pltpu.VMEM((D_MODEL, D_FF), BF),
            pltpu.VMEM((D_MODEL, D_FF), BF),
            pltpu.VMEM((D_FF, D_MODEL), BF),
            pltpu.SMEM((4,), jnp.int32),
            pltpu.SemaphoreType.DMA((2,)),
            pltpu.SemaphoreType.DMA((2,)),
            pltpu.SemaphoreType.DMA,
            pltpu.SemaphoreType.DMA((2,)),
        ],
    )
    return pl.pallas_call(
        functools.partial(_expert_kernel, nt, l),
        grid_spec=grid_spec,
        out_shape=jax.ShapeDtypeStruct((nt * R_TILE, D_MODEL), F32),
        compiler_params=pltpu.CompilerParams(dimension_semantics=("arbitrary",), vmem_limit_bytes=VMEM_LIMIT),
        name="experts",
    )(plan["blk_e"], plan["blk_row0"], plan["blk_t0"], plan["blk_t1"], plan["n_used"], plan["pad8"], plan["dst"],
      plan["off"], plan["tot8"], plan["next_e"],
      xs, p["w_gu"], p["b_gate"], p["b_up"], p["w_dn"], p["b_dn"], p["pair_perm"])


def _pair_perm():
    m = np.zeros((2 * LANES, 2 * LANES), np.float32)
    j = np.arange(LANES)
    m[2 * j, j] = 1.0
    m[2 * j + 1, LANES + j] = 1.0
    return jnp.asarray(m, BF)


def _combine_kernel(final, y_ref, x_ref, mod_ref, rl_ref, tw_ref, fg_ref, o_ref):
    r_lane = lax.broadcasted_iota(jnp.int32, (TM, R_TILE), 1)
    rl = rl_ref[...]
    tw = tw_ref[...]
    sel = jnp.zeros((TM, R_TILE), F32)
    for k in range(TOP_K):
        sel = jnp.where(r_lane == rl[:, k:k + 1], tw[:, k:k + 1], sel)
    ffn = _dot(sel.astype(BF), y_ref[...].astype(BF))
    x = x_ref[...] + mod_ref[5:6, :] * ffn
    if final:
        x = _rms(x) * fg_ref[...]
    o_ref[...] = x


def _combine(l, final, y, x, mod, rl, tw, final_g, geo, tile0, n_tiles):
    seq_row = geo["seq_row"]
    tile = lambda w: pl.BlockSpec((TM, w), lambda i: (i + tile0, 0))
    return pl.pallas_call(
        functools.partial(_combine_kernel, final),
        grid=(n_tiles,),
        in_specs=[
            pl.BlockSpec((R_TILE, D_MODEL), lambda i: (i + tile0, 0)),
            tile(D_MODEL),
            pl.BlockSpec((None, None, 6, D_MODEL), lambda i: (l, seq_row(i + tile0), 0, 0)),
            tile(LANES),
            tile(LANES),
            _const_spec((1, D_MODEL)),
        ],
        out_specs=pl.BlockSpec((TM, D_MODEL), lambda i: (i, 0)),
        out_shape=jax.ShapeDtypeStruct((n_tiles * TM, D_MODEL), F32),
        compiler_params=pltpu.CompilerParams(dimension_semantics=("arbitrary",), vmem_limit_bytes=VMEM_LIMIT),
        name="combine",
    )(y, x, mod, rl, tw, final_g)


def _expert_plan(meta, nt):
    pad8 = meta[:, 0, :N_EXPERTS]
    off = meta[:, 1, :N_EXPERTS]
    ends = jnp.cumsum(pad8, axis=0)
    dst = ends - pad8
    tot = ends[-1]
    nb = (tot + MOE_BM - 1) // MOE_BM
    nb_end = jnp.cumsum(nb)
    n_blocks = (nt * TM * TOP_K + nt * N_EXPERTS * 7) // MOE_BM + N_EXPERTS
    b = jnp.arange(n_blocks, dtype=jnp.int32)
    blk_e = jnp.minimum(jnp.sum(nb_end[None, :] <= b[:, None], axis=1), N_EXPERTS - 1).astype(jnp.int32)
    blk_row0 = (b - (nb_end - nb)[blk_e]) * MOE_BM
    ends_b = ends[:, blk_e]
    dst_b = dst[:, blk_e]
    blk_t0 = jnp.sum(ends_b <= blk_row0[None, :], axis=0)
    blk_t1 = jnp.sum(dst_b < blk_row0[None, :] + MOE_BM, axis=0)
    i32 = lambda v: v.astype(jnp.int32)
    ids = jnp.arange(N_EXPERTS, dtype=jnp.int32)
    later = jnp.where((nb > 0)[None, :] & (ids[None, :] > ids[:, None]), ids[None, :], N_EXPERTS)
    next_e = jnp.min(later, axis=1)
    next_e = jnp.where(next_e == N_EXPERTS, -1, next_e)
    return {"next_e": i32(next_e), "blk_e": blk_e, "blk_row0": i32(blk_row0), "blk_t0": i32(blk_t0), "blk_t1": i32(blk_t1),
            "n_used": i32(nb_end[-1]).reshape(1), "pad8": i32(pad8.reshape(-1)), "dst": i32(dst.reshape(-1)),
            "off": i32(off.reshape(-1)), "tot8": i32(jnp.sum(pad8, axis=1))}


def _rope_tables(n_pos):
    pos = np.arange(n_pos)
    row, col = pos // GRID_W, pos % GRID_W
    lane = np.arange(LANES)

    def build(active, r, half):
        n_rot = 4 * half
        is_col = (r % n_rot) >= 2 * half
        rr = r % (2 * half)
        freq = np.power(ROPE_BASE, -(rr % half).astype(np.float64) / half)
        p = np.where(is_col[None, :], col[:, None], row[:, None]).astype(np.float64)
        ang = p * freq[None, :]
        first = rr < half
        cos = np.where(active[None, :], np.cos(ang), 1.0)
        sin_a = np.where((active & first)[None, :], -np.sin(ang), 0.0)
        sin_b = np.where((active & ~first)[None, :], np.sin(ang), 0.0)
        return [cos, sin_a, sin_b]

    tabs = build(np.ones(LANES, bool), lane % GQA_HEAD_DIM, GQA_HEAD_DIM // 4)
    in_rope = (lane >= MLA_NOPE) & (lane < MLA_NOPE + MLA_ROPE)
    tabs += build(in_rope, np.maximum(lane - MLA_NOPE, 0) % MLA_ROPE, MLA_ROPE // 4)
    tabs += build(lane < MLA_ROPE, lane % MLA_ROPE, MLA_ROPE // 4)
    table = np.concatenate(tabs, axis=1)
    ident = np.concatenate([np.ones((TM, LANES)), np.zeros((TM, LANES)), np.zeros((TM, LANES))] * 3, axis=1)
    return jnp.asarray(np.concatenate([ident, table], axis=0), F32)


def _placement():
    e = np.zeros((LANES, MLA_HEADS * LANES), np.float32)
    for h in range(MLA_HEADS):
        for r in range(MLA_ROPE):
            e[r, h * LANES + MLA_NOPE + r] = 1.0
    return jnp.asarray(e, BF)


def _split_hi_lo(w):
    hi = w.astype(BF)
    lo = (w - hi.astype(F32)).astype(BF)
    return jnp.concatenate([hi, lo], axis=-1)


def _block_diag_ones(n, g):
    idx = np.arange(n) // g
    return jnp.asarray((idx[:, None] == idx[None, :]).astype(np.float32), BF)


def kernel(x_prompt, x_sample, cache_gqa_k, cache_gqa_v, cache_mla_ckv, cache_mla_krope, c, c_ctx, norm1_g, norm2_g, w_mod, b_mod, w_in, conv_dw, conv_dw_b, conv_ln_g, conv_ln_b, w_conv_out, pool_w, pool_scale, w_pool_out, gqa_qn_g, gqa_kn_g, w_gqa_out, mla_qn_g, w_mla_q_up, mla_kvn_g, w_mla_kv_up, w_mla_out, w_o, w_router, b_router, w_gu, b_gu, w_dn, b_dn, final_g):
    bp, seq, d = x_prompt.shape
    bs, ls, _ = x_sample.shape
    depth = w_in.shape[0]
    past = cache_gqa_k.shape[2]
    assert seq == TM and d == D_MODEL and ls % TM == 0 and (bp * seq) % ls == 0
    tps = ls // TM
    n_p = bp
    nt = n_p + bs * tps
    geo = {
        "np": n_p, "bs": bs, "tps": tps, "nt": nt,
        "seq_row": lambda i: jnp.where(i < n_p, 0, 1 + (i - n_p) // tps),
        "rope_blk": lambda i: jnp.where(i < n_p, 0, 1 + (i - n_p) % tps),
    }

    n_cond = -(-(1 + bs) // 8) * 8
    cond = jnp.zeros((n_cond, d), F32).at[0].set(c_ctx).at[1:1 + bs].set(c)
    mod = _modulation(cond, w_mod, b_mod).reshape(depth, n_cond, 6, d)

    row = lambda v: v.reshape(depth, 1, -1)
    w1 = jnp.pad(w_in[:, :, :_SPLIT_GATE], ((0, 0), (0, 0), (0, W1_COLS - _SPLIT_GATE))).astype(BF)
    wqup = jnp.pad(w_mla_q_up.reshape(depth, MLA_Q_RANK, MLA_HEADS, MLA_NOPE + MLA_ROPE),
                   ((0, 0), (0, 0), (0, 0), (0, LANES - MLA_NOPE - MLA_ROPE)))
    wkv = w_mla_kv_up.reshape(depth, MLA_KV_RANK, MLA_HEADS, MLA_NOPE + MLA_V)
    wk_pad = jnp.pad(wkv[..., :MLA_NOPE], ((0, 0), (0, 0), (0, 0), (0, LANES - MLA_NOPE)))
    wkvup = jnp.concatenate([wk_pad.reshape(depth, MLA_KV_RANK, MLA_HEADS * LANES),
                             wkv[..., MLA_NOPE:].reshape(depth, MLA_KV_RANK, MLA_HEADS * MLA_V)], axis=-1)
    bgu = b_gu.reshape(depth, N_EXPERTS, 1, D_FF, 2)
    p = {
        "norm1_g": row(norm1_g), "norm2_g": row(norm2_g),
        "w1": w1, "wgate": w_in[:, :, _SPLIT_GATE:].astype(BF),
        "rope": _rope_tables(ls), "place": _placement(), "ones_bd": _block_diag_ones(512, GQA_HEAD_DIM),
        "ltri": jnp.asarray(np.tril(np.ones((TM, TM), np.float32), -1), BF),
        "upper": jnp.asarray(np.triu(np.ones((LANES, LANES), np.float32), 1), F32),
        "qn_g": row(jnp.tile(gqa_qn_g, (1, GQA_HEADS))), "kn_g": row(jnp.tile(gqa_kn_g, (1, GQA_KV_HEADS))),
        "cqn_g": row(mla_qn_g), "kvn_g": row(mla_kvn_g),
        "wqup": wqup.reshape(depth, MLA_Q_RANK, MLA_HEADS * LANES).astype(BF), "wkvup": wkvup.astype(BF),
        "conv_dw": conv_dw, "conv_dw_b": row(conv_dw_b), "conv_ln_g": row(conv_ln_g), "conv_ln_b": row(conv_ln_b),
        "w_conv_out": w_conv_out.astype(BF), "pool_w": pool_w.astype(BF), "pool_scale": row(pool_scale),
        "w_pool_out": w_pool_out.astype(BF), "w_gqa_out": w_gqa_out.astype(BF), "w_mla_out": w_mla_out.astype(BF),
        "w_o": w_o.astype(BF),
        "w_router": _split_hi_lo(jnp.pad(w_router, ((0, 0), (0, 0), (0, LANES - N_EXPERTS)))),
        "b_router": row(jnp.pad(b_router, ((0, 0), (0, LANES - N_EXPERTS)))),
        "w_gu": w_gu, "b_gate": bgu[..., 0], "b_up": bgu[..., 1], "pair_perm": _pair_perm(),
        "w_dn": w_dn, "b_dn": b_dn.reshape(depth, N_EXPERTS, 1, d),
    }

    ckm, cvm = _cache_prep(cache_mla_ckv, jnp.pad(cache_mla_krope, ((0, 0), (0, 0), (0, 0), (0, LANES - MLA_ROPE))), p)
    cache = (cache_gqa_k.reshape(bs, depth, past, LANES).astype(BF),
             cache_gqa_v.reshape(bs, depth, past, LANES).astype(BF), ckm, cvm)

    x = jnp.concatenate([x_prompt.reshape(bp * seq, d), x_sample.reshape(bs * ls, d)], axis=0)
    n_ptok = bp * seq
    states = []
    for l in range(depth):
        a, b, q, k, v, qm, km, vm, ks, vs, ckvs, krs = _proj(l, x, mod, geo, p)
        states.append((ks[:n_ptok], vs[:n_ptok], ckvs[:n_ptok], krs[:n_ptok]))
        attn = _attention(l, geo, q, k, v, qm, km, vm, cache)
        x_mid, xs, slot_row, top_w, meta = _merge(l, x, mod, geo, a, b, attn, p)
        y = _experts(l, xs, _expert_plan(meta, nt), p, nt)
        combine = functools.partial(_combine, l, l == depth - 1, y, x_mid, mod, slot_row, top_w,
                                    final_g.reshape(1, d), geo)
        if l < depth - 1:
            x = combine(0, nt)

    y_prompt = combine(0, n_p).reshape(bp, seq, d)
    y_sample = combine(n_p, nt - n_p).reshape(bs, ls, d)
    st = lambda j, shape: jnp.stack([s[j].reshape(shape) for s in states], axis=1)
    return (y_prompt, y_sample,
            st(0, (bp, seq, GQA_KV_HEADS, GQA_HEAD_DIM)), st(1, (bp, seq, GQA_KV_HEADS, GQA_HEAD_DIM)),
            st(2, (bp, seq, MLA_KV_RANK)), st(3, (bp, seq, MLA_ROPE)))
```
